```python
import math
import jax, jax.numpy as jnp
from jax import lax
import numpy as np

D_MODEL = 1024
BATCH = 8
SEQ = 4096
DEPTH = 2

MIX_W = D_MODEL
N_MIXERS = 4
GROUP_W = MIX_W // N_MIXERS
NORM_EPS = 1e-6

A_HEADS = 4
A_HEAD_DIM = GROUP_W // A_HEADS
DILATED_PATTERNS = ((128, 1), (512, 4), (2048, 16))
N_BUCKETS = 32
REL_MAX_DIST = 2048

CONV_WIDTH = 31
CONV_GROUPS = 4
CONV_LN_EPS = 1e-5

RWKV_HEADS = 4
RWKV_HEAD_DIM = GROUP_W // RWKV_HEADS
RWKV_DECAY_LORA = 32
RWKV_AAA_LORA = 32
RWKV_GATE_LORA = 64
RWKV_LN_EPS = 64e-5

GLA_HEADS = 4
GLA_KEY_DIM = GROUP_W // 2 // GLA_HEADS
GLA_VAL_DIM = GROUP_W // GLA_HEADS
GLA_GATE_LORA = 16
GLA_TAU = 16.0
GLA_CHUNK = 64

D_FF = 2816
N_EXPERTS = 8
TOP_K = 2
MOE_BLOCK = 256
N_DENSE = (DEPTH + 1) // 2
N_MOE = DEPTH // 2

SPLIT_A = (GROUP_W, GROUP_W, GROUP_W)
SPLIT_B = (GROUP_W, GROUP_W)
SPLIT_C = (GROUP_W, GROUP_W, GROUP_W, RWKV_DECAY_LORA, RWKV_AAA_LORA, RWKV_GATE_LORA)
SPLIT_D = (GLA_HEADS * GLA_KEY_DIM, GLA_HEADS * GLA_KEY_DIM, GROUP_W, GLA_GATE_LORA, GROUP_W)
SEGMENTS = (sum(SPLIT_A), sum(SPLIT_B), sum(SPLIT_C), sum(SPLIT_D))
C_IN_W = sum(SPLIT_C)
IN_W = sum(SEGMENTS)

kernel_name = 'hybrid_dilated_conv_rwkv7_gla_moe'


def _split(z, sizes):
    return jnp.split(z, np.cumsum(sizes)[:-1].tolist(), axis=-1)


def _rms_norm(x, g, eps=NORM_EPS):
    xf = x.astype(jnp.float32)
    return xf * lax.rsqrt(jnp.mean(xf * xf, axis=-1, keepdims=True) + eps) * g.astype(jnp.float32)


def _layer_norm(x, eps):
    mu = jnp.mean(x, axis=-1, keepdims=True)
    xc = x - mu
    return xc * lax.rsqrt(jnp.mean(xc * xc, axis=-1, keepdims=True) + eps)


def _t5_bucket(dist):
    max_exact = N_BUCKETS // 2
    n = np.maximum(dist, 0)
    large = max_exact + (np.log(np.maximum(n, 1) / max_exact) / math.log(REL_MAX_DIST / max_exact)
                         * (N_BUCKETS - max_exact)).astype(np.int32)
    large = np.minimum(large, N_BUCKETS - 1)
    return np.where(n < max_exact, n, large).astype(np.int32)


def _dilated_branch(q, k, v, rel_bias, window, dilation):
    B, S, H, E = q.shape
    W = window // dilation
    L = S // dilation
    nb = -(-L // W)
    Lp = nb * W

    def to_blocks(z):
        z = z.reshape(B, L, dilation, H, E).transpose(0, 2, 3, 1, 4)
        z = jnp.pad(z, ((0, 0), (0, 0), (0, 0), (0, Lp - L), (0, 0)))
        return z.reshape(B, dilation, H, nb, W, E)

    def with_prev(z):
        prev = jnp.pad(z[:, :, :, :-1], ((0, 0), (0, 0), (0, 0), (1, 0), (0, 0), (0, 0)))
        return jnp.concatenate([prev, z], axis=4)

    qb = to_blocks(q)
    kb = with_prev(to_blocks(k))
    vb = with_prev(to_blocks(v))
    s = jnp.einsum('bdhnqe,bdhnke->bdhnqk', qb, kb)

    delta = np.arange(W)[:, None] + W - np.arange(2 * W)[None, :]
    band = (delta >= 0) & (delta <= W)
    bucket = _t5_bucket(np.clip(delta, 0, W) * dilation)
    bias = jnp.transpose(rel_bias.astype(jnp.float32)[bucket], (2, 0, 1))
    key_ok = ~((np.arange(nb)[:, None] == 0) & (np.arange(2 * W)[None, :] < W))
    mask = band[None] & key_ok[:, None, :]

    s = jnp.where(mask, s + bias[:, None], -jnp.inf)
    m = jnp.max(s, axis=-1, keepdims=True)
    p = jnp.exp(s - m)
    den = jnp.sum(p, axis=-1, keepdims=True)
    o = jnp.einsum('bdhnqk,bdhnke->bdhnqe', p, vb) / den
    lse = (m + jnp.log(den))[..., 0]
    o = o.reshape(B, dilation, H, Lp, E)[:, :, :, :L].transpose(0, 3, 1, 2, 4).reshape(B, S, H, E)
    lse = lse.reshape(B, dilation, H, Lp)[:, :, :, :L].transpose(0, 3, 1, 2).reshape(B, S, H)
    return o, lse


def _mixer_dilated(za, q_g, k_g, rel_bias):
    B, S, _ = za.shape
    qa, ka, va = _split(za, SPLIT_A)
    shp = (B, S, A_HEADS, A_HEAD_DIM)
    q = _rms_norm(qa.reshape(shp), q_g) * (A_HEAD_DIM ** -0.5)
    k = _rms_norm(ka.reshape(shp), k_g)
    v = va.reshape(shp).astype(jnp.float32)
    outs, lses = [], []
    for window, dilation in DILATED_PATTERNS:
        o, lse = _dilated_branch(q, k, v, rel_bias, window, dilation)
        outs.append(o)
        lses.append(lse)
    wts = jax.nn.softmax(jnp.stack(lses, 0), axis=0)
    o = jnp.sum(wts[..., None] * jnp.stack(outs, 0), axis=0)
    return o.reshape(B, S, GROUP_W)


def _mixer_conv(zb, conv_w, conv_b, ln_g, ln_b):
    B, S, _ = zb.shape
    u, gate = _split(zb.astype(jnp.float32), SPLIT_B)
    h = u * jax.nn.sigmoid(gate)
    h = lax.conv_general_dilated(h, conv_w.astype(jnp.float32)[:, None, :], window_strides=(1,),
                                 padding=[(CONV_WIDTH - 1, 0)], dimension_numbers=('NWC', 'WIO', 'NWC'),
                                 feature_group_count=GROUP_W) + conv_b.astype(jnp.float32)
    h = _layer_norm(h.reshape(B, S, CONV_GROUPS, GROUP_W // CONV_GROUPS), CONV_LN_EPS).reshape(B, S, GROUP_W)
    h = h * ln_g.astype(jnp.float32) + ln_b.astype(jnp.float32)
    return jax.nn.silu(h)


def _mixer_rwkv7(zc, mu, w0, w2, a0, a2, g2, k_k, k_a, r_k, ln_g, ln_b):
    B, S, _ = zc.shape
    f32 = jnp.float32
    zc = zc.astype(f32)
    prev = jnp.pad(zc[:, :-1], ((0, 0), (1, 0), (0, 0)))
    zc = zc + (prev - zc) * mu.astype(f32)
    r, k, v, wd, ad, gd = _split(zc, SPLIT_C)
    w_log = -jax.nn.softplus(-(w0.astype(f32) + jnp.tanh(wd) @ w2.astype(f32))) - 0.5
    decay = jnp.exp(-jnp.exp(w_log))
    a = jax.nn.sigmoid(a0.astype(f32) + ad @ a2.astype(f32))
    g = jax.nn.sigmoid(gd) @ g2.astype(f32)

    def heads(t):
        return t.reshape(B, S, RWKV_HEADS, RWKV_HEAD_DIM)

    kk = heads(k * k_k.astype(f32))
    kk = kk / jnp.maximum(jnp.linalg.norm(kk, axis=-1, keepdims=True), 1e-12)
    k = heads(k * (1.0 + (a - 1.0) * k_a.astype(f32)))
    r, v, decay, a = heads(r), heads(v), heads(decay), heads(a)

    def step(state, inp):
        r_t, w_t, k_t, v_t, kk_t, a_t = inp
        sa = jnp.einsum('bhvk,bhk->bhv', state, -kk_t)
        state = (state * w_t[:, :, None, :] + sa[..., None] * (kk_t * a_t)[:, :, None, :]
                 + v_t[..., None] * k_t[:, :, None, :])
        return state, jnp.einsum('bhvk,bhk->bhv', state, r_t)

    xs = tuple(jnp.moveaxis(t, 1, 0) for t in (r, decay, k, v, kk, a))
    s0 = jnp.zeros((B, RWKV_HEADS, RWKV_HEAD_DIM, RWKV_HEAD_DIM), f32)
    _, out = lax.scan(step, s0, xs)
    out = jnp.moveaxis(out, 0, 1)
    out = _layer_norm(out, RWKV_LN_EPS).reshape(B, S, GROUP_W) * ln_g.astype(f32) + ln_b.astype(f32)
    bonus = jnp.sum(r * k * r_k.astype(f32), axis=-1, keepdims=True) * v
    out = out + bonus.reshape(B, S, GROUP_W)
    return out * g


def _gla_chunked(q, k, v, gk):
    B, S, H, K = q.shape
    V = v.shape[-1]
    n = S // GLA_CHUNK

    def chunks(z):
        return z.reshape(B, n, GLA_CHUNK, H, z.shape[-1]).transpose(0, 3, 1, 2, 4)

    q, k, v, gk = chunks(q), chunks(k), chunks(v), chunks(gk)
    b = jnp.cumsum(gk, axis=3)
    q_d = q * jnp.exp(b)
    k_d = k * jnp.exp(-b)
    causal = np.tril(np.ones((GLA_CHUNK, GLA_CHUNK), dtype=bool))
    att = jnp.where(causal, jnp.einsum('bhncd,bhnjd->bhncj', q_d, k_d), 0.0)
    o = jnp.einsum('bhncj,bhnje->bhnce', att, v)
    b_last = b[:, :, :, -1:]
    contrib = jnp.einsum('bhncd,bhnce->bhnde', k * jnp.exp(b_last - b), v)
    chunk_decay = jnp.exp(b_last[:, :, :, 0])

    def step(state, inp):
        dec, con = inp
        return dec[..., None] * state + con, state

    _, s_prev = lax.scan(step, jnp.zeros((B, H, K, V), jnp.float32),
                         (jnp.moveaxis(chunk_decay, 2, 0), jnp.moveaxis(contrib, 2, 0)))
    s_prev = jnp.moveaxis(s_prev, 0, 2)
    o = o + jnp.einsum('bhncd,bhnde->bhnce', q_d, s_prev)
    return o.transpose(0, 2, 3, 1, 4).reshape(B, S, H, V)


def _mixer_gla(zd, g2, gb, norm_g):
    B, S, _ = zd.shape
    f32 = jnp.float32
    qd, kd, vd, gd, og = _split(zd.astype(f32), SPLIT_D)
    q = qd.reshape(B, S, GLA_HEADS, GLA_KEY_DIM) * (GLA_KEY_DIM ** -0.5)
    k = kd.reshape(B, S, GLA_HEADS, GLA_KEY_DIM)
    v = vd.reshape(B, S, GLA_HEADS, GLA_VAL_DIM)
    gk = jax.nn.log_sigmoid(gd @ g2.astype(f32) + gb.astype(f32)) / GLA_TAU
    o = _gla_chunked(q, k, v, gk.reshape(B, S, GLA_HEADS, GLA_KEY_DIM))
    o = _rms_norm(o, norm_g) * jax.nn.silu(og.reshape(B, S, GLA_HEADS, GLA_VAL_DIM))
    return o.reshape(B, S, GROUP_W)


def _swiglu(x, wg, wu, wd):
    return (jax.nn.silu(x @ wg) * (x @ wu)) @ wd


def _moe(xn, router, wg, wu, wd):
    B, S, D = xn.shape
    T = B * S
    xt = xn.reshape(T, D)
    logits = (xt @ router).astype(jnp.float32)
    top_logits, top_idx = lax.top_k(logits, TOP_K)
    gates = jax.nn.softmax(top_logits, axis=-1)
    n_assign = T * TOP_K
    exp_id = top_idx.reshape(-1).astype(jnp.int32)
    tok_id = jnp.repeat(jnp.arange(T, dtype=jnp.int32), TOP_K)
    order = jnp.argsort(exp_id)
    e_sorted, t_sorted, g_sorted = exp_id[order], tok_id[order], gates.reshape(-1)[order]
    counts = jax.ops.segment_sum(jnp.ones_like(exp_id), exp_id, num_segments=N_EXPERTS)
    starts = jnp.cumsum(counts) - counts
    padded = (counts + MOE_BLOCK - 1) // MOE_BLOCK * MOE_BLOCK
    pends = jnp.cumsum(padded)
    pstarts = pends - padded
    dest = pstarts[e_sorted] + jnp.arange(n_assign, dtype=jnp.int32) - starts[e_sorted]
    n_blocks = -(-n_assign // MOE_BLOCK) + N_EXPERTS
    buf = jnp.zeros((n_blocks * MOE_BLOCK, D), xt.dtype).at[dest].set(xt[t_sorted])
    block_expert = jnp.minimum(jnp.searchsorted(pends, jnp.arange(n_blocks) * MOE_BLOCK, side='right'),
                               N_EXPERTS - 1)

    def run_block(args):
        xb, e = args
        return _swiglu(xb, wg[e], wu[e], wd[e])

    yb = lax.map(run_block, (buf.reshape(n_blocks, MOE_BLOCK, D), block_expert))
    y_sorted = yb.reshape(-1, D)[dest].astype(jnp.float32) * g_sorted[:, None]
    y = jnp.zeros((T, D), jnp.float32).at[t_sorted].add(y_sorted)
    return y.reshape(B, S, D)


def setup_inputs(seed: int = 0) -> dict:
    key = jax.random.key(seed)
    ks = iter(jax.random.split(key, 40))
    f32 = jnp.float32

    def nrm(shape, scale):
        return scale * jax.random.normal(next(ks), shape, f32)

    def gain(shape):
        return 1.0 + nrm(shape, 0.1)

    def unif(shape, lo, hi):
        return jax.random.uniform(next(ks), shape, f32, lo, hi)

    return {
        'x': nrm((BATCH, SEQ, D_MODEL), 1.0),
        'rel_bias': nrm((N_BUCKETS, A_HEADS), 0.5),
        'mix_norm_g': gain((DEPTH, D_MODEL)),
        'w_in': nrm((DEPTH, D_MODEL, IN_W), D_MODEL ** -0.5),
        'w_out': nrm((DEPTH, MIX_W, D_MODEL), MIX_W ** -0.5),
        'ffn_norm_g': gain((DEPTH, D_MODEL)),
        'attn_q_norm_g': gain((DEPTH, A_HEAD_DIM)),
        'attn_k_norm_g': gain((DEPTH, A_HEAD_DIM)),
        'conv_w': nrm((DEPTH, CONV_WIDTH, GROUP_W), CONV_WIDTH ** -0.5),
        'conv_b': nrm((DEPTH, GROUP_W), 0.02),
        'conv_ln_g': gain((DEPTH, GROUP_W)),
        'conv_ln_b': nrm((DEPTH, GROUP_W), 0.02),
        'rwkv_mu': unif((DEPTH, C_IN_W), 0.0, 1.0),
        'rwkv_w0': unif((DEPTH, GROUP_W), -5.0, 1.0),
        'rwkv_w2': nrm((DEPTH, RWKV_DECAY_LORA, GROUP_W), 0.1),
        'rwkv_a0': nrm((DEPTH, GROUP_W), 0.5),
        'rwkv_a2': nrm((DEPTH, RWKV_AAA_LORA, GROUP_W), 0.5 * RWKV_AAA_LORA ** -0.5),
        'rwkv_g2': nrm((DEPTH, RWKV_GATE_LORA, GROUP_W), RWKV_GATE_LORA ** -0.5),
        'rwkv_k_k': 0.85 + nrm((DEPTH, GROUP_W), 0.1),
        'rwkv_k_a': gain((DEPTH, GROUP_W)),
        'rwkv_r_k': nrm((DEPTH, RWKV_HEADS, RWKV_HEAD_DIM), 0.1),
        'rwkv_ln_g': gain((DEPTH, GROUP_W)),
        'rwkv_ln_b': nrm((DEPTH, GROUP_W), 0.02),
        'gla_g2': nrm((DEPTH, GLA_GATE_LORA, GLA_HEADS * GLA_KEY_DIM), GLA_GATE_LORA ** -0.5),
        'gla_gb': nrm((DEPTH, GLA_HEADS * GLA_KEY_DIM), 0.5),
        'gla_norm_g': gain((DEPTH, GLA_VAL_DIM)),
        'ffn_wg': nrm((N_DENSE, D_MODEL, D_FF), D_MODEL ** -0.5),
        'ffn_wu': nrm((N_DENSE, D_MODEL, D_FF), D_MODEL ** -0.5),
        'ffn_wd': nrm((N_DENSE, D_FF, D_MODEL), D_FF ** -0.5),
        'moe_router': nrm((N_MOE, D_MODEL, N_EXPERTS), D_MODEL ** -0.5),
        'moe_wg': nrm((N_MOE, N_EXPERTS, D_MODEL, D_FF), D_MODEL ** -0.5),
        'moe_wu': nrm((N_MOE, N_EXPERTS, D_MODEL, D_FF), D_MODEL ** -0.5),
        'moe_wd': nrm((N_MOE, N_EXPERTS, D_FF, D_MODEL), D_FF ** -0.5),
    }


def reference(x, rel_bias, mix_norm_g, w_in, w_out, ffn_norm_g, attn_q_norm_g, attn_k_norm_g,
              conv_w, conv_b, conv_ln_g, conv_ln_b, rwkv_mu, rwkv_w0, rwkv_w2, rwkv_a0, rwkv_a2,
              rwkv_g2, rwkv_k_k, rwkv_k_a, rwkv_r_k, rwkv_ln_g, rwkv_ln_b, gla_g2, gla_gb, gla_norm_g,
              ffn_wg, ffn_wu, ffn_wd, moe_router, moe_wg, moe_wu, moe_wd):
    h = x
    for layer in range(DEPTH):
        xn = _rms_norm(h, mix_norm_g[layer]).astype(h.dtype)
        z = xn @ w_in[layer]
        za, zb, zc, zd = _split(z, SEGMENTS)
        ya = _mixer_dilated(za, attn_q_norm_g[layer], attn_k_norm_g[layer], rel_bias)
        yb = _mixer_conv(zb, conv_w[layer], conv_b[layer], conv_ln_g[layer], conv_ln_b[layer])
        yc = _mixer_rwkv7(zc, rwkv_mu[layer], rwkv_w0[layer], rwkv_w2[layer], rwkv_a0[layer], rwkv_a2[layer],
                          rwkv_g2[layer], rwkv_k_k[layer], rwkv_k_a[layer], rwkv_r_k[layer],
                          rwkv_ln_g[layer], rwkv_ln_b[layer])
        yd = _mixer_gla(zd, gla_g2[layer], gla_gb[layer], gla_norm_g[layer])
        y = jnp.concatenate([ya, yb, yc, yd], axis=-1).astype(h.dtype)
        h = h + (y @ w_out[layer]).astype(h.dtype)
        hn = _rms_norm(h, ffn_norm_g[layer]).astype(h.dtype)
        i = layer // 2
        if layer % 2 == 0:
            f = _swiglu(hn, ffn_wg[i], ffn_wu[i], ffn_wd[i])
        else:
            f = _moe(hn, moe_router[i], moe_wg[i], moe_wu[i], moe_wd[i])
        h = h + f.astype(h.dtype)
    return h
```

```python
import functools
import math

import numpy as np
import jax
import jax.numpy as jnp
from jax import lax
from jax.experimental import pallas as pl
from jax.experimental.pallas import tpu as pltpu

F32 = jnp.float32
BF16 = jnp.bfloat16
HI = lax.Precision.HIGHEST

D_MODEL = 1024
DEPTH = 2
GROUP_W = 256
NORM_EPS = 1e-6
HEAD_DIM = 64
DILATED_PATTERNS = ((128, 1), (512, 4), (2048, 16))
ATT_BLOCK = 128
N_BUCKETS = 32
REL_MAX_DIST = 2048
CONV_WIDTH = 31
CONV_HALO = 32
CONV_LN_EPS = 1e-5
RWKV_LN_EPS = 64e-5
GLA_KEY_DIM = 32
GLA_TAU = 16.0
CHUNK = 64
D_FF = 2816
N_EXPERTS = 8
IN_W = 2960
IN_WP = 3072
LANE = 128
VMEM_LIMIT = 48 * 1024 * 1024

OFF_AQ, OFF_AK, OFF_AV = 0, 256, 512
OFF_BU, OFF_BG = 768, 1024
OFF_CR, OFF_CK, OFF_CV, OFF_CL = 1280, 1536, 1792, 2048
OFF_DQ, OFF_DV, OFF_DO, OFF_DK, OFF_DG = 2176, 2304, 2560, 2816, 2944


def _cparams(*sem):
    return pltpu.CompilerParams(dimension_semantics=sem, vmem_limit_bytes=VMEM_LIMIT)


def _dot(a, b, prec=None):
    return jnp.dot(a, b, preferred_element_type=F32, precision=prec)


def _mm(a, b):
    return jnp.dot(a.astype(BF16), b.astype(BF16), preferred_element_type=F32)


def _mm_nt(a, b):
    return lax.dot_general(a.astype(BF16), b.astype(BF16), (((1,), (1,)), ((), ())),
                           preferred_element_type=F32)


def _mm_tn(a, b):
    return lax.dot_general(a.astype(BF16), b.astype(BF16), (((0,), (0,)), ((), ())),
                           preferred_element_type=F32)


def _sigmoid(x):
    return 1.0 / (1.0 + jnp.exp(-x))


def _softplus(x):
    return jnp.maximum(x, 0.0) + jnp.log(1.0 + jnp.exp(-jnp.abs(x)))


def _inproj_kernel(x_ref, g_ref, w_ref, z_ref, xn_ref):
    @pl.when(pl.program_id(1) == 0)
    def _():
        x = x_ref[...]
        ms = jnp.mean(x * x, axis=-1, keepdims=True)
        xn_ref[...] = (x * lax.rsqrt(ms + NORM_EPS) * g_ref[...]).astype(BF16)

    z_ref[...] = _dot(xn_ref[...], w_ref[...])


def _inproj(h2, g, w):
    T = h2.shape[0]
    tm, tn = 1024, 768
    return pl.pallas_call(
        _inproj_kernel,
        out_shape=jax.ShapeDtypeStruct((T, IN_WP), F32),
        grid=(T // tm, IN_WP // tn),
        in_specs=[pl.BlockSpec((tm, D_MODEL), lambda i, j: (i, 0)),
                  pl.BlockSpec((1, D_MODEL), lambda i, j: (0, 0)),
                  pl.BlockSpec((D_MODEL, tn), lambda i, j: (0, j))],
        out_specs=pl.BlockSpec((tm, tn), lambda i, j: (i, j)),
        scratch_shapes=[pltpu.VMEM((tm, D_MODEL), BF16)],
        compiler_params=_cparams("parallel", "arbitrary"),
        name="inproj",
    )(h2, g, w)


def _attn_kernel(q_ref, k_ref, v_ref, bias_ref, qg_ref, kg_ref, gm_ref, o_ref, lse_ref, qn_ref, kn_ref,
                 *, n_pairs):
    L = q_ref.shape[0]
    nb = L // ATT_BLOCK
    rows = min(L, 512)

    def norm_body(i, c):
        sl = pl.ds(pl.multiple_of(i * rows, rows), rows)
        for pr in range(n_pairs):
            ln = slice(pr * LANE, (pr + 1) * LANE)
            q = q_ref[sl, ln]
            k = k_ref[sl, ln]
            qn_ref[sl, ln] = q * lax.rsqrt(_dot(q * q, gm_ref[...], HI) + NORM_EPS) * qg_ref[...]
            kn_ref[sl, ln] = k * lax.rsqrt(_dot(k * k, gm_ref[...], HI) + NORM_EPS) * kg_ref[...]
        return c

    lax.fori_loop(0, L // rows, norm_body, 0)

    lane = lax.broadcasted_iota(jnp.int32, (ATT_BLOCK, LANE), 1)
    head0 = lane < HEAD_DIM
    col = lax.broadcasted_iota(jnp.int32, (ATT_BLOCK, 2 * ATT_BLOCK), 1)

    def body(n, c):
        cur = pl.ds(pl.multiple_of(n * ATT_BLOCK, ATT_BLOCK), ATT_BLOCK)
        prv = pl.ds(pl.multiple_of(jnp.maximum(n - 1, 0) * ATT_BLOCK, ATT_BLOCK), ATT_BLOCK)
        no_prev = jnp.logical_and(n == 0, col < ATT_BLOCK)
        for pr in range(n_pairs):
            ln = slice(pr * LANE, (pr + 1) * LANE)
            qt = qn_ref[cur, ln]
            kt = jnp.concatenate([kn_ref[prv, ln], kn_ref[cur, ln]], axis=0).astype(BF16)
            vt = jnp.concatenate([v_ref[prv, ln], v_ref[cur, ln]], axis=0).astype(BF16)
            outs, lses = [], []
            for hh in range(2):
                qh = jnp.where(head0 if hh == 0 else jnp.logical_not(head0), qt, 0.0)
                s = _mm_nt(qh, kt) + bias_ref[2 * pr + hh]
                s = jnp.where(no_prev, -jnp.inf, s)
                m = jnp.max(s, axis=-1, keepdims=True)
                p = jnp.exp(s - m)
                den = jnp.sum(p, axis=-1, keepdims=True)
                outs.append(_mm(p, vt) / den)
                lses.append(m + jnp.log(den))
            o_ref[cur, ln] = jnp.where(head0, outs[0], outs[1])
            lse_ref[cur, ln] = jnp.where(head0, lses[0], lses[1])
        return c

    lax.fori_loop(0, nb, body, 0)


def _attn_branch(z3, bias, qg, kg, gm, dilation):
    B, S, _ = z3.shape
    d = dilation
    L = S // d
    lw = LANE if d == 1 else GROUP_W
    n_pairs = lw // LANE
    n_lb = GROUP_W // lw
    zv = z3.reshape(B, L, d * IN_WP)
    per_r = IN_WP // lw

    def spec(off):
        return pl.BlockSpec((None, L, lw), lambda b, r, p: (b, 0, r * per_r + off // lw + p))

    o_spec = pl.BlockSpec((None, L, lw), lambda b, r, p: (b, 0, r * n_lb + p))
    if n_lb == 1:
        bias_spec = pl.BlockSpec((4, ATT_BLOCK, 2 * ATT_BLOCK), lambda b, r, p: (0, 0, 0))
    else:
        bias_spec = pl.BlockSpec((2, ATT_BLOCK, 2 * ATT_BLOCK), lambda b, r, p: (p, 0, 0))
    vec = pl.BlockSpec((1, LANE), lambda b, r, p: (0, 0))
    o, lse = pl.pallas_call(
        functools.partial(_attn_kernel, n_pairs=n_pairs),
        out_shape=[jax.ShapeDtypeStruct((B, L, d * GROUP_W), F32)] * 2,
        grid=(B, d, n_lb),
        in_specs=[spec(OFF_AQ), spec(OFF_AK), spec(OFF_AV), bias_spec, vec, vec,
                  pl.BlockSpec((LANE, LANE), lambda b, r, p: (0, 0))],
        out_specs=[o_spec, o_spec],
        scratch_shapes=[pltpu.VMEM((L, lw), F32), pltpu.VMEM((L, lw), F32)],
        compiler_params=_cparams("parallel", "parallel", "parallel"),
        name=f"attn_d{d}",
    )(zv, zv, zv, bias, qg, kg, gm)
    return o.reshape(B * S, GROUP_W), lse.reshape(B * S, GROUP_W)


def _attn_merge_kernel(o1, o2, o3, l1, l2, l3, y_ref):
    a, b, c = l1[...], l2[...], l3[...]
    m = jnp.maximum(jnp.maximum(a, b), c)
    ea, eb, ec = jnp.exp(a - m), jnp.exp(b - m), jnp.exp(c - m)
    y = (ea * o1[...] + eb * o2[...] + ec * o3[...]) / (ea + eb + ec)
    y_ref[...] = y.astype(BF16)


def _attn_merge(outs, lses):
    T = outs[0].shape[0]
    tm = 2048
    spec = pl.BlockSpec((tm, GROUP_W), lambda i: (i, 0))
    return pl.pallas_call(
        _attn_merge_kernel,
        out_shape=jax.ShapeDtypeStruct((T, GROUP_W), BF16),
        grid=(T // tm,),
        in_specs=[spec] * 6,
        out_specs=spec,
        compiler_params=_cparams("parallel"),
        name="attn_merge",
    )(*outs, *lses)


def _t5_bucket(dist):
    max_exact = N_BUCKETS // 2
    n = np.maximum(dist, 0)
    large = max_exact + (np.log(np.maximum(n, 1) / max_exact) / math.log(REL_MAX_DIST / max_exact)
                         * (N_BUCKETS - max_exact)).astype(np.int32)
    large = np.minimum(large, N_BUCKETS - 1)
    return np.where(n < max_exact, n, large).astype(np.int32)


def _attn_bias(rel_bias, dilation):
    W = ATT_BLOCK
    delta = np.arange(W)[:, None] + W - np.arange(2 * W)[None, :]
    band = (delta >= 0) & (delta <= W)
    bucket = _t5_bucket(np.clip(delta, 0, W) * dilation)
    bias = jnp.transpose(rel_bias.astype(F32)[bucket], (2, 0, 1))
    return jnp.where(band[None], bias, -jnp.inf)


def _conv_kernel(u_ref, g_ref, cw_ref, pv_ref, gm_ref, o_ref, hbuf):
    ts = u_ref.shape[0]
    t = pl.program_id(1)

    @pl.when(t == 0)
    def _():
        hbuf[0:CONV_HALO, :] = jnp.zeros((CONV_HALO, GROUP_W), F32)

    @pl.when(t > 0)
    def _():
        hbuf[0:CONV_HALO, :] = hbuf[ts:ts + CONV_HALO, :]

    hbuf[CONV_HALO:CONV_HALO + ts, :] = u_ref[...] * _sigmoid(g_ref[...])
    rows = 64
    first = CONV_HALO - (CONV_WIDTH - 1)
    for c in range(ts // rows):
        acc = jnp.zeros((rows, GROUP_W), F32) + pv_ref[0:1, :]
        for j in range(CONV_WIDTH):
            s = first + j + c * rows
            acc = acc + cw_ref[j:j + 1, :] * hbuf[s:s + rows, :]
        mu = _dot(acc, gm_ref[...], HI)
        xc = acc - mu
        var = _dot(xc * xc, gm_ref[...], HI)
        y = xc * lax.rsqrt(var + CONV_LN_EPS) * pv_ref[1:2, :] + pv_ref[2:3, :]
        o_ref[c * rows:(c + 1) * rows, :] = (y * _sigmoid(y)).astype(BF16)


def _conv_mixer(z3, conv_w, pvec, gm):
    B, S, _ = z3.shape
    ts = 512
    return pl.pallas_call(
        _conv_kernel,
        out_shape=jax.ShapeDtypeStruct((B, S, GROUP_W), BF16),
        grid=(B, S // ts),
        in_specs=[pl.BlockSpec((None, ts, GROUP_W), lambda b, t: (b, t, OFF_BU // GROUP_W)),
                  pl.BlockSpec((None, ts, GROUP_W), lambda b, t: (b, t, OFF_BG // GROUP_W)),
                  pl.BlockSpec((CONV_WIDTH, GROUP_W), lambda b, t: (0, 0)),
                  pl.BlockSpec((8, GROUP_W), lambda b, t: (0, 0)),
                  pl.BlockSpec((GROUP_W, GROUP_W), lambda b, t: (0, 0))],
        out_specs=pl.BlockSpec((None, ts, GROUP_W), lambda b, t: (b, t, 0)),
        scratch_shapes=[pltpu.VMEM((ts + CONV_HALO, GROUP_W), F32)],
        compiler_params=_cparams("parallel", "arbitrary"),
        name="conv_mixer",
    )(z3, z3, conv_w, pvec, gm)


_RW_MU_R, _RW_MU_K, _RW_MU_V, _RW_MU_L, _RW_W0, _RW_A0, _RW_KK, _RW_KA, _RW_RK, _RW_LNG, _RW_LNB = range(11)
SHIFT_PAD = 8


def _rwkv_kernel(r_ref, k_ref, v_ref, l_ref, pv_ref, w2_ref, a2_ref, g2_ref, bd_ref, st_ref, in_ref, lt_ref,
                 o_ref, rbuf, kbuf, vbuf, lbuf, lw_s, kk_s, be_s, km_s, rr_s, vv_s, gg_s, bo_s, sm_ref):
    ts = r_ref.shape[0]
    t = pl.program_id(1)
    bufs = ((rbuf, r_ref), (kbuf, k_ref), (vbuf, v_ref), (lbuf, l_ref))

    @pl.when(t == 0)
    def _():
        for buf, _ in bufs:
            buf[0:SHIFT_PAD, :] = jnp.zeros((SHIFT_PAD, buf.shape[1]), F32)
        sm_ref[...] = jnp.zeros(sm_ref.shape, F32)

    @pl.when(t > 0)
    def _():
        for buf, _ in bufs:
            buf[0:SHIFT_PAD, :] = buf[ts:ts + SHIFT_PAD, :]

    for buf, ref in bufs:
        buf[SHIFT_PAD:SHIFT_PAD + ts, :] = ref[...]

    def pv(i, w=GROUP_W):
        return pv_ref[i:i + 1, 0:w]

    def shifted(buf, mu):
        cur = buf[SHIFT_PAD:SHIFT_PAD + ts, :]
        prev = buf[SHIFT_PAD - 1:SHIFT_PAD - 1 + ts, :]
        return cur + (prev - cur) * mu

    bd = bd_ref[...]
    r = shifted(rbuf, pv(_RW_MU_R))
    k = shifted(kbuf, pv(_RW_MU_K))
    v = shifted(vbuf, pv(_RW_MU_V))
    lo = shifted(lbuf, pv(_RW_MU_L, LANE))
    w_log = -_softplus(-(pv(_RW_W0) + _dot(jnp.tanh(lo), w2_ref[...], HI))) - 0.5
    a = _sigmoid(pv(_RW_A0) + _dot(lo, a2_ref[...], HI))
    kk = k * pv(_RW_KK)
    kk = kk / jnp.maximum(jnp.sqrt(_dot(kk * kk, bd, HI)), 1e-12)
    km = k * (1.0 + (a - 1.0) * pv(_RW_KA))
    lw_s[...] = -jnp.exp(w_log)
    kk_s[...] = kk
    be_s[...] = kk * a
    km_s[...] = km
    rr_s[...] = r
    vv_s[...] = v
    gg_s[...] = _dot(_sigmoid(lo), g2_ref[...], HI)
    bo_s[...] = _dot(r * km * pv(_RW_RK), bd, HI) * v

    strict = st_ref[...]
    incl = in_ref[...]
    eye = incl - strict
    n_stack = GROUP_W // HEAD_DIM

    def tile4(x):
        return jnp.concatenate([x] * n_stack, axis=0)

    def fold(x):
        return x[0:CHUNK] + x[CHUNK:2 * CHUNK] + x[2 * CHUNK:3 * CHUNK] + x[3 * CHUNK:4 * CHUNK]

    def chunk(c, carry):
        sl = pl.ds(pl.multiple_of(c * CHUNK, CHUNK), CHUNK)
        lw = lw_s[sl, :]
        cum = _dot(lt_ref[...], lw, HI)
        last = cum[CHUNK - 1:CHUNK, :]
        e_neg = jnp.exp(-cum)
        e_last = jnp.exp(last - cum)
        kkc, bec, kmc, vc = kk_s[sl, :], be_s[sl, :], km_s[sl, :], vv_s[sl, :]
        a_bar = tile4(-kkc * jnp.exp(cum - lw)) * bd
        r_bar = tile4(rr_s[sl, :] * jnp.exp(cum)) * bd
        v_blk = tile4(vc) * bd
        big = _mm_nt(jnp.concatenate([a_bar, r_bar], axis=0),
                     jnp.concatenate([tile4(bec * e_neg), tile4(kmc * e_neg)], axis=0))
        n4 = 4 * CHUNK
        nmat = big[0:n4, 0:n4] * strict
        a_ak = big[0:n4, n4:2 * n4] * strict
        a_rb = big[n4:2 * n4, 0:n4] * incl
        a_rk = big[n4:2 * n4, n4:2 * n4] * incl
        tinv = eye + nmat
        npow = nmat
        for _ in range(5):
            npow = _mm(npow, npow)
            tinv = tinv + _mm(tinv, npow)
        w_blk = _mm(tinv, a_bar)
        u0 = _mm(tinv, _mm(a_ak, v_blk))
        o0 = _mm(a_rb, u0) + _mm(a_rk, v_blk)
        q_eff = r_bar + _mm(a_rb, w_blk)
        b_hat = tile4(bec * e_last) * bd
        k_hat = tile4(kmc * e_last) * bd
        g_mat = eye * jnp.exp(last) + _mm_tn(b_hat, w_blk)
        h_mat = _mm_tn(b_hat, u0) + _mm_tn(k_hat, v_blk)
        sm = sm_ref[...]
        out = _mm(fold(q_eff), sm) + fold(o0)
        sm_ref[...] = _mm(g_mat, sm) + h_mat
        gmean = bd * (1.0 / HEAD_DIM)
        mu = _dot(out, gmean, HI)
        xc = out - mu
        var = _dot(xc * xc, gmean, HI)
        y = xc * lax.rsqrt(var + RWKV_LN_EPS) * pv(_RW_LNG) + pv(_RW_LNB)
        o_ref[sl, :] = ((y + bo_s[sl, :]) * gg_s[sl, :]).astype(BF16)
        return carry

    lax.fori_loop(0, ts // CHUNK, chunk, 0)


def _rwkv_mixer(z3, pvec, w2p, a2p, g2p, bd, strict, incl, ltri):
    B, S, _ = z3.shape
    ts = 512

    def seg(off, w):
        return pl.BlockSpec((None, ts, w), lambda b, t: (b, t, off // w))

    def const(shape):
        return pl.BlockSpec(shape, lambda b, t: (0, 0))

    sq = (4 * CHUNK, 4 * CHUNK)
    wide = pltpu.VMEM((ts, GROUP_W), F32)
    return pl.pallas_call(
        _rwkv_kernel,
        out_shape=jax.ShapeDtypeStruct((B, S, GROUP_W), BF16),
        grid=(B, S // ts),
        in_specs=[seg(OFF_CR, GROUP_W), seg(OFF_CK, GROUP_W), seg(OFF_CV, GROUP_W), seg(OFF_CL, LANE),
                  const((16, GROUP_W)), const((LANE, GROUP_W)), const((LANE, GROUP_W)), const((LANE, GROUP_W)),
                  const(sq), const(sq), const(sq), const((CHUNK, CHUNK))],
        out_specs=pl.BlockSpec((None, ts, GROUP_W), lambda b, t: (b, t, 0)),
        scratch_shapes=[pltpu.VMEM((ts + SHIFT_PAD, GROUP_W), F32)] * 3
                       + [pltpu.VMEM((ts + SHIFT_PAD, LANE), F32)]
                       + [wide] * 8 + [pltpu.VMEM(sq, F32)],
        compiler_params=_cparams("parallel", "arbitrary"),
        name="rwkv7",
    )(z3, z3, z3, z3, pvec, w2p, a2p, g2p, bd, strict, incl, ltri)


def _gla_kernel(q_ref, k_ref, v_ref, og_ref, gd_ref, g2_ref, pv_ref, bd_ref, qm_ref, am_ref, lt_ref, o_ref,
                gk_s, st_ref):
    ts = q_ref.shape[0]

    @pl.when(pl.program_id(1) == 0)
    def _():
        st_ref[...] = jnp.zeros(st_ref.shape, F32)

    x = _dot(gd_ref[...], g2_ref[...], HI) + pv_ref[0:1, 0:LANE]
    gk_s[...] = -_softplus(-x) * (1.0 / GLA_TAU)
    bd = bd_ref[...]
    qmask = qm_ref[...]
    n_heads = GROUP_W // HEAD_DIM

    def chunk(c, carry):
        sl = pl.ds(pl.multiple_of(c * CHUNK, CHUNK), CHUNK)
        b = _dot(lt_ref[...], gk_s[sl, :], HI)
        last = b[CHUNK - 1:CHUNK, :]
        kc = k_ref[sl, :]
        vc = v_ref[sl, :]
        q_d = q_ref[sl, :] * (GLA_KEY_DIM ** -0.5) * jnp.exp(b)
        k_d = kc * jnp.exp(-b)
        att = _mm_nt(jnp.concatenate([q_d] * n_heads, axis=0) * qmask, k_d) * am_ref[...]
        o_blk = _mm(att, vc) * bd
        o = o_blk[0:CHUNK] + o_blk[CHUNK:2 * CHUNK] + o_blk[2 * CHUNK:3 * CHUNK] + o_blk[3 * CHUNK:4 * CHUNK]
        st = st_ref[...]
        o = o + _mm_nt(q_d, st)
        st_ref[...] = st * jnp.exp(last) + _mm_tn(vc, kc * jnp.exp(last - b)) * qmask
        ms = _dot(o * o, bd * (1.0 / HEAD_DIM), HI)
        og = og_ref[sl, :]
        y = o * lax.rsqrt(ms + NORM_EPS) * pv_ref[1:2, :] * (og * _sigmoid(og))
        o_ref[sl, :] = y.astype(BF16)
        return carry

    lax.fori_loop(0, ts // CHUNK, chunk, 0)


def _gla_mixer(z3, g2p, pvec, bd, qmask, amask, ltri):
    B, S, _ = z3.shape
    ts = 512

    def seg(off, w):
        return pl.BlockSpec((None, ts, w), lambda b, t: (b, t, off // w))

    def const(shape):
        return pl.BlockSpec(shape, lambda b, t: (0, 0))

    return pl.pallas_call(
        _gla_kernel,
        out_shape=jax.ShapeDtypeStruct((B, S, GROUP_W), BF16),
        grid=(B, S // ts),
        in_specs=[seg(OFF_DQ, LANE), seg(OFF_DK, LANE), seg(OFF_DV, GROUP_W), seg(OFF_DO, GROUP_W),
                  seg(OFF_DG, LANE), const((LANE, LANE)), const((8, GROUP_W)), const((GROUP_W, GROUP_W)),
                  const((GROUP_W, LANE)), const((GROUP_W, CHUNK)), const((CHUNK, CHUNK))],
        out_specs=pl.BlockSpec((None, ts, GROUP_W), lambda b, t: (b, t, 0)),
        scratch_shapes=[pltpu.VMEM((ts, LANE), F32), pltpu.VMEM((GROUP_W, LANE), F32)],
        compiler_params=_cparams("parallel", "arbitrary"),
        name="gla",
    )(z3, z3, z3, z3, z3, g2p, pvec, bd, qmask, amask, ltri)


def _outproj_kernel(ya, yb, yc, yd, w_ref, h_ref, g_ref, ho_ref, hn_ref):
    acc = h_ref[...]
    for i, y in enumerate((ya, yb, yc, yd)):
        acc = acc + _dot(y[...], w_ref[i * GROUP_W:(i + 1) * GROUP_W, :])
    ho_ref[...] = acc
    ms = jnp.mean(acc * acc, axis=-1, keepdims=True)
    hn_ref[...] = (acc * lax.rsqrt(ms + NORM_EPS) * g_ref[...]).astype(BF16)


def _outproj(ys, w, h2, g):
    T = h2.shape[0]
    tm = 512
    yspec = pl.BlockSpec((tm, GROUP_W), lambda i: (i, 0))
    row = pl.BlockSpec((tm, D_MODEL), lambda i: (i, 0))
    return pl.pallas_call(
        _outproj_kernel,
        out_shape=[jax.ShapeDtypeStruct((T, D_MODEL), F32), jax.ShapeDtypeStruct((T, D_MODEL), BF16)],
        grid=(T // tm,),
        in_specs=[yspec] * 4 + [pl.BlockSpec((D_MODEL, D_MODEL), lambda i: (0, 0)), row,
                                pl.BlockSpec((1, D_MODEL), lambda i: (0, 0))],
        out_specs=[row, row],
        compiler_params=_cparams("parallel"),
        name="outproj",
    )(*ys, w, h2, g)


def _ffn_kernel(x_ref, h_ref, wg_ref, wu_ref, wd_ref, o_ref):
    @pl.when(pl.program_id(1) == 0)
    def _():
        o_ref[...] = h_ref[...]

    x = x_ref[...]
    g = _dot(x, wg_ref[...].astype(BF16))
    u = _dot(x, wu_ref[...].astype(BF16))
    act = (g * _sigmoid(g) * u).astype(BF16)
    o_ref[...] += _dot(act, wd_ref[...].astype(BF16))


def _ffn(hn, h2, wg, wu, wd):
    T = hn.shape[0]
    tm, tf = 1024, 256
    row = pl.BlockSpec((tm, D_MODEL), lambda i, f: (i, 0))
    return pl.pallas_call(
        _ffn_kernel,
        out_shape=jax.ShapeDtypeStruct((T, D_MODEL), F32),
        grid=(T // tm, D_FF // tf),
        in_specs=[row, row,
                  pl.BlockSpec((D_MODEL, tf), lambda i, f: (0, f)),
                  pl.BlockSpec((D_MODEL, tf), lambda i, f: (0, f)),
                  pl.BlockSpec((tf, D_MODEL), lambda i, f: (f, 0))],
        out_specs=row,
        compiler_params=_cparams("parallel", "arbitrary"),
        name="ffn",
    )(hn, h2, wg, wu, wd)


def _router_kernel(h_ref, g_ref, wr_ref, gate_ref):
    x = h_ref[...]
    ms = jnp.mean(x * x, axis=-1, keepdims=True)
    xn = x * lax.rsqrt(ms + NORM_EPS) * g_ref[...]
    logits = _dot(xn, wr_ref[...], HI)
    lane = lax.broadcasted_iota(jnp.int32, logits.shape, 1).astype(F32)
    logits = jnp.where(lane < N_EXPERTS, logits, -jnp.inf)
    m1 = jnp.max(logits, axis=-1, keepdims=True)
    i1 = jnp.min(jnp.where(logits == m1, lane, float(LANE)), axis=-1, keepdims=True)
    rest = jnp.where(lane == i1, -jnp.inf, logits)
    m2 = jnp.max(rest, axis=-1, keepdims=True)
    i2 = jnp.min(jnp.where(rest == m2, lane, float(LANE)), axis=-1, keepdims=True)
    e2 = jnp.exp(m2 - m1)
    den = 1.0 + e2
    gate_ref[...] = jnp.where(lane == i1, 1.0 / den, jnp.where(lane == i2, e2 / den, 0.0))


def _router(h2, g, wr):
    T = h2.shape[0]
    tm = 1024
    return pl.pallas_call(
        _router_kernel,
        out_shape=jax.ShapeDtypeStruct((T, LANE), F32),
        grid=(T // tm,),
        in_specs=[pl.BlockSpec((tm, D_MODEL), lambda i: (i, 0)),
                  pl.BlockSpec((1, D_MODEL), lambda i: (0, 0)),
                  pl.BlockSpec((D_MODEL, LANE), lambda i: (0, 0))],
        out_specs=pl.BlockSpec((tm, LANE), lambda i: (i, 0)),
        compiler_params=_cparams("parallel"),
        name="router",
    )(h2, g, wr)


def _moe_dense_kernel(x_ref, h_ref, gate_ref, wg_ref, wu_ref, wd_ref, o_ref):
    e = pl.program_id(1)
    f = pl.program_id(2)

    @pl.when(jnp.logical_and(e == 0, f == 0))
    def _():
        o_ref[...] = h_ref[...]

    gates = gate_ref[...]
    lane = lax.broadcasted_iota(jnp.int32, gates.shape, 1)
    gcol = jnp.sum(jnp.where(lane == e, gates, 0.0), axis=-1, keepdims=True)
    x = x_ref[...]
    g = _dot(x, wg_ref[...].astype(BF16))
    u = _dot(x, wu_ref[...].astype(BF16))
    act = (g * _sigmoid(g) * u).astype(BF16)
    o_ref[...] += gcol * _dot(act, wd_ref[...].astype(BF16))


def _moe_dense(hn, h2, gates, wg, wu, wd):
    T = hn.shape[0]
    tm, tf = 1024, 256
    row = pl.BlockSpec((tm, D_MODEL), lambda i, e, f: (i, 0))
    return pl.pallas_call(
        _moe_dense_kernel,
        out_shape=jax.ShapeDtypeStruct((T, D_MODEL), F32),
        grid=(T // tm, N_EXPERTS, D_FF // tf),
        in_specs=[row, row, pl.BlockSpec((tm, LANE), lambda i, e, f: (i, 0)),
                  pl.BlockSpec((None, D_MODEL, tf), lambda i, e, f: (e, 0, f)),
                  pl.BlockSpec((None, D_MODEL, tf), lambda i, e, f: (e, 0, f)),
                  pl.BlockSpec((None, tf, D_MODEL), lambda i, e, f: (e, f, 0))],
        out_specs=row,
        compiler_params=_cparams("parallel", "arbitrary", "arbitrary"),
        name="moe_dense",
    )(hn, h2, gates, wg, wu, wd)


def _block_diag_ones(n, blk):
    i = np.arange(n)
    return (i[:, None] // blk == i[None, :] // blk).astype(np.float32)


def _consts():
    n4 = 4 * CHUNK
    i = np.arange(n4)
    bd = _block_diag_ones(n4, CHUNK)
    tr, tc = i[:, None] % CHUNK, i[None, :] % CHUNK
    strict = bd * (tr > tc)
    incl = bd * (tr >= tc)
    ltri = np.tril(np.ones((CHUNK, CHUNK), np.float32))
    pair_mean = _block_diag_ones(LANE, HEAD_DIM) / HEAD_DIM
    qmask = (i[:, None] // CHUNK == np.arange(LANE)[None, :] // GLA_KEY_DIM).astype(np.float32)
    amask = (i[:, None] % CHUNK >= np.arange(CHUNK)[None, :]).astype(np.float32)
    return {k: jnp.asarray(v, F32) for k, v in dict(
        bd=bd, strict=strict, incl=incl, ltri=ltri, pair_mean=pair_mean, qmask=qmask, amask=amask).items()}


def _pad_rows(w, row0, total):
    return jnp.zeros((total, w.shape[1]), F32).at[row0:row0 + w.shape[0]].set(w.astype(F32))


def _pack_rows(rows, width, n_rows):
    out = jnp.zeros((n_rows, width), F32)
    for i, r in enumerate(rows):
        r = r.reshape(-1).astype(F32)
        out = out.at[i, :r.shape[0]].set(r)
    return out


def kernel(x, rel_bias, mix_norm_g, w_in, w_out, ffn_norm_g, attn_q_norm_g, attn_k_norm_g, conv_w, conv_b, conv_ln_g, conv_ln_b, rwkv_mu, rwkv_w0, rwkv_w2, rwkv_a0, rwkv_a2, rwkv_g2, rwkv_k_k, rwkv_k_a, rwkv_r_k, rwkv_ln_g, rwkv_ln_b, gla_g2, gla_gb, gla_norm_g, ffn_wg, ffn_wu, ffn_wd, moe_router, moe_wg, moe_wu, moe_wd):
    B, S, D = x.shape
    T = B * S
    cs = _consts()
    biases = [_attn_bias(rel_bias, d) for _, d in DILATED_PATTERNS]
    h = x.reshape(T, D)
    for layer in range(DEPTH):
        w = w_in[layer]
        dq, dk, dv, dg, do = (w[:, 2176:2304], w[:, 2304:2432], w[:, 2432:2688], w[:, 2688:2704], w[:, 2704:2960])
        wp = jnp.concatenate([w[:, :2176], dq, dv, do, dk, dg, jnp.zeros((D, IN_WP - IN_W), w.dtype)],
                             axis=1).astype(BF16)
        z = _inproj(h, mix_norm_g[layer][None, :], wp)
        z3 = z.reshape(B, S, IN_WP)

        qg = jnp.tile(attn_q_norm_g[layer].astype(F32) * (HEAD_DIM ** -0.5), 2)[None, :]
        kg = jnp.tile(attn_k_norm_g[layer].astype(F32), 2)[None, :]
        outs, lses = [], []
        for (_, d), bias in zip(DILATED_PATTERNS, biases):
            o, lse = _attn_branch(z3, bias, qg, kg, cs["pair_mean"], d)
            outs.append(o)
            lses.append(lse)
        ya = _attn_merge(outs, lses)

        conv_pv = _pack_rows([conv_b[layer], conv_ln_g[layer], conv_ln_b[layer]], GROUP_W, 8)
        group_mean = cs["bd"] * (1.0 / HEAD_DIM)
        yb = _conv_mixer(z3, conv_w[layer].astype(F32), conv_pv, group_mean).reshape(T, GROUP_W)

        mu = rwkv_mu[layer]
        rw_pv = _pack_rows([mu[0:256], mu[256:512], mu[512:768], mu[768:896], rwkv_w0[layer], rwkv_a0[layer],
                            rwkv_k_k[layer], rwkv_k_a[layer], rwkv_r_k[layer], rwkv_ln_g[layer],
                            rwkv_ln_b[layer]], GROUP_W, 16)
        yc = _rwkv_mixer(z3, rw_pv, _pad_rows(rwkv_w2[layer], 0, LANE), _pad_rows(rwkv_a2[layer], 32, LANE),
                         _pad_rows(rwkv_g2[layer], 64, LANE), cs["bd"], cs["strict"], cs["incl"],
                         cs["ltri"]).reshape(T, GROUP_W)

        gla_pv = _pack_rows([gla_gb[layer], jnp.tile(gla_norm_g[layer], GROUP_W // HEAD_DIM)], GROUP_W, 8)
        yd = _gla_mixer(z3, _pad_rows(gla_g2[layer], 0, LANE), gla_pv, cs["bd"], cs["qmask"], cs["amask"],
                        cs["ltri"]).reshape(T, GROUP_W)

        h, hn = _outproj((ya, yb, yc, yd), w_out[layer].astype(BF16), h, ffn_norm_g[layer][None, :])

        i = layer // 2
        if layer % 2 == 0:
            h = _ffn(hn, h, ffn_wg[i], ffn_wu[i], ffn_wd[i])
        else:
            wr = jnp.zeros((D, LANE), F32).at[:, :N_EXPERTS].set(moe_router[i].astype(F32))
            h_pre = h
            gates = _router(h_pre, ffn_norm_g[layer][None, :], wr)
            h = _moe_dense(hn, h_pre, gates, moe_wg[i], moe_wu[i], moe_wd[i])
    return h.reshape(B, S, D)
```

```python
import functools
import math

import numpy as np
import jax
import jax.numpy as jnp
from jax import lax
from jax.experimental import pallas as pl
from jax.experimental.pallas import tpu as pltpu

F32 = jnp.float32
BF16 = jnp.bfloat16
I32 = jnp.int32
HI = lax.Precision.HIGHEST

D_MODEL = 1024
DEPTH = 2
GROUP_W = 256
NORM_EPS = 1e-6
HEAD_DIM = 64
DILATED_PATTERNS = ((128, 1), (512, 4), (2048, 16))
ATT_BLOCK = 128
N_BUCKETS = 32
REL_MAX_DIST = 2048
CONV_WIDTH = 31
CONV_HALO = 32
CONV_LN_EPS = 1e-5
RWKV_LN_EPS = 64e-5
GLA_KEY_DIM = 32
GLA_TAU = 16.0
CHUNK = 64
D_FF = 2816
N_EXPERTS = 8
TOP_K = 2
IN_W = 2960
IN_WP = 3072
LANE = 128
VMEM_LIMIT = 48 * 1024 * 1024
MOE_TILE = 1024
ROUTE_TILE = 512

OFF_AQ, OFF_AK, OFF_AV = 0, 256, 512
OFF_BU, OFF_BG = 768, 1024
OFF_CR, OFF_CK, OFF_CV, OFF_CL = 1280, 1536, 1792, 2048
OFF_DQ, OFF_DV, OFF_DO, OFF_DK, OFF_DG = 2176, 2304, 2560, 2816, 2944


def _cparams(*sem):
    return pltpu.CompilerParams(dimension_semantics=sem, vmem_limit_bytes=VMEM_LIMIT)


def _dot(a, b, prec=None):
    return jnp.dot(a, b, preferred_element_type=F32, precision=prec)


def _mm(a, b):
    return jnp.dot(a.astype(BF16), b.astype(BF16), preferred_element_type=F32)


def _mm_nt(a, b):
    return lax.dot_general(a.astype(BF16), b.astype(BF16), (((1,), (1,)), ((), ())),
                           preferred_element_type=F32)


def _mm_tn(a, b):
    return lax.dot_general(a.astype(BF16), b.astype(BF16), (((0,), (0,)), ((), ())),
                           preferred_element_type=F32)


def _sigmoid(x):
    return 1.0 / (1.0 + jnp.exp(-x))


def _softplus(x):
    return jnp.maximum(x, 0.0) + jnp.log(1.0 + jnp.exp(-jnp.abs(x)))


def _inproj_kernel(x_ref, g_ref, w_ref, z_ref, xn_ref):
    @pl.when(pl.program_id(1) == 0)
    def _():
        x = x_ref[...]
        ms = jnp.mean(x * x, axis=-1, keepdims=True)
        xn_ref[...] = (x * lax.rsqrt(ms + NORM_EPS) * g_ref[...]).astype(BF16)

    z_ref[...] = _dot(xn_ref[...], w_ref[...])


def _inproj(h2, g, w):
    T = h2.shape[0]
    tm, tn = 1024, 768
    return pl.pallas_call(
        _inproj_kernel,
        out_shape=jax.ShapeDtypeStruct((T, IN_WP), F32),
        grid=(T // tm, IN_WP // tn),
        in_specs=[pl.BlockSpec((tm, D_MODEL), lambda i, j: (i, 0)),
                  pl.BlockSpec((1, D_MODEL), lambda i, j: (0, 0)),
                  pl.BlockSpec((D_MODEL, tn), lambda i, j: (0, j))],
        out_specs=pl.BlockSpec((tm, tn), lambda i, j: (i, j)),
        scratch_shapes=[pltpu.VMEM((tm, D_MODEL), BF16)],
        compiler_params=_cparams("parallel", "arbitrary"),
        name="inproj",
    )(h2, g, w)


def _t5_bucket(dist):
    max_exact = N_BUCKETS // 2
    n = np.maximum(dist, 0)
    large = max_exact + (np.log(np.maximum(n, 1) / max_exact) / math.log(REL_MAX_DIST / max_exact)
                         * (N_BUCKETS - max_exact)).astype(np.int32)
    large = np.minimum(large, N_BUCKETS - 1)
    return np.where(n < max_exact, n, large).astype(np.int32)


def _bucket_table():
    W = ATT_BLOCK
    delta = np.arange(W)[:, None] + W - np.arange(2 * W)[None, :]
    band = (delta >= 0) & (delta <= W)
    tabs = [np.where(band, _t5_bucket(np.clip(delta, 0, W) * d), -1) for _, d in DILATED_PATTERNS]
    return jnp.asarray(np.stack(tabs), I32)


def _bias_kernel(rb_ref, bk_ref, o_ref):
    n_heads = o_ref.shape[1]
    for di in range(len(DILATED_PATTERNS)):
        bk = bk_ref[di]
        accs = [jnp.full(bk.shape, -jnp.inf, F32) for _ in range(n_heads)]
        for b in range(N_BUCKETS):
            hit = bk == b
            accs = [jnp.where(hit, rb_ref[b, hh], a) for hh, a in enumerate(accs)]
        for hh in range(n_heads):
            o_ref[di, hh] = accs[hh]


def _attn_bias_tables(rel_bias):
    n_heads = rel_bias.shape[1]
    shape = (len(DILATED_PATTERNS), n_heads, ATT_BLOCK, 2 * ATT_BLOCK)
    return pl.pallas_call(
        _bias_kernel,
        out_shape=jax.ShapeDtypeStruct(shape, F32),
        in_specs=[pl.BlockSpec(memory_space=pltpu.SMEM), pl.BlockSpec(memory_space=pltpu.VMEM)],
        out_specs=pl.BlockSpec(memory_space=pltpu.VMEM),
        name="attn_bias",
    )(rel_bias.astype(F32), _bucket_table())


ATT_MERGE_ROWS = 256


def _attn_kernel(q_ref, k_ref, v_ref, bias_ref, qg_ref, kg_ref, gm_ref, o_ref,
                 qn, kn, qd, kd, vd, od, ld, acc, lse):
    S = q_ref.shape[0]
    rows = 512

    def norm_body(i, c):
        sl = pl.ds(pl.multiple_of(i * rows, rows), rows)
        q = q_ref[sl, :]
        k = k_ref[sl, :]
        qn[sl, :] = q * lax.rsqrt(_dot(q * q, gm_ref[...], HI) + NORM_EPS) * qg_ref[...]
        kn[sl, :] = k * lax.rsqrt(_dot(k * k, gm_ref[...], HI) + NORM_EPS) * kg_ref[...]
        return c

    lax.fori_loop(0, S // rows, norm_body, 0)

    lane = lax.broadcasted_iota(I32, (ATT_BLOCK, LANE), 1)
    head0 = lane < HEAD_DIM
    col = lax.broadcasted_iota(I32, (ATT_BLOCK, 2 * ATT_BLOCK), 1)

    def run_blocks(di, nb, qs, ks, vs, o_dst, l_dst):
        def body(n, c):
            cur = pl.ds(pl.multiple_of(n * ATT_BLOCK, ATT_BLOCK), ATT_BLOCK)
            prv = pl.ds(pl.multiple_of(jnp.maximum(n - 1, 0) * ATT_BLOCK, ATT_BLOCK), ATT_BLOCK)
            no_prev = jnp.logical_and(n == 0, col < ATT_BLOCK)
            qt = qs[cur, :]
            kt = jnp.concatenate([ks[prv, :], ks[cur, :]], axis=0).astype(BF16)
            vt = jnp.concatenate([vs[prv, :], vs[cur, :]], axis=0).astype(BF16)
            outs, lses = [], []
            for hh in range(2):
                qh = jnp.where(head0 if hh == 0 else jnp.logical_not(head0), qt, 0.0)
                s = _mm_nt(qh, kt) + bias_ref[di, hh]
                s = jnp.where(no_prev, -jnp.inf, s)
                m = jnp.max(s, axis=-1, keepdims=True)
                p = jnp.exp(s - m)
                den = jnp.sum(p, axis=-1, keepdims=True)
                outs.append(_mm(p, vt) / den)
                lses.append(m + jnp.log(den))
            o_dst[cur, :] = jnp.where(head0, outs[0], outs[1])
            l_dst[cur, :] = jnp.where(head0, lses[0], lses[1])
            return c

        lax.fori_loop(0, nb, body, 0)

    for di, (_, d) in enumerate(DILATED_PATTERNS):
        L = S // d
        nb = L // ATT_BLOCK
        if d == 1:
            run_blocks(di, nb, qn, kn, v_ref, acc, lse)
            continue
        for r in range(d):
            res = pl.ds(r, L, stride=d)
            qd[0:L, :] = qn[res, :]
            kd[0:L, :] = kn[res, :]
            vd[0:L, :] = v_ref[res, :]
            run_blocks(di, nb, qd, kd, vd, od, ld)
            mr = min(L, ATT_MERGE_ROWS)
            for c in range(L // mr):
                piece = pl.ds(r + c * mr * d, mr, stride=d)
                l0, a0 = lse[piece, :], acc[piece, :]
                l1, a1 = ld[c * mr:(c + 1) * mr, :], od[c * mr:(c + 1) * mr, :]
                m = jnp.maximum(l0, l1)
                e0, e1 = jnp.exp(l0 - m), jnp.exp(l1 - m)
                tot = e0 + e1
                acc[piece, :] = (a0 * e0 + a1 * e1) / tot
                lse[piece, :] = m + jnp.log(tot)

    def out_body(i, c):
        sl = pl.ds(pl.multiple_of(i * rows, rows), rows)
        o_ref[sl, :] = acc[sl, :].astype(BF16)
        return c

    lax.fori_loop(0, S // rows, out_body, 0)


def _attn_mixer(z3, bias, qg, kg, gm):
    B, S, _ = z3.shape
    n_lb = GROUP_W // LANE
    l_max = S // DILATED_PATTERNS[1][1]

    def spec(off):
        return pl.BlockSpec((None, S, LANE), lambda b, p: (b, 0, off // LANE + p))

    vec = pl.BlockSpec((1, LANE), lambda b, p: (0, 0))
    full = pltpu.VMEM((S, LANE), F32)
    part = pltpu.VMEM((l_max, LANE), F32)
    return pl.pallas_call(
        _attn_kernel,
        out_shape=jax.ShapeDtypeStruct((B, S, GROUP_W), BF16),
        grid=(B, n_lb),
        in_specs=[spec(OFF_AQ), spec(OFF_AK), spec(OFF_AV),
                  pl.BlockSpec((len(DILATED_PATTERNS), 2, ATT_BLOCK, 2 * ATT_BLOCK), lambda b, p: (0, p, 0, 0)),
                  vec, vec, pl.BlockSpec((LANE, LANE), lambda b, p: (0, 0))],
        out_specs=pl.BlockSpec((None, S, LANE), lambda b, p: (b, 0, p)),
        scratch_shapes=[full, full, part, part, part, part, part, full, full],
        compiler_params=_cparams("parallel", "parallel"),
        name="attn",
    )(z3, z3, z3, bias, qg, kg, gm)


def _conv_kernel(u_ref, g_ref, cw_ref, pv_ref, gm_ref, o_ref, hbuf):
    ts = u_ref.shape[0]
    t = pl.program_id(1)

    @pl.when(t == 0)
    def _():
        hbuf[0:CONV_HALO, :] = jnp.zeros((CONV_HALO, GROUP_W), F32)

    @pl.when(t > 0)
    def _():
        hbuf[0:CONV_HALO, :] = hbuf[ts:ts + CONV_HALO, :]

    hbuf[CONV_HALO:CONV_HALO + ts, :] = u_ref[...] * _sigmoid(g_ref[...])
    rows = 64
    first = CONV_HALO - (CONV_WIDTH - 1)
    for c in range(ts // rows):
        acc = jnp.zeros((rows, GROUP_W), F32) + pv_ref[0:1, :]
        for j in range(CONV_WIDTH):
            s = first + j + c * rows
            acc = acc + cw_ref[j:j + 1, :] * hbuf[s:s + rows, :]
        mu = _dot(acc, gm_ref[...], HI)
        xc = acc - mu
        var = _dot(xc * xc, gm_ref[...], HI)
        y = xc * lax.rsqrt(var + CONV_LN_EPS) * pv_ref[1:2, :] + pv_ref[2:3, :]
        o_ref[c * rows:(c + 1) * rows, :] = (y * _sigmoid(y)).astype(BF16)


def _conv_mixer(z3, conv_w, pvec, gm):
    B, S, _ = z3.shape
    ts = 512
    return pl.pallas_call(
        _conv_kernel,
        out_shape=jax.ShapeDtypeStruct((B, S, GROUP_W), BF16),
        grid=(B, S // ts),
        in_specs=[pl.BlockSpec((None, ts, GROUP_W), lambda b, t: (b, t, OFF_BU // GROUP_W)),
                  pl.BlockSpec((None, ts, GROUP_W), lambda b, t: (b, t, OFF_BG // GROUP_W)),
                  pl.BlockSpec((CONV_WIDTH, GROUP_W), lambda b, t: (0, 0)),
                  pl.BlockSpec((8, GROUP_W), lambda b, t: (0, 0)),
                  pl.BlockSpec((GROUP_W, GROUP_W), lambda b, t: (0, 0))],
        out_specs=pl.BlockSpec((None, ts, GROUP_W), lambda b, t: (b, t, 0)),
        scratch_shapes=[pltpu.VMEM((ts + CONV_HALO, GROUP_W), F32)],
        compiler_params=_cparams("parallel", "arbitrary"),
        name="conv_mixer",
    )(z3, z3, conv_w, pvec, gm)


_RW_MU_R, _RW_MU_K, _RW_MU_V, _RW_MU_L, _RW_W0, _RW_A0, _RW_KK, _RW_KA, _RW_RK, _RW_LNG, _RW_LNB = range(11)
SHIFT_PAD = 8


def _rwkv_kernel(r_ref, k_ref, v_ref, l_ref, pv_ref, w2_ref, a2_ref, g2_ref, bd_ref, st_ref, in_ref, lt_ref,
                 o_ref, rbuf, kbuf, vbuf, lbuf, lw_s, kk_s, be_s, km_s, rr_s, vv_s, gg_s, bo_s, sm_ref):
    ts = r_ref.shape[0]
    t = pl.program_id(1)
    bufs = ((rbuf, r_ref), (kbuf, k_ref), (vbuf, v_ref), (lbuf, l_ref))

    @pl.when(t == 0)
    def _():
        for buf, _ in bufs:
            buf[0:SHIFT_PAD, :] = jnp.zeros((SHIFT_PAD, buf.shape[1]), F32)
        sm_ref[...] = jnp.zeros(sm_ref.shape, F32)

    @pl.when(t > 0)
    def _():
        for buf, _ in bufs:
            buf[0:SHIFT_PAD, :] = buf[ts:ts + SHIFT_PAD, :]

    for buf, ref in bufs:
        buf[SHIFT_PAD:SHIFT_PAD + ts, :] = ref[...]

    def pv(i, w=GROUP_W):
        return pv_ref[i:i + 1, 0:w]

    def shifted(buf, mu):
        cur = buf[SHIFT_PAD:SHIFT_PAD + ts, :]
        prev = buf[SHIFT_PAD - 1:SHIFT_PAD - 1 + ts, :]
        return cur + (prev - cur) * mu

    bd = bd_ref[...]
    r = shifted(rbuf, pv(_RW_MU_R))
    k = shifted(kbuf, pv(_RW_MU_K))
    v = shifted(vbuf, pv(_RW_MU_V))
    lo = shifted(lbuf, pv(_RW_MU_L, LANE))
    w_log = -_softplus(-(pv(_RW_W0) + _dot(jnp.tanh(lo), w2_ref[...], HI))) - 0.5
    a = _sigmoid(pv(_RW_A0) + _dot(lo, a2_ref[...], HI))
    kk = k * pv(_RW_KK)
    kk = kk / jnp.maximum(jnp.sqrt(_dot(kk * kk, bd, HI)), 1e-12)
    km = k * (1.0 + (a - 1.0) * pv(_RW_KA))
    lw_s[...] = -jnp.exp(w_log)
    kk_s[...] = kk
    be_s[...] = kk * a
    km_s[...] = km
    rr_s[...] = r
    vv_s[...] = v
    gg_s[...] = _dot(_sigmoid(lo), g2_ref[...], HI)
    bo_s[...] = _dot(r * km * pv(_RW_RK), bd, HI) * v

    strict = st_ref[...]
    incl = in_ref[...]
    eye = incl - strict
    n_stack = GROUP_W // HEAD_DIM

    def tile4(x):
        return jnp.concatenate([x] * n_stack, axis=0)

    def fold(x):
        return x[0:CHUNK] + x[CHUNK:2 * CHUNK] + x[2 * CHUNK:3 * CHUNK] + x[3 * CHUNK:4 * CHUNK]

    def chunk(c, carry):
        sl = pl.ds(pl.multiple_of(c * CHUNK, CHUNK), CHUNK)
        lw = lw_s[sl, :]
        cum = _dot(lt_ref[...], lw, HI)
        last = cum[CHUNK - 1:CHUNK, :]
        e_neg = jnp.exp(-cum)
        e_last = jnp.exp(last - cum)
        kkc, bec, kmc, vc = kk_s[sl, :], be_s[sl, :], km_s[sl, :], vv_s[sl, :]
        a_bar = tile4(-kkc * jnp.exp(cum - lw)) * bd
        r_bar = tile4(rr_s[sl, :] * jnp.exp(cum)) * bd
        v_blk = tile4(vc) * bd
        big = _mm_nt(jnp.concatenate([a_bar, r_bar], axis=0),
                     jnp.concatenate([tile4(bec * e_neg), tile4(kmc * e_neg)], axis=0))
        n4 = 4 * CHUNK
        nmat = big[0:n4, 0:n4] * strict
        a_ak = big[0:n4, n4:2 * n4] * strict
        a_rb = big[n4:2 * n4, 0:n4] * incl
        a_rk = big[n4:2 * n4, n4:2 * n4] * incl
        tinv = eye + nmat
        npow = nmat
        for _ in range(5):
            npow = _mm(npow, npow)
            tinv = tinv + _mm(tinv, npow)
        w_blk = _mm(tinv, a_bar)
        u0 = _mm(tinv, _mm(a_ak, v_blk))
        o0 = _mm(a_rb, u0) + _mm(a_rk, v_blk)
        q_eff = r_bar + _mm(a_rb, w_blk)
        b_hat = tile4(bec * e_last) * bd
        k_hat = tile4(kmc * e_last) * bd
        g_mat = eye * jnp.exp(last) + _mm_tn(b_hat, w_blk)
        h_mat = _mm_tn(b_hat, u0) + _mm_tn(k_hat, v_blk)
        sm = sm_ref[...]
        out = _mm(fold(q_eff), sm) + fold(o0)
        sm_ref[...] = _mm(g_mat, sm) + h_mat
        gmean = bd * (1.0 / HEAD_DIM)
        mu = _dot(out, gmean, HI)
        xc = out - mu
        var = _dot(xc * xc, gmean, HI)
        y = xc * lax.rsqrt(var + RWKV_LN_EPS) * pv(_RW_LNG) + pv(_RW_LNB)
        o_ref[sl, :] = ((y + bo_s[sl, :]) * gg_s[sl, :]).astype(BF16)
        return carry

    lax.fori_loop(0, ts // CHUNK, chunk, 0)


def _rwkv_mixer(z3, pvec, w2p, a2p, g2p, bd, strict, incl, ltri):
    B, S, _ = z3.shape
    ts = 512

    def seg(off, w):
        return pl.BlockSpec((None, ts, w), lambda b, t: (b, t, off // w))

    def const(shape):
        return pl.BlockSpec(shape, lambda b, t: (0, 0))

    sq = (4 * CHUNK, 4 * CHUNK)
    wide = pltpu.VMEM((ts, GROUP_W), F32)
    return pl.pallas_call(
        _rwkv_kernel,
        out_shape=jax.ShapeDtypeStruct((B, S, GROUP_W), BF16),
        grid=(B, S // ts),
        in_specs=[seg(OFF_CR, GROUP_W), seg(OFF_CK, GROUP_W), seg(OFF_CV, GROUP_W), seg(OFF_CL, LANE),
                  const((16, GROUP_W)), const((LANE, GROUP_W)), const((LANE, GROUP_W)), const((LANE, GROUP_W)),
                  const(sq), const(sq), const(sq), const((CHUNK, CHUNK))],
        out_specs=pl.BlockSpec((None, ts, GROUP_W), lambda b, t: (b, t, 0)),
        scratch_shapes=[pltpu.VMEM((ts + SHIFT_PAD, GROUP_W), F32)] * 3
                       + [pltpu.VMEM((ts + SHIFT_PAD, LANE), F32)]
                       + [wide] * 8 + [pltpu.VMEM(sq, F32)],
        compiler_params=_cparams("parallel", "arbitrary"),
        name="rwkv7",
    )(z3, z3, z3, z3, pvec, w2p, a2p, g2p, bd, strict, incl, ltri)


def _gla_kernel(q_ref, k_ref, v_ref, og_ref, gd_ref, g2_ref, pv_ref, bd_ref, qm_ref, am_ref, lt_ref, o_ref,
                gk_s, st_ref):
    ts = q_ref.shape[0]

    @pl.when(pl.program_id(1) == 0)
    def _():
        st_ref[...] = jnp.zeros(st_ref.shape, F32)

    x = _dot(gd_ref[...], g2_ref[...], HI) + pv_ref[0:1, 0:LANE]
    gk_s[...] = -_softplus(-x) * (1.0 / GLA_TAU)
    bd = bd_ref[...]
    qmask = qm_ref[...]
    n_heads = GROUP_W // HEAD_DIM

    def chunk(c, carry):
        sl = pl.ds(pl.multiple_of(c * CHUNK, CHUNK), CHUNK)
        b = _dot(lt_ref[...], gk_s[sl, :], HI)
        last = b[CHUNK - 1:CHUNK, :]
        kc = k_ref[sl, :]
        vc = v_ref[sl, :]
        q_d = q_ref[sl, :] * (GLA_KEY_DIM ** -0.5) * jnp.exp(b)
        k_d = kc * jnp.exp(-b)
        att = _mm_nt(jnp.concatenate([q_d] * n_heads, axis=0) * qmask, k_d) * am_ref[...]
        o_blk = _mm(att, vc) * bd
        o = o_blk[0:CHUNK] + o_blk[CHUNK:2 * CHUNK] + o_blk[2 * CHUNK:3 * CHUNK] + o_blk[3 * CHUNK:4 * CHUNK]
        st = st_ref[...]
        o = o + _mm_nt(q_d, st)
        st_ref[...] = st * jnp.exp(last) + _mm_tn(vc, kc * jnp.exp(last - b)) * qmask
        ms = _dot(o * o, bd * (1.0 / HEAD_DIM), HI)
        og = og_ref[sl, :]
        y = o * lax.rsqrt(ms + NORM_EPS) * pv_ref[1:2, :] * (og * _sigmoid(og))
        o_ref[sl, :] = y.astype(BF16)
        return carry

    lax.fori_loop(0, ts // CHUNK, chunk, 0)


def _gla_mixer(z3, g2p, pvec, bd, qmask, amask, ltri):
    B, S, _ = z3.shape
    ts = 512

    def seg(off, w):
        return pl.BlockSpec((None, ts, w), lambda b, t: (b, t, off // w))

    def const(shape):
        return pl.BlockSpec(shape, lambda b, t: (0, 0))

    return pl.pallas_call(
        _gla_kernel,
        out_shape=jax.ShapeDtypeStruct((B, S, GROUP_W), BF16),
        grid=(B, S // ts),
        in_specs=[seg(OFF_DQ, LANE), seg(OFF_DK, LANE), seg(OFF_DV, GROUP_W), seg(OFF_DO, GROUP_W),
                  seg(OFF_DG, LANE), const((LANE, LANE)), const((8, GROUP_W)), const((GROUP_W, GROUP_W)),
                  const((GROUP_W, LANE)), const((GROUP_W, CHUNK)), const((CHUNK, CHUNK))],
        out_specs=pl.BlockSpec((None, ts, GROUP_W), lambda b, t: (b, t, 0)),
        scratch_shapes=[pltpu.VMEM((ts, LANE), F32), pltpu.VMEM((GROUP_W, LANE), F32)],
        compiler_params=_cparams("parallel", "arbitrary"),
        name="gla",
    )(z3, z3, z3, z3, z3, g2p, pvec, bd, qmask, amask, ltri)


def _outproj_kernel(ya, yb, yc, yd, w_ref, h_ref, g_ref, ho_ref, hn_ref):
    acc = h_ref[...]
    for i, y in enumerate((ya, yb, yc, yd)):
        acc = acc + _dot(y[...], w_ref[i * GROUP_W:(i + 1) * GROUP_W, :])
    ho_ref[...] = acc
    ms = jnp.mean(acc * acc, axis=-1, keepdims=True)
    hn_ref[...] = (acc * lax.rsqrt(ms + NORM_EPS) * g_ref[...]).astype(BF16)


def _outproj(ys, w, h2, g):
    T = h2.shape[0]
    tm = 512
    yspec = pl.BlockSpec((tm, GROUP_W), lambda i: (i, 0))
    row = pl.BlockSpec((tm, D_MODEL), lambda i: (i, 0))
    return pl.pallas_call(
        _outproj_kernel,
        out_shape=[jax.ShapeDtypeStruct((T, D_MODEL), F32), jax.ShapeDtypeStruct((T, D_MODEL), BF16)],
        grid=(T // tm,),
        in_specs=[yspec] * 4 + [pl.BlockSpec((D_MODEL, D_MODEL), lambda i: (0, 0)), row,
                                pl.BlockSpec((1, D_MODEL), lambda i: (0, 0))],
        out_specs=[row, row],
        compiler_params=_cparams("parallel"),
        name="outproj",
    )(*ys, w, h2, g)


def _ffn_kernel(x_ref, h_ref, wg_ref, wu_ref, wd_ref, o_ref):
    @pl.when(pl.program_id(1) == 0)
    def _():
        o_ref[...] = h_ref[...]

    x = x_ref[...]
    g = _dot(x, wg_ref[...].astype(BF16))
    u = _dot(x, wu_ref[...].astype(BF16))
    act = (g * _sigmoid(g) * u).astype(BF16)
    o_ref[...] += _dot(act, wd_ref[...].astype(BF16))


def _ffn(hn, h2, wg, wu, wd):
    T = hn.shape[0]
    tm, tf = 1024, 256
    row = pl.BlockSpec((tm, D_MODEL), lambda i, f: (i, 0))
    return pl.pallas_call(
        _ffn_kernel,
        out_shape=jax.ShapeDtypeStruct((T, D_MODEL), F32),
        grid=(T // tm, D_FF // tf),
        in_specs=[row, row,
                  pl.BlockSpec((D_MODEL, tf), lambda i, f: (0, f)),
                  pl.BlockSpec((D_MODEL, tf), lambda i, f: (0, f)),
                  pl.BlockSpec((tf, D_MODEL), lambda i, f: (f, 0))],
        out_specs=row,
        compiler_params=_cparams("parallel", "arbitrary"),
        name="ffn",
    )(hn, h2, wg, wu, wd)


_RT_E0, _RT_E1, _RT_G0, _RT_G1, _RT_R0, _RT_R1 = range(6)


def _router_kernel(h_ref, g_ref, wr_ref, tri_ref, meta_ref, cnt_ref, run_ref):
    @pl.when(pl.program_id(0) == 0)
    def _():
        run_ref[...] = jnp.zeros(run_ref.shape, F32)

    x = h_ref[...]
    ms = jnp.mean(x * x, axis=-1, keepdims=True)
    xn = x * lax.rsqrt(ms + NORM_EPS) * g_ref[...]
    logits = _dot(xn, wr_ref[...], HI)
    lane = lax.broadcasted_iota(I32, logits.shape, 1).astype(F32)
    logits = jnp.where(lane < N_EXPERTS, logits, -jnp.inf)
    m1 = jnp.max(logits, axis=-1, keepdims=True)
    i1 = jnp.min(jnp.where(logits == m1, lane, float(LANE)), axis=-1, keepdims=True)
    rest = jnp.where(lane == i1, -jnp.inf, logits)
    m2 = jnp.max(rest, axis=-1, keepdims=True)
    i2 = jnp.min(jnp.where(rest == m2, lane, float(LANE)), axis=-1, keepdims=True)
    e2 = jnp.exp(m2 - m1)
    den = 1.0 + e2
    hit1 = lane == i1
    hit2 = lane == i2
    member = jnp.where(jnp.logical_or(hit1, hit2), 1.0, 0.0)
    run = run_ref[0:1, :]
    rank = _dot(tri_ref[...], member.astype(BF16)) + run
    r1 = jnp.sum(jnp.where(hit1, rank, 0.0), axis=-1, keepdims=True)
    r2 = jnp.sum(jnp.where(hit2, rank, 0.0), axis=-1, keepdims=True)
    run = run + jnp.sum(member, axis=0, keepdims=True)
    run_ref[0:1, :] = run
    cnt_ref[...] = jnp.broadcast_to(run, cnt_ref.shape)
    rec = jnp.zeros(logits.shape, F32)
    for idx, val in ((_RT_E0, i1), (_RT_E1, i2), (_RT_G0, 1.0 / den), (_RT_G1, e2 / den), (_RT_R0, r1),
                     (_RT_R1, r2)):
        rec = jnp.where(lane == float(idx), val, rec)
    meta_ref[...] = rec


def _router(h2, g, wr, tri):
    T = h2.shape[0]
    tm = tri.shape[0]
    return pl.pallas_call(
        _router_kernel,
        out_shape=[jax.ShapeDtypeStruct((T, LANE), F32), jax.ShapeDtypeStruct((8, LANE), F32)],
        grid=(T // tm,),
        in_specs=[pl.BlockSpec((tm, D_MODEL), lambda i: (i, 0)),
                  pl.BlockSpec((1, D_MODEL), lambda i: (0, 0)),
                  pl.BlockSpec((D_MODEL, LANE), lambda i: (0, 0)),
                  pl.BlockSpec((tm, tm), lambda i: (0, 0))],
        out_specs=[pl.BlockSpec((tm, LANE), lambda i: (i, 0)), pl.BlockSpec((8, LANE), lambda i: (0, 0))],
        scratch_shapes=[pltpu.VMEM((8, LANE), F32)],
        compiler_params=_cparams("arbitrary"),
        name="router",
    )(h2, g, wr, tri)


def _row_copy(src_ref, src_row, dst_ref, dst_row, sem):
    return pltpu.make_async_copy(src_ref.at[pl.ds(src_row, 1)], dst_ref.at[pl.ds(dst_row, 1)], sem)


def _dispatch_kernel(pos_ref, hn_ref, xs_in_ref, xs_ref, buf, sem):
    del xs_in_ref
    tm = hn_ref.shape[0]
    buf[...] = hn_ref[...].astype(F32)

    def issue(j, c):
        for s in range(TOP_K):
            _row_copy(buf, j, xs_ref, pos_ref[0, s * tm + j], sem).start()
        return c

    lax.fori_loop(0, tm, issue, 0, unroll=8)
    for _ in range(TOP_K):
        pltpu.make_async_copy(buf, xs_ref.at[pl.ds(0, tm)], sem).wait()


def _dispatch(pos, hn, n_rows):
    T = hn.shape[0]
    tm = ROUTE_TILE
    xs0 = jnp.zeros((n_rows, D_MODEL), F32)
    return pl.pallas_call(
        _dispatch_kernel,
        out_shape=jax.ShapeDtypeStruct((n_rows, D_MODEL), F32),
        grid=(T // tm,),
        in_specs=[pl.BlockSpec((None, 1, TOP_K * tm), lambda i: (i, 0, 0), memory_space=pltpu.SMEM),
                  pl.BlockSpec((tm, D_MODEL), lambda i: (i, 0)),
                  pl.BlockSpec(memory_space=pl.ANY)],
        out_specs=pl.BlockSpec(memory_space=pl.ANY),
        scratch_shapes=[pltpu.VMEM((tm, D_MODEL), F32), pltpu.SemaphoreType.DMA],
        input_output_aliases={2: 0},
        compiler_params=_cparams("arbitrary"),
        name="moe_dispatch",
    )(pos, hn, xs0)


def _moe_ffn_kernel(te_ref, nu_ref, x_ref, wg_ref, wu_ref, wd_ref, o_ref, xb_ref):
    del te_ref
    i = pl.program_id(0)
    f = pl.program_id(1)

    @pl.when(i < nu_ref[0])
    def _():
        @pl.when(f == 0)
        def _():
            xb_ref[...] = x_ref[...].astype(BF16)
            o_ref[...] = jnp.zeros(o_ref.shape, F32)

        x = xb_ref[...]
        g = _dot(x, wg_ref[...].astype(BF16))
        u = _dot(x, wu_ref[...].astype(BF16))
        act = (g * _sigmoid(g) * u).astype(BF16)
        o_ref[...] += _dot(act, wd_ref[...].astype(BF16))

    @pl.when(jnp.logical_and(i >= nu_ref[0], f == 0))
    def _():
        o_ref[...] = jnp.zeros(o_ref.shape, F32)


def _moe_ffn(tile_expert, n_used, xs, wg, wu, wd):
    n_rows = xs.shape[0]
    tm, tf = MOE_TILE, 256
    nf = D_FF // tf

    def row_map(i, f, te, nu):
        return (jnp.minimum(i, nu[0] - 1), 0)

    def out_map(i, f, te, nu):
        return (i, 0)

    def up_map(i, f, te, nu):
        return (te[i], 0, jnp.where(i < nu[0], f, nf - 1))

    def down_map(i, f, te, nu):
        return (te[i], jnp.where(i < nu[0], f, nf - 1), 0)

    return pl.pallas_call(
        _moe_ffn_kernel,
        out_shape=jax.ShapeDtypeStruct((n_rows, D_MODEL), F32),
        grid_spec=pltpu.PrefetchScalarGridSpec(
            num_scalar_prefetch=2,
            grid=(n_rows // tm, nf),
            in_specs=[pl.BlockSpec((tm, D_MODEL), row_map),
                      pl.BlockSpec((None, D_MODEL, tf), up_map),
                      pl.BlockSpec((None, D_MODEL, tf), up_map),
                      pl.BlockSpec((None, tf, D_MODEL), down_map)],
            out_specs=pl.BlockSpec((tm, D_MODEL), out_map),
            scratch_shapes=[pltpu.VMEM((tm, D_MODEL), BF16)]),
        compiler_params=_cparams("arbitrary", "arbitrary"),
        name="moe_ffn",
    )(tile_expert, n_used, xs, wg, wu, wd)


def _combine_kernel(pos_ref, meta_ref, h_ref, yb_ref, o_ref, buf0, buf1, sems):
    tm = h_ref.shape[0]
    bufs = (buf0, buf1)

    def issue(j, c):
        for s in range(TOP_K):
            _row_copy(yb_ref, pos_ref[0, s * tm + j], bufs[s], j, sems.at[s]).start()
        return c

    lax.fori_loop(0, tm, issue, 0, unroll=8)
    meta = meta_ref[...]
    lane = lax.broadcasted_iota(I32, meta.shape, 1)
    g0 = jnp.sum(jnp.where(lane == _RT_G0, meta, 0.0), axis=-1, keepdims=True)
    g1 = jnp.sum(jnp.where(lane == _RT_G1, meta, 0.0), axis=-1, keepdims=True)
    for s in range(TOP_K):
        pltpu.make_async_copy(yb_ref.at[pl.ds(0, tm)], bufs[s], sems.at[s]).wait()
    o_ref[...] = h_ref[...] + (g0 * buf0[...] + g1 * buf1[...])


def _combine(pos, meta, h2, yb):
    T = h2.shape[0]
    tm = ROUTE_TILE
    row = pl.BlockSpec((tm, D_MODEL), lambda i: (i, 0))
    return pl.pallas_call(
        _combine_kernel,
        out_shape=jax.ShapeDtypeStruct((T, D_MODEL), F32),
        grid=(T // tm,),
        in_specs=[pl.BlockSpec((None, 1, TOP_K * tm), lambda i: (i, 0, 0), memory_space=pltpu.SMEM),
                  pl.BlockSpec((tm, LANE), lambda i: (i, 0)), row,
                  pl.BlockSpec(memory_space=pl.ANY)],
        out_specs=row,
        scratch_shapes=[pltpu.VMEM((tm, D_MODEL), F32), pltpu.VMEM((tm, D_MODEL), F32),
                        pltpu.SemaphoreType.DMA((TOP_K,))],
        compiler_params=_cparams("arbitrary"),
        name="moe_combine",
    )(pos, meta, h2, yb)


def _moe(hn, h2, norm_g, router_w, wg, wu, wd, tri):
    T, D = h2.shape
    wr = jnp.zeros((D, LANE), F32).at[:, :N_EXPERTS].set(router_w.astype(F32))
    meta, cnt = _router(h2, norm_g, wr, tri)
    counts = cnt[0, :N_EXPERTS].astype(I32)
    padded = (counts + MOE_TILE - 1) // MOE_TILE * MOE_TILE
    ends = jnp.cumsum(padded)
    starts = ends - padded
    n_rows = TOP_K * T + N_EXPERTS * MOE_TILE
    n_tiles = n_rows // MOE_TILE
    experts = jnp.arange(N_EXPERTS, dtype=I32)

    def slot(e_lane, r_lane):
        e = meta[:, e_lane].astype(I32)
        start = jnp.sum(jnp.where(e[:, None] == experts[None, :], starts[None, :], 0), axis=1)
        return start + meta[:, r_lane].astype(I32)

    nt = T // ROUTE_TILE
    pos = jnp.concatenate([slot(_RT_E0, _RT_R0).reshape(nt, ROUTE_TILE),
                           slot(_RT_E1, _RT_R1).reshape(nt, ROUTE_TILE)], axis=1)[:, None, :]
    tile_start = jnp.arange(n_tiles, dtype=I32) * MOE_TILE
    tile_expert = jnp.minimum(jnp.sum(ends[None, :] <= tile_start[:, None], axis=1), N_EXPERTS - 1).astype(I32)
    n_used = (ends[-1:] // MOE_TILE).astype(I32)
    xs = _dispatch(pos, hn, n_rows)
    yb = _moe_ffn(tile_expert, n_used, xs, wg, wu, wd)
    return _combine(pos, meta, h2, yb)


def _block_diag_ones(n, blk):
    i = np.arange(n)
    return (i[:, None] // blk == i[None, :] // blk).astype(np.float32)


def _consts():
    n4 = 4 * CHUNK
    i = np.arange(n4)
    bd = _block_diag_ones(n4, CHUNK)
    tr, tc = i[:, None] % CHUNK, i[None, :] % CHUNK
    strict = bd * (tr > tc)
    incl = bd * (tr >= tc)
    ltri = np.tril(np.ones((CHUNK, CHUNK), np.float32))
    pair_mean = _block_diag_ones(LANE, HEAD_DIM) / HEAD_DIM
    qmask = (i[:, None] // CHUNK == np.arange(LANE)[None, :] // GLA_KEY_DIM).astype(np.float32)
    amask = (i[:, None] % CHUNK >= np.arange(CHUNK)[None, :]).astype(np.float32)
    out = {k: jnp.asarray(v, F32) for k, v in dict(
        bd=bd, strict=strict, incl=incl, ltri=ltri, pair_mean=pair_mean, qmask=qmask, amask=amask).items()}
    out["route_tri"] = jnp.asarray(np.tril(np.ones((1024, 1024), np.float32), -1), BF16)
    return out


def _pad_rows(w, row0, total):
    return jnp.zeros((total, w.shape[1]), F32).at[row0:row0 + w.shape[0]].set(w.astype(F32))


def _pack_rows(rows, width, n_rows):
    out = jnp.zeros((n_rows, width), F32)
    for i, r in enumerate(rows):
        r = r.reshape(-1).astype(F32)
        out = out.at[i, :r.shape[0]].set(r)
    return out


def kernel(x, rel_bias, mix_norm_g, w_in, w_out, ffn_norm_g, attn_q_norm_g, attn_k_norm_g, conv_w, conv_b, conv_ln_g, conv_ln_b, rwkv_mu, rwkv_w0, rwkv_w2, rwkv_a0, rwkv_a2, rwkv_g2, rwkv_k_k, rwkv_k_a, rwkv_r_k, rwkv_ln_g, rwkv_ln_b, gla_g2, gla_gb, gla_norm_g, ffn_wg, ffn_wu, ffn_wd, moe_router, moe_wg, moe_wu, moe_wd):
    B, S, D = x.shape
    T = B * S
    cs = _consts()
    bias = _attn_bias_tables(rel_bias)
    h = x.reshape(T, D)
    for layer in range(DEPTH):
        w = w_in[layer]
        dq, dk, dv, dg, do = (w[:, 2176:2304], w[:, 2304:2432], w[:, 2432:2688], w[:, 2688:2704], w[:, 2704:2960])
        wp = jnp.concatenate([w[:, :2176], dq, dv, do, dk, dg, jnp.zeros((D, IN_WP - IN_W), w.dtype)],
                             axis=1).astype(BF16)
        z = _inproj(h, mix_norm_g[layer][None, :], wp)
        z3 = z.reshape(B, S, IN_WP)

        qg = jnp.tile(attn_q_norm_g[layer].astype(F32) * (HEAD_DIM ** -0.5), 2)[None, :]
        kg = jnp.tile(attn_k_norm_g[layer].astype(F32), 2)[None, :]
        ya = _attn_mixer(z3, bias, qg, kg, cs["pair_mean"]).reshape(T, GROUP_W)

        conv_pv = _pack_rows([conv_b[layer], conv_ln_g[layer], conv_ln_b[layer]], GROUP_W, 8)
        group_mean = cs["bd"] * (1.0 / HEAD_DIM)
        yb = _conv_mixer(z3, conv_w[layer].astype(F32), conv_pv, group_mean).reshape(T, GROUP_W)

        mu = rwkv_mu[layer]
        rw_pv = _pack_rows([mu[0:256], mu[256:512], mu[512:768], mu[768:896], rwkv_w0[layer], rwkv_a0[layer],
                            rwkv_k_k[layer], rwkv_k_a[layer], rwkv_r_k[layer], rwkv_ln_g[layer],
                            rwkv_ln_b[layer]], GROUP_W, 16)
        yc = _rwkv_mixer(z3, rw_pv, _pad_rows(rwkv_w2[layer], 0, LANE), _pad_rows(rwkv_a2[layer], 32, LANE),
                         _pad_rows(rwkv_g2[layer], 64, LANE), cs["bd"], cs["strict"], cs["incl"],
                         cs["ltri"]).reshape(T, GROUP_W)

        gla_pv = _pack_rows([gla_gb[layer], jnp.tile(gla_norm_g[layer], GROUP_W // HEAD_DIM)], GROUP_W, 8)
        yd = _gla_mixer(z3, _pad_rows(gla_g2[layer], 0, LANE), gla_pv, cs["bd"], cs["qmask"], cs["amask"],
                        cs["ltri"]).reshape(T, GROUP_W)

        norm_g = ffn_norm_g[layer][None, :]
        h, hn = _outproj((ya, yb, yc, yd), w_out[layer].astype(BF16), h, norm_g)

        i = layer // 2
        if layer % 2 == 0:
            h = _ffn(hn, h, ffn_wg[i], ffn_wu[i], ffn_wd[i])
        else:
            h = _moe(hn, h, norm_g, moe_router[i], moe_wg[i], moe_wu[i], moe_wd[i], cs["route_tri"])
    return h.reshape(B, S, D)
```

```python
import functools
import math

import numpy as np
import jax
import jax.numpy as jnp
from jax import lax
from jax.experimental import pallas as pl
from jax.experimental.pallas import tpu as pltpu

F32 = jnp.float32
BF16 = jnp.bfloat16
I32 = jnp.int32
HI = lax.Precision.HIGHEST

D_MODEL = 1024
DEPTH = 2
GROUP_W = 256
NORM_EPS = 1e-6
HEAD_DIM = 64
DILATED_PATTERNS = ((128, 1), (512, 4), (2048, 16))
ATT_BLOCK = 128
N_BUCKETS = 32
REL_MAX_DIST = 2048
CONV_WIDTH = 31
CONV_HALO = 32
CONV_LN_EPS = 1e-5
RWKV_LN_EPS = 64e-5
GLA_KEY_DIM = 32
GLA_TAU = 16.0
CHUNK = 64
D_FF = 2816
N_EXPERTS = 8
TOP_K = 2
IN_W = 2960
IN_WP = 3072
LANE = 128
SUBLANE = 8
VMEM_LIMIT = 48 * 1024 * 1024
MOE_TILE = 1024
ROUTE_TILE = 512

OFF_AQ, OFF_AK, OFF_AV = 0, 256, 512
OFF_BU, OFF_BG = 768, 1024
OFF_CR, OFF_CK, OFF_CV, OFF_CL = 1280, 1536, 1792, 2048
OFF_DQ, OFF_DV, OFF_DO, OFF_DK, OFF_DG = 2176, 2304, 2560, 2816, 2944


def _cparams(*sem):
    return pltpu.CompilerParams(dimension_semantics=sem, vmem_limit_bytes=VMEM_LIMIT)


def _dot(a, b, prec=None):
    return jnp.dot(a, b, preferred_element_type=F32, precision=prec)


def _mm(a, b):
    return jnp.dot(a.astype(BF16), b.astype(BF16), preferred_element_type=F32)


def _mm_nt(a, b):
    return lax.dot_general(a.astype(BF16), b.astype(BF16), (((1,), (1,)), ((), ())),
                           preferred_element_type=F32)


def _mm_tn(a, b):
    return lax.dot_general(a.astype(BF16), b.astype(BF16), (((0,), (0,)), ((), ())),
                           preferred_element_type=F32)


def _split3(x):
    x1 = x.astype(BF16)
    r1 = x - x1.astype(F32)
    x2 = r1.astype(BF16)
    x3 = (r1 - x2.astype(F32)).astype(BF16)
    return x1, x2, x3


def _dot_exact_rhs(x, m):
    n = x.shape[0]
    y = _dot(jnp.concatenate(_split3(x), axis=0), m)
    return y[0:n] + y[n:2 * n] + y[2 * n:3 * n]


def _dot_exact_lhs(m, x):
    n = x.shape[1]
    y = _dot(m, jnp.concatenate(_split3(x), axis=1))
    return y[:, 0:n] + y[:, n:2 * n] + y[:, 2 * n:3 * n]


def _each(fn, *lists):
    return [fn(*args) for args in zip(*lists)]


def _sigmoid(x):
    return 1.0 / (1.0 + jnp.exp(-x))


def _softplus(x):
    return jnp.maximum(x, 0.0) + jnp.log(1.0 + jnp.exp(-jnp.abs(x)))


def _inproj_kernel(x_ref, g_ref, w_ref, z_ref, xn_ref):
    @pl.when(pl.program_id(1) == 0)
    def _():
        x = x_ref[...]
        ms = jnp.mean(x * x, axis=-1, keepdims=True)
        xn_ref[...] = (x * lax.rsqrt(ms + NORM_EPS) * g_ref[...]).astype(BF16)

    z_ref[...] = _dot(xn_ref[...], w_ref[...])


def _inproj(h2, g, w):
    T = h2.shape[0]
    tm, tn = 1024, 768
    return pl.pallas_call(
        _inproj_kernel,
        out_shape=jax.ShapeDtypeStruct((T, IN_WP), F32),
        grid=(T // tm, IN_WP // tn),
        in_specs=[pl.BlockSpec((tm, D_MODEL), lambda i, j: (i, 0)),
                  pl.BlockSpec((1, D_MODEL), lambda i, j: (0, 0)),
                  pl.BlockSpec((D_MODEL, tn), lambda i, j: (0, j))],
        out_specs=pl.BlockSpec((tm, tn), lambda i, j: (i, j)),
        scratch_shapes=[pltpu.VMEM((tm, D_MODEL), BF16)],
        compiler_params=_cparams("parallel", "arbitrary"),
        name="inproj",
    )(h2, g, w)


def _t5_bucket(dist):
    max_exact = N_BUCKETS // 2
    n = np.maximum(dist, 0)
    large = max_exact + (np.log(np.maximum(n, 1) / max_exact) / math.log(REL_MAX_DIST / max_exact)
                         * (N_BUCKETS - max_exact)).astype(np.int32)
    large = np.minimum(large, N_BUCKETS - 1)
    return np.where(n < max_exact, n, large).astype(np.int32)


def _bucket_table():
    W = ATT_BLOCK
    delta = np.arange(W)[:, None] + W - np.arange(2 * W)[None, :]
    band = (delta >= 0) & (delta <= W)
    tabs = [np.where(band, _t5_bucket(np.clip(delta, 0, W) * d), -1) for _, d in DILATED_PATTERNS]
    return jnp.asarray(np.stack(tabs), I32)


def _bias_kernel(rb_ref, bk_ref, o_ref):
    n_heads = o_ref.shape[1]
    for di in range(len(DILATED_PATTERNS)):
        bk = bk_ref[di]
        accs = [jnp.full(bk.shape, -jnp.inf, F32) for _ in range(n_heads)]
        for b in range(N_BUCKETS):
            hit = bk == b
            accs = [jnp.where(hit, rb_ref[b, hh], a) for hh, a in enumerate(accs)]
        for hh in range(n_heads):
            o_ref[di, hh] = accs[hh]


def _attn_bias_tables(rel_bias):
    n_heads = rel_bias.shape[1]
    shape = (len(DILATED_PATTERNS), n_heads, ATT_BLOCK, 2 * ATT_BLOCK)
    return pl.pallas_call(
        _bias_kernel,
        out_shape=jax.ShapeDtypeStruct(shape, F32),
        in_specs=[pl.BlockSpec(memory_space=pltpu.SMEM), pl.BlockSpec(memory_space=pltpu.VMEM)],
        out_specs=pl.BlockSpec(memory_space=pltpu.VMEM),
        name="attn_bias",
    )(rel_bias.astype(F32), _bucket_table())


ATT_MERGE_ROWS = 256
ATT_UNROLL = 4


def _attn_kernel(q_ref, k_ref, v_ref, bias_ref, qg_ref, kg_ref, gm_ref, o_ref,
                 qn, kn, qd, kd, vd, od, ld, acc, lse):
    S = q_ref.shape[0]
    rows = 512

    def norm_body(i, c):
        sl = pl.ds(pl.multiple_of(i * rows, rows), rows)
        q = q_ref[sl, :]
        k = k_ref[sl, :]
        gm16 = gm_ref[...].astype(BF16)
        qn[sl, :] = q * lax.rsqrt(_dot_exact_rhs(q * q, gm16) + NORM_EPS) * qg_ref[...]
        kn[sl, :] = k * lax.rsqrt(_dot_exact_rhs(k * k, gm16) + NORM_EPS) * kg_ref[...]
        return c

    lax.fori_loop(0, S // rows, norm_body, 0)

    lane = lax.broadcasted_iota(I32, (ATT_BLOCK, LANE), 1)
    head0 = lane < HEAD_DIM
    col = lax.broadcasted_iota(I32, (ATT_BLOCK, 2 * ATT_BLOCK), 1)

    def run_blocks(di, nb, qs, ks, vs, o_dst, l_dst):
        nu = min(nb, ATT_UNROLL)

        def blk(n):
            return pl.ds(pl.multiple_of(n * ATT_BLOCK, ATT_BLOCK), ATT_BLOCK)

        def body(i, c):
            base = i * nu
            rows = [blk(jnp.maximum(base - 1, 0))] + [blk(base + u) for u in range(nu)]
            kb = [ks[r, :].astype(BF16) for r in rows]
            vb = [vs[r, :].astype(BF16) for r in rows]
            kt = [jnp.concatenate([kb[u], kb[u + 1]], axis=0) for u in range(nu)]
            vt = [jnp.concatenate([vb[u], vb[u + 1]], axis=0) for u in range(nu)]
            qt = [qs[r, :] for r in rows[1:]]
            heads = [(u, hh) for u in range(nu) for hh in range(2)]
            s = [_mm_nt(jnp.where(head0 if hh == 0 else jnp.logical_not(head0), qt[u], 0.0), kt[u])
                 + bias_ref[di, hh] for u, hh in heads]
            no_prev = jnp.logical_and(i == 0, col < ATT_BLOCK)
            s = [jnp.where(no_prev, -jnp.inf, x) if u == 0 else x for (u, hh), x in zip(heads, s)]
            m = [jnp.max(x, axis=-1, keepdims=True) for x in s]
            p = _each(lambda x, mx: jnp.exp(x - mx), s, m)
            den = [jnp.sum(x, axis=-1, keepdims=True) for x in p]
            o = [_mm(x, vt[u]) / dn for (u, hh), x, dn in zip(heads, p, den)]
            ls = _each(lambda mx, dn: mx + jnp.log(dn), m, den)
            for u in range(nu):
                o_dst[rows[u + 1], :] = jnp.where(head0, o[2 * u], o[2 * u + 1])
                l_dst[rows[u + 1], :] = jnp.where(head0, ls[2 * u], ls[2 * u + 1])
            return c

        lax.fori_loop(0, nb // nu, body, 0)

    for di, (_, d) in enumerate(DILATED_PATTERNS):
        L = S // d
        nb = L // ATT_BLOCK
        if d == 1:
            run_blocks(di, nb, qn, kn, v_ref, acc, lse)
            continue
        for r in range(d):
            res = pl.ds(r, L, stride=d)
            qd[0:L, :] = qn[res, :]
            kd[0:L, :] = kn[res, :]
            vd[0:L, :] = v_ref[res, :]
            run_blocks(di, nb, qd, kd, vd, od, ld)
            mr = min(L, ATT_MERGE_ROWS)
            for c in range(L // mr):
                piece = pl.ds(r + c * mr * d, mr, stride=d)
                l0, a0 = lse[piece, :], acc[piece, :]
                l1, a1 = ld[c * mr:(c + 1) * mr, :], od[c * mr:(c + 1) * mr, :]
                m = jnp.maximum(l0, l1)
                e0, e1 = jnp.exp(l0 - m), jnp.exp(l1 - m)
                tot = e0 + e1
                acc[piece, :] = (a0 * e0 + a1 * e1) / tot
                lse[piece, :] = m + jnp.log(tot)

    def out_body(i, c):
        sl = pl.ds(pl.multiple_of(i * rows, rows), rows)
        o_ref[sl, :] = acc[sl, :].astype(BF16)
        return c

    lax.fori_loop(0, S // rows, out_body, 0)


def _attn_mixer(z3, bias, qg, kg, gm):
    B, S, _ = z3.shape
    n_lb = GROUP_W // LANE
    l_max = S // DILATED_PATTERNS[1][1]

    def spec(off):
        return pl.BlockSpec((None, S, LANE), lambda b, p: (b, 0, off // LANE + p))

    vec = pl.BlockSpec((1, LANE), lambda b, p: (0, 0))
    full = pltpu.VMEM((S, LANE), F32)
    part = pltpu.VMEM((l_max, LANE), F32)
    return pl.pallas_call(
        _attn_kernel,
        out_shape=jax.ShapeDtypeStruct((B, S, GROUP_W), BF16),
        grid=(B, n_lb),
        in_specs=[spec(OFF_AQ), spec(OFF_AK), spec(OFF_AV),
                  pl.BlockSpec((len(DILATED_PATTERNS), 2, ATT_BLOCK, 2 * ATT_BLOCK), lambda b, p: (0, p, 0, 0)),
                  vec, vec, pl.BlockSpec((LANE, LANE), lambda b, p: (0, 0))],
        out_specs=pl.BlockSpec((None, S, LANE), lambda b, p: (b, 0, p)),
        scratch_shapes=[full, full, part, part, part, part, part, full, full],
        compiler_params=_cparams("parallel", "parallel"),
        name="attn",
    )(z3, z3, z3, bias, qg, kg, gm)


def _conv_kernel(u_ref, g_ref, cw_ref, pv_ref, gm_ref, o_ref, hbuf, hsh):
    ts = u_ref.shape[0]
    t = pl.program_id(1)

    @pl.when(t == 0)
    def _():
        hbuf[0:CONV_HALO, :] = jnp.zeros((CONV_HALO, GROUP_W), F32)

    @pl.when(t > 0)
    def _():
        hbuf[0:CONV_HALO, :] = hbuf[ts:ts + CONV_HALO, :]

    hbuf[CONV_HALO:CONV_HALO + ts, :] = u_ref[...] * _sigmoid(g_ref[...])
    first = CONV_HALO - (CONV_WIDTH - 1)
    n_shift = hsh.shape[1]
    for ph in range(1, SUBLANE):
        hsh[ph - 1, :, :] = hbuf[ph:ph + n_shift, :]
    rows = 64
    gm16 = gm_ref[...].astype(BF16)
    for c in range(ts // rows):
        acc = jnp.zeros((rows, GROUP_W), F32) + pv_ref[0:1, :]
        for j in range(CONV_WIDTH):
            ph = (first + j) % SUBLANE
            s = first + j - ph + c * rows
            tap = hbuf[s:s + rows, :] if ph == 0 else hsh[ph - 1, s:s + rows, :]
            acc = acc + cw_ref[j:j + 1, :] * tap
        mu = _dot_exact_rhs(acc, gm16)
        xc = acc - mu
        var = _dot_exact_rhs(xc * xc, gm16)
        y = xc * lax.rsqrt(var + CONV_LN_EPS) * pv_ref[1:2, :] + pv_ref[2:3, :]
        o_ref[c * rows:(c + 1) * rows, :] = (y * _sigmoid(y)).astype(BF16)


def _conv_mixer(z3, conv_w, pvec, gm):
    B, S, _ = z3.shape
    ts = 512
    return pl.pallas_call(
        _conv_kernel,
        out_shape=jax.ShapeDtypeStruct((B, S, GROUP_W), BF16),
        grid=(B, S // ts),
        in_specs=[pl.BlockSpec((None, ts, GROUP_W), lambda b, t: (b, t, OFF_BU // GROUP_W)),
                  pl.BlockSpec((None, ts, GROUP_W), lambda b, t: (b, t, OFF_BG // GROUP_W)),
                  pl.BlockSpec((CONV_WIDTH, GROUP_W), lambda b, t: (0, 0)),
                  pl.BlockSpec((8, GROUP_W), lambda b, t: (0, 0)),
                  pl.BlockSpec((GROUP_W, GROUP_W), lambda b, t: (0, 0))],
        out_specs=pl.BlockSpec((None, ts, GROUP_W), lambda b, t: (b, t, 0)),
        scratch_shapes=[pltpu.VMEM((ts + CONV_HALO, GROUP_W), F32),
                        pltpu.VMEM((SUBLANE - 1, ts + CONV_HALO - SUBLANE, GROUP_W), F32)],
        compiler_params=_cparams("parallel", "arbitrary"),
        name="conv_mixer",
    )(z3, z3, conv_w, pvec, gm)


_RW_MU_R, _RW_MU_K, _RW_MU_V, _RW_MU_L, _RW_W0, _RW_A0, _RW_KK, _RW_KA, _RW_RK, _RW_LNG, _RW_LNB = range(11)
SHIFT_PAD = 8
RWKV_GROUP = 4
GLA_GROUP = 4


def _rwkv_kernel(r_ref, k_ref, v_ref, l_ref, pv_ref, w2_ref, a2_ref, g2_ref, bd_ref, st_ref, in_ref, lt_ref,
                 o_ref, rbuf, kbuf, vbuf, lbuf, lw_s, kk_s, be_s, km_s, rr_s, vv_s, gg_s, bo_s, sm_ref):
    ts = r_ref.shape[0]
    t = pl.program_id(1)
    bufs = ((rbuf, r_ref), (kbuf, k_ref), (vbuf, v_ref), (lbuf, l_ref))

    @pl.when(t == 0)
    def _():
        for buf, _ in bufs:
            buf[0:SHIFT_PAD, :] = jnp.zeros((SHIFT_PAD, buf.shape[1]), F32)
        sm_ref[...] = jnp.zeros(sm_ref.shape, F32)

    @pl.when(t > 0)
    def _():
        for buf, _ in bufs:
            buf[0:SHIFT_PAD, :] = buf[ts:ts + SHIFT_PAD, :]

    for buf, ref in bufs:
        buf[SHIFT_PAD:SHIFT_PAD + ts, :] = ref[...]

    def pv(i, w=GROUP_W):
        return pv_ref[i:i + 1, 0:w]

    def shifted(buf, mu):
        cur = buf[SHIFT_PAD:SHIFT_PAD + ts, :]
        prev = buf[SHIFT_PAD - 1:SHIFT_PAD - 1 + ts, :]
        return cur + (prev - cur) * mu

    bd = bd_ref[...]
    bd16 = bd.astype(BF16)
    r = shifted(rbuf, pv(_RW_MU_R))
    k = shifted(kbuf, pv(_RW_MU_K))
    v = shifted(vbuf, pv(_RW_MU_V))
    lo = shifted(lbuf, pv(_RW_MU_L, LANE))
    w_log = -_softplus(-(pv(_RW_W0) + _dot(jnp.tanh(lo), w2_ref[...], HI))) - 0.5
    a = _sigmoid(pv(_RW_A0) + _dot(lo, a2_ref[...], HI))
    kk = k * pv(_RW_KK)
    kk = kk / jnp.maximum(jnp.sqrt(_dot_exact_rhs(kk * kk, bd16)), 1e-12)
    km = k * (1.0 + (a - 1.0) * pv(_RW_KA))
    lw_s[...] = -jnp.exp(w_log)
    kk_s[...] = kk
    be_s[...] = kk * a
    km_s[...] = km
    rr_s[...] = r
    vv_s[...] = v
    gg_s[...] = _dot(_sigmoid(lo), g2_ref[...], HI)
    bo_s[...] = _dot_exact_rhs(r * km * pv(_RW_RK), bd16) * v

    strict = st_ref[...]
    incl = in_ref[...]
    eye = incl - strict
    lt16 = lt_ref[...].astype(BF16)
    gmean16 = (bd * (1.0 / HEAD_DIM)).astype(BF16)
    n_stack = GROUP_W // HEAD_DIM
    n4 = n_stack * CHUNK

    def tile4(x):
        return jnp.concatenate([x] * n_stack, axis=0)

    def fold(x):
        return x[0:CHUNK] + x[CHUNK:2 * CHUNK] + x[2 * CHUNK:3 * CHUNK] + x[3 * CHUNK:4 * CHUNK]

    def group(gi, carry):
        sls = [pl.ds(pl.multiple_of((gi * RWKV_GROUP + j) * CHUNK, CHUNK), CHUNK) for j in range(RWKV_GROUP)]
        lw = [lw_s[s, :] for s in sls]
        cum = _each(lambda x: _dot_exact_lhs(lt16, x), lw)
        last = [c[CHUNK - 1:CHUNK, :] for c in cum]
        e_neg = [jnp.exp(-c) for c in cum]
        e_last = _each(lambda l, c: jnp.exp(l - c), last, cum)
        bec = [be_s[s, :] for s in sls]
        kmc = [km_s[s, :] for s in sls]
        a_bar = _each(lambda s, c, w: tile4(-kk_s[s, :] * jnp.exp(c - w)) * bd, sls, cum, lw)
        r_bar = _each(lambda s, c: tile4(rr_s[s, :] * jnp.exp(c)) * bd, sls, cum)
        v_blk = [tile4(vv_s[s, :]) * bd for s in sls]
        big = _each(lambda ab, rb, be, km_, en: _mm_nt(
            jnp.concatenate([ab, rb], axis=0),
            jnp.concatenate([tile4(be * en), tile4(km_ * en)], axis=0)), a_bar, r_bar, bec, kmc, e_neg)
        nmat = [b[0:n4, 0:n4] * strict for b in big]
        a_ak = [b[0:n4, n4:2 * n4] * strict for b in big]
        a_rb = [b[n4:2 * n4, 0:n4] * incl for b in big]
        a_rk = [b[n4:2 * n4, n4:2 * n4] * incl for b in big]
        tinv = [eye + n for n in nmat]
        npow = nmat
        for _ in range(5):
            npow = _each(lambda n: _mm(n, n), npow)
            tinv = _each(lambda t_, n: t_ + _mm(t_, n), tinv, npow)
        w_blk = _each(_mm, tinv, a_bar)
        u0 = _each(lambda t_, ak, vb: _mm(t_, _mm(ak, vb)), tinv, a_ak, v_blk)
        o0 = _each(lambda rb, u, rk, vb: fold(_mm(rb, u) + _mm(rk, vb)), a_rb, u0, a_rk, v_blk)
        q_eff = _each(lambda rbar, rb, w: fold(rbar + _mm(rb, w)), r_bar, a_rb, w_blk)
        gh = _each(lambda be, el, w, u: _mm_tn(tile4(be * el) * bd, jnp.concatenate([w, u], axis=1)),
                   bec, e_last, w_blk, u0)
        g_mat = _each(lambda l, x: eye * jnp.exp(l) + x[:, 0:n4], last, gh)
        h_mat = _each(lambda x, km_, el, vb: x[:, n4:2 * n4] + _mm_tn(tile4(km_ * el) * bd, vb),
                      gh, kmc, e_last, v_blk)
        sm = sm_ref[...]
        outs = []
        for j in range(RWKV_GROUP):
            outs.append(_mm(q_eff[j], sm) + o0[j])
            sm = _mm(g_mat[j], sm) + h_mat[j]
        sm_ref[...] = sm
        for s, out in zip(sls, outs):
            mu = _dot_exact_rhs(out, gmean16)
            xc = out - mu
            var = _dot_exact_rhs(xc * xc, gmean16)
            y = xc * lax.rsqrt(var + RWKV_LN_EPS) * pv(_RW_LNG) + pv(_RW_LNB)
            o_ref[s, :] = ((y + bo_s[s, :]) * gg_s[s, :]).astype(BF16)
        return carry

    lax.fori_loop(0, ts // (CHUNK * RWKV_GROUP), group, 0)


def _rwkv_mixer(z3, pvec, w2p, a2p, g2p, bd, strict, incl, ltri):
    B, S, _ = z3.shape
    ts = 512

    def seg(off, w):
        return pl.BlockSpec((None, ts, w), lambda b, t: (b, t, off // w))

    def const(shape):
        return pl.BlockSpec(shape, lambda b, t: (0, 0))

    sq = (4 * CHUNK, 4 * CHUNK)
    wide = pltpu.VMEM((ts, GROUP_W), F32)
    return pl.pallas_call(
        _rwkv_kernel,
        out_shape=jax.ShapeDtypeStruct((B, S, GROUP_W), BF16),
        grid=(B, S // ts),
        in_specs=[seg(OFF_CR, GROUP_W), seg(OFF_CK, GROUP_W), seg(OFF_CV, GROUP_W), seg(OFF_CL, LANE),
                  const((16, GROUP_W)), const((LANE, GROUP_W)), const((LANE, GROUP_W)), const((LANE, GROUP_W)),
                  const(sq), const(sq), const(sq), const((CHUNK, CHUNK))],
        out_specs=pl.BlockSpec((None, ts, GROUP_W), lambda b, t: (b, t, 0)),
        scratch_shapes=[pltpu.VMEM((ts + SHIFT_PAD, GROUP_W), F32)] * 3
                       + [pltpu.VMEM((ts + SHIFT_PAD, LANE), F32)]
                       + [wide] * 8 + [pltpu.VMEM(sq, F32)],
        compiler_params=_cparams("parallel", "arbitrary"),
        name="rwkv7",
    )(z3, z3, z3, z3, pvec, w2p, a2p, g2p, bd, strict, incl, ltri)


def _gla_kernel(q_ref, k_ref, v_ref, og_ref, gd_ref, g2_ref, pv_ref, bd_ref, qm_ref, am_ref, lt_ref, o_ref,
                gk_s, st_ref):
    ts = q_ref.shape[0]

    @pl.when(pl.program_id(1) == 0)
    def _():
        st_ref[...] = jnp.zeros(st_ref.shape, F32)

    x = _dot(gd_ref[...], g2_ref[...], HI) + pv_ref[0:1, 0:LANE]
    gk_s[...] = -_softplus(-x) * (1.0 / GLA_TAU)
    bd = bd_ref[...]
    qmask = qm_ref[...]
    n_heads = GROUP_W // HEAD_DIM

    lt16 = lt_ref[...].astype(BF16)
    gmean16 = (bd * (1.0 / HEAD_DIM)).astype(BF16)

    def fold(x):
        return x[0:CHUNK] + x[CHUNK:2 * CHUNK] + x[2 * CHUNK:3 * CHUNK] + x[3 * CHUNK:4 * CHUNK]

    def group(gi, carry):
        sls = [pl.ds(pl.multiple_of((gi * GLA_GROUP + j) * CHUNK, CHUNK), CHUNK) for j in range(GLA_GROUP)]
        b = [_dot_exact_lhs(lt16, gk_s[s, :]) for s in sls]
        last = [x[CHUNK - 1:CHUNK, :] for x in b]
        kc = [k_ref[s, :] for s in sls]
        vc = [v_ref[s, :] for s in sls]
        q_d = _each(lambda s, x: q_ref[s, :] * (GLA_KEY_DIM ** -0.5) * jnp.exp(x), sls, b)
        k_d = _each(lambda k_, x: k_ * jnp.exp(-x), kc, b)
        att = _each(lambda q_, k_: _mm_nt(jnp.concatenate([q_] * n_heads, axis=0) * qmask, k_) * am_ref[...],
                    q_d, k_d)
        o_in = _each(lambda a_, v_: fold(_mm(a_, v_) * bd), att, vc)
        upd = _each(lambda v_, k_, l, x: _mm_tn(v_, k_ * jnp.exp(l - x)) * qmask, vc, kc, last, b)
        st = st_ref[...]
        outs = []
        for j in range(GLA_GROUP):
            outs.append(o_in[j] + _mm_nt(q_d[j], st))
            st = st * jnp.exp(last[j]) + upd[j]
        st_ref[...] = st
        for s, o in zip(sls, outs):
            ms = _dot_exact_rhs(o * o, gmean16)
            og = og_ref[s, :]
            y = o * lax.rsqrt(ms + NORM_EPS) * pv_ref[1:2, :] * (og * _sigmoid(og))
            o_ref[s, :] = y.astype(BF16)
        return carry

    lax.fori_loop(0, ts // (CHUNK * GLA_GROUP), group, 0)


def _gla_mixer(z3, g2p, pvec, bd, qmask, amask, ltri):
    B, S, _ = z3.shape
    ts = 512

    def seg(off, w):
        return pl.BlockSpec((None, ts, w), lambda b, t: (b, t, off // w))

    def const(shape):
        return pl.BlockSpec(shape, lambda b, t: (0, 0))

    return pl.pallas_call(
        _gla_kernel,
        out_shape=jax.ShapeDtypeStruct((B, S, GROUP_W), BF16),
        grid=(B, S // ts),
        in_specs=[seg(OFF_DQ, LANE), seg(OFF_DK, LANE), seg(OFF_DV, GROUP_W), seg(OFF_DO, GROUP_W),
                  seg(OFF_DG, LANE), const((LANE, LANE)), const((8, GROUP_W)), const((GROUP_W, GROUP_W)),
                  const((GROUP_W, LANE)), const((GROUP_W, CHUNK)), const((CHUNK, CHUNK))],
        out_specs=pl.BlockSpec((None, ts, GROUP_W), lambda b, t: (b, t, 0)),
        scratch_shapes=[pltpu.VMEM((ts, LANE), F32), pltpu.VMEM((GROUP_W, LANE), F32)],
        compiler_params=_cparams("parallel", "arbitrary"),
        name="gla",
    )(z3, z3, z3, z3, z3, g2p, pvec, bd, qmask, amask, ltri)


def _outproj_kernel(ya, yb, yc, yd, w_ref, h_ref, g_ref, ho_ref, hn_ref):
    acc = h_ref[...]
    for i, y in enumerate((ya, yb, yc, yd)):
        acc = acc + _dot(y[...], w_ref[i * GROUP_W:(i + 1) * GROUP_W, :])
    ho_ref[...] = acc
    ms = jnp.mean(acc * acc, axis=-1, keepdims=True)
    hn_ref[...] = (acc * lax.rsqrt(ms + NORM_EPS) * g_ref[...]).astype(BF16)


def _outproj(ys, w, h2, g):
    T = h2.shape[0]
    tm = 512
    yspec = pl.BlockSpec((tm, GROUP_W), lambda i: (i, 0))
    row = pl.BlockSpec((tm, D_MODEL), lambda i: (i, 0))
    return pl.pallas_call(
        _outproj_kernel,
        out_shape=[jax.ShapeDtypeStruct((T, D_MODEL), F32), jax.ShapeDtypeStruct((T, D_MODEL), BF16)],
        grid=(T // tm,),
        in_specs=[yspec] * 4 + [pl.BlockSpec((D_MODEL, D_MODEL), lambda i: (0, 0)), row,
                                pl.BlockSpec((1, D_MODEL), lambda i: (0, 0))],
        out_specs=[row, row],
        compiler_params=_cparams("parallel"),
        name="outproj",
    )(*ys, w, h2, g)


def _ffn_kernel(x_ref, h_ref, wg_ref, wu_ref, wd_ref, o_ref):
    @pl.when(pl.program_id(1) == 0)
    def _():
        o_ref[...] = h_ref[...]

    x = x_ref[...]
    g = _dot(x, wg_ref[...].astype(BF16))
    u = _dot(x, wu_ref[...].astype(BF16))
    act = (g * _sigmoid(g) * u).astype(BF16)
    o_ref[...] += _dot(act, wd_ref[...].astype(BF16))


def _ffn(hn, h2, wg, wu, wd):
    T = hn.shape[0]
    tm, tf = 1024, 256
    row = pl.BlockSpec((tm, D_MODEL), lambda i, f: (i, 0))
    return pl.pallas_call(
        _ffn_kernel,
        out_shape=jax.ShapeDtypeStruct((T, D_MODEL), F32),
        grid=(T // tm, D_FF // tf),
        in_specs=[row, row,
                  pl.BlockSpec((D_MODEL, tf), lambda i, f: (0, f)),
                  pl.BlockSpec((D_MODEL, tf), lambda i, f: (0, f)),
                  pl.BlockSpec((tf, D_MODEL), lambda i, f: (f, 0))],
        out_specs=row,
        compiler_params=_cparams("parallel", "arbitrary"),
        name="ffn",
    )(hn, h2, wg, wu, wd)


_RT_E0, _RT_E1, _RT_G0, _RT_G1, _RT_R0, _RT_R1 = range(6)


def _router_kernel(h_ref, g_ref, wr_ref, tri_ref, meta_ref, cnt_ref, run_ref):
    @pl.when(pl.program_id(0) == 0)
    def _():
        run_ref[...] = jnp.zeros(run_ref.shape, F32)

    x = h_ref[...]
    ms = jnp.mean(x * x, axis=-1, keepdims=True)
    xn = x * lax.rsqrt(ms + NORM_EPS) * g_ref[...]
    logits = _dot(xn, wr_ref[...], HI)
    lane = lax.broadcasted_iota(I32, logits.shape, 1).astype(F32)
    logits = jnp.where(lane < N_EXPERTS, logits, -jnp.inf)
    m1 = jnp.max(logits, axis=-1, keepdims=True)
    i1 = jnp.min(jnp.where(logits == m1, lane, float(LANE)), axis=-1, keepdims=True)
    rest = jnp.where(lane == i1, -jnp.inf, logits)
    m2 = jnp.max(rest, axis=-1, keepdims=True)
    i2 = jnp.min(jnp.where(rest == m2, lane, float(LANE)), axis=-1, keepdims=True)
    e2 = jnp.exp(m2 - m1)
    den = 1.0 + e2
    hit1 = lane == i1
    hit2 = lane == i2
    member = jnp.where(jnp.logical_or(hit1, hit2), 1.0, 0.0)
    run = run_ref[0:1, :]
    rank = _dot(tri_ref[...], member.astype(BF16)) + run
    r1 = jnp.sum(jnp.where(hit1, rank, 0.0), axis=-1, keepdims=True)
    r2 = jnp.sum(jnp.where(hit2, rank, 0.0), axis=-1, keepdims=True)
    run = run + jnp.sum(member, axis=0, keepdims=True)
    run_ref[0:1, :] = run
    cnt_ref[...] = jnp.broadcast_to(run, cnt_ref.shape)
    rec = jnp.zeros(logits.shape, F32)
    for idx, val in ((_RT_E0, i1), (_RT_E1, i2), (_RT_G0, 1.0 / den), (_RT_G1, e2 / den), (_RT_R0, r1),
                     (_RT_R1, r2)):
        rec = jnp.where(lane == float(idx), val, rec)
    meta_ref[...] = rec


def _router(h2, g, wr, tri):
    T = h2.shape[0]
    tm = tri.shape[0]
    return pl.pallas_call(
        _router_kernel,
        out_shape=[jax.ShapeDtypeStruct((T, LANE), F32), jax.ShapeDtypeStruct((8, LANE), F32)],
        grid=(T // tm,),
        in_specs=[pl.BlockSpec((tm, D_MODEL), lambda i: (i, 0)),
                  pl.BlockSpec((1, D_MODEL), lambda i: (0, 0)),
                  pl.BlockSpec((D_MODEL, LANE), lambda i: (0, 0)),
                  pl.BlockSpec((tm, tm), lambda i: (0, 0))],
        out_specs=[pl.BlockSpec((tm, LANE), lambda i: (i, 0)), pl.BlockSpec((8, LANE), lambda i: (0, 0))],
        scratch_shapes=[pltpu.VMEM((8, LANE), F32)],
        compiler_params=_cparams("arbitrary"),
        name="router",
    )(h2, g, wr, tri)


def _row_copy(src_ref, src_row, dst_ref, dst_row, sem):
    return pltpu.make_async_copy(src_ref.at[pl.ds(src_row, 1)], dst_ref.at[pl.ds(dst_row, 1)], sem)


def _dispatch_kernel(pos_ref, hn_ref, xs_in_ref, xs_ref, buf, sem):
    del xs_in_ref
    tm = hn_ref.shape[0]
    buf[...] = hn_ref[...].astype(F32)

    def issue(j, c):
        for s in range(TOP_K):
            _row_copy(buf, j, xs_ref, pos_ref[0, s * tm + j], sem).start()
        return c

    lax.fori_loop(0, tm, issue, 0, unroll=8)
    for _ in range(TOP_K):
        pltpu.make_async_copy(buf, xs_ref.at[pl.ds(0, tm)], sem).wait()


def _dispatch(pos, hn, n_rows):
    T = hn.shape[0]
    tm = ROUTE_TILE
    xs0 = jnp.zeros((n_rows, D_MODEL), F32)
    return pl.pallas_call(
        _dispatch_kernel,
        out_shape=jax.ShapeDtypeStruct((n_rows, D_MODEL), F32),
        grid=(T // tm,),
        in_specs=[pl.BlockSpec((None, 1, TOP_K * tm), lambda i: (i, 0, 0), memory_space=pltpu.SMEM),
                  pl.BlockSpec((tm, D_MODEL), lambda i: (i, 0)),
                  pl.BlockSpec(memory_space=pl.ANY)],
        out_specs=pl.BlockSpec(memory_space=pl.ANY),
        scratch_shapes=[pltpu.VMEM((tm, D_MODEL), F32), pltpu.SemaphoreType.DMA],
        input_output_aliases={2: 0},
        compiler_params=_cparams("arbitrary"),
        name="moe_dispatch",
    )(pos, hn, xs0)


def _moe_ffn_kernel(te_ref, nu_ref, x_ref, wg_ref, wu_ref, wd_ref, o_ref, xb_ref):
    del te_ref
    i = pl.program_id(0)
    f = pl.program_id(1)

    @pl.when(i < nu_ref[0])
    def _():
        @pl.when(f == 0)
        def _():
            xb_ref[...] = x_ref[...].astype(BF16)
            o_ref[...] = jnp.zeros(o_ref.shape, F32)

        x = xb_ref[...]
        g = _dot(x, wg_ref[...].astype(BF16))
        u = _dot(x, wu_ref[...].astype(BF16))
        act = (g * _sigmoid(g) * u).astype(BF16)
        o_ref[...] += _dot(act, wd_ref[...].astype(BF16))

    @pl.when(jnp.logical_and(i >= nu_ref[0], f == 0))
    def _():
        o_ref[...] = jnp.zeros(o_ref.shape, F32)


def _moe_ffn(tile_expert, n_used, xs, wg, wu, wd):
    n_rows = xs.shape[0]
    tm, tf = MOE_TILE, 256
    nf = D_FF // tf

    def row_map(i, f, te, nu):
        return (jnp.maximum(jnp.minimum(i, nu[0] - 1), 0), 0)

    def out_map(i, f, te, nu):
        return (i, 0)

    def up_map(i, f, te, nu):
        return (te[i], 0, jnp.where(i < nu[0], f, nf - 1))

    def down_map(i, f, te, nu):
        return (te[i], jnp.where(i < nu[0], f, nf - 1), 0)

    return pl.pallas_call(
        _moe_ffn_kernel,
        out_shape=jax.ShapeDtypeStruct((n_rows, D_MODEL), F32),
        grid_spec=pltpu.PrefetchScalarGridSpec(
            num_scalar_prefetch=2,
            grid=(n_rows // tm, nf),
            in_specs=[pl.BlockSpec((tm, D_MODEL), row_map),
                      pl.BlockSpec((None, D_MODEL, tf), up_map),
                      pl.BlockSpec((None, D_MODEL, tf), up_map),
                      pl.BlockSpec((None, tf, D_MODEL), down_map)],
            out_specs=pl.BlockSpec((tm, D_MODEL), out_map),
            scratch_shapes=[pltpu.VMEM((tm, D_MODEL), BF16)]),
        compiler_params=_cparams("arbitrary", "arbitrary"),
        name="moe_ffn",
    )(tile_expert, n_used, xs, wg, wu, wd)


def _combine_kernel(pos_ref, meta_ref, h_ref, yb_ref, o_ref, buf0, buf1, sems):
    tm = h_ref.shape[0]
    bufs = (buf0, buf1)

    def issue(j, c):
        for s in range(TOP_K):
            _row_copy(yb_ref, pos_ref[0, s * tm + j], bufs[s], j, sems.at[s]).start()
        return c

    lax.fori_loop(0, tm, issue, 0, unroll=8)
    meta = meta_ref[...]
    lane = lax.broadcasted_iota(I32, meta.shape, 1)
    g0 = jnp.sum(jnp.where(lane == _RT_G0, meta, 0.0), axis=-1, keepdims=True)
    g1 = jnp.sum(jnp.where(lane == _RT_G1, meta, 0.0), axis=-1, keepdims=True)
    for s in range(TOP_K):
        pltpu.make_async_copy(yb_ref.at[pl.ds(0, tm)], bufs[s], sems.at[s]).wait()
    o_ref[...] = h_ref[...] + (g0 * buf0[...] + g1 * buf1[...])


def _combine(pos, meta, h2, yb):
    T = h2.shape[0]
    tm = ROUTE_TILE
    row = pl.BlockSpec((tm, D_MODEL), lambda i: (i, 0))
    return pl.pallas_call(
        _combine_kernel,
        out_shape=jax.ShapeDtypeStruct((T, D_MODEL), F32),
        grid=(T // tm,),
        in_specs=[pl.BlockSpec((None, 1, TOP_K * tm), lambda i: (i, 0, 0), memory_space=pltpu.SMEM),
                  pl.BlockSpec((tm, LANE), lambda i: (i, 0)), row,
                  pl.BlockSpec(memory_space=pl.ANY)],
        out_specs=row,
        scratch_shapes=[pltpu.VMEM((tm, D_MODEL), F32), pltpu.VMEM((tm, D_MODEL), F32),
                        pltpu.SemaphoreType.DMA((TOP_K,))],
        compiler_params=_cparams("arbitrary"),
        name="moe_combine",
    )(pos, meta, h2, yb)


def _moe(hn, h2, norm_g, router_w, wg, wu, wd, tri):
    T, D = h2.shape
    wr = jnp.zeros((D, LANE), F32).at[:, :N_EXPERTS].set(router_w.astype(F32))
    meta, cnt = _router(h2, norm_g, wr, tri)
    counts = cnt[0, :N_EXPERTS].astype(I32)
    padded = (counts + MOE_TILE - 1) // MOE_TILE * MOE_TILE
    ends = jnp.cumsum(padded)
    starts = ends - padded
    n_rows = TOP_K * T + N_EXPERTS * MOE_TILE
    n_tiles = n_rows // MOE_TILE
    experts = jnp.arange(N_EXPERTS, dtype=I32)

    def slot(e_lane, r_lane):
        e = meta[:, e_lane].astype(I32)
        start = jnp.sum(jnp.where(e[:, None] == experts[None, :], starts[None, :], 0), axis=1)
        return start + meta[:, r_lane].astype(I32)

    nt = T // ROUTE_TILE
    pos = jnp.concatenate([slot(_RT_E0, _RT_R0).reshape(nt, ROUTE_TILE),
                           slot(_RT_E1, _RT_R1).reshape(nt, ROUTE_TILE)], axis=1)[:, None, :]
    tile_start = jnp.arange(n_tiles, dtype=I32) * MOE_TILE
    tile_expert = jnp.minimum(jnp.sum(ends[None, :] <= tile_start[:, None], axis=1), N_EXPERTS - 1).astype(I32)
    n_used = (ends[-1:] // MOE_TILE).astype(I32)
    xs = _dispatch(pos, hn, n_rows)
    yb = _moe_ffn(tile_expert, n_used, xs, wg, wu, wd)
    return _combine(pos, meta, h2, yb)


def _block_diag_ones(n, blk):
    i = np.arange(n)
    return (i[:, None] // blk == i[None, :] // blk).astype(np.float32)


def _consts():
    n4 = 4 * CHUNK
    i = np.arange(n4)
    bd = _block_diag_ones(n4, CHUNK)
    tr, tc = i[:, None] % CHUNK, i[None, :] % CHUNK
    strict = bd * (tr > tc)
    incl = bd * (tr >= tc)
    ltri = np.tril(np.ones((CHUNK, CHUNK), np.float32))
    pair_mean = _block_diag_ones(LANE, HEAD_DIM) / HEAD_DIM
    qmask = (i[:, None] // CHUNK == np.arange(LANE)[None, :] // GLA_KEY_DIM).astype(np.float32)
    amask = (i[:, None] % CHUNK >= np.arange(CHUNK)[None, :]).astype(np.float32)
    out = {k: jnp.asarray(v, F32) for k, v in dict(
        bd=bd, strict=strict, incl=incl, ltri=ltri, pair_mean=pair_mean, qmask=qmask, amask=amask).items()}
    out["route_tri"] = jnp.asarray(np.tril(np.ones((1024, 1024), np.float32), -1), BF16)
    return out


def _pad_rows(w, row0, total):
    return jnp.zeros((total, w.shape[1]), F32).at[row0:row0 + w.shape[0]].set(w.astype(F32))


def _pack_rows(rows, width, n_rows):
    out = jnp.zeros((n_rows, width), F32)
    for i, r in enumerate(rows):
        r = r.reshape(-1).astype(F32)
        out = out.at[i, :r.shape[0]].set(r)
    return out


def kernel(x, rel_bias, mix_norm_g, w_in, w_out, ffn_norm_g, attn_q_norm_g, attn_k_norm_g, conv_w, conv_b, conv_ln_g, conv_ln_b, rwkv_mu, rwkv_w0, rwkv_w2, rwkv_a0, rwkv_a2, rwkv_g2, rwkv_k_k, rwkv_k_a, rwkv_r_k, rwkv_ln_g, rwkv_ln_b, gla_g2, gla_gb, gla_norm_g, ffn_wg, ffn_wu, ffn_wd, moe_router, moe_wg, moe_wu, moe_wd):
    B, S, D = x.shape
    T = B * S
    cs = _consts()
    bias = _attn_bias_tables(rel_bias)
    h = x.reshape(T, D)
    for layer in range(DEPTH):
        w = w_in[layer]
        dq, dk, dv, dg, do = (w[:, 2176:2304], w[:, 2304:2432], w[:, 2432:2688], w[:, 2688:2704], w[:, 2704:2960])
        wp = jnp.concatenate([w[:, :2176], dq, dv, do, dk, dg, jnp.zeros((D, IN_WP - IN_W), w.dtype)],
                             axis=1).astype(BF16)
        z = _inproj(h, mix_norm_g[layer][None, :], wp)
        z3 = z.reshape(B, S, IN_WP)

        qg = jnp.tile(attn_q_norm_g[layer].astype(F32) * (HEAD_DIM ** -0.5), 2)[None, :]
        kg = jnp.tile(attn_k_norm_g[layer].astype(F32), 2)[None, :]
        ya = _attn_mixer(z3, bias, qg, kg, cs["pair_mean"]).reshape(T, GROUP_W)

        conv_pv = _pack_rows([conv_b[layer], conv_ln_g[layer], conv_ln_b[layer]], GROUP_W, 8)
        group_mean = cs["bd"] * (1.0 / HEAD_DIM)
        yb = _conv_mixer(z3, conv_w[layer].astype(F32), conv_pv, group_mean).reshape(T, GROUP_W)

        mu = rwkv_mu[layer]
        rw_pv = _pack_rows([mu[0:256], mu[256:512], mu[512:768], mu[768:896], rwkv_w0[layer], rwkv_a0[layer],
                            rwkv_k_k[layer], rwkv_k_a[layer], rwkv_r_k[layer], rwkv_ln_g[layer],
                            rwkv_ln_b[layer]], GROUP_W, 16)
        yc = _rwkv_mixer(z3, rw_pv, _pad_rows(rwkv_w2[layer], 0, LANE), _pad_rows(rwkv_a2[layer], 32, LANE),
                         _pad_rows(rwkv_g2[layer], 64, LANE), cs["bd"], cs["strict"], cs["incl"],
                         cs["ltri"]).reshape(T, GROUP_W)

        gla_pv = _pack_rows([gla_gb[layer], jnp.tile(gla_norm_g[layer], GROUP_W // HEAD_DIM)], GROUP_W, 8)
        yd = _gla_mixer(z3, _pad_rows(gla_g2[layer], 0, LANE), gla_pv, cs["bd"], cs["qmask"], cs["amask"],
                        cs["ltri"]).reshape(T, GROUP_W)

        norm_g = ffn_norm_g[layer][None, :]
        h, hn = _outproj((ya, yb, yc, yd), w_out[layer].astype(BF16), h, norm_g)

        i = layer // 2
        if layer % 2 == 0:
            h = _ffn(hn, h, ffn_wg[i], ffn_wu[i], ffn_wd[i])
        else:
            h = _moe(hn, h, norm_g, moe_router[i], moe_wg[i], moe_wu[i], moe_wd[i], cs["route_tri"])
    return h.reshape(B, S, D)
```

```python
import functools
import math

import numpy as np
import jax
import jax.numpy as jnp
from jax import lax
from jax.experimental import pallas as pl
from jax.experimental.pallas import tpu as pltpu

F32 = jnp.float32
BF16 = jnp.bfloat16
I32 = jnp.int32
HI = lax.Precision.HIGHEST

D_MODEL = 1024
DEPTH = 2
GROUP_W = 256
NORM_EPS = 1e-6
HEAD_DIM = 64
DILATED_PATTERNS = ((128, 1), (512, 4), (2048, 16))
ATT_BLOCK = 128
N_BUCKETS = 32
REL_MAX_DIST = 2048
CONV_WIDTH = 31
CONV_HALO = 32
CONV_LN_EPS = 1e-5
RWKV_LN_EPS = 64e-5
GLA_KEY_DIM = 32
GLA_TAU = 16.0
CHUNK = 64
D_FF = 2816
N_EXPERTS = 8
TOP_K = 2
IN_W = 2960
IN_WP = 3072
LANE = 128
SUBLANE = 8
VMEM_LIMIT = 48 * 1024 * 1024
MOE_TILE = 512
FFN_ROWS = 512
ROUTE_TILE = 512

OFF_AQ, OFF_AK, OFF_AV = 0, 256, 512
OFF_BU, OFF_BG = 768, 1024
OFF_CR, OFF_CK, OFF_CV, OFF_CL = 1280, 1536, 1792, 2048
OFF_DQ, OFF_DV, OFF_DO, OFF_DK, OFF_DG = 2176, 2304, 2560, 2816, 2944


def _cparams(*sem):
    return pltpu.CompilerParams(dimension_semantics=sem, vmem_limit_bytes=VMEM_LIMIT)


def _dot(a, b, prec=None):
    return jnp.dot(a, b, preferred_element_type=F32, precision=prec)


def _mm(a, b):
    return jnp.dot(a.astype(BF16), b.astype(BF16), preferred_element_type=F32)


def _mm_nt(a, b):
    return lax.dot_general(a.astype(BF16), b.astype(BF16), (((1,), (1,)), ((), ())),
                           preferred_element_type=F32)


def _mm_tn(a, b):
    return lax.dot_general(a.astype(BF16), b.astype(BF16), (((0,), (0,)), ((), ())),
                           preferred_element_type=F32)


def _split3(x):
    x1 = x.astype(BF16)
    r1 = x - x1.astype(F32)
    x2 = r1.astype(BF16)
    x3 = (r1 - x2.astype(F32)).astype(BF16)
    return x1, x2, x3


def _dot_exact_rhs(x, m):
    n = x.shape[0]
    y = _dot(jnp.concatenate(_split3(x), axis=0), m)
    return y[0:n] + y[n:2 * n] + y[2 * n:3 * n]


def _dot_exact_lhs(m, x):
    n = x.shape[1]
    y = _dot(m, jnp.concatenate(_split3(x), axis=1))
    return y[:, 0:n] + y[:, n:2 * n] + y[:, 2 * n:3 * n]


def _each(fn, *lists):
    return [fn(*args) for args in zip(*lists)]


def _sigmoid(x):
    return 1.0 / (1.0 + jnp.exp(-x))


def _softplus(x):
    return jnp.maximum(x, 0.0) + jnp.log(1.0 + jnp.exp(-jnp.abs(x)))


def _inproj_kernel(x_ref, g_ref, w_ref, z_ref, xn_ref):
    @pl.when(pl.program_id(1) == 0)
    def _():
        x = x_ref[...]
        ms = jnp.mean(x * x, axis=-1, keepdims=True)
        xn_ref[...] = (x * lax.rsqrt(ms + NORM_EPS) * g_ref[...]).astype(BF16)

    z_ref[...] = _dot(xn_ref[...], w_ref[...])


def _inproj(h2, g, w):
    T = h2.shape[0]
    tm, tn = 1024, 768
    return pl.pallas_call(
        _inproj_kernel,
        out_shape=jax.ShapeDtypeStruct((T, IN_WP), F32),
        grid=(T // tm, IN_WP // tn),
        in_specs=[pl.BlockSpec((tm, D_MODEL), lambda i, j: (i, 0)),
                  pl.BlockSpec((1, D_MODEL), lambda i, j: (0, 0)),
                  pl.BlockSpec((D_MODEL, tn), lambda i, j: (0, j))],
        out_specs=pl.BlockSpec((tm, tn), lambda i, j: (i, j)),
        scratch_shapes=[pltpu.VMEM((tm, D_MODEL), BF16)],
        compiler_params=_cparams("parallel", "arbitrary"),
        name="inproj",
    )(h2, g, w)


def _t5_bucket(dist):
    max_exact = N_BUCKETS // 2
    n = np.maximum(dist, 0)
    large = max_exact + (np.log(np.maximum(n, 1) / max_exact) / math.log(REL_MAX_DIST / max_exact)
                         * (N_BUCKETS - max_exact)).astype(np.int32)
    large = np.minimum(large, N_BUCKETS - 1)
    return np.where(n < max_exact, n, large).astype(np.int32)


def _bucket_table():
    W = ATT_BLOCK
    delta = np.arange(W)[:, None] + W - np.arange(2 * W)[None, :]
    band = (delta >= 0) & (delta <= W)
    tabs = [np.where(band, _t5_bucket(np.clip(delta, 0, W) * d), -1) for _, d in DILATED_PATTERNS]
    return jnp.asarray(np.stack(tabs), I32)


def _bias_kernel(rb_ref, bk_ref, o_ref):
    n_heads = o_ref.shape[1]
    for di in range(len(DILATED_PATTERNS)):
        bk = bk_ref[di]
        accs = [jnp.full(bk.shape, -jnp.inf, F32) for _ in range(n_heads)]
        for b in range(N_BUCKETS):
            hit = bk == b
            accs = [jnp.where(hit, rb_ref[b, hh], a) for hh, a in enumerate(accs)]
        for hh in range(n_heads):
            o_ref[di, hh] = accs[hh]


def _attn_bias_tables(rel_bias):
    n_heads = rel_bias.shape[1]
    shape = (len(DILATED_PATTERNS), n_heads, ATT_BLOCK, 2 * ATT_BLOCK)
    return pl.pallas_call(
        _bias_kernel,
        out_shape=jax.ShapeDtypeStruct(shape, F32),
        in_specs=[pl.BlockSpec(memory_space=pltpu.SMEM), pl.BlockSpec(memory_space=pltpu.VMEM)],
        out_specs=pl.BlockSpec(memory_space=pltpu.VMEM),
        name="attn_bias",
    )(rel_bias.astype(F32), _bucket_table())


ATT_MERGE_ROWS = 256
ATT_UNROLL = 4


def _attn_kernel(q_ref, k_ref, v_ref, bias_ref, qg_ref, kg_ref, gm_ref, o_ref,
                 qn, kn, qd, kd, vd, od, ld, acc, lse):
    S = q_ref.shape[0]
    rows = 512

    def norm_body(i, c):
        sl = pl.ds(pl.multiple_of(i * rows, rows), rows)
        q = q_ref[sl, :]
        k = k_ref[sl, :]
        gm16 = gm_ref[...].astype(BF16)
        qn[sl, :] = q * lax.rsqrt(_dot_exact_rhs(q * q, gm16) + NORM_EPS) * qg_ref[...]
        kn[sl, :] = k * lax.rsqrt(_dot_exact_rhs(k * k, gm16) + NORM_EPS) * kg_ref[...]
        return c

    lax.fori_loop(0, S // rows, norm_body, 0)

    lane = lax.broadcasted_iota(I32, (ATT_BLOCK, LANE), 1)
    head0 = lane < HEAD_DIM
    col = lax.broadcasted_iota(I32, (ATT_BLOCK, 2 * ATT_BLOCK), 1)

    def run_blocks(di, nb, qs, ks, vs, o_dst, l_dst):
        nu = min(nb, ATT_UNROLL)

        def blk(n):
            return pl.ds(pl.multiple_of(n * ATT_BLOCK, ATT_BLOCK), ATT_BLOCK)

        def body(i, c):
            base = i * nu
            rows = [blk(jnp.maximum(base - 1, 0))] + [blk(base + u) for u in range(nu)]
            kb = [ks[r, :].astype(BF16) for r in rows]
            vb = [vs[r, :].astype(BF16) for r in rows]
            kt = [jnp.concatenate([kb[u], kb[u + 1]], axis=0) for u in range(nu)]
            vt = [jnp.concatenate([vb[u], vb[u + 1]], axis=0) for u in range(nu)]
            qt = [qs[r, :] for r in rows[1:]]
            heads = [(u, hh) for u in range(nu) for hh in range(2)]
            s = [_mm_nt(jnp.where(head0 if hh == 0 else jnp.logical_not(head0), qt[u], 0.0), kt[u])
                 + bias_ref[di, hh] for u, hh in heads]
            no_prev = jnp.logical_and(i == 0, col < ATT_BLOCK)
            s = [jnp.where(no_prev, -jnp.inf, x) if u == 0 else x for (u, hh), x in zip(heads, s)]
            m = [jnp.max(x, axis=-1, keepdims=True) for x in s]
            p = _each(lambda x, mx: jnp.exp(x - mx), s, m)
            den = [jnp.sum(x, axis=-1, keepdims=True) for x in p]
            o = [_mm(x, vt[u]) / dn for (u, hh), x, dn in zip(heads, p, den)]
            ls = _each(lambda mx, dn: mx + jnp.log(dn), m, den)
            for u in range(nu):
                o_dst[rows[u + 1], :] = jnp.where(head0, o[2 * u], o[2 * u + 1])
                l_dst[rows[u + 1], :] = jnp.where(head0, ls[2 * u], ls[2 * u + 1])
            return c

        lax.fori_loop(0, nb // nu, body, 0)

    for di, (_, d) in enumerate(DILATED_PATTERNS):
        L = S // d
        nb = L // ATT_BLOCK
        if d == 1:
            run_blocks(di, nb, qn, kn, v_ref, acc, lse)
            continue
        for r in range(d):
            res = pl.ds(r, L, stride=d)
            qd[0:L, :] = qn[res, :]
            kd[0:L, :] = kn[res, :]
            vd[0:L, :] = v_ref[res, :]
            run_blocks(di, nb, qd, kd, vd, od, ld)
            mr = min(L, ATT_MERGE_ROWS)
            for c in range(L // mr):
                piece = pl.ds(r + c * mr * d, mr, stride=d)
                l0, a0 = lse[piece, :], acc[piece, :]
                l1, a1 = ld[c * mr:(c + 1) * mr, :], od[c * mr:(c + 1) * mr, :]
                m = jnp.maximum(l0, l1)
                e0, e1 = jnp.exp(l0 - m), jnp.exp(l1 - m)
                tot = e0 + e1
                acc[piece, :] = (a0 * e0 + a1 * e1) / tot
                lse[piece, :] = m + jnp.log(tot)

    def out_body(i, c):
        sl = pl.ds(pl.multiple_of(i * rows, rows), rows)
        o_ref[sl, :] = acc[sl, :].astype(BF16)
        return c

    lax.fori_loop(0, S // rows, out_body, 0)


def _attn_mixer(z3, bias, qg, kg, gm):
    B, S, _ = z3.shape
    n_lb = GROUP_W // LANE
    l_max = S // DILATED_PATTERNS[1][1]

    def spec(off):
        return pl.BlockSpec((None, S, LANE), lambda b, p: (b, 0, off // LANE + p))

    vec = pl.BlockSpec((1, LANE), lambda b, p: (0, 0))
    full = pltpu.VMEM((S, LANE), F32)
    part = pltpu.VMEM((l_max, LANE), F32)
    return pl.pallas_call(
        _attn_kernel,
        out_shape=jax.ShapeDtypeStruct((B, S, GROUP_W), BF16),
        grid=(B, n_lb),
        in_specs=[spec(OFF_AQ), spec(OFF_AK), spec(OFF_AV),
                  pl.BlockSpec((len(DILATED_PATTERNS), 2, ATT_BLOCK, 2 * ATT_BLOCK), lambda b, p: (0, p, 0, 0)),
                  vec, vec, pl.BlockSpec((LANE, LANE), lambda b, p: (0, 0))],
        out_specs=pl.BlockSpec((None, S, LANE), lambda b, p: (b, 0, p)),
        scratch_shapes=[full, full, part, part, part, part, part, full, full],
        compiler_params=_cparams("parallel", "parallel"),
        name="attn",
    )(z3, z3, z3, bias, qg, kg, gm)


def _conv_kernel(u_ref, g_ref, cw_ref, pv_ref, gm_ref, o_ref, hbuf, hsh):
    ts = u_ref.shape[0]
    t = pl.program_id(1)

    @pl.when(t == 0)
    def _():
        hbuf[0:CONV_HALO, :] = jnp.zeros((CONV_HALO, GROUP_W), F32)

    @pl.when(t > 0)
    def _():
        hbuf[0:CONV_HALO, :] = hbuf[ts:ts + CONV_HALO, :]

    hbuf[CONV_HALO:CONV_HALO + ts, :] = u_ref[...] * _sigmoid(g_ref[...])
    first = CONV_HALO - (CONV_WIDTH - 1)
    n_shift = hsh.shape[1]
    for ph in range(1, SUBLANE):
        hsh[ph - 1, :, :] = hbuf[ph:ph + n_shift, :]
    rows = 64
    gm16 = gm_ref[...].astype(BF16)
    for c in range(ts // rows):
        acc = jnp.zeros((rows, GROUP_W), F32) + pv_ref[0:1, :]
        for j in range(CONV_WIDTH):
            ph = (first + j) % SUBLANE
            s = first + j - ph + c * rows
            tap = hbuf[s:s + rows, :] if ph == 0 else hsh[ph - 1, s:s + rows, :]
            acc = acc + cw_ref[j:j + 1, :] * tap
        mu = _dot_exact_rhs(acc, gm16)
        xc = acc - mu
        var = _dot_exact_rhs(xc * xc, gm16)
        y = xc * lax.rsqrt(var + CONV_LN_EPS) * pv_ref[1:2, :] + pv_ref[2:3, :]
        o_ref[c * rows:(c + 1) * rows, :] = (y * _sigmoid(y)).astype(BF16)


def _conv_mixer(z3, conv_w, pvec, gm):
    B, S, _ = z3.shape
    ts = 512
    return pl.pallas_call(
        _conv_kernel,
        out_shape=jax.ShapeDtypeStruct((B, S, GROUP_W), BF16),
        grid=(B, S // ts),
        in_specs=[pl.BlockSpec((None, ts, GROUP_W), lambda b, t: (b, t, OFF_BU // GROUP_W)),
                  pl.BlockSpec((None, ts, GROUP_W), lambda b, t: (b, t, OFF_BG // GROUP_W)),
                  pl.BlockSpec((CONV_WIDTH, GROUP_W), lambda b, t: (0, 0)),
                  pl.BlockSpec((8, GROUP_W), lambda b, t: (0, 0)),
                  pl.BlockSpec((GROUP_W, GROUP_W), lambda b, t: (0, 0))],
        out_specs=pl.BlockSpec((None, ts, GROUP_W), lambda b, t: (b, t, 0)),
        scratch_shapes=[pltpu.VMEM((ts + CONV_HALO, GROUP_W), F32),
                        pltpu.VMEM((SUBLANE - 1, ts + CONV_HALO - SUBLANE, GROUP_W), F32)],
        compiler_params=_cparams("parallel", "arbitrary"),
        name="conv_mixer",
    )(z3, z3, conv_w, pvec, gm)


_RW_MU_R, _RW_MU_K, _RW_MU_V, _RW_MU_L, _RW_W0, _RW_A0, _RW_KK, _RW_KA, _RW_RK, _RW_LNG, _RW_LNB = range(11)
SHIFT_PAD = 8
RWKV_GROUP = 4
GLA_GROUP = 4


def _rwkv_kernel(r_ref, k_ref, v_ref, l_ref, pv_ref, w2_ref, a2_ref, g2_ref, bd_ref, st_ref, in_ref, lt_ref,
                 o_ref, rbuf, kbuf, vbuf, lbuf, lw_s, kk_s, be_s, km_s, rr_s, vv_s, gg_s, bo_s, sm_ref):
    ts = r_ref.shape[0]
    t = pl.program_id(1)
    bufs = ((rbuf, r_ref), (kbuf, k_ref), (vbuf, v_ref), (lbuf, l_ref))

    @pl.when(t == 0)
    def _():
        for buf, _ in bufs:
            buf[0:SHIFT_PAD, :] = jnp.zeros((SHIFT_PAD, buf.shape[1]), F32)
        sm_ref[...] = jnp.zeros(sm_ref.shape, F32)

    @pl.when(t > 0)
    def _():
        for buf, _ in bufs:
            buf[0:SHIFT_PAD, :] = buf[ts:ts + SHIFT_PAD, :]

    for buf, ref in bufs:
        buf[SHIFT_PAD:SHIFT_PAD + ts, :] = ref[...]

    def pv(i, w=GROUP_W):
        return pv_ref[i:i + 1, 0:w]

    def shifted(buf, mu):
        cur = buf[SHIFT_PAD:SHIFT_PAD + ts, :]
        prev = buf[SHIFT_PAD - 1:SHIFT_PAD - 1 + ts, :]
        return cur + (prev - cur) * mu

    bd = bd_ref[...]
    bd16 = bd.astype(BF16)
    r = shifted(rbuf, pv(_RW_MU_R))
    k = shifted(kbuf, pv(_RW_MU_K))
    v = shifted(vbuf, pv(_RW_MU_V))
    lo = shifted(lbuf, pv(_RW_MU_L, LANE))
    w_log = -_softplus(-(pv(_RW_W0) + _dot(jnp.tanh(lo), w2_ref[...], HI))) - 0.5
    a = _sigmoid(pv(_RW_A0) + _dot(lo, a2_ref[...], HI))
    kk = k * pv(_RW_KK)
    kk = kk / jnp.maximum(jnp.sqrt(_dot_exact_rhs(kk * kk, bd16)), 1e-12)
    km = k * (1.0 + (a - 1.0) * pv(_RW_KA))
    lw_s[...] = -jnp.exp(w_log)
    kk_s[...] = kk
    be_s[...] = kk * a
    km_s[...] = km
    rr_s[...] = r
    vv_s[...] = v
    gg_s[...] = _dot(_sigmoid(lo), g2_ref[...], HI)
    bo_s[...] = _dot_exact_rhs(r * km * pv(_RW_RK), bd16) * v

    strict = st_ref[...]
    incl = in_ref[...]
    eye = incl - strict
    lt16 = lt_ref[...].astype(BF16)
    gmean16 = (bd * (1.0 / HEAD_DIM)).astype(BF16)
    n_stack = GROUP_W // HEAD_DIM
    n4 = n_stack * CHUNK

    def tile4(x):
        return jnp.concatenate([x] * n_stack, axis=0)

    def fold(x):
        return x[0:CHUNK] + x[CHUNK:2 * CHUNK] + x[2 * CHUNK:3 * CHUNK] + x[3 * CHUNK:4 * CHUNK]

    def group(gi, carry):
        sls = [pl.ds(pl.multiple_of((gi * RWKV_GROUP + j) * CHUNK, CHUNK), CHUNK) for j in range(RWKV_GROUP)]
        lw = [lw_s[s, :] for s in sls]
        cum = _each(lambda x: _dot_exact_lhs(lt16, x), lw)
        last = [c[CHUNK - 1:CHUNK, :] for c in cum]
        e_neg = [jnp.exp(-c) for c in cum]
        e_last = _each(lambda l, c: jnp.exp(l - c), last, cum)
        bec = [be_s[s, :] for s in sls]
        kmc = [km_s[s, :] for s in sls]
        a_bar = _each(lambda s, c, w: tile4(-kk_s[s, :] * jnp.exp(c - w)) * bd, sls, cum, lw)
        r_bar = _each(lambda s, c: tile4(rr_s[s, :] * jnp.exp(c)) * bd, sls, cum)
        v_blk = [tile4(vv_s[s, :]) * bd for s in sls]
        big = _each(lambda ab, rb, be, km_, en: _mm_nt(
            jnp.concatenate([ab, rb], axis=0),
            jnp.concatenate([tile4(be * en), tile4(km_ * en)], axis=0)), a_bar, r_bar, bec, kmc, e_neg)
        nmat = [b[0:n4, 0:n4] * strict for b in big]
        a_ak = [b[0:n4, n4:2 * n4] * strict for b in big]
        a_rb = [b[n4:2 * n4, 0:n4] * incl for b in big]
        a_rk = [b[n4:2 * n4, n4:2 * n4] * incl for b in big]
        tinv = [eye + n for n in nmat]
        npow = nmat
        for _ in range(5):
            npow = _each(lambda n: _mm(n, n), npow)
            tinv = _each(lambda t_, n: t_ + _mm(t_, n), tinv, npow)
        w_blk = _each(_mm, tinv, a_bar)
        u0 = _each(lambda t_, ak, vb: _mm(t_, _mm(ak, vb)), tinv, a_ak, v_blk)
        o0 = _each(lambda rb, u, rk, vb: fold(_mm(rb, u) + _mm(rk, vb)), a_rb, u0, a_rk, v_blk)
        q_eff = _each(lambda rbar, rb, w: fold(rbar + _mm(rb, w)), r_bar, a_rb, w_blk)
        gh = _each(lambda be, el, w, u: _mm_tn(tile4(be * el) * bd, jnp.concatenate([w, u], axis=1)),
                   bec, e_last, w_blk, u0)
        g_mat = _each(lambda l, x: eye * jnp.exp(l) + x[:, 0:n4], last, gh)
        h_mat = _each(lambda x, km_, el, vb: x[:, n4:2 * n4] + _mm_tn(tile4(km_ * el) * bd, vb),
                      gh, kmc, e_last, v_blk)
        sm = sm_ref[...]
        outs = []
        for j in range(RWKV_GROUP):
            outs.append(_mm(q_eff[j], sm) + o0[j])
            sm = _mm(g_mat[j], sm) + h_mat[j]
        sm_ref[...] = sm
        for s, out in zip(sls, outs):
            mu = _dot_exact_rhs(out, gmean16)
            xc = out - mu
            var = _dot_exact_rhs(xc * xc, gmean16)
            y = xc * lax.rsqrt(var + RWKV_LN_EPS) * pv(_RW_LNG) + pv(_RW_LNB)
            o_ref[s, :] = ((y + bo_s[s, :]) * gg_s[s, :]).astype(BF16)
        return carry

    lax.fori_loop(0, ts // (CHUNK * RWKV_GROUP), group, 0)


def _rwkv_mixer(z3, pvec, w2p, a2p, g2p, bd, strict, incl, ltri):
    B, S, _ = z3.shape
    ts = 512

    def seg(off, w):
        return pl.BlockSpec((None, ts, w), lambda b, t: (b, t, off // w))

    def const(shape):
        return pl.BlockSpec(shape, lambda b, t: (0, 0))

    sq = (4 * CHUNK, 4 * CHUNK)
    wide = pltpu.VMEM((ts, GROUP_W), F32)
    return pl.pallas_call(
        _rwkv_kernel,
        out_shape=jax.ShapeDtypeStruct((B, S, GROUP_W), BF16),
        grid=(B, S // ts),
        in_specs=[seg(OFF_CR, GROUP_W), seg(OFF_CK, GROUP_W), seg(OFF_CV, GROUP_W), seg(OFF_CL, LANE),
                  const((16, GROUP_W)), const((LANE, GROUP_W)), const((LANE, GROUP_W)), const((LANE, GROUP_W)),
                  const(sq), const(sq), const(sq), const((CHUNK, CHUNK))],
        out_specs=pl.BlockSpec((None, ts, GROUP_W), lambda b, t: (b, t, 0)),
        scratch_shapes=[pltpu.VMEM((ts + SHIFT_PAD, GROUP_W), F32)] * 3
                       + [pltpu.VMEM((ts + SHIFT_PAD, LANE), F32)]
                       + [wide] * 8 + [pltpu.VMEM(sq, F32)],
        compiler_params=_cparams("parallel", "arbitrary"),
        name="rwkv7",
    )(z3, z3, z3, z3, pvec, w2p, a2p, g2p, bd, strict, incl, ltri)


def _gla_kernel(q_ref, k_ref, v_ref, og_ref, gd_ref, g2_ref, pv_ref, bd_ref, qm_ref, am_ref, lt_ref, o_ref,
                gk_s, st_ref):
    ts = q_ref.shape[0]

    @pl.when(pl.program_id(1) == 0)
    def _():
        st_ref[...] = jnp.zeros(st_ref.shape, F32)

    x = _dot(gd_ref[...], g2_ref[...], HI) + pv_ref[0:1, 0:LANE]
    gk_s[...] = -_softplus(-x) * (1.0 / GLA_TAU)
    bd = bd_ref[...]
    qmask = qm_ref[...]
    n_heads = GROUP_W // HEAD_DIM

    lt16 = lt_ref[...].astype(BF16)
    gmean16 = (bd * (1.0 / HEAD_DIM)).astype(BF16)

    def fold(x):
        return x[0:CHUNK] + x[CHUNK:2 * CHUNK] + x[2 * CHUNK:3 * CHUNK] + x[3 * CHUNK:4 * CHUNK]

    def group(gi, carry):
        sls = [pl.ds(pl.multiple_of((gi * GLA_GROUP + j) * CHUNK, CHUNK), CHUNK) for j in range(GLA_GROUP)]
        b = [_dot_exact_lhs(lt16, gk_s[s, :]) for s in sls]
        last = [x[CHUNK - 1:CHUNK, :] for x in b]
        kc = [k_ref[s, :] for s in sls]
        vc = [v_ref[s, :] for s in sls]
        q_d = _each(lambda s, x: q_ref[s, :] * (GLA_KEY_DIM ** -0.5) * jnp.exp(x), sls, b)
        k_d = _each(lambda k_, x: k_ * jnp.exp(-x), kc, b)
        att = _each(lambda q_, k_: _mm_nt(jnp.concatenate([q_] * n_heads, axis=0) * qmask, k_) * am_ref[...],
                    q_d, k_d)
        o_in = _each(lambda a_, v_: fold(_mm(a_, v_) * bd), att, vc)
        upd = _each(lambda v_, k_, l, x: _mm_tn(v_, k_ * jnp.exp(l - x)) * qmask, vc, kc, last, b)
        st = st_ref[...]
        outs = []
        for j in range(GLA_GROUP):
            outs.append(o_in[j] + _mm_nt(q_d[j], st))
            st = st * jnp.exp(last[j]) + upd[j]
        st_ref[...] = st
        for s, o in zip(sls, outs):
            ms = _dot_exact_rhs(o * o, gmean16)
            og = og_ref[s, :]
            y = o * lax.rsqrt(ms + NORM_EPS) * pv_ref[1:2, :] * (og * _sigmoid(og))
            o_ref[s, :] = y.astype(BF16)
        return carry

    lax.fori_loop(0, ts // (CHUNK * GLA_GROUP), group, 0)


def _gla_mixer(z3, g2p, pvec, bd, qmask, amask, ltri):
    B, S, _ = z3.shape
    ts = 512

    def seg(off, w):
        return pl.BlockSpec((None, ts, w), lambda b, t: (b, t, off // w))

    def const(shape):
        return pl.BlockSpec(shape, lambda b, t: (0, 0))

    return pl.pallas_call(
        _gla_kernel,
        out_shape=jax.ShapeDtypeStruct((B, S, GROUP_W), BF16),
        grid=(B, S // ts),
        in_specs=[seg(OFF_DQ, LANE), seg(OFF_DK, LANE), seg(OFF_DV, GROUP_W), seg(OFF_DO, GROUP_W),
                  seg(OFF_DG, LANE), const((LANE, LANE)), const((8, GROUP_W)), const((GROUP_W, GROUP_W)),
                  const((GROUP_W, LANE)), const((GROUP_W, CHUNK)), const((CHUNK, CHUNK))],
        out_specs=pl.BlockSpec((None, ts, GROUP_W), lambda b, t: (b, t, 0)),
        scratch_shapes=[pltpu.VMEM((ts, LANE), F32), pltpu.VMEM((GROUP_W, LANE), F32)],
        compiler_params=_cparams("parallel", "arbitrary"),
        name="gla",
    )(z3, z3, z3, z3, z3, g2p, pvec, bd, qmask, amask, ltri)


_RT_E0, _RT_E1, _RT_G0, _RT_G1, _RT_R0, _RT_R1 = range(6)


def _route(xn, wr_ref, tri_ref, run_ref, meta_ref, cnt_ref):
    @pl.when(pl.program_id(0) == 0)
    def _():
        run_ref[...] = jnp.zeros(run_ref.shape, F32)

    tm = xn.shape[0]
    x_hi = xn.astype(BF16)
    x_lo = (xn - x_hi.astype(F32)).astype(BF16)
    y = _dot(jnp.concatenate([x_hi, x_lo], axis=0), wr_ref[...])
    logits = (y[0:tm, 0:LANE] + y[0:tm, LANE:2 * LANE]) + (y[tm:2 * tm, 0:LANE] + y[tm:2 * tm, LANE:2 * LANE])
    lane = lax.broadcasted_iota(I32, logits.shape, 1).astype(F32)
    logits = jnp.where(lane < N_EXPERTS, logits, -jnp.inf)
    m1 = jnp.max(logits, axis=-1, keepdims=True)
    i1 = jnp.min(jnp.where(logits == m1, lane, float(LANE)), axis=-1, keepdims=True)
    rest = jnp.where(lane == i1, -jnp.inf, logits)
    m2 = jnp.max(rest, axis=-1, keepdims=True)
    i2 = jnp.min(jnp.where(rest == m2, lane, float(LANE)), axis=-1, keepdims=True)
    e2 = jnp.exp(m2 - m1)
    den = 1.0 + e2
    hit1 = lane == i1
    hit2 = lane == i2
    member = jnp.where(jnp.logical_or(hit1, hit2), 1.0, 0.0)
    run = run_ref[0:1, :]
    rank = _dot(tri_ref[...], member.astype(BF16)) + run
    r1 = jnp.sum(jnp.where(hit1, rank, 0.0), axis=-1, keepdims=True)
    r2 = jnp.sum(jnp.where(hit2, rank, 0.0), axis=-1, keepdims=True)
    run = run + jnp.sum(member, axis=0, keepdims=True)
    run_ref[0:1, :] = run
    cnt_ref[...] = jnp.broadcast_to(run, cnt_ref.shape)
    rec = jnp.zeros(logits.shape, F32)
    for idx, val in ((_RT_E0, i1), (_RT_E1, i2), (_RT_G0, 1.0 / den), (_RT_G1, e2 / den), (_RT_R0, r1),
                     (_RT_R1, r2)):
        rec = jnp.where(lane == float(idx), val, rec)
    meta_ref[...] = rec


def _outproj_kernel(ya, yb, yc, yd, w_ref, h_ref, g_ref, *rest, route):
    if route:
        wr_ref, tri_ref, ho_ref, hn_ref, meta_ref, cnt_ref, run_ref = rest
    else:
        ho_ref, hn_ref = rest
    acc = h_ref[...]
    for i, y in enumerate((ya, yb, yc, yd)):
        acc = acc + _dot(y[...], w_ref[i * GROUP_W:(i + 1) * GROUP_W, :])
    ho_ref[...] = acc
    ms = jnp.mean(acc * acc, axis=-1, keepdims=True)
    xn = acc * lax.rsqrt(ms + NORM_EPS) * g_ref[...]
    hn_ref[...] = xn.astype(BF16)
    if route:
        _route(xn, wr_ref, tri_ref, run_ref, meta_ref, cnt_ref)


def _outproj(ys, w, h2, g, router=None):
    T = h2.shape[0]
    tm = ROUTE_TILE
    yspec = pl.BlockSpec((tm, GROUP_W), lambda i: (i, 0))
    row = pl.BlockSpec((tm, D_MODEL), lambda i: (i, 0))
    in_specs = [yspec] * 4 + [pl.BlockSpec((D_MODEL, D_MODEL), lambda i: (0, 0)), row,
                              pl.BlockSpec((1, D_MODEL), lambda i: (0, 0))]
    out_shape = [jax.ShapeDtypeStruct((T, D_MODEL), F32), jax.ShapeDtypeStruct((T, D_MODEL), BF16)]
    out_specs = [row, row]
    args = (*ys, w, h2, g)
    scratch = []
    if router is not None:
        in_specs += [pl.BlockSpec((D_MODEL, 2 * LANE), lambda i: (0, 0)), pl.BlockSpec((tm, tm), lambda i: (0, 0))]
        out_shape += [jax.ShapeDtypeStruct((T, LANE), F32), jax.ShapeDtypeStruct((SUBLANE, LANE), F32)]
        out_specs += [pl.BlockSpec((tm, LANE), lambda i: (i, 0)), pl.BlockSpec((SUBLANE, LANE), lambda i: (0, 0))]
        args += router
        scratch = [pltpu.VMEM((SUBLANE, LANE), F32)]
    return pl.pallas_call(
        functools.partial(_outproj_kernel, route=router is not None),
        out_shape=out_shape,
        grid=(T // tm,),
        in_specs=in_specs,
        out_specs=out_specs,
        scratch_shapes=scratch,
        compiler_params=_cparams("arbitrary"),
        name="outproj_route" if router is not None else "outproj",
    )(*args)


FFN_SPLIT = 2


def _swiglu_rows(x, wg_ref, wu_ref, wd_ref):
    piece = D_FF // FFN_SPLIT
    out = None
    for s in range(FFN_SPLIT):
        cols = slice(s * piece, (s + 1) * piece)
        g = _dot(x, wg_ref[:, cols])
        u = _dot(x, wu_ref[:, cols])
        y = _dot((g * _sigmoid(g) * u).astype(BF16), wd_ref[cols, :])
        out = y if out is None else out + y
    return out


def _ffn_kernel(x_ref, h_ref, wg_ref, wu_ref, wd_ref, o_ref):
    o_ref[...] = h_ref[...] + _swiglu_rows(x_ref[...], wg_ref, wu_ref, wd_ref)


def _resident(shape, index_map):
    return pl.BlockSpec(shape, index_map, pipeline_mode=pl.Buffered(1))


def _ffn(hn, h2, wg, wu, wd):
    T = hn.shape[0]
    tm = FFN_ROWS
    row = pl.BlockSpec((tm, D_MODEL), lambda i: (i, 0))
    return pl.pallas_call(
        _ffn_kernel,
        out_shape=jax.ShapeDtypeStruct((T, D_MODEL), F32),
        grid=(T // tm,),
        in_specs=[row, row,
                  _resident((D_MODEL, D_FF), lambda i: (0, 0)),
                  _resident((D_MODEL, D_FF), lambda i: (0, 0)),
                  _resident((D_FF, D_MODEL), lambda i: (0, 0))],
        out_specs=row,
        compiler_params=_cparams("parallel"),
        name="ffn",
    )(hn, h2, wg, wu, wd)


def _row_copy(src_ref, src_row, dst_ref, dst_row, sem):
    return pltpu.make_async_copy(src_ref.at[pl.ds(src_row, 1)], dst_ref.at[pl.ds(dst_row, 1)], sem)


def _dispatch_kernel(ps_ref, pl_ref, pos_ref, hn_ref, xs_ref, buf, zrow, sem, zsem):
    tm = hn_ref.shape[0]

    @pl.when(pl.program_id(0) == 0)
    def _():
        zrow[...] = jnp.zeros(zrow.shape, F32)
        for e in range(N_EXPERTS):
            def fill(r, c, e=e):
                _row_copy(zrow, 0, xs_ref, ps_ref[e] + r, zsem).start()
                return c

            lax.fori_loop(0, pl_ref[e], fill, 0)
        for e in range(N_EXPERTS):
            def drain(r, c, e=e):
                _row_copy(zrow, 0, xs_ref, ps_ref[e] + r, zsem).wait()
                return c

            lax.fori_loop(0, pl_ref[e], drain, 0)

    buf[...] = hn_ref[...].astype(F32)

    def issue(j, c):
        for s in range(TOP_K):
            _row_copy(buf, j, xs_ref, pos_ref[0, s * tm + j], sem).start()
        return c

    lax.fori_loop(0, tm, issue, 0, unroll=8)
    for _ in range(TOP_K):
        pltpu.make_async_copy(buf, xs_ref.at[pl.ds(0, tm)], sem).wait()


def _dispatch(pad_start, pad_len, pos, hn, n_rows):
    T = hn.shape[0]
    tm = ROUTE_TILE
    return pl.pallas_call(
        _dispatch_kernel,
        out_shape=jax.ShapeDtypeStruct((n_rows, D_MODEL), F32),
        grid_spec=pltpu.PrefetchScalarGridSpec(
            num_scalar_prefetch=2,
            grid=(T // tm,),
            in_specs=[pl.BlockSpec((None, 1, TOP_K * tm), lambda i, ps, pn: (i, 0, 0), memory_space=pltpu.SMEM),
                      pl.BlockSpec((tm, D_MODEL), lambda i, ps, pn: (i, 0))],
            out_specs=pl.BlockSpec(memory_space=pl.ANY),
            scratch_shapes=[pltpu.VMEM((tm, D_MODEL), F32), pltpu.VMEM((SUBLANE, D_MODEL), F32),
                            pltpu.SemaphoreType.DMA, pltpu.SemaphoreType.DMA]),
        compiler_params=_cparams("arbitrary"),
        name="moe_dispatch",
    )(pad_start, pad_len, pos, hn)


def _moe_ffn_kernel(te_ref, nu_ref, x_ref, wg_ref, wu_ref, wd_ref, o_ref):
    del te_ref
    i = pl.program_id(0)

    @pl.when(i < nu_ref[0])
    def _():
        o_ref[...] = _swiglu_rows(x_ref[...].astype(BF16), wg_ref, wu_ref, wd_ref)

    @pl.when(i >= nu_ref[0])
    def _():
        o_ref[...] = jnp.zeros(o_ref.shape, F32)


def _moe_ffn(tile_expert, n_used, xs, wg, wu, wd):
    n_rows = xs.shape[0]
    tm = MOE_TILE

    def row_map(i, te, nu):
        return (jnp.maximum(jnp.minimum(i, nu[0] - 1), 0), 0)

    def expert_map(i, te, nu):
        return (te[i], 0, 0)

    return pl.pallas_call(
        _moe_ffn_kernel,
        out_shape=jax.ShapeDtypeStruct((n_rows, D_MODEL), F32),
        grid_spec=pltpu.PrefetchScalarGridSpec(
            num_scalar_prefetch=2,
            grid=(n_rows // tm,),
            in_specs=[pl.BlockSpec((tm, D_MODEL), row_map),
                      _resident((None, D_MODEL, D_FF), expert_map),
                      _resident((None, D_MODEL, D_FF), expert_map),
                      _resident((None, D_FF, D_MODEL), expert_map)],
            out_specs=pl.BlockSpec((tm, D_MODEL), lambda i, te, nu: (i, 0))),
        compiler_params=_cparams("arbitrary"),
        name="moe_ffn",
    )(tile_expert, n_used, xs, wg, wu, wd)


def _combine_kernel(pos_ref, meta_ref, h_ref, yb_ref, o_ref, buf0, buf1, sems):
    tm = h_ref.shape[0]
    bufs = (buf0, buf1)

    def issue(j, c):
        for s in range(TOP_K):
            _row_copy(yb_ref, pos_ref[0, s * tm + j], bufs[s], j, sems.at[s]).start()
        return c

    lax.fori_loop(0, tm, issue, 0, unroll=8)
    meta = meta_ref[...]
    lane = lax.broadcasted_iota(I32, meta.shape, 1)
    g0 = jnp.sum(jnp.where(lane == _RT_G0, meta, 0.0), axis=-1, keepdims=True)
    g1 = jnp.sum(jnp.where(lane == _RT_G1, meta, 0.0), axis=-1, keepdims=True)
    for s in range(TOP_K):
        pltpu.make_async_copy(yb_ref.at[pl.ds(0, tm)], bufs[s], sems.at[s]).wait()
    o_ref[...] = h_ref[...] + (g0 * buf0[...] + g1 * buf1[...])


def _combine(pos, meta, h2, yb):
    T = h2.shape[0]
    tm = ROUTE_TILE
    row = pl.BlockSpec((tm, D_MODEL), lambda i: (i, 0))
    return pl.pallas_call(
        _combine_kernel,
        out_shape=jax.ShapeDtypeStruct((T, D_MODEL), F32),
        grid=(T // tm,),
        in_specs=[pl.BlockSpec((None, 1, TOP_K * tm), lambda i: (i, 0, 0), memory_space=pltpu.SMEM),
                  pl.BlockSpec((tm, LANE), lambda i: (i, 0)), row,
                  pl.BlockSpec(memory_space=pl.ANY)],
        out_specs=row,
        scratch_shapes=[pltpu.VMEM((tm, D_MODEL), F32), pltpu.VMEM((tm, D_MODEL), F32),
                        pltpu.SemaphoreType.DMA((TOP_K,))],
        compiler_params=_cparams("arbitrary"),
        name="moe_combine",
    )(pos, meta, h2, yb)


def _moe(hn, h2, meta, cnt, wg, wu, wd):
    T, D = h2.shape
    counts = cnt[0, :N_EXPERTS].astype(I32)
    padded = (counts + MOE_TILE - 1) // MOE_TILE * MOE_TILE
    ends = jnp.cumsum(padded)
    starts = ends - padded
    n_rows = TOP_K * T + N_EXPERTS * MOE_TILE
    n_tiles = n_rows // MOE_TILE
    experts = jnp.arange(N_EXPERTS, dtype=I32)

    def slot(e_lane, r_lane):
        e = meta[:, e_lane].astype(I32)
        start = jnp.sum(jnp.where(e[:, None] == experts[None, :], starts[None, :], 0), axis=1)
        return start + meta[:, r_lane].astype(I32)

    nt = T // ROUTE_TILE
    pos = jnp.concatenate([slot(_RT_E0, _RT_R0).reshape(nt, ROUTE_TILE),
                           slot(_RT_E1, _RT_R1).reshape(nt, ROUTE_TILE)], axis=1)[:, None, :]
    tile_start = jnp.arange(n_tiles, dtype=I32) * MOE_TILE
    tile_expert = jnp.minimum(jnp.sum(ends[None, :] <= tile_start[:, None], axis=1), N_EXPERTS - 1).astype(I32)
    n_used = (ends[-1:] // MOE_TILE).astype(I32)
    pad_start = starts + counts
    pad_end = jnp.where(experts == N_EXPERTS - 1, n_rows, ends)
    xs = _dispatch(pad_start.astype(I32), (pad_end - pad_start).astype(I32), pos, hn, n_rows)
    yb = _moe_ffn(tile_expert, n_used, xs, wg.astype(BF16), wu.astype(BF16), wd.astype(BF16))
    return _combine(pos, meta, h2, yb)


def _block_diag_ones(n, blk):
    i = np.arange(n)
    return (i[:, None] // blk == i[None, :] // blk).astype(np.float32)


def _consts():
    n4 = 4 * CHUNK
    i = np.arange(n4)
    bd = _block_diag_ones(n4, CHUNK)
    tr, tc = i[:, None] % CHUNK, i[None, :] % CHUNK
    strict = bd * (tr > tc)
    incl = bd * (tr >= tc)
    ltri = np.tril(np.ones((CHUNK, CHUNK), np.float32))
    pair_mean = _block_diag_ones(LANE, HEAD_DIM) / HEAD_DIM
    qmask = (i[:, None] // CHUNK == np.arange(LANE)[None, :] // GLA_KEY_DIM).astype(np.float32)
    amask = (i[:, None] % CHUNK >= np.arange(CHUNK)[None, :]).astype(np.float32)
    out = {k: jnp.asarray(v, F32) for k, v in dict(
        bd=bd, strict=strict, incl=incl, ltri=ltri, pair_mean=pair_mean, qmask=qmask, amask=amask).items()}
    out["route_tri"] = jnp.asarray(np.tril(np.ones((ROUTE_TILE, ROUTE_TILE), np.float32), -1), BF16)
    return out


def _pad_rows(w, row0, total):
    return jnp.zeros((total, w.shape[1]), F32).at[row0:row0 + w.shape[0]].set(w.astype(F32))


def _pack_rows(rows, width, n_rows):
    out = jnp.zeros((n_rows, width), F32)
    for i, r in enumerate(rows):
        r = r.reshape(-1).astype(F32)
        out = out.at[i, :r.shape[0]].set(r)
    return out


def kernel(x, rel_bias, mix_norm_g, w_in, w_out, ffn_norm_g, attn_q_norm_g, attn_k_norm_g, conv_w, conv_b, conv_ln_g, conv_ln_b, rwkv_mu, rwkv_w0, rwkv_w2, rwkv_a0, rwkv_a2, rwkv_g2, rwkv_k_k, rwkv_k_a, rwkv_r_k, rwkv_ln_g, rwkv_ln_b, gla_g2, gla_gb, gla_norm_g, ffn_wg, ffn_wu, ffn_wd, moe_router, moe_wg, moe_wu, moe_wd):
    B, S, D = x.shape
    T = B * S
    cs = _consts()
    bias = _attn_bias_tables(rel_bias)
    h = x.reshape(T, D)
    for layer in range(DEPTH):
        w = w_in[layer]
        dq, dk, dv, dg, do = (w[:, 2176:2304], w[:, 2304:2432], w[:, 2432:2688], w[:, 2688:2704], w[:, 2704:2960])
        wp = jnp.concatenate([w[:, :2176], dq, dv, do, dk, dg, jnp.zeros((D, IN_WP - IN_W), w.dtype)],
                             axis=1).astype(BF16)
        z = _inproj(h, mix_norm_g[layer][None, :], wp)
        z3 = z.reshape(B, S, IN_WP)

        qg = jnp.tile(attn_q_norm_g[layer].astype(F32) * (HEAD_DIM ** -0.5), 2)[None, :]
        kg = jnp.tile(attn_k_norm_g[layer].astype(F32), 2)[None, :]
        ya = _attn_mixer(z3, bias, qg, kg, cs["pair_mean"]).reshape(T, GROUP_W)

        conv_pv = _pack_rows([conv_b[layer], conv_ln_g[layer], conv_ln_b[layer]], GROUP_W, 8)
        group_mean = cs["bd"] * (1.0 / HEAD_DIM)
        yb = _conv_mixer(z3, conv_w[layer].astype(F32), conv_pv, group_mean).reshape(T, GROUP_W)

        mu = rwkv_mu[layer]
        rw_pv = _pack_rows([mu[0:256], mu[256:512], mu[512:768], mu[768:896], rwkv_w0[layer], rwkv_a0[layer],
                            rwkv_k_k[layer], rwkv_k_a[layer], rwkv_r_k[layer], rwkv_ln_g[layer],
                            rwkv_ln_b[layer]], GROUP_W, 16)
        yc = _rwkv_mixer(z3, rw_pv, _pad_rows(rwkv_w2[layer], 0, LANE), _pad_rows(rwkv_a2[layer], 32, LANE),
                         _pad_rows(rwkv_g2[layer], 64, LANE), cs["bd"], cs["strict"], cs["incl"],
                         cs["ltri"]).reshape(T, GROUP_W)

        gla_pv = _pack_rows([gla_gb[layer], jnp.tile(gla_norm_g[layer], GROUP_W // HEAD_DIM)], GROUP_W, 8)
        yd = _gla_mixer(z3, _pad_rows(gla_g2[layer], 0, LANE), gla_pv, cs["bd"], cs["qmask"], cs["amask"],
                        cs["ltri"]).reshape(T, GROUP_W)

        norm_g = ffn_norm_g[layer][None, :]
        ys = (ya, yb, yc, yd)
        i = layer // 2
        if layer % 2 == 0:
            h, hn = _outproj(ys, w_out[layer].astype(BF16), h, norm_g)
            h = _ffn(hn, h, ffn_wg[i].astype(BF16), ffn_wu[i].astype(BF16), ffn_wd[i].astype(BF16))
        else:
            wr = jnp.zeros((D, LANE), F32).at[:, :N_EXPERTS].set(moe_router[i].astype(F32))
            wr_hi = wr.astype(BF16)
            wr = jnp.concatenate([wr_hi, (wr - wr_hi.astype(F32)).astype(BF16)], axis=1)
            h, hn, meta, cnt = _outproj(ys, w_out[layer].astype(BF16), h, norm_g, router=(wr, cs["route_tri"]))
            h = _moe(hn, h, meta, cnt, moe_wg[i], moe_wu[i], moe_wd[i])
    return h.reshape(B, S, D)
```

```python
import functools
import math

import numpy as np
import jax
import jax.numpy as jnp
from jax import lax
from jax.experimental import pallas as pl
from jax.experimental.pallas import tpu as pltpu

F32 = jnp.float32
BF16 = jnp.bfloat16
I32 = jnp.int32
HI = lax.Precision.HIGHEST

D_MODEL = 1024
DEPTH = 2
GROUP_W = 256
NORM_EPS = 1e-6
HEAD_DIM = 64
DILATED_PATTERNS = ((128, 1), (512, 4), (2048, 16))
ATT_BLOCK = 128
N_BUCKETS = 32
REL_MAX_DIST = 2048
CONV_WIDTH = 31
CONV_HALO = 32
CONV_LN_EPS = 1e-5
RWKV_LN_EPS = 64e-5
GLA_KEY_DIM = 32
GLA_TAU = 16.0
CHUNK = 64
D_FF = 2816
N_EXPERTS = 8
TOP_K = 2
IN_W = 2960
IN_WP = 3072
LANE = 128
SUBLANE = 8
VMEM_LIMIT = 48 * 1024 * 1024
MOE_TILE = 512
FFN_ROWS = 512
ROUTE_TILE = 512

OFF_AQ, OFF_AK, OFF_AV = 0, 256, 512
OFF_BU, OFF_BG = 768, 1024
OFF_CR, OFF_CK, OFF_CV, OFF_CL = 1280, 1536, 1792, 2048
OFF_DQ, OFF_DV, OFF_DO, OFF_DK, OFF_DG = 2176, 2304, 2560, 2816, 2944


def _cparams(*sem):
    return pltpu.CompilerParams(dimension_semantics=sem, vmem_limit_bytes=VMEM_LIMIT)


def _dot(a, b, prec=None):
    return jnp.dot(a, b, preferred_element_type=F32, precision=prec)


def _mm(a, b):
    return jnp.dot(a.astype(BF16), b.astype(BF16), preferred_element_type=F32)


def _mm_nt(a, b):
    return lax.dot_general(a.astype(BF16), b.astype(BF16), (((1,), (1,)), ((), ())),
                           preferred_element_type=F32)


def _mm_tn(a, b):
    return lax.dot_general(a.astype(BF16), b.astype(BF16), (((0,), (0,)), ((), ())),
                           preferred_element_type=F32)


def _split3(x):
    x1 = x.astype(BF16)
    r1 = x - x1.astype(F32)
    x2 = r1.astype(BF16)
    x3 = (r1 - x2.astype(F32)).astype(BF16)
    return x1, x2, x3


def _dot_exact_rhs(x, m):
    n = x.shape[0]
    y = _dot(jnp.concatenate(_split3(x), axis=0), m)
    return y[0:n] + y[n:2 * n] + y[2 * n:3 * n]


def _dot_exact_lhs(m, x):
    n = x.shape[1]
    y = _dot(m, jnp.concatenate(_split3(x), axis=1))
    return y[:, 0:n] + y[:, n:2 * n] + y[:, 2 * n:3 * n]


def _each(fn, *lists):
    return [fn(*args) for args in zip(*lists)]


def _sigmoid(x):
    return 1.0 / (1.0 + jnp.exp(-x))


def _softplus(x):
    return jnp.maximum(x, 0.0) + jnp.log(1.0 + jnp.exp(-jnp.abs(x)))


def _resident(shape, index_map):
    return pl.BlockSpec(shape, index_map, pipeline_mode=pl.Buffered(1))


def _inproj_kernel(x_ref, g_ref, w_ref, z_ref):
    x = x_ref[...]
    ms = jnp.mean(x * x, axis=-1, keepdims=True)
    xn = (x * lax.rsqrt(ms + NORM_EPS) * g_ref[...]).astype(BF16)
    z_ref[...] = _dot(xn, w_ref[...])


def _inproj(h2, g, w):
    T = h2.shape[0]
    tm = FFN_ROWS
    return pl.pallas_call(
        _inproj_kernel,
        out_shape=jax.ShapeDtypeStruct((T, IN_WP), F32),
        grid=(T // tm,),
        in_specs=[pl.BlockSpec((tm, D_MODEL), lambda i: (i, 0)),
                  pl.BlockSpec((1, D_MODEL), lambda i: (0, 0)),
                  _resident((D_MODEL, IN_WP), lambda i: (0, 0))],
        out_specs=pl.BlockSpec((tm, IN_WP), lambda i: (i, 0)),
        compiler_params=_cparams("parallel"),
        name="inproj",
    )(h2, g, w)


def _t5_bucket(dist):
    max_exact = N_BUCKETS // 2
    n = np.maximum(dist, 0)
    large = max_exact + (np.log(np.maximum(n, 1) / max_exact) / math.log(REL_MAX_DIST / max_exact)
                         * (N_BUCKETS - max_exact)).astype(np.int32)
    large = np.minimum(large, N_BUCKETS - 1)
    return np.where(n < max_exact, n, large).astype(np.int32)


def _bucket_table():
    W = ATT_BLOCK
    delta = np.arange(W)[:, None] + W - np.arange(2 * W)[None, :]
    band = (delta >= 0) & (delta <= W)
    tabs = [np.where(band, _t5_bucket(np.clip(delta, 0, W) * d), -1) for _, d in DILATED_PATTERNS]
    return jnp.asarray(np.stack(tabs), I32)


def _bias_kernel(rb_ref, bk_ref, o_ref):
    n_heads = o_ref.shape[1]
    for di in range(len(DILATED_PATTERNS)):
        bk = bk_ref[di]
        accs = [jnp.full(bk.shape, -jnp.inf, F32) for _ in range(n_heads)]
        for b in range(N_BUCKETS):
            hit = bk == b
            accs = [jnp.where(hit, rb_ref[b, hh], a) for hh, a in enumerate(accs)]
        for hh in range(n_heads):
            o_ref[di, hh] = accs[hh]


def _attn_bias_tables(rel_bias):
    n_heads = rel_bias.shape[1]
    shape = (len(DILATED_PATTERNS), n_heads, ATT_BLOCK, 2 * ATT_BLOCK)
    return pl.pallas_call(
        _bias_kernel,
        out_shape=jax.ShapeDtypeStruct(shape, F32),
        in_specs=[pl.BlockSpec(memory_space=pltpu.SMEM), pl.BlockSpec(memory_space=pltpu.VMEM)],
        out_specs=pl.BlockSpec(memory_space=pltpu.VMEM),
        name="attn_bias",
    )(rel_bias.astype(F32), _bucket_table())


ATT_MERGE_ROWS = 256
ATT_UNROLL = 4


def _attn_kernel(q_ref, k_ref, v_ref, bias_ref, qg_ref, kg_ref, gm_ref, o_ref,
                 qn, kn, qd, kd, vd, od, ld, acc, lse):
    S = q_ref.shape[0]
    rows = 512

    def norm_body(i, c):
        sl = pl.ds(pl.multiple_of(i * rows, rows), rows)
        q = q_ref[sl, :]
        k = k_ref[sl, :]
        gm16 = gm_ref[...].astype(BF16)
        qn[sl, :] = q * lax.rsqrt(_dot_exact_rhs(q * q, gm16) + NORM_EPS) * qg_ref[...]
        kn[sl, :] = k * lax.rsqrt(_dot_exact_rhs(k * k, gm16) + NORM_EPS) * kg_ref[...]
        return c

    lax.fori_loop(0, S // rows, norm_body, 0)

    lane = lax.broadcasted_iota(I32, (ATT_BLOCK, LANE), 1)
    head0 = lane < HEAD_DIM
    col = lax.broadcasted_iota(I32, (ATT_BLOCK, 2 * ATT_BLOCK), 1)

    def run_blocks(di, nb, qs, ks, vs, o_dst, l_dst):
        nu = min(nb, ATT_UNROLL)

        def blk(n):
            return pl.ds(pl.multiple_of(n * ATT_BLOCK, ATT_BLOCK), ATT_BLOCK)

        def body(i, c):
            base = i * nu
            rows = [blk(jnp.maximum(base - 1, 0))] + [blk(base + u) for u in range(nu)]
            kb = [ks[r, :].astype(BF16) for r in rows]
            vb = [vs[r, :].astype(BF16) for r in rows]
            kt = [jnp.concatenate([kb[u], kb[u + 1]], axis=0) for u in range(nu)]
            vt = [jnp.concatenate([vb[u], vb[u + 1]], axis=0) for u in range(nu)]
            qt = [qs[r, :] for r in rows[1:]]
            heads = [(u, hh) for u in range(nu) for hh in range(2)]
            s = [_mm_nt(jnp.where(head0 if hh == 0 else jnp.logical_not(head0), qt[u], 0.0), kt[u])
                 + bias_ref[di, hh] for u, hh in heads]
            no_prev = jnp.logical_and(i == 0, col < ATT_BLOCK)
            s = [jnp.where(no_prev, -jnp.inf, x) if u == 0 else x for (u, hh), x in zip(heads, s)]
            m = [jnp.max(x, axis=-1, keepdims=True) for x in s]
            p = _each(lambda x, mx: jnp.exp(x - mx), s, m)
            den = [jnp.sum(x, axis=-1, keepdims=True) for x in p]
            o = [_mm(x, vt[u]) / dn for (u, hh), x, dn in zip(heads, p, den)]
            ls = _each(lambda mx, dn: mx + jnp.log(dn), m, den)
            for u in range(nu):
                o_dst[rows[u + 1], :] = jnp.where(head0, o[2 * u], o[2 * u + 1])
                l_dst[rows[u + 1], :] = jnp.where(head0, ls[2 * u], ls[2 * u + 1])
            return c

        lax.fori_loop(0, nb // nu, body, 0)

    for di, (_, d) in enumerate(DILATED_PATTERNS):
        L = S // d
        nb = L // ATT_BLOCK
        if d == 1:
            run_blocks(di, nb, qn, kn, v_ref, acc, lse)
            continue
        for r in range(d):
            res = pl.ds(r, L, stride=d)
            qd[0:L, :] = qn[res, :]
            kd[0:L, :] = kn[res, :]
            vd[0:L, :] = v_ref[res, :]
            run_blocks(di, nb, qd, kd, vd, od, ld)
            mr = min(L, ATT_MERGE_ROWS)
            for c in range(L // mr):
                piece = pl.ds(r + c * mr * d, mr, stride=d)
                l0, a0 = lse[piece, :], acc[piece, :]
                l1, a1 = ld[c * mr:(c + 1) * mr, :], od[c * mr:(c + 1) * mr, :]
                m = jnp.maximum(l0, l1)
                e0, e1 = jnp.exp(l0 - m), jnp.exp(l1 - m)
                tot = e0 + e1
                acc[piece, :] = (a0 * e0 + a1 * e1) / tot
                lse[piece, :] = m + jnp.log(tot)

    def out_body(i, c):
        sl = pl.ds(pl.multiple_of(i * rows, rows), rows)
        o_ref[sl, :] = acc[sl, :].astype(BF16)
        return c

    lax.fori_loop(0, S // rows, out_body, 0)


def _attn_mixer(z3, bias, qg, kg, gm):
    B, S, _ = z3.shape
    n_lb = GROUP_W // LANE
    l_max = S // DILATED_PATTERNS[1][1]

    def spec(off):
        return pl.BlockSpec((None, S, LANE), lambda b, p: (b, 0, off // LANE + p))

    vec = pl.BlockSpec((1, LANE), lambda b, p: (0, 0))
    full = pltpu.VMEM((S, LANE), F32)
    part = pltpu.VMEM((l_max, LANE), F32)
    return pl.pallas_call(
        _attn_kernel,
        out_shape=jax.ShapeDtypeStruct((B, S, GROUP_W), BF16),
        grid=(B, n_lb),
        in_specs=[spec(OFF_AQ), spec(OFF_AK), spec(OFF_AV),
                  pl.BlockSpec((len(DILATED_PATTERNS), 2, ATT_BLOCK, 2 * ATT_BLOCK), lambda b, p: (0, p, 0, 0)),
                  vec, vec, pl.BlockSpec((LANE, LANE), lambda b, p: (0, 0))],
        out_specs=pl.BlockSpec((None, S, LANE), lambda b, p: (b, 0, p)),
        scratch_shapes=[full, full, part, part, part, part, part, full, full],
        compiler_params=_cparams("parallel", "parallel"),
        name="attn",
    )(z3, z3, z3, bias, qg, kg, gm)


def _conv_kernel(u_ref, g_ref, cw_ref, pv_ref, gm_ref, o_ref, hbuf, hsh):
    ts = u_ref.shape[0]
    t = pl.program_id(1)

    @pl.when(t == 0)
    def _():
        hbuf[0:CONV_HALO, :] = jnp.zeros((CONV_HALO, GROUP_W), F32)

    @pl.when(t > 0)
    def _():
        hbuf[0:CONV_HALO, :] = hbuf[ts:ts + CONV_HALO, :]

    hbuf[CONV_HALO:CONV_HALO + ts, :] = u_ref[...] * _sigmoid(g_ref[...])
    first = CONV_HALO - (CONV_WIDTH - 1)
    n_shift = hsh.shape[1]
    for ph in range(1, SUBLANE):
        hsh[ph - 1, :, :] = hbuf[ph:ph + n_shift, :]
    rows = 64
    gm16 = gm_ref[...].astype(BF16)
    for c in range(ts // rows):
        acc = jnp.zeros((rows, GROUP_W), F32) + pv_ref[0:1, :]
        for j in range(CONV_WIDTH):
            ph = (first + j) % SUBLANE
            s = first + j - ph + c * rows
            tap = hbuf[s:s + rows, :] if ph == 0 else hsh[ph - 1, s:s + rows, :]
            acc = acc + cw_ref[j:j + 1, :] * tap
        mu = _dot_exact_rhs(acc, gm16)
        xc = acc - mu
        var = _dot_exact_rhs(xc * xc, gm16)
        y = xc * lax.rsqrt(var + CONV_LN_EPS) * pv_ref[1:2, :] + pv_ref[2:3, :]
        o_ref[c * rows:(c + 1) * rows, :] = (y * _sigmoid(y)).astype(BF16)


def _conv_mixer(z3, conv_w, pvec, gm):
    B, S, _ = z3.shape
    ts = 512
    return pl.pallas_call(
        _conv_kernel,
        out_shape=jax.ShapeDtypeStruct((B, S, GROUP_W), BF16),
        grid=(B, S // ts),
        in_specs=[pl.BlockSpec((None, ts, GROUP_W), lambda b, t: (b, t, OFF_BU // GROUP_W)),
                  pl.BlockSpec((None, ts, GROUP_W), lambda b, t: (b, t, OFF_BG // GROUP_W)),
                  pl.BlockSpec((CONV_WIDTH, GROUP_W), lambda b, t: (0, 0)),
                  pl.BlockSpec((8, GROUP_W), lambda b, t: (0, 0)),
                  pl.BlockSpec((GROUP_W, GROUP_W), lambda b, t: (0, 0))],
        out_specs=pl.BlockSpec((None, ts, GROUP_W), lambda b, t: (b, t, 0)),
        scratch_shapes=[pltpu.VMEM((ts + CONV_HALO, GROUP_W), F32),
                        pltpu.VMEM((SUBLANE - 1, ts + CONV_HALO - SUBLANE, GROUP_W), F32)],
        compiler_params=_cparams("parallel", "arbitrary"),
        name="conv_mixer",
    )(z3, z3, conv_w, pvec, gm)


_RW_MU_R, _RW_MU_K, _RW_MU_V, _RW_MU_L, _RW_W0, _RW_A0, _RW_KK, _RW_KA, _RW_RK, _RW_LNG, _RW_LNB = range(11)
SHIFT_PAD = 8
RWKV_GROUP = 4
GLA_GROUP = 4


def _rwkv_kernel(r_ref, k_ref, v_ref, l_ref, pv_ref, w2_ref, a2_ref, g2_ref, bd_ref, st_ref, in_ref, lt_ref,
                 o_ref, rbuf, kbuf, vbuf, lbuf, lw_s, kk_s, be_s, km_s, rr_s, vv_s, gg_s, bo_s, sm_ref):
    ts = r_ref.shape[0]
    t = pl.program_id(1)
    bufs = ((rbuf, r_ref), (kbuf, k_ref), (vbuf, v_ref), (lbuf, l_ref))

    @pl.when(t == 0)
    def _():
        for buf, _ in bufs:
            buf[0:SHIFT_PAD, :] = jnp.zeros((SHIFT_PAD, buf.shape[1]), F32)
        sm_ref[...] = jnp.zeros(sm_ref.shape, F32)

    @pl.when(t > 0)
    def _():
        for buf, _ in bufs:
            buf[0:SHIFT_PAD, :] = buf[ts:ts + SHIFT_PAD, :]

    for buf, ref in bufs:
        buf[SHIFT_PAD:SHIFT_PAD + ts, :] = ref[...]

    def pv(i, w=GROUP_W):
        return pv_ref[i:i + 1, 0:w]

    def shifted(buf, mu):
        cur = buf[SHIFT_PAD:SHIFT_PAD + ts, :]
        prev = buf[SHIFT_PAD - 1:SHIFT_PAD - 1 + ts, :]
        return cur + (prev - cur) * mu

    bd = bd_ref[...]
    bd16 = bd.astype(BF16)
    r = shifted(rbuf, pv(_RW_MU_R))
    k = shifted(kbuf, pv(_RW_MU_K))
    v = shifted(vbuf, pv(_RW_MU_V))
    lo = shifted(lbuf, pv(_RW_MU_L, LANE))
    w_log = -_softplus(-(pv(_RW_W0) + _dot(jnp.tanh(lo), w2_ref[...], HI))) - 0.5
    a = _sigmoid(pv(_RW_A0) + _dot(lo, a2_ref[...], HI))
    kk = k * pv(_RW_KK)
    kk = kk / jnp.maximum(jnp.sqrt(_dot_exact_rhs(kk * kk, bd16)), 1e-12)
    km = k * (1.0 + (a - 1.0) * pv(_RW_KA))
    lw_s[...] = -jnp.exp(w_log)
    kk_s[...] = kk
    be_s[...] = kk * a
    km_s[...] = km
    rr_s[...] = r
    vv_s[...] = v
    gg_s[...] = _dot(_sigmoid(lo), g2_ref[...], HI)
    bo_s[...] = _dot_exact_rhs(r * km * pv(_RW_RK), bd16) * v

    n2 = LANE // HEAD_DIM * CHUNK
    bd2 = bd[0:n2, 0:LANE]
    strict = st_ref[0:n2, 0:n2]
    incl = in_ref[0:n2, 0:n2]
    eye = incl - strict
    lt16 = lt_ref[...].astype(BF16)
    gmean16 = (bd2 * (1.0 / HEAD_DIM)).astype(BF16)
    n_pairs = GROUP_W // LANE

    def tile2(x):
        return jnp.concatenate([x] * (LANE // HEAD_DIM), axis=0)

    def fold(x):
        return x[0:CHUNK] + x[CHUNK:2 * CHUNK]

    def group(gi, carry):
        rows = [pl.ds(pl.multiple_of((gi * RWKV_GROUP + j) * CHUNK, CHUNK), CHUNK) for j in range(RWKV_GROUP)]
        items = [(s, slice(p * LANE, (p + 1) * LANE)) for s in rows for p in range(n_pairs)]
        lw = [lw_s[s, ln] for s, ln in items]
        cum = _each(lambda x: _dot_exact_lhs(lt16, x), lw)
        last = [c[CHUNK - 1:CHUNK, :] for c in cum]
        e_neg = [jnp.exp(-c) for c in cum]
        e_last = _each(lambda l, c: jnp.exp(l - c), last, cum)
        bec = [be_s[s, ln] for s, ln in items]
        kmc = [km_s[s, ln] for s, ln in items]
        a_bar = _each(lambda it, c, w: tile2(-kk_s[it[0], it[1]] * jnp.exp(c - w)) * bd2, items, cum, lw)
        r_bar = _each(lambda it, c: tile2(rr_s[it[0], it[1]] * jnp.exp(c)) * bd2, items, cum)
        v_blk = [tile2(vv_s[s, ln]) * bd2 for s, ln in items]
        big = _each(lambda ab, rb, be, km_, en: _mm_nt(
            jnp.concatenate([ab, rb], axis=0),
            jnp.concatenate([tile2(be * en), tile2(km_ * en)], axis=0)), a_bar, r_bar, bec, kmc, e_neg)
        nmat = [b[0:n2, 0:n2] * strict for b in big]
        a_ak = [b[0:n2, n2:2 * n2] * strict for b in big]
        a_rb = [b[n2:2 * n2, 0:n2] * incl for b in big]
        a_rk = [b[n2:2 * n2, n2:2 * n2] * incl for b in big]
        tinv = [eye + n for n in nmat]
        npow = nmat
        for _ in range(5):
            npow = _each(lambda n: _mm(n, n), npow)
            tinv = _each(lambda t_, n: t_ + _mm(t_, n), tinv, npow)
        w_blk = _each(_mm, tinv, a_bar)
        u0 = _each(lambda t_, ak, vb: _mm(t_, _mm(ak, vb)), tinv, a_ak, v_blk)
        o0 = _each(lambda rb, u, rk, vb: fold(_mm(rb, u) + _mm(rk, vb)), a_rb, u0, a_rk, v_blk)
        q_eff = _each(lambda rbar, rb, w: fold(rbar + _mm(rb, w)), r_bar, a_rb, w_blk)
        gh = _each(lambda be, el, w, u: _mm_tn(tile2(be * el) * bd2, jnp.concatenate([w, u], axis=1)),
                   bec, e_last, w_blk, u0)
        g_mat = _each(lambda l, x: eye * jnp.exp(l) + x[:, 0:LANE], last, gh)
        h_mat = _each(lambda x, km_, el, vb: x[:, LANE:2 * LANE] + _mm_tn(tile2(km_ * el) * bd2, vb),
                      gh, kmc, e_last, v_blk)
        outs = [None] * len(items)
        for p in range(n_pairs):
            sm = sm_ref[p]
            for j in range(RWKV_GROUP):
                i = j * n_pairs + p
                outs[i] = _mm(q_eff[i], sm) + o0[i]
                sm = _mm(g_mat[i], sm) + h_mat[i]
            sm_ref[p] = sm
        for (s, ln), out in zip(items, outs):
            mu = _dot_exact_rhs(out, gmean16)
            xc = out - mu
            var = _dot_exact_rhs(xc * xc, gmean16)
            y = xc * lax.rsqrt(var + RWKV_LN_EPS) * pv_ref[_RW_LNG:_RW_LNG + 1, ln] + pv_ref[_RW_LNB:_RW_LNB + 1, ln]
            o_ref[s, ln] = ((y + bo_s[s, ln]) * gg_s[s, ln]).astype(BF16)
        return carry

    lax.fori_loop(0, ts // (CHUNK * RWKV_GROUP), group, 0)


def _rwkv_mixer(z3, pvec, w2p, a2p, g2p, bd, strict, incl, ltri):
    B, S, _ = z3.shape
    ts = 512

    def seg(off, w):
        return pl.BlockSpec((None, ts, w), lambda b, t: (b, t, off // w))

    def const(shape):
        return pl.BlockSpec(shape, lambda b, t: (0, 0))

    sq = (4 * CHUNK, 4 * CHUNK)
    wide = pltpu.VMEM((ts, GROUP_W), F32)
    return pl.pallas_call(
        _rwkv_kernel,
        out_shape=jax.ShapeDtypeStruct((B, S, GROUP_W), BF16),
        grid=(B, S // ts),
        in_specs=[seg(OFF_CR, GROUP_W), seg(OFF_CK, GROUP_W), seg(OFF_CV, GROUP_W), seg(OFF_CL, LANE),
                  const((16, GROUP_W)), const((LANE, GROUP_W)), const((LANE, GROUP_W)), const((LANE, GROUP_W)),
                  const(sq), const(sq), const(sq), const((CHUNK, CHUNK))],
        out_specs=pl.BlockSpec((None, ts, GROUP_W), lambda b, t: (b, t, 0)),
        scratch_shapes=[pltpu.VMEM((ts + SHIFT_PAD, GROUP_W), F32)] * 3
                       + [pltpu.VMEM((ts + SHIFT_PAD, LANE), F32)]
                       + [wide] * 8 + [pltpu.VMEM((GROUP_W // LANE, LANE // HEAD_DIM * CHUNK, LANE), F32)],
        compiler_params=_cparams("parallel", "arbitrary"),
        name="rwkv7",
    )(z3, z3, z3, z3, pvec, w2p, a2p, g2p, bd, strict, incl, ltri)


def _gla_kernel(q_ref, k_ref, v_ref, og_ref, gd_ref, g2_ref, pv_ref, bd_ref, qm_ref, am_ref, lt_ref, o_ref,
                gk_s, st_ref):
    ts = q_ref.shape[0]

    @pl.when(pl.program_id(1) == 0)
    def _():
        st_ref[...] = jnp.zeros(st_ref.shape, F32)

    x = _dot(gd_ref[...], g2_ref[...], HI) + pv_ref[0:1, 0:LANE]
    gk_s[...] = -_softplus(-x) * (1.0 / GLA_TAU)
    bd = bd_ref[...]
    qmask = qm_ref[...]
    n_heads = GROUP_W // HEAD_DIM

    lt16 = lt_ref[...].astype(BF16)
    gmean16 = (bd * (1.0 / HEAD_DIM)).astype(BF16)

    def fold(x):
        return x[0:CHUNK] + x[CHUNK:2 * CHUNK] + x[2 * CHUNK:3 * CHUNK] + x[3 * CHUNK:4 * CHUNK]

    def group(gi, carry):
        sls = [pl.ds(pl.multiple_of((gi * GLA_GROUP + j) * CHUNK, CHUNK), CHUNK) for j in range(GLA_GROUP)]
        b = [_dot_exact_lhs(lt16, gk_s[s, :]) for s in sls]
        last = [x[CHUNK - 1:CHUNK, :] for x in b]
        kc = [k_ref[s, :] for s in sls]
        vc = [v_ref[s, :] for s in sls]
        q_d = _each(lambda s, x: q_ref[s, :] * (GLA_KEY_DIM ** -0.5) * jnp.exp(x), sls, b)
        k_d = _each(lambda k_, x: k_ * jnp.exp(-x), kc, b)
        att = _each(lambda q_, k_: _mm_nt(jnp.concatenate([q_] * n_heads, axis=0) * qmask, k_) * am_ref[...],
                    q_d, k_d)
        o_in = _each(lambda a_, v_: fold(_mm(a_, v_) * bd), att, vc)
        upd = _each(lambda v_, k_, l, x: _mm_tn(v_, k_ * jnp.exp(l - x)) * qmask, vc, kc, last, b)
        st = st_ref[...]
        outs = []
        for j in range(GLA_GROUP):
            outs.append(o_in[j] + _mm_nt(q_d[j], st))
            st = st * jnp.exp(last[j]) + upd[j]
        st_ref[...] = st
        for s, o in zip(sls, outs):
            ms = _dot_exact_rhs(o * o, gmean16)
            og = og_ref[s, :]
            y = o * lax.rsqrt(ms + NORM_EPS) * pv_ref[1:2, :] * (og * _sigmoid(og))
            o_ref[s, :] = y.astype(BF16)
        return carry

    lax.fori_loop(0, ts // (CHUNK * GLA_GROUP), group, 0)


def _gla_mixer(z3, g2p, pvec, bd, qmask, amask, ltri):
    B, S, _ = z3.shape
    ts = 512

    def seg(off, w):
        return pl.BlockSpec((None, ts, w), lambda b, t: (b, t, off // w))

    def const(shape):
        return pl.BlockSpec(shape, lambda b, t: (0, 0))

    return pl.pallas_call(
        _gla_kernel,
        out_shape=jax.ShapeDtypeStruct((B, S, GROUP_W), BF16),
        grid=(B, S // ts),
        in_specs=[seg(OFF_DQ, LANE), seg(OFF_DK, LANE), seg(OFF_DV, GROUP_W), seg(OFF_DO, GROUP_W),
                  seg(OFF_DG, LANE), const((LANE, LANE)), const((8, GROUP_W)), const((GROUP_W, GROUP_W)),
                  const((GROUP_W, LANE)), const((GROUP_W, CHUNK)), const((CHUNK, CHUNK))],
        out_specs=pl.BlockSpec((None, ts, GROUP_W), lambda b, t: (b, t, 0)),
        scratch_shapes=[pltpu.VMEM((ts, LANE), F32), pltpu.VMEM((GROUP_W, LANE), F32)],
        compiler_params=_cparams("parallel", "arbitrary"),
        name="gla",
    )(z3, z3, z3, z3, z3, g2p, pvec, bd, qmask, amask, ltri)


_RT_E0, _RT_E1, _RT_G0, _RT_G1, _RT_R0, _RT_R1 = range(6)


def _route(xn, wr_ref, tri_ref, run_ref, meta_ref, cnt_ref):
    @pl.when(pl.program_id(0) == 0)
    def _():
        run_ref[...] = jnp.zeros(run_ref.shape, F32)

    tm = xn.shape[0]
    x_hi = xn.astype(BF16)
    x_lo = (xn - x_hi.astype(F32)).astype(BF16)
    y = _dot(jnp.concatenate([x_hi, x_lo], axis=0), wr_ref[...])
    logits = (y[0:tm, 0:LANE] + y[0:tm, LANE:2 * LANE]) + (y[tm:2 * tm, 0:LANE] + y[tm:2 * tm, LANE:2 * LANE])
    lane = lax.broadcasted_iota(I32, logits.shape, 1).astype(F32)
    logits = jnp.where(lane < N_EXPERTS, logits, -jnp.inf)
    m1 = jnp.max(logits, axis=-1, keepdims=True)
    i1 = jnp.min(jnp.where(logits == m1, lane, float(LANE)), axis=-1, keepdims=True)
    rest = jnp.where(lane == i1, -jnp.inf, logits)
    m2 = jnp.max(rest, axis=-1, keepdims=True)
    i2 = jnp.min(jnp.where(rest == m2, lane, float(LANE)), axis=-1, keepdims=True)
    e2 = jnp.exp(m2 - m1)
    den = 1.0 + e2
    hit1 = lane == i1
    hit2 = lane == i2
    member = jnp.where(jnp.logical_or(hit1, hit2), 1.0, 0.0)
    run = run_ref[0:1, :]
    rank = _dot(tri_ref[...], member.astype(BF16)) + run
    r1 = jnp.sum(jnp.where(hit1, rank, 0.0), axis=-1, keepdims=True)
    r2 = jnp.sum(jnp.where(hit2, rank, 0.0), axis=-1, keepdims=True)
    run = run + jnp.sum(member, axis=0, keepdims=True)
    run_ref[0:1, :] = run
    cnt_ref[...] = jnp.broadcast_to(run, cnt_ref.shape)
    rec = jnp.zeros(logits.shape, F32)
    for idx, val in ((_RT_E0, i1), (_RT_E1, i2), (_RT_G0, 1.0 / den), (_RT_G1, e2 / den), (_RT_R0, r1),
                     (_RT_R1, r2)):
        rec = jnp.where(lane == float(idx), val, rec)
    meta_ref[...] = rec


def _outproj_kernel(ya, yb, yc, yd, w_ref, h_ref, g_ref, *rest, route):
    if route:
        wr_ref, tri_ref, ho_ref, hn_ref, meta_ref, cnt_ref, run_ref = rest
    else:
        ho_ref, hn_ref = rest
    acc = h_ref[...]
    for i, y in enumerate((ya, yb, yc, yd)):
        acc = acc + _dot(y[...], w_ref[i * GROUP_W:(i + 1) * GROUP_W, :])
    ho_ref[...] = acc
    ms = jnp.mean(acc * acc, axis=-1, keepdims=True)
    xn = acc * lax.rsqrt(ms + NORM_EPS) * g_ref[...]
    hn_ref[...] = xn.astype(BF16)
    if route:
        _route(xn, wr_ref, tri_ref, run_ref, meta_ref, cnt_ref)


def _outproj(ys, w, h2, g, router=None):
    T = h2.shape[0]
    tm = ROUTE_TILE
    yspec = pl.BlockSpec((tm, GROUP_W), lambda i: (i, 0))
    row = pl.BlockSpec((tm, D_MODEL), lambda i: (i, 0))
    in_specs = [yspec] * 4 + [pl.BlockSpec((D_MODEL, D_MODEL), lambda i: (0, 0)), row,
                              pl.BlockSpec((1, D_MODEL), lambda i: (0, 0))]
    out_shape = [jax.ShapeDtypeStruct((T, D_MODEL), F32), jax.ShapeDtypeStruct((T, D_MODEL), BF16)]
    out_specs = [row, row]
    args = (*ys, w, h2, g)
    scratch = []
    if router is not None:
        in_specs += [pl.BlockSpec((D_MODEL, 2 * LANE), lambda i: (0, 0)), pl.BlockSpec((tm, tm), lambda i: (0, 0))]
        out_shape += [jax.ShapeDtypeStruct((T, LANE), F32), jax.ShapeDtypeStruct((SUBLANE, LANE), F32)]
        out_specs += [pl.BlockSpec((tm, LANE), lambda i: (i, 0)), pl.BlockSpec((SUBLANE, LANE), lambda i: (0, 0))]
        args += router
        scratch = [pltpu.VMEM((SUBLANE, LANE), F32)]
    return pl.pallas_call(
        functools.partial(_outproj_kernel, route=router is not None),
        out_shape=out_shape,
        grid=(T // tm,),
        in_specs=in_specs,
        out_specs=out_specs,
        scratch_shapes=scratch,
        compiler_params=_cparams("arbitrary"),
        name="outproj_route" if router is not None else "outproj",
    )(*args)


FFN_SPLIT = 2


def _swiglu_rows(x, wg_ref, wu_ref, wd_ref):
    piece = D_FF // FFN_SPLIT
    out = None
    for s in range(FFN_SPLIT):
        cols = slice(s * piece, (s + 1) * piece)
        g = _dot(x, wg_ref[:, cols])
        u = _dot(x, wu_ref[:, cols])
        y = _dot((g * _sigmoid(g) * u).astype(BF16), wd_ref[cols, :])
        out = y if out is None else out + y
    return out


def _ffn_kernel(x_ref, h_ref, wg_ref, wu_ref, wd_ref, o_ref):
    o_ref[...] = h_ref[...] + _swiglu_rows(x_ref[...], wg_ref, wu_ref, wd_ref)


def _ffn(hn, h2, wg, wu, wd):
    T = hn.shape[0]
    tm = FFN_ROWS
    row = pl.BlockSpec((tm, D_MODEL), lambda i: (i, 0))
    return pl.pallas_call(
        _ffn_kernel,
        out_shape=jax.ShapeDtypeStruct((T, D_MODEL), F32),
        grid=(T // tm,),
        in_specs=[row, row,
                  _resident((D_MODEL, D_FF), lambda i: (0, 0)),
                  _resident((D_MODEL, D_FF), lambda i: (0, 0)),
                  _resident((D_FF, D_MODEL), lambda i: (0, 0))],
        out_specs=row,
        compiler_params=_cparams("parallel"),
        name="ffn",
    )(hn, h2, wg, wu, wd)


def _row_copy(src_ref, src_row, dst_ref, dst_row, sem):
    return pltpu.make_async_copy(src_ref.at[pl.ds(src_row, 1)], dst_ref.at[pl.ds(dst_row, 1)], sem)


def _dispatch_kernel(ps_ref, pl_ref, pos_ref, hn_ref, xs_ref, buf, zrow, sem, zsem):
    tm = hn_ref.shape[0]

    @pl.when(pl.program_id(0) == 0)
    def _():
        zrow[...] = jnp.zeros(zrow.shape, F32)
        for e in range(N_EXPERTS):
            def fill(r, c, e=e):
                _row_copy(zrow, 0, xs_ref, ps_ref[e] + r, zsem).start()
                return c

            lax.fori_loop(0, pl_ref[e], fill, 0)
        for e in range(N_EXPERTS):
            def drain(r, c, e=e):
                _row_copy(zrow, 0, xs_ref, ps_ref[e] + r, zsem).wait()
                return c

            lax.fori_loop(0, pl_ref[e], drain, 0)

    buf[...] = hn_ref[...].astype(F32)

    def issue(j, c):
        for s in range(TOP_K):
            _row_copy(buf, j, xs_ref, pos_ref[0, s * tm + j], sem).start()
        return c

    lax.fori_loop(0, tm, issue, 0, unroll=8)
    for _ in range(TOP_K):
        pltpu.make_async_copy(buf, xs_ref.at[pl.ds(0, tm)], sem).wait()


def _dispatch(pad_start, pad_len, pos, hn, n_rows):
    T = hn.shape[0]
    tm = ROUTE_TILE
    return pl.pallas_call(
        _dispatch_kernel,
        out_shape=jax.ShapeDtypeStruct((n_rows, D_MODEL), F32),
        grid_spec=pltpu.PrefetchScalarGridSpec(
            num_scalar_prefetch=2,
            grid=(T // tm,),
            in_specs=[pl.BlockSpec((None, 1, TOP_K * tm), lambda i, ps, pn: (i, 0, 0), memory_space=pltpu.SMEM),
                      pl.BlockSpec((tm, D_MODEL), lambda i, ps, pn: (i, 0))],
            out_specs=pl.BlockSpec(memory_space=pl.ANY),
            scratch_shapes=[pltpu.VMEM((tm, D_MODEL), F32), pltpu.VMEM((SUBLANE, D_MODEL), F32),
                            pltpu.SemaphoreType.DMA, pltpu.SemaphoreType.DMA]),
        compiler_params=_cparams("arbitrary"),
        name="moe_dispatch",
    )(pad_start, pad_len, pos, hn)


def _moe_ffn_kernel(te_ref, nu_ref, x_ref, wg_ref, wu_ref, wd_ref, o_ref):
    del te_ref
    i = pl.program_id(0)

    @pl.when(i < nu_ref[0])
    def _():
        o_ref[...] = _swiglu_rows(x_ref[...].astype(BF16), wg_ref, wu_ref, wd_ref)

    @pl.when(i >= nu_ref[0])
    def _():
        o_ref[...] = jnp.zeros(o_ref.shape, F32)


def _moe_ffn(tile_expert, n_used, xs, wg, wu, wd):
    n_rows = xs.shape[0]
    tm = MOE_TILE

    def row_map(i, te, nu):
        return (jnp.maximum(jnp.minimum(i, nu[0] - 1), 0), 0)

    def expert_map(i, te, nu):
        return (te[i], 0, 0)

    return pl.pallas_call(
        _moe_ffn_kernel,
        out_shape=jax.ShapeDtypeStruct((n_rows, D_MODEL), F32),
        grid_spec=pltpu.PrefetchScalarGridSpec(
            num_scalar_prefetch=2,
            grid=(n_rows // tm,),
            in_specs=[pl.BlockSpec((tm, D_MODEL), row_map),
                      _resident((None, D_MODEL, D_FF), expert_map),
                      _resident((None, D_MODEL, D_FF), expert_map),
                      _resident((None, D_FF, D_MODEL), expert_map)],
            out_specs=pl.BlockSpec((tm, D_MODEL), lambda i, te, nu: (i, 0))),
        compiler_params=_cparams("arbitrary"),
        name="moe_ffn",
    )(tile_expert, n_used, xs, wg, wu, wd)


def _combine_kernel(pos_ref, meta_ref, h_ref, yb_ref, o_ref, buf0, buf1, sems):
    tm = h_ref.shape[0]
    bufs = (buf0, buf1)

    def issue(j, c):
        for s in range(TOP_K):
            _row_copy(yb_ref, pos_ref[0, s * tm + j], bufs[s], j, sems.at[s]).start()
        return c

    lax.fori_loop(0, tm, issue, 0, unroll=8)
    meta = meta_ref[...]
    lane = lax.broadcasted_iota(I32, meta.shape, 1)
    g0 = jnp.sum(jnp.where(lane == _RT_G0, meta, 0.0), axis=-1, keepdims=True)
    g1 = jnp.sum(jnp.where(lane == _RT_G1, meta, 0.0), axis=-1, keepdims=True)
    for s in range(TOP_K):
        pltpu.make_async_copy(yb_ref.at[pl.ds(0, tm)], bufs[s], sems.at[s]).wait()
    o_ref[...] = h_ref[...] + (g0 * buf0[...] + g1 * buf1[...])


def _combine(pos, meta, h2, yb):
    T = h2.shape[0]
    tm = ROUTE_TILE
    row = pl.BlockSpec((tm, D_MODEL), lambda i: (i, 0))
    return pl.pallas_call(
        _combine_kernel,
        out_shape=jax.ShapeDtypeStruct((T, D_MODEL), F32),
        grid=(T // tm,),
        in_specs=[pl.BlockSpec((None, 1, TOP_K * tm), lambda i: (i, 0, 0), memory_space=pltpu.SMEM),
                  pl.BlockSpec((tm, LANE), lambda i: (i, 0)), row,
                  pl.BlockSpec(memory_space=pl.ANY)],
        out_specs=row,
        scratch_shapes=[pltpu.VMEM((tm, D_MODEL), F32), pltpu.VMEM((tm, D_MODEL), F32),
                        pltpu.SemaphoreType.DMA((TOP_K,))],
        compiler_params=_cparams("arbitrary"),
        name="moe_combine",
    )(pos, meta, h2, yb)


def _moe(hn, h2, meta, cnt, wg, wu, wd):
    T, D = h2.shape
    counts = cnt[0, :N_EXPERTS].astype(I32)
    padded = (counts + MOE_TILE - 1) // MOE_TILE * MOE_TILE
    ends = jnp.cumsum(padded)
    starts = ends - padded
    n_rows = TOP_K * T + N_EXPERTS * MOE_TILE
    n_tiles = n_rows // MOE_TILE
    experts = jnp.arange(N_EXPERTS, dtype=I32)

    def slot(e_lane, r_lane):
        e = meta[:, e_lane].astype(I32)
        start = jnp.sum(jnp.where(e[:, None] == experts[None, :], starts[None, :], 0), axis=1)
        return start + meta[:, r_lane].astype(I32)

    nt = T // ROUTE_TILE
    pos = jnp.concatenate([slot(_RT_E0, _RT_R0).reshape(nt, ROUTE_TILE),
                           slot(_RT_E1, _RT_R1).reshape(nt, ROUTE_TILE)], axis=1)[:, None, :]
    tile_start = jnp.arange(n_tiles, dtype=I32) * MOE_TILE
    tile_expert = jnp.minimum(jnp.sum(ends[None, :] <= tile_start[:, None], axis=1), N_EXPERTS - 1).astype(I32)
    n_used = (ends[-1:] // MOE_TILE).astype(I32)
    pad_start = starts + counts
    pad_end = jnp.where(experts == N_EXPERTS - 1, n_rows, ends)
    xs = _dispatch(pad_start.astype(I32), (pad_end - pad_start).astype(I32), pos, hn, n_rows)
    yb = _moe_ffn(tile_expert, n_used, xs, wg.astype(BF16), wu.astype(BF16), wd.astype(BF16))
    return _combine(pos, meta, h2, yb)


def _block_diag_ones(n, blk):
    i = np.arange(n)
    return (i[:, None] // blk == i[None, :] // blk).astype(np.float32)


def _consts():
    n4 = 4 * CHUNK
    i = np.arange(n4)
    bd = _block_diag_ones(n4, CHUNK)
    tr, tc = i[:, None] % CHUNK, i[None, :] % CHUNK
    strict = bd * (tr > tc)
    incl = bd * (tr >= tc)
    ltri = np.tril(np.ones((CHUNK, CHUNK), np.float32))
    pair_mean = _block_diag_ones(LANE, HEAD_DIM) / HEAD_DIM
    qmask = (i[:, None] // CHUNK == np.arange(LANE)[None, :] // GLA_KEY_DIM).astype(np.float32)
    amask = (i[:, None] % CHUNK >= np.arange(CHUNK)[None, :]).astype(np.float32)
    out = {k: jnp.asarray(v, F32) for k, v in dict(
        bd=bd, strict=strict, incl=incl, ltri=ltri, pair_mean=pair_mean, qmask=qmask, amask=amask).items()}
    out["route_tri"] = jnp.asarray(np.tril(np.ones((ROUTE_TILE, ROUTE_TILE), np.float32), -1), BF16)
    return out


def _pad_rows(w, row0, total):
    return jnp.zeros((total, w.shape[1]), F32).at[row0:row0 + w.shape[0]].set(w.astype(F32))


def _pack_rows(rows, width, n_rows):
    out = jnp.zeros((n_rows, width), F32)
    for i, r in enumerate(rows):
        r = r.reshape(-1).astype(F32)
        out = out.at[i, :r.shape[0]].set(r)
    return out


def kernel(x, rel_bias, mix_norm_g, w_in, w_out, ffn_norm_g, attn_q_norm_g, attn_k_norm_g, conv_w, conv_b, conv_ln_g, conv_ln_b, rwkv_mu, rwkv_w0, rwkv_w2, rwkv_a0, rwkv_a2, rwkv_g2, rwkv_k_k, rwkv_k_a, rwkv_r_k, rwkv_ln_g, rwkv_ln_b, gla_g2, gla_gb, gla_norm_g, ffn_wg, ffn_wu, ffn_wd, moe_router, moe_wg, moe_wu, moe_wd):
    B, S, D = x.shape
    T = B * S
    cs = _consts()
    bias = _attn_bias_tables(rel_bias)
    h = x.reshape(T, D)
    for layer in range(DEPTH):
        w = w_in[layer]
        dq, dk, dv, dg, do = (w[:, 2176:2304], w[:, 2304:2432], w[:, 2432:2688], w[:, 2688:2704], w[:, 2704:2960])
        wp = jnp.concatenate([w[:, :2176], dq, dv, do, dk, dg, jnp.zeros((D, IN_WP - IN_W), w.dtype)],
                             axis=1).astype(BF16)
        z = _inproj(h, mix_norm_g[layer][None, :], wp)
        z3 = z.reshape(B, S, IN_WP)

        qg = jnp.tile(attn_q_norm_g[layer].astype(F32) * (HEAD_DIM ** -0.5), 2)[None, :]
        kg = jnp.tile(attn_k_norm_g[layer].astype(F32), 2)[None, :]
        ya = _attn_mixer(z3, bias, qg, kg, cs["pair_mean"]).reshape(T, GROUP_W)

        conv_pv = _pack_rows([conv_b[layer], conv_ln_g[layer], conv_ln_b[layer]], GROUP_W, 8)
        group_mean = cs["bd"] * (1.0 / HEAD_DIM)
        yb = _conv_mixer(z3, conv_w[layer].astype(F32), conv_pv, group_mean).reshape(T, GROUP_W)

        mu = rwkv_mu[layer]
        rw_pv = _pack_rows([mu[0:256], mu[256:512], mu[512:768], mu[768:896], rwkv_w0[layer], rwkv_a0[layer],
                            rwkv_k_k[layer], rwkv_k_a[layer], rwkv_r_k[layer], rwkv_ln_g[layer],
                            rwkv_ln_b[layer]], GROUP_W, 16)
        yc = _rwkv_mixer(z3, rw_pv, _pad_rows(rwkv_w2[layer], 0, LANE), _pad_rows(rwkv_a2[layer], 32, LANE),
                         _pad_rows(rwkv_g2[layer], 64, LANE), cs["bd"], cs["strict"], cs["incl"],
                         cs["ltri"]).reshape(T, GROUP_W)

        gla_pv = _pack_rows([gla_gb[layer], jnp.tile(gla_norm_g[layer], GROUP_W // HEAD_DIM)], GROUP_W, 8)
        yd = _gla_mixer(z3, _pad_rows(gla_g2[layer], 0, LANE), gla_pv, cs["bd"], cs["qmask"], cs["amask"],
                        cs["ltri"]).reshape(T, GROUP_W)

        norm_g = ffn_norm_g[layer][None, :]
        ys = (ya, yb, yc, yd)
        i = layer // 2
        if layer % 2 == 0:
            h, hn = _outproj(ys, w_out[layer].astype(BF16), h, norm_g)
            h = _ffn(hn, h, ffn_wg[i].astype(BF16), ffn_wu[i].astype(BF16), ffn_wd[i].astype(BF16))
        else:
            wr = jnp.zeros((D, LANE), F32).at[:, :N_EXPERTS].set(moe_router[i].astype(F32))
            wr_hi = wr.astype(BF16)
            wr = jnp.concatenate([wr_hi, (wr - wr_hi.astype(F32)).astype(BF16)], axis=1)
            h, hn, meta, cnt = _outproj(ys, w_out[layer].astype(BF16), h, norm_g, router=(wr, cs["route_tri"]))
            h = _moe(hn, h, meta, cnt, moe_wg[i], moe_wu[i], moe_wd[i])
    return h.reshape(B, S, D)
```

```python
import functools
import math

import numpy as np
import jax
import jax.numpy as jnp
from jax import lax
from jax.experimental import pallas as pl
from jax.experimental.pallas import tpu as pltpu

F32 = jnp.float32
BF16 = jnp.bfloat16
I32 = jnp.int32
HI = lax.Precision.HIGHEST

D_MODEL = 1024
DEPTH = 2
GROUP_W = 256
NORM_EPS = 1e-6
HEAD_DIM = 64
DILATED_PATTERNS = ((128, 1), (512, 4), (2048, 16))
ATT_BLOCK = 128
N_BUCKETS = 32
REL_MAX_DIST = 2048
CONV_WIDTH = 31
CONV_HALO = 32
CONV_LN_EPS = 1e-5
RWKV_LN_EPS = 64e-5
GLA_KEY_DIM = 32
GLA_TAU = 16.0
CHUNK = 64
D_FF = 2816
N_EXPERTS = 8
TOP_K = 2
IN_W = 2960
IN_WP = 3072
LANE = 128
SUBLANE = 8
VMEM_LIMIT = 48 * 1024 * 1024
MOE_TILE = 512
FFN_ROWS = 512
ROUTE_TILE = 512

OFF_AQ, OFF_AK, OFF_AV = 0, 256, 512
OFF_BU, OFF_BG = 768, 1024
OFF_CR, OFF_CK, OFF_CV, OFF_CL = 1280, 1536, 1792, 2048
OFF_DQ, OFF_DV, OFF_DO, OFF_DK, OFF_DG = 2176, 2304, 2560, 2816, 2944


def _cparams(*sem):
    return pltpu.CompilerParams(dimension_semantics=sem, vmem_limit_bytes=VMEM_LIMIT)


def _dot(a, b, prec=None):
    return jnp.dot(a, b, preferred_element_type=F32, precision=prec)


def _mm(a, b):
    return jnp.dot(a.astype(BF16), b.astype(BF16), preferred_element_type=F32)


def _mm_nt(a, b):
    return lax.dot_general(a.astype(BF16), b.astype(BF16), (((1,), (1,)), ((), ())),
                           preferred_element_type=F32)


def _mm_tn(a, b):
    return lax.dot_general(a.astype(BF16), b.astype(BF16), (((0,), (0,)), ((), ())),
                           preferred_element_type=F32)


def _split3(x):
    x1 = x.astype(BF16)
    r1 = x - x1.astype(F32)
    x2 = r1.astype(BF16)
    x3 = (r1 - x2.astype(F32)).astype(BF16)
    return x1, x2, x3


def _dot_exact_rhs(x, m):
    n = x.shape[0]
    y = _dot(jnp.concatenate(_split3(x), axis=0), m)
    return y[0:n] + y[n:2 * n] + y[2 * n:3 * n]


def _dot_exact_lhs(m, x):
    n = x.shape[1]
    y = _dot(m, jnp.concatenate(_split3(x), axis=1))
    return y[:, 0:n] + y[:, n:2 * n] + y[:, 2 * n:3 * n]


def _each(fn, *lists):
    return [fn(*args) for args in zip(*lists)]


def _sigmoid(x):
    return 1.0 / (1.0 + jnp.exp(-x))


def _softplus(x):
    return jnp.maximum(x, 0.0) + jnp.log(1.0 + jnp.exp(-jnp.abs(x)))


def _resident(shape, index_map):
    return pl.BlockSpec(shape, index_map, pipeline_mode=pl.Buffered(1))


def _inproj_kernel(x_ref, g_ref, w_ref, z_ref):
    x = x_ref[...]
    ms = jnp.mean(x * x, axis=-1, keepdims=True)
    xn = (x * lax.rsqrt(ms + NORM_EPS) * g_ref[...]).astype(BF16)
    z_ref[...] = _dot(xn, w_ref[...])


def _inproj(h2, g, w):
    T = h2.shape[0]
    tm = FFN_ROWS
    return pl.pallas_call(
        _inproj_kernel,
        out_shape=jax.ShapeDtypeStruct((T, IN_WP), F32),
        grid=(T // tm,),
        in_specs=[pl.BlockSpec((tm, D_MODEL), lambda i: (i, 0)),
                  pl.BlockSpec((1, D_MODEL), lambda i: (0, 0)),
                  _resident((D_MODEL, IN_WP), lambda i: (0, 0))],
        out_specs=pl.BlockSpec((tm, IN_WP), lambda i: (i, 0)),
        compiler_params=_cparams("parallel"),
        name="inproj",
    )(h2, g, w)


def _t5_bucket(dist):
    max_exact = N_BUCKETS // 2
    n = np.maximum(dist, 0)
    large = max_exact + (np.log(np.maximum(n, 1) / max_exact) / math.log(REL_MAX_DIST / max_exact)
                         * (N_BUCKETS - max_exact)).astype(np.int32)
    large = np.minimum(large, N_BUCKETS - 1)
    return np.where(n < max_exact, n, large).astype(np.int32)


def _bucket_table():
    W = ATT_BLOCK
    delta = np.arange(W)[:, None] + W - np.arange(2 * W)[None, :]
    band = (delta >= 0) & (delta <= W)
    tabs = [np.where(band, _t5_bucket(np.clip(delta, 0, W) * d), -1) for _, d in DILATED_PATTERNS]
    return jnp.asarray(np.stack(tabs), I32)


def _bias_kernel(rb_ref, bk_ref, o_ref):
    n_heads = o_ref.shape[1]
    for di in range(len(DILATED_PATTERNS)):
        bk = bk_ref[di]
        accs = [jnp.full(bk.shape, -jnp.inf, F32) for _ in range(n_heads)]
        for b in range(N_BUCKETS):
            hit = bk == b
            accs = [jnp.where(hit, rb_ref[b, hh], a) for hh, a in enumerate(accs)]
        for hh in range(n_heads):
            o_ref[di, hh] = accs[hh]


def _attn_bias_tables(rel_bias):
    n_heads = rel_bias.shape[1]
    shape = (len(DILATED_PATTERNS), n_heads, ATT_BLOCK, 2 * ATT_BLOCK)
    return pl.pallas_call(
        _bias_kernel,
        out_shape=jax.ShapeDtypeStruct(shape, F32),
        in_specs=[pl.BlockSpec(memory_space=pltpu.SMEM), pl.BlockSpec(memory_space=pltpu.VMEM)],
        out_specs=pl.BlockSpec(memory_space=pltpu.VMEM),
        name="attn_bias",
    )(rel_bias.astype(F32), _bucket_table())


ATT_MERGE_ROWS = 256
ATT_UNROLL = 4


def _attn_kernel(q_ref, k_ref, v_ref, bias_ref, qg_ref, kg_ref, gm_ref, o_ref,
                 qn, kn, qd, kd, vd, od, ld, acc, lse):
    S = q_ref.shape[0]
    rows = 512

    def norm_body(i, c):
        sl = pl.ds(pl.multiple_of(i * rows, rows), rows)
        q = q_ref[sl, :]
        k = k_ref[sl, :]
        gm16 = gm_ref[...].astype(BF16)
        qn[sl, :] = q * lax.rsqrt(_dot_exact_rhs(q * q, gm16) + NORM_EPS) * qg_ref[...]
        kn[sl, :] = k * lax.rsqrt(_dot_exact_rhs(k * k, gm16) + NORM_EPS) * kg_ref[...]
        return c

    lax.fori_loop(0, S // rows, norm_body, 0)

    lane = lax.broadcasted_iota(I32, (ATT_BLOCK, LANE), 1)
    head0 = lane < HEAD_DIM
    col = lax.broadcasted_iota(I32, (ATT_BLOCK, 2 * ATT_BLOCK), 1)

    def run_blocks(di, nb, qs, ks, vs, o_dst, l_dst):
        nu = min(nb, ATT_UNROLL)

        def blk(n):
            return pl.ds(pl.multiple_of(n * ATT_BLOCK, ATT_BLOCK), ATT_BLOCK)

        def body(i, c):
            base = i * nu
            rows = [blk(jnp.maximum(base - 1, 0))] + [blk(base + u) for u in range(nu)]
            kb = [ks[r, :].astype(BF16) for r in rows]
            vb = [vs[r, :].astype(BF16) for r in rows]
            kt = [jnp.concatenate([kb[u], kb[u + 1]], axis=0) for u in range(nu)]
            vt = [jnp.concatenate([vb[u], vb[u + 1]], axis=0) for u in range(nu)]
            qt = [qs[r, :] for r in rows[1:]]
            heads = [(u, hh) for u in range(nu) for hh in range(2)]
            s = [_mm_nt(jnp.where(head0 if hh == 0 else jnp.logical_not(head0), qt[u], 0.0), kt[u])
                 + bias_ref[di, hh] for u, hh in heads]
            no_prev = jnp.logical_and(i == 0, col < ATT_BLOCK)
            s = [jnp.where(no_prev, -jnp.inf, x) if u == 0 else x for (u, hh), x in zip(heads, s)]
            m = [jnp.max(x, axis=-1, keepdims=True) for x in s]
            p = _each(lambda x, mx: jnp.exp(x - mx), s, m)
            den = [jnp.sum(x, axis=-1, keepdims=True) for x in p]
            o = [_mm(x, vt[u]) / dn for (u, hh), x, dn in zip(heads, p, den)]
            ls = _each(lambda mx, dn: mx + jnp.log(dn), m, den)
            for u in range(nu):
                o_dst[rows[u + 1], :] = jnp.where(head0, o[2 * u], o[2 * u + 1])
                l_dst[rows[u + 1], :] = jnp.where(head0, ls[2 * u], ls[2 * u + 1])
            return c

        lax.fori_loop(0, nb // nu, body, 0)

    for di, (_, d) in enumerate(DILATED_PATTERNS):
        L = S // d
        nb = L // ATT_BLOCK
        if d == 1:
            run_blocks(di, nb, qn, kn, v_ref, acc, lse)
            continue
        for r in range(d):
            res = pl.ds(r, L, stride=d)
            qd[0:L, :] = qn[res, :]
            kd[0:L, :] = kn[res, :]
            vd[0:L, :] = v_ref[res, :]
            run_blocks(di, nb, qd, kd, vd, od, ld)
            mr = min(L, ATT_MERGE_ROWS)
            for c in range(L // mr):
                piece = pl.ds(r + c * mr * d, mr, stride=d)
                l0, a0 = lse[piece, :], acc[piece, :]
                l1, a1 = ld[c * mr:(c + 1) * mr, :], od[c * mr:(c + 1) * mr, :]
                m = jnp.maximum(l0, l1)
                e0, e1 = jnp.exp(l0 - m), jnp.exp(l1 - m)
                tot = e0 + e1
                acc[piece, :] = (a0 * e0 + a1 * e1) / tot
                lse[piece, :] = m + jnp.log(tot)

    def out_body(i, c):
        sl = pl.ds(pl.multiple_of(i * rows, rows), rows)
        o_ref[sl, :] = acc[sl, :].astype(BF16)
        return c

    lax.fori_loop(0, S // rows, out_body, 0)


def _attn_mixer(z3, bias, qg, kg, gm):
    B, S, _ = z3.shape
    n_lb = GROUP_W // LANE
    l_max = S // DILATED_PATTERNS[1][1]

    def spec(off):
        return pl.BlockSpec((None, S, LANE), lambda b, p: (b, 0, off // LANE + p))

    vec = pl.BlockSpec((1, LANE), lambda b, p: (0, 0))
    full = pltpu.VMEM((S, LANE), F32)
    part = pltpu.VMEM((l_max, LANE), F32)
    return pl.pallas_call(
        _attn_kernel,
        out_shape=jax.ShapeDtypeStruct((B, S, GROUP_W), BF16),
        grid=(B, n_lb),
        in_specs=[spec(OFF_AQ), spec(OFF_AK), spec(OFF_AV),
                  pl.BlockSpec((len(DILATED_PATTERNS), 2, ATT_BLOCK, 2 * ATT_BLOCK), lambda b, p: (0, p, 0, 0)),
                  vec, vec, pl.BlockSpec((LANE, LANE), lambda b, p: (0, 0))],
        out_specs=pl.BlockSpec((None, S, LANE), lambda b, p: (b, 0, p)),
        scratch_shapes=[full, full, part, part, part, part, part, full, full],
        compiler_params=_cparams("parallel", "parallel"),
        name="attn",
    )(z3, z3, z3, bias, qg, kg, gm)


def _conv_kernel(u_ref, g_ref, cw_ref, pv_ref, gm_ref, o_ref, hbuf, hsh, *, carry_phase):
    ts = u_ref.shape[0]
    t = pl.program_id(1)

    if carry_phase:
        @pl.when(t == 0)
        def _():
            hbuf[0:CONV_HALO, :] = jnp.zeros((CONV_HALO, GROUP_W), F32)

        @pl.when(t > 0)
        def _():
            hbuf[0:CONV_HALO, :] = hbuf[ts:ts + CONV_HALO, :]

        return

    hbuf[CONV_HALO:CONV_HALO + ts, :] = u_ref[...] * _sigmoid(g_ref[...])
    first = CONV_HALO - (CONV_WIDTH - 1)
    n_shift = hsh.shape[1]
    for ph in range(1, SUBLANE):
        hsh[ph - 1, :, :] = hbuf[ph:ph + n_shift, :]
    rows = 64
    gm16 = gm_ref[...].astype(BF16)
    for c in range(ts // rows):
        acc = jnp.zeros((rows, GROUP_W), F32) + pv_ref[0:1, :]
        for j in range(CONV_WIDTH):
            ph = (first + j) % SUBLANE
            s = first + j - ph + c * rows
            tap = hbuf[s:s + rows, :] if ph == 0 else hsh[ph - 1, s:s + rows, :]
            acc = acc + cw_ref[j:j + 1, :] * tap
        mu = _dot_exact_rhs(acc, gm16)
        xc = acc - mu
        var = _dot_exact_rhs(xc * xc, gm16)
        y = xc * lax.rsqrt(var + CONV_LN_EPS) * pv_ref[1:2, :] + pv_ref[2:3, :]
        o_ref[c * rows:(c + 1) * rows, :] = (y * _sigmoid(y)).astype(BF16)


MIX_ROWS = 512


def _seg(off, w):
    return pl.BlockSpec((None, MIX_ROWS, w), lambda b, t: (b, t, off // w))


def _const(shape):
    return pl.BlockSpec(shape, lambda b, t: (0, 0))


def _conv_part(z3, conv_w, pvec, gm):
    ts = MIX_ROWS
    return dict(kernel=_conv_kernel,
                in_specs=[_seg(OFF_BU, GROUP_W), _seg(OFF_BG, GROUP_W), _const((CONV_WIDTH, GROUP_W)),
                          _const((8, GROUP_W)), _const((GROUP_W, GROUP_W))],
                args=(z3, z3, conv_w, pvec, gm),
                scratch=[pltpu.VMEM((ts + CONV_HALO, GROUP_W), F32),
                         pltpu.VMEM((SUBLANE - 1, ts + CONV_HALO - SUBLANE, GROUP_W), F32)])


def _mixers_kernel(*refs, parts):
    n_in = sum(p[1] for p in parts)
    ins, outs, scr = refs[:n_in], refs[n_in:n_in + len(parts)], refs[n_in + len(parts):]
    for carry_phase in (True, False):
        i = s = 0
        for k, (fn, ni, ns) in enumerate(parts):
            fn(*ins[i:i + ni], outs[k], *scr[s:s + ns], carry_phase=carry_phase)
            i += ni
            s += ns


def _mixers(z3, parts):
    B, S, _ = z3.shape
    out_spec = pl.BlockSpec((None, MIX_ROWS, GROUP_W), lambda b, t: (b, t, 0))
    meta = tuple((p["kernel"], len(p["in_specs"]), len(p["scratch"])) for p in parts)
    return pl.pallas_call(
        functools.partial(_mixers_kernel, parts=meta),
        out_shape=[jax.ShapeDtypeStruct((B, S, GROUP_W), BF16)] * len(parts),
        grid=(B, S // MIX_ROWS),
        in_specs=[sp for p in parts for sp in p["in_specs"]],
        out_specs=[out_spec] * len(parts),
        scratch_shapes=[sc for p in parts for sc in p["scratch"]],
        compiler_params=_cparams("parallel", "arbitrary"),
        name="mixers_bcd",
    )(*[a for p in parts for a in p["args"]])


_RW_MU_R, _RW_MU_K, _RW_MU_V, _RW_MU_L, _RW_W0, _RW_A0, _RW_KK, _RW_KA, _RW_RK, _RW_LNG, _RW_LNB = range(11)
SHIFT_PAD = 8
RWKV_GROUP = 8
GLA_GROUP = 8


def _rwkv_kernel(r_ref, k_ref, v_ref, l_ref, pv_ref, w2_ref, a2_ref, g2_ref, bd_ref, st_ref, in_ref, lt_ref,
                 o_ref, rbuf, kbuf, vbuf, lbuf, lw_s, kk_s, be_s, km_s, rr_s, vv_s, gg_s, bo_s, sm_ref, *,
                 carry_phase):
    ts = r_ref.shape[0]
    t = pl.program_id(1)
    bufs = ((rbuf, r_ref), (kbuf, k_ref), (vbuf, v_ref), (lbuf, l_ref))

    if carry_phase:
        @pl.when(t == 0)
        def _():
            for buf, _ in bufs:
                buf[0:SHIFT_PAD, :] = jnp.zeros((SHIFT_PAD, buf.shape[1]), F32)
            sm_ref[...] = jnp.zeros(sm_ref.shape, F32)

        @pl.when(t > 0)
        def _():
            for buf, _ in bufs:
                buf[0:SHIFT_PAD, :] = buf[ts:ts + SHIFT_PAD, :]

        return

    for buf, ref in bufs:
        buf[SHIFT_PAD:SHIFT_PAD + ts, :] = ref[...]

    def pv(i, w=GROUP_W):
        return pv_ref[i:i + 1, 0:w]

    def shifted(buf, mu):
        cur = buf[SHIFT_PAD:SHIFT_PAD + ts, :]
        prev = buf[SHIFT_PAD - 1:SHIFT_PAD - 1 + ts, :]
        return cur + (prev - cur) * mu

    bd = bd_ref[...]
    bd16 = bd.astype(BF16)
    r = shifted(rbuf, pv(_RW_MU_R))
    k = shifted(kbuf, pv(_RW_MU_K))
    v = shifted(vbuf, pv(_RW_MU_V))
    lo = shifted(lbuf, pv(_RW_MU_L, LANE))
    w_log = -_softplus(-(pv(_RW_W0) + _dot(jnp.tanh(lo), w2_ref[...], HI))) - 0.5
    a = _sigmoid(pv(_RW_A0) + _dot(lo, a2_ref[...], HI))
    kk = k * pv(_RW_KK)
    kk = kk / jnp.maximum(jnp.sqrt(_dot_exact_rhs(kk * kk, bd16)), 1e-12)
    km = k * (1.0 + (a - 1.0) * pv(_RW_KA))
    lw_s[...] = -jnp.exp(w_log)
    kk_s[...] = kk
    be_s[...] = kk * a
    km_s[...] = km
    rr_s[...] = r
    vv_s[...] = v
    gg_s[...] = _dot(_sigmoid(lo), g2_ref[...], HI)
    bo_s[...] = _dot_exact_rhs(r * km * pv(_RW_RK), bd16) * v

    n2 = LANE // HEAD_DIM * CHUNK
    bd2 = bd[0:n2, 0:LANE]
    strict = st_ref[0:n2, 0:n2]
    incl = in_ref[0:n2, 0:n2]
    eye = incl - strict
    lt16 = lt_ref[...].astype(BF16)
    gmean16 = (bd2 * (1.0 / HEAD_DIM)).astype(BF16)
    n_pairs = GROUP_W // LANE

    def tile2(x):
        return jnp.concatenate([x] * (LANE // HEAD_DIM), axis=0)

    def fold(x):
        return x[0:CHUNK] + x[CHUNK:2 * CHUNK]

    def group(gi, carry):
        rows = [pl.ds(pl.multiple_of((gi * RWKV_GROUP + j) * CHUNK, CHUNK), CHUNK) for j in range(RWKV_GROUP)]
        items = [(s, slice(p * LANE, (p + 1) * LANE)) for s in rows for p in range(n_pairs)]
        lw = [lw_s[s, ln] for s, ln in items]
        cum = _each(lambda x: _dot_exact_lhs(lt16, x), lw)
        last = [c[CHUNK - 1:CHUNK, :] for c in cum]
        e_neg = [jnp.exp(-c) for c in cum]
        e_last = _each(lambda l, c: jnp.exp(l - c), last, cum)
        bec = [be_s[s, ln] for s, ln in items]
        kmc = [km_s[s, ln] for s, ln in items]
        a_bar = _each(lambda it, c, w: tile2(-kk_s[it[0], it[1]] * jnp.exp(c - w)) * bd2, items, cum, lw)
        r_bar = _each(lambda it, c: tile2(rr_s[it[0], it[1]] * jnp.exp(c)) * bd2, items, cum)
        v_blk = [tile2(vv_s[s, ln]) * bd2 for s, ln in items]
        big = _each(lambda ab, rb, be, km_, en: _mm_nt(
            jnp.concatenate([ab, rb], axis=0),
            jnp.concatenate([tile2(be * en), tile2(km_ * en)], axis=0)), a_bar, r_bar, bec, kmc, e_neg)
        nmat = [b[0:n2, 0:n2] * strict for b in big]
        a_ak = [b[0:n2, n2:2 * n2] * strict for b in big]
        a_rb = [b[n2:2 * n2, 0:n2] * incl for b in big]
        a_rk = [b[n2:2 * n2, n2:2 * n2] * incl for b in big]
        tinv = [eye + n for n in nmat]
        npow = nmat
        for _ in range(5):
            npow = _each(lambda n: _mm(n, n), npow)
            tinv = _each(lambda t_, n: t_ + _mm(t_, n), tinv, npow)
        w_blk = _each(_mm, tinv, a_bar)
        u0 = _each(lambda t_, ak, vb: _mm(t_, _mm(ak, vb)), tinv, a_ak, v_blk)
        o0 = _each(lambda rb, u, rk, vb: fold(_mm(rb, u) + _mm(rk, vb)), a_rb, u0, a_rk, v_blk)
        q_eff = _each(lambda rbar, rb, w: fold(rbar + _mm(rb, w)), r_bar, a_rb, w_blk)
        gh = _each(lambda be, el, w, u: _mm_tn(tile2(be * el) * bd2, jnp.concatenate([w, u], axis=1)),
                   bec, e_last, w_blk, u0)
        g_mat = _each(lambda l, x: eye * jnp.exp(l) + x[:, 0:LANE], last, gh)
        h_mat = _each(lambda x, km_, el, vb: x[:, LANE:2 * LANE] + _mm_tn(tile2(km_ * el) * bd2, vb),
                      gh, kmc, e_last, v_blk)
        outs = [None] * len(items)
        for p in range(n_pairs):
            sm = sm_ref[p]
            for j in range(RWKV_GROUP):
                i = j * n_pairs + p
                outs[i] = _mm(q_eff[i], sm) + o0[i]
                sm = _mm(g_mat[i], sm) + h_mat[i]
            sm_ref[p] = sm
        for (s, ln), out in zip(items, outs):
            mu = _dot_exact_rhs(out, gmean16)
            xc = out - mu
            var = _dot_exact_rhs(xc * xc, gmean16)
            y = xc * lax.rsqrt(var + RWKV_LN_EPS) * pv_ref[_RW_LNG:_RW_LNG + 1, ln] + pv_ref[_RW_LNB:_RW_LNB + 1, ln]
            o_ref[s, ln] = ((y + bo_s[s, ln]) * gg_s[s, ln]).astype(BF16)
        return carry

    lax.fori_loop(0, ts // (CHUNK * RWKV_GROUP), group, 0)


def _rwkv_part(z3, pvec, w2p, a2p, g2p, bd, strict, incl, ltri):
    ts = MIX_ROWS
    sq = (4 * CHUNK, 4 * CHUNK)
    wide = pltpu.VMEM((ts, GROUP_W), F32)
    return dict(kernel=_rwkv_kernel,
                in_specs=[_seg(OFF_CR, GROUP_W), _seg(OFF_CK, GROUP_W), _seg(OFF_CV, GROUP_W), _seg(OFF_CL, LANE),
                          _const((16, GROUP_W)), _const((LANE, GROUP_W)), _const((LANE, GROUP_W)),
                          _const((LANE, GROUP_W)), _const(sq), _const(sq), _const(sq), _const((CHUNK, CHUNK))],
                args=(z3, z3, z3, z3, pvec, w2p, a2p, g2p, bd, strict, incl, ltri),
                scratch=[pltpu.VMEM((ts + SHIFT_PAD, GROUP_W), F32)] * 3
                        + [pltpu.VMEM((ts + SHIFT_PAD, LANE), F32)]
                        + [wide] * 8 + [pltpu.VMEM((GROUP_W // LANE, LANE // HEAD_DIM * CHUNK, LANE), F32)])


def _gla_kernel(q_ref, k_ref, v_ref, og_ref, gd_ref, g2_ref, pv_ref, bd_ref, qm_ref, am_ref, lt_ref, o_ref,
                gk_s, st_ref, *, carry_phase):
    ts = q_ref.shape[0]

    if carry_phase:
        @pl.when(pl.program_id(1) == 0)
        def _():
            st_ref[...] = jnp.zeros(st_ref.shape, F32)

        return

    x = _dot(gd_ref[...], g2_ref[...], HI) + pv_ref[0:1, 0:LANE]
    gk_s[...] = -_softplus(-x) * (1.0 / GLA_TAU)
    bd = bd_ref[...]
    qmask = qm_ref[...]
    n_heads = GROUP_W // HEAD_DIM

    lt16 = lt_ref[...].astype(BF16)
    gmean16 = (bd * (1.0 / HEAD_DIM)).astype(BF16)

    def fold(x):
        return x[0:CHUNK] + x[CHUNK:2 * CHUNK] + x[2 * CHUNK:3 * CHUNK] + x[3 * CHUNK:4 * CHUNK]

    def group(gi, carry):
        sls = [pl.ds(pl.multiple_of((gi * GLA_GROUP + j) * CHUNK, CHUNK), CHUNK) for j in range(GLA_GROUP)]
        b = [_dot_exact_lhs(lt16, gk_s[s, :]) for s in sls]
        last = [x[CHUNK - 1:CHUNK, :] for x in b]
        kc = [k_ref[s, :] for s in sls]
        vc = [v_ref[s, :] for s in sls]
        q_d = _each(lambda s, x: q_ref[s, :] * (GLA_KEY_DIM ** -0.5) * jnp.exp(x), sls, b)
        k_d = _each(lambda k_, x: k_ * jnp.exp(-x), kc, b)
        att = _each(lambda q_, k_: _mm_nt(jnp.concatenate([q_] * n_heads, axis=0) * qmask, k_) * am_ref[...],
                    q_d, k_d)
        o_in = _each(lambda a_, v_: fold(_mm(a_, v_) * bd), att, vc)
        upd = _each(lambda v_, k_, l, x: _mm_tn(v_, k_ * jnp.exp(l - x)) * qmask, vc, kc, last, b)
        st = st_ref[...]
        outs = []
        for j in range(GLA_GROUP):
            outs.append(o_in[j] + _mm_nt(q_d[j], st))
            st = st * jnp.exp(last[j]) + upd[j]
        st_ref[...] = st
        for s, o in zip(sls, outs):
            ms = _dot_exact_rhs(o * o, gmean16)
            og = og_ref[s, :]
            y = o * lax.rsqrt(ms + NORM_EPS) * pv_ref[1:2, :] * (og * _sigmoid(og))
            o_ref[s, :] = y.astype(BF16)
        return carry

    lax.fori_loop(0, ts // (CHUNK * GLA_GROUP), group, 0)


def _gla_part(z3, g2p, pvec, bd, qmask, amask, ltri):
    return dict(kernel=_gla_kernel,
                in_specs=[_seg(OFF_DQ, LANE), _seg(OFF_DK, LANE), _seg(OFF_DV, GROUP_W), _seg(OFF_DO, GROUP_W),
                          _seg(OFF_DG, LANE), _const((LANE, LANE)), _const((8, GROUP_W)),
                          _const((GROUP_W, GROUP_W)), _const((GROUP_W, LANE)), _const((GROUP_W, CHUNK)),
                          _const((CHUNK, CHUNK))],
                args=(z3, z3, z3, z3, z3, g2p, pvec, bd, qmask, amask, ltri),
                scratch=[pltpu.VMEM((MIX_ROWS, LANE), F32), pltpu.VMEM((GROUP_W, LANE), F32)])


_RT_E0, _RT_E1, _RT_G0, _RT_G1, _RT_R0, _RT_R1 = range(6)


def _route(xn, wr_ref, tri_ref, run_ref, meta_ref, cnt_ref):
    @pl.when(pl.program_id(0) == 0)
    def _():
        run_ref[...] = jnp.zeros(run_ref.shape, F32)

    tm = xn.shape[0]
    x_hi = xn.astype(BF16)
    x_lo = (xn - x_hi.astype(F32)).astype(BF16)
    y = _dot(jnp.concatenate([x_hi, x_lo], axis=0), wr_ref[...])
    logits = (y[0:tm, 0:LANE] + y[0:tm, LANE:2 * LANE]) + (y[tm:2 * tm, 0:LANE] + y[tm:2 * tm, LANE:2 * LANE])
    lane = lax.broadcasted_iota(I32, logits.shape, 1).astype(F32)
    logits = jnp.where(lane < N_EXPERTS, logits, -jnp.inf)
    m1 = jnp.max(logits, axis=-1, keepdims=True)
    i1 = jnp.min(jnp.where(logits == m1, lane, float(LANE)), axis=-1, keepdims=True)
    rest = jnp.where(lane == i1, -jnp.inf, logits)
    m2 = jnp.max(rest, axis=-1, keepdims=True)
    i2 = jnp.min(jnp.where(rest == m2, lane, float(LANE)), axis=-1, keepdims=True)
    e2 = jnp.exp(m2 - m1)
    den = 1.0 + e2
    hit1 = lane == i1
    hit2 = lane == i2
    member = jnp.where(jnp.logical_or(hit1, hit2), 1.0, 0.0)
    run = run_ref[0:1, :]
    rank = _dot(tri_ref[...], member.astype(BF16)) + run
    r1 = jnp.sum(jnp.where(hit1, rank, 0.0), axis=-1, keepdims=True)
    r2 = jnp.sum(jnp.where(hit2, rank, 0.0), axis=-1, keepdims=True)
    run = run + jnp.sum(member, axis=0, keepdims=True)
    run_ref[0:1, :] = run
    cnt_ref[...] = jnp.broadcast_to(run, cnt_ref.shape)
    rec = jnp.zeros(logits.shape, F32)
    for idx, val in ((_RT_E0, i1), (_RT_E1, i2), (_RT_G0, 1.0 / den), (_RT_G1, e2 / den), (_RT_R0, r1),
                     (_RT_R1, r2)):
        rec = jnp.where(lane == float(idx), val, rec)
    meta_ref[...] = rec


def _outproj_kernel(ya, yb, yc, yd, w_ref, h_ref, g_ref, *rest, route):
    if route:
        wr_ref, tri_ref, ho_ref, hn_ref, meta_ref, cnt_ref, run_ref = rest
    else:
        ho_ref, hn_ref = rest
    acc = h_ref[...]
    for i, y in enumerate((ya, yb, yc, yd)):
        acc = acc + _dot(y[...], w_ref[i * GROUP_W:(i + 1) * GROUP_W, :])
    ho_ref[...] = acc
    ms = jnp.mean(acc * acc, axis=-1, keepdims=True)
    xn = acc * lax.rsqrt(ms + NORM_EPS) * g_ref[...]
    hn_ref[...] = xn.astype(BF16)
    if route:
        _route(xn, wr_ref, tri_ref, run_ref, meta_ref, cnt_ref)


def _outproj(ys, w, h2, g, router=None):
    T = h2.shape[0]
    tm = ROUTE_TILE
    yspec = pl.BlockSpec((tm, GROUP_W), lambda i: (i, 0))
    row = pl.BlockSpec((tm, D_MODEL), lambda i: (i, 0))
    in_specs = [yspec] * 4 + [pl.BlockSpec((D_MODEL, D_MODEL), lambda i: (0, 0)), row,
                              pl.BlockSpec((1, D_MODEL), lambda i: (0, 0))]
    out_shape = [jax.ShapeDtypeStruct((T, D_MODEL), F32), jax.ShapeDtypeStruct((T, D_MODEL), BF16)]
    out_specs = [row, row]
    args = (*ys, w, h2, g)
    scratch = []
    if router is not None:
        in_specs += [pl.BlockSpec((D_MODEL, 2 * LANE), lambda i: (0, 0)), pl.BlockSpec((tm, tm), lambda i: (0, 0))]
        out_shape += [jax.ShapeDtypeStruct((T, LANE), F32), jax.ShapeDtypeStruct((SUBLANE, LANE), F32)]
        out_specs += [pl.BlockSpec((tm, LANE), lambda i: (i, 0)), pl.BlockSpec((SUBLANE, LANE), lambda i: (0, 0))]
        args += router
        scratch = [pltpu.VMEM((SUBLANE, LANE), F32)]
    return pl.pallas_call(
        functools.partial(_outproj_kernel, route=router is not None),
        out_shape=out_shape,
        grid=(T // tm,),
        in_specs=in_specs,
        out_specs=out_specs,
        scratch_shapes=scratch,
        compiler_params=_cparams("arbitrary"),
        name="outproj_route" if router is not None else "outproj",
    )(*args)


FFN_SPLIT = 2


def _swiglu_rows(x, wg_ref, wu_ref, wd_ref):
    piece = D_FF // FFN_SPLIT
    out = None
    for s in range(FFN_SPLIT):
        cols = slice(s * piece, (s + 1) * piece)
        g = _dot(x, wg_ref[:, cols])
        u = _dot(x, wu_ref[:, cols])
        y = _dot((g * _sigmoid(g) * u).astype(BF16), wd_ref[cols, :])
        out = y if out is None else out + y
    return out


def _ffn_kernel(x_ref, h_ref, wg_ref, wu_ref, wd_ref, o_ref):
    o_ref[...] = h_ref[...] + _swiglu_rows(x_ref[...], wg_ref, wu_ref, wd_ref)


def _ffn(hn, h2, wg, wu, wd):
    T = hn.shape[0]
    tm = FFN_ROWS
    row = pl.BlockSpec((tm, D_MODEL), lambda i: (i, 0))
    return pl.pallas_call(
        _ffn_kernel,
        out_shape=jax.ShapeDtypeStruct((T, D_MODEL), F32),
        grid=(T // tm,),
        in_specs=[row, row,
                  _resident((D_MODEL, D_FF), lambda i: (0, 0)),
                  _resident((D_MODEL, D_FF), lambda i: (0, 0)),
                  _resident((D_FF, D_MODEL), lambda i: (0, 0))],
        out_specs=row,
        compiler_params=_cparams("parallel"),
        name="ffn",
    )(hn, h2, wg, wu, wd)


def _row_copy(src_ref, src_row, dst_ref, dst_row, sem):
    return pltpu.make_async_copy(src_ref.at[pl.ds(src_row, 1)], dst_ref.at[pl.ds(dst_row, 1)], sem)


def _dispatch_kernel(ps_ref, pl_ref, pos_ref, hn_ref, xs_ref, buf, zrow, sem, zsem):
    tm = hn_ref.shape[0]

    @pl.when(pl.program_id(0) == 0)
    def _():
        zrow[...] = jnp.zeros(zrow.shape, F32)
        for e in range(N_EXPERTS):
            def fill(r, c, e=e):
                _row_copy(zrow, 0, xs_ref, ps_ref[e] + r, zsem).start()
                return c

            lax.fori_loop(0, pl_ref[e], fill, 0)
        for e in range(N_EXPERTS):
            def drain(r, c, e=e):
                _row_copy(zrow, 0, xs_ref, ps_ref[e] + r, zsem).wait()
                return c

            lax.fori_loop(0, pl_ref[e], drain, 0)

    buf[...] = hn_ref[...].astype(F32)

    def issue(j, c):
        for s in range(TOP_K):
            _row_copy(buf, j, xs_ref, pos_ref[0, s * tm + j], sem).start(priority=s)
        return c

    lax.fori_loop(0, tm, issue, 0, unroll=8)
    for _ in range(TOP_K):
        pltpu.make_async_copy(buf, xs_ref.at[pl.ds(0, tm)], sem).wait()


def _dispatch(pad_start, pad_len, pos, hn, n_rows):
    T = hn.shape[0]
    tm = ROUTE_TILE
    return pl.pallas_call(
        _dispatch_kernel,
        out_shape=jax.ShapeDtypeStruct((n_rows, D_MODEL), F32),
        grid_spec=pltpu.PrefetchScalarGridSpec(
            num_scalar_prefetch=2,
            grid=(T // tm,),
            in_specs=[pl.BlockSpec((None, 1, TOP_K * tm), lambda i, ps, pn: (i, 0, 0), memory_space=pltpu.SMEM),
                      pl.BlockSpec((tm, D_MODEL), lambda i, ps, pn: (i, 0))],
            out_specs=pl.BlockSpec(memory_space=pl.ANY),
            scratch_shapes=[pltpu.VMEM((tm, D_MODEL), F32), pltpu.VMEM((SUBLANE, D_MODEL), F32),
                            pltpu.SemaphoreType.DMA, pltpu.SemaphoreType.DMA]),
        compiler_params=_cparams("arbitrary"),
        name="moe_dispatch",
    )(pad_start, pad_len, pos, hn)


def _moe_ffn_kernel(te_ref, nu_ref, x_ref, wg_ref, wu_ref, wd_ref, o_ref):
    del te_ref
    i = pl.program_id(0)

    @pl.when(i < nu_ref[0])
    def _():
        o_ref[...] = _swiglu_rows(x_ref[...].astype(BF16), wg_ref, wu_ref, wd_ref)

    @pl.when(i >= nu_ref[0])
    def _():
        o_ref[...] = jnp.zeros(o_ref.shape, F32)


def _moe_ffn(tile_expert, n_used, xs, wg, wu, wd):
    n_rows = xs.shape[0]
    tm = MOE_TILE

    def row_map(i, te, nu):
        return (jnp.maximum(jnp.minimum(i, nu[0] - 1), 0), 0)

    def expert_map(i, te, nu):
        return (te[i], 0, 0)

    return pl.pallas_call(
        _moe_ffn_kernel,
        out_shape=jax.ShapeDtypeStruct((n_rows, D_MODEL), F32),
        grid_spec=pltpu.PrefetchScalarGridSpec(
            num_scalar_prefetch=2,
            grid=(n_rows // tm,),
            in_specs=[pl.BlockSpec((tm, D_MODEL), row_map),
                      _resident((None, D_MODEL, D_FF), expert_map),
                      _resident((None, D_MODEL, D_FF), expert_map),
                      _resident((None, D_FF, D_MODEL), expert_map)],
            out_specs=pl.BlockSpec((tm, D_MODEL), lambda i, te, nu: (i, 0))),
        compiler_params=_cparams("arbitrary"),
        name="moe_ffn",
    )(tile_expert, n_used, xs, wg, wu, wd)


def _combine_kernel(pos_ref, meta_ref, h_ref, yb_ref, o_ref, buf0, buf1, sems):
    tm = h_ref.shape[0]
    bufs = (buf0, buf1)

    def issue(j, c):
        for s in range(TOP_K):
            _row_copy(yb_ref, pos_ref[0, s * tm + j], bufs[s], j, sems.at[s]).start(priority=s)
        return c

    lax.fori_loop(0, tm, issue, 0, unroll=8)
    meta = meta_ref[...]
    lane = lax.broadcasted_iota(I32, meta.shape, 1)
    g0 = jnp.sum(jnp.where(lane == _RT_G0, meta, 0.0), axis=-1, keepdims=True)
    g1 = jnp.sum(jnp.where(lane == _RT_G1, meta, 0.0), axis=-1, keepdims=True)
    for s in range(TOP_K):
        pltpu.make_async_copy(yb_ref.at[pl.ds(0, tm)], bufs[s], sems.at[s]).wait()
    o_ref[...] = h_ref[...] + (g0 * buf0[...] + g1 * buf1[...])


def _combine(pos, meta, h2, yb):
    T = h2.shape[0]
    tm = ROUTE_TILE
    row = pl.BlockSpec((tm, D_MODEL), lambda i: (i, 0))
    return pl.pallas_call(
        _combine_kernel,
        out_shape=jax.ShapeDtypeStruct((T, D_MODEL), F32),
        grid=(T // tm,),
        in_specs=[pl.BlockSpec((None, 1, TOP_K * tm), lambda i: (i, 0, 0), memory_space=pltpu.SMEM),
                  pl.BlockSpec((tm, LANE), lambda i: (i, 0)), row,
                  pl.BlockSpec(memory_space=pl.ANY)],
        out_specs=row,
        scratch_shapes=[pltpu.VMEM((tm, D_MODEL), F32), pltpu.VMEM((tm, D_MODEL), F32),
                        pltpu.SemaphoreType.DMA((TOP_K,))],
        compiler_params=_cparams("arbitrary"),
        name="moe_combine",
    )(pos, meta, h2, yb)


def _moe(hn, h2, meta, cnt, wg, wu, wd):
    T, D = h2.shape
    counts = cnt[0, :N_EXPERTS].astype(I32)
    padded = (counts + MOE_TILE - 1) // MOE_TILE * MOE_TILE
    ends = jnp.cumsum(padded)
    starts = ends - padded
    n_rows = TOP_K * T + N_EXPERTS * MOE_TILE
    n_tiles = n_rows // MOE_TILE
    experts = jnp.arange(N_EXPERTS, dtype=I32)

    def slot(e_lane, r_lane):
        e = meta[:, e_lane].astype(I32)
        start = jnp.sum(jnp.where(e[:, None] == experts[None, :], starts[None, :], 0), axis=1)
        return start + meta[:, r_lane].astype(I32)

    nt = T // ROUTE_TILE
    pos = jnp.concatenate([slot(_RT_E0, _RT_R0).reshape(nt, ROUTE_TILE),
                           slot(_RT_E1, _RT_R1).reshape(nt, ROUTE_TILE)], axis=1)[:, None, :]
    tile_start = jnp.arange(n_tiles, dtype=I32) * MOE_TILE
    tile_expert = jnp.minimum(jnp.sum(ends[None, :] <= tile_start[:, None], axis=1), N_EXPERTS - 1).astype(I32)
    n_used = (ends[-1:] // MOE_TILE).astype(I32)
    pad_start = starts + counts
    pad_end = jnp.where(experts == N_EXPERTS - 1, n_rows, ends)
    xs = _dispatch(pad_start.astype(I32), (pad_end - pad_start).astype(I32), pos, hn, n_rows)
    yb = _moe_ffn(tile_expert, n_used, xs, wg.astype(BF16), wu.astype(BF16), wd.astype(BF16))
    return _combine(pos, meta, h2, yb)


def _block_diag_ones(n, blk):
    i = np.arange(n)
    return (i[:, None] // blk == i[None, :] // blk).astype(np.float32)


def _consts():
    n4 = 4 * CHUNK
    i = np.arange(n4)
    bd = _block_diag_ones(n4, CHUNK)
    tr, tc = i[:, None] % CHUNK, i[None, :] % CHUNK
    strict = bd * (tr > tc)
    incl = bd * (tr >= tc)
    ltri = np.tril(np.ones((CHUNK, CHUNK), np.float32))
    pair_mean = _block_diag_ones(LANE, HEAD_DIM) / HEAD_DIM
    qmask = (i[:, None] // CHUNK == np.arange(LANE)[None, :] // GLA_KEY_DIM).astype(np.float32)
    amask = (i[:, None] % CHUNK >= np.arange(CHUNK)[None, :]).astype(np.float32)
    out = {k: jnp.asarray(v, F32) for k, v in dict(
        bd=bd, strict=strict, incl=incl, ltri=ltri, pair_mean=pair_mean, qmask=qmask, amask=amask).items()}
    out["route_tri"] = jnp.asarray(np.tril(np.ones((ROUTE_TILE, ROUTE_TILE), np.float32), -1), BF16)
    return out


def _pad_rows(w, row0, total):
    return jnp.zeros((total, w.shape[1]), F32).at[row0:row0 + w.shape[0]].set(w.astype(F32))


def _pack_rows(rows, width, n_rows):
    out = jnp.zeros((n_rows, width), F32)
    for i, r in enumerate(rows):
        r = r.reshape(-1).astype(F32)
        out = out.at[i, :r.shape[0]].set(r)
    return out


def kernel(x, rel_bias, mix_norm_g, w_in, w_out, ffn_norm_g, attn_q_norm_g, attn_k_norm_g, conv_w, conv_b, conv_ln_g, conv_ln_b, rwkv_mu, rwkv_w0, rwkv_w2, rwkv_a0, rwkv_a2, rwkv_g2, rwkv_k_k, rwkv_k_a, rwkv_r_k, rwkv_ln_g, rwkv_ln_b, gla_g2, gla_gb, gla_norm_g, ffn_wg, ffn_wu, ffn_wd, moe_router, moe_wg, moe_wu, moe_wd):
    B, S, D = x.shape
    T = B * S
    cs = _consts()
    bias = _attn_bias_tables(rel_bias)
    h = x.reshape(T, D)
    for layer in range(DEPTH):
        w = w_in[layer]
        dq, dk, dv, dg, do = (w[:, 2176:2304], w[:, 2304:2432], w[:, 2432:2688], w[:, 2688:2704], w[:, 2704:2960])
        wp = jnp.concatenate([w[:, :2176], dq, dv, do, dk, dg, jnp.zeros((D, IN_WP - IN_W), w.dtype)],
                             axis=1).astype(BF16)
        z = _inproj(h, mix_norm_g[layer][None, :], wp)
        z3 = z.reshape(B, S, IN_WP)

        qg = jnp.tile(attn_q_norm_g[layer].astype(F32) * (HEAD_DIM ** -0.5), 2)[None, :]
        kg = jnp.tile(attn_k_norm_g[layer].astype(F32), 2)[None, :]
        ya = _attn_mixer(z3, bias, qg, kg, cs["pair_mean"]).reshape(T, GROUP_W)

        conv_pv = _pack_rows([conv_b[layer], conv_ln_g[layer], conv_ln_b[layer]], GROUP_W, 8)
        group_mean = cs["bd"] * (1.0 / HEAD_DIM)
        mu = rwkv_mu[layer]
        rw_pv = _pack_rows([mu[0:256], mu[256:512], mu[512:768], mu[768:896], rwkv_w0[layer], rwkv_a0[layer],
                            rwkv_k_k[layer], rwkv_k_a[layer], rwkv_r_k[layer], rwkv_ln_g[layer],
                            rwkv_ln_b[layer]], GROUP_W, 16)
        gla_pv = _pack_rows([gla_gb[layer], jnp.tile(gla_norm_g[layer], GROUP_W // HEAD_DIM)], GROUP_W, 8)
        yc, yd, yb = _mixers(z3, [
            _rwkv_part(z3, rw_pv, _pad_rows(rwkv_w2[layer], 0, LANE), _pad_rows(rwkv_a2[layer], 32, LANE),
                       _pad_rows(rwkv_g2[layer], 64, LANE), cs["bd"], cs["strict"], cs["incl"], cs["ltri"]),
            _gla_part(z3, _pad_rows(gla_g2[layer], 0, LANE), gla_pv, cs["bd"], cs["qmask"], cs["amask"],
                      cs["ltri"]),
            _conv_part(z3, conv_w[layer].astype(F32), conv_pv, group_mean)])

        norm_g = ffn_norm_g[layer][None, :]
        ys = (ya, yb.reshape(T, GROUP_W), yc.reshape(T, GROUP_W), yd.reshape(T, GROUP_W))
        i = layer // 2
        if layer % 2 == 0:
            h, hn = _outproj(ys, w_out[layer].astype(BF16), h, norm_g)
            h = _ffn(hn, h, ffn_wg[i].astype(BF16), ffn_wu[i].astype(BF16), ffn_wd[i].astype(BF16))
        else:
            wr = jnp.zeros((D, LANE), F32).at[:, :N_EXPERTS].set(moe_router[i].astype(F32))
            wr_hi = wr.astype(BF16)
            wr = jnp.concatenate([wr_hi, (wr - wr_hi.astype(F32)).astype(BF16)], axis=1)
            h, hn, meta, cnt = _outproj(ys, w_out[layer].astype(BF16), h, norm_g, router=(wr, cs["route_tri"]))
            h = _moe(hn, h, meta, cnt, moe_wg[i], moe_wu[i], moe_wd[i])
    return h.reshape(B, S, D)
```

```python
import functools
import math

import numpy as np
import jax
import jax.numpy as jnp
from jax import lax
from jax.experimental import pallas as pl
from jax.experimental.pallas import tpu as pltpu

F32 = jnp.float32
BF16 = jnp.bfloat16
I32 = jnp.int32
HI = lax.Precision.HIGHEST

D_MODEL = 1024
DEPTH = 2
GROUP_W = 256
NORM_EPS = 1e-6
HEAD_DIM = 64
DILATED_PATTERNS = ((128, 1), (512, 4), (2048, 16))
ATT_BLOCK = 128
N_BUCKETS = 32
REL_MAX_DIST = 2048
CONV_WIDTH = 31
CONV_HALO = 32
CONV_LN_EPS = 1e-5
RWKV_LN_EPS = 64e-5
GLA_KEY_DIM = 32
GLA_TAU = 16.0
CHUNK = 64
D_FF = 2816
N_EXPERTS = 8
TOP_K = 2
IN_W = 2960
IN_WP = 3072
LANE = 128
SUBLANE = 8
VMEM_LIMIT = 48 * 1024 * 1024
MOE_TILE = 512
FFN_ROWS = 512
ROUTE_TILE = 512

OFF_AQ, OFF_AK, OFF_AV = 0, 256, 512
OFF_BU, OFF_BG = 768, 1024
OFF_CR, OFF_CK, OFF_CV, OFF_CL = 1280, 1536, 1792, 2048
OFF_DQ, OFF_DV, OFF_DO, OFF_DK, OFF_DG = 2176, 2304, 2560, 2816, 2944


def _cparams(*sem):
    return pltpu.CompilerParams(dimension_semantics=sem, vmem_limit_bytes=VMEM_LIMIT)


def _dot(a, b, prec=None):
    return jnp.dot(a, b, preferred_element_type=F32, precision=prec)


def _mm(a, b):
    return jnp.dot(a.astype(BF16), b.astype(BF16), preferred_element_type=F32)


def _mm_nt(a, b):
    return lax.dot_general(a.astype(BF16), b.astype(BF16), (((1,), (1,)), ((), ())),
                           preferred_element_type=F32)


def _mm_tn(a, b):
    return lax.dot_general(a.astype(BF16), b.astype(BF16), (((0,), (0,)), ((), ())),
                           preferred_element_type=F32)


def _split3(x):
    x1 = x.astype(BF16)
    r1 = x - x1.astype(F32)
    x2 = r1.astype(BF16)
    x3 = (r1 - x2.astype(F32)).astype(BF16)
    return x1, x2, x3


def _dot_exact_rhs(x, m):
    n = x.shape[0]
    y = _dot(jnp.concatenate(_split3(x), axis=0), m)
    return y[0:n] + y[n:2 * n] + y[2 * n:3 * n]


def _dot_exact_lhs(m, x):
    n = x.shape[1]
    y = _dot(m, jnp.concatenate(_split3(x), axis=1))
    return y[:, 0:n] + y[:, n:2 * n] + y[:, 2 * n:3 * n]


def _each(fn, *lists):
    return [fn(*args) for args in zip(*lists)]


def _sigmoid(x):
    return 1.0 / (1.0 + jnp.exp(-x))


def _softplus(x):
    return jnp.maximum(x, 0.0) + jnp.log(1.0 + jnp.exp(-jnp.abs(x)))


def _resident(shape, index_map):
    return pl.BlockSpec(shape, index_map, pipeline_mode=pl.Buffered(1))


def _inproj_kernel(x_ref, g_ref, w_ref, z_ref):
    x = x_ref[...]
    ms = jnp.mean(x * x, axis=-1, keepdims=True)
    xn = (x * lax.rsqrt(ms + NORM_EPS) * g_ref[...]).astype(BF16)
    z_ref[...] = _dot(xn, w_ref[...])


def _inproj(h2, g, w):
    T = h2.shape[0]
    tm = FFN_ROWS
    return pl.pallas_call(
        _inproj_kernel,
        out_shape=jax.ShapeDtypeStruct((T, IN_WP), F32),
        grid=(T // tm,),
        in_specs=[pl.BlockSpec((tm, D_MODEL), lambda i: (i, 0)),
                  pl.BlockSpec((1, D_MODEL), lambda i: (0, 0)),
                  _resident((D_MODEL, IN_WP), lambda i: (0, 0))],
        out_specs=pl.BlockSpec((tm, IN_WP), lambda i: (i, 0)),
        compiler_params=_cparams("parallel"),
        name="inproj",
    )(h2, g, w)


def _t5_bucket(dist):
    max_exact = N_BUCKETS // 2
    n = np.maximum(dist, 0)
    large = max_exact + (np.log(np.maximum(n, 1) / max_exact) / math.log(REL_MAX_DIST / max_exact)
                         * (N_BUCKETS - max_exact)).astype(np.int32)
    large = np.minimum(large, N_BUCKETS - 1)
    return np.where(n < max_exact, n, large).astype(np.int32)


def _bucket_table():
    W = ATT_BLOCK
    delta = np.arange(W)[:, None] + W - np.arange(2 * W)[None, :]
    band = (delta >= 0) & (delta <= W)
    tabs = [np.where(band, _t5_bucket(np.clip(delta, 0, W) * d), -1) for _, d in DILATED_PATTERNS]
    return jnp.asarray(np.stack(tabs), I32)


def _bias_kernel(rb_ref, bk_ref, o_ref):
    n_heads = o_ref.shape[1]
    for di in range(len(DILATED_PATTERNS)):
        bk = bk_ref[di]
        accs = [jnp.full(bk.shape, -jnp.inf, F32) for _ in range(n_heads)]
        for b in range(N_BUCKETS):
            hit = bk == b
            accs = [jnp.where(hit, rb_ref[b, hh], a) for hh, a in enumerate(accs)]
        for hh in range(n_heads):
            o_ref[di, hh] = accs[hh]


def _attn_bias_tables(rel_bias):
    n_heads = rel_bias.shape[1]
    shape = (len(DILATED_PATTERNS), n_heads, ATT_BLOCK, 2 * ATT_BLOCK)
    return pl.pallas_call(
        _bias_kernel,
        out_shape=jax.ShapeDtypeStruct(shape, F32),
        in_specs=[pl.BlockSpec(memory_space=pltpu.SMEM), pl.BlockSpec(memory_space=pltpu.VMEM)],
        out_specs=pl.BlockSpec(memory_space=pltpu.VMEM),
        name="attn_bias",
    )(rel_bias.astype(F32), _bucket_table())


ATT_MERGE_ROWS = 256
ATT_UNROLL = 4


def _attn_kernel(q_ref, k_ref, v_ref, bias_ref, qg_ref, kg_ref, gm_ref, o_ref,
                 qn, kn, qd, kd, vd, od, ld, acc, lse):
    S = q_ref.shape[0]
    rows = 512

    def norm_body(i, c):
        sl = pl.ds(pl.multiple_of(i * rows, rows), rows)
        q = q_ref[sl, :]
        k = k_ref[sl, :]
        gm16 = gm_ref[...].astype(BF16)
        qn[sl, :] = q * lax.rsqrt(_dot_exact_rhs(q * q, gm16) + NORM_EPS) * qg_ref[...]
        kn[sl, :] = k * lax.rsqrt(_dot_exact_rhs(k * k, gm16) + NORM_EPS) * kg_ref[...]
        return c

    lax.fori_loop(0, S // rows, norm_body, 0)

    lane = lax.broadcasted_iota(I32, (ATT_BLOCK, LANE), 1)
    head0 = lane < HEAD_DIM
    col = lax.broadcasted_iota(I32, (ATT_BLOCK, 2 * ATT_BLOCK), 1)

    def run_blocks(di, nb, qs, ks, vs, o_dst, l_dst):
        nu = min(nb, ATT_UNROLL)

        def blk(n):
            return pl.ds(pl.multiple_of(n * ATT_BLOCK, ATT_BLOCK), ATT_BLOCK)

        def body(i, c):
            base = i * nu
            rows = [blk(jnp.maximum(base - 1, 0))] + [blk(base + u) for u in range(nu)]
            kb = [ks[r, :].astype(BF16) for r in rows]
            vb = [vs[r, :].astype(BF16) for r in rows]
            kt = [jnp.concatenate([kb[u], kb[u + 1]], axis=0) for u in range(nu)]
            vt = [jnp.concatenate([vb[u], vb[u + 1]], axis=0) for u in range(nu)]
            qt = [qs[r, :] for r in rows[1:]]
            heads = [(u, hh) for u in range(nu) for hh in range(2)]
            s = [_mm_nt(jnp.where(head0 if hh == 0 else jnp.logical_not(head0), qt[u], 0.0), kt[u])
                 + bias_ref[di, hh] for u, hh in heads]
            no_prev = jnp.logical_and(i == 0, col < ATT_BLOCK)
            s = [jnp.where(no_prev, -jnp.inf, x) if u == 0 else x for (u, hh), x in zip(heads, s)]
            m = [jnp.max(x, axis=-1, keepdims=True) for x in s]
            p = _each(lambda x, mx: jnp.exp(x - mx), s, m)
            den = [jnp.sum(x, axis=-1, keepdims=True) for x in p]
            o = [_mm(x, vt[u]) / dn for (u, hh), x, dn in zip(heads, p, den)]
            ls = _each(lambda mx, dn: mx + jnp.log(dn), m, den)
            for u in range(nu):
                o_dst[rows[u + 1], :] = jnp.where(head0, o[2 * u], o[2 * u + 1])
                l_dst[rows[u + 1], :] = jnp.where(head0, ls[2 * u], ls[2 * u + 1])
            return c

        lax.fori_loop(0, nb // nu, body, 0)

    for di, (_, d) in enumerate(DILATED_PATTERNS):
        L = S // d
        nb = L // ATT_BLOCK
        if d == 1:
            run_blocks(di, nb, qn, kn, v_ref, acc, lse)
            continue
        for r in range(d):
            res = pl.ds(r, L, stride=d)
            qd[0:L, :] = qn[res, :]
            kd[0:L, :] = kn[res, :]
            vd[0:L, :] = v_ref[res, :]
            run_blocks(di, nb, qd, kd, vd, od, ld)
            mr = min(L, ATT_MERGE_ROWS)
            for c in range(L // mr):
                piece = pl.ds(r + c * mr * d, mr, stride=d)
                l0, a0 = lse[piece, :], acc[piece, :]
                l1, a1 = ld[c * mr:(c + 1) * mr, :], od[c * mr:(c + 1) * mr, :]
                m = jnp.maximum(l0, l1)
                e0, e1 = jnp.exp(l0 - m), jnp.exp(l1 - m)
                tot = e0 + e1
                acc[piece, :] = (a0 * e0 + a1 * e1) / tot
                lse[piece, :] = m + jnp.log(tot)

    def out_body(i, c):
        sl = pl.ds(pl.multiple_of(i * rows, rows), rows)
        o_ref[sl, :] = acc[sl, :].astype(BF16)
        return c

    lax.fori_loop(0, S // rows, out_body, 0)


def _attn_mixer(z3, bias, qg, kg, gm):
    B, S, _ = z3.shape
    n_lb = GROUP_W // LANE
    l_max = S // DILATED_PATTERNS[1][1]

    def spec(off):
        return pl.BlockSpec((None, S, LANE), lambda b, p: (b, 0, off // LANE + p))

    vec = pl.BlockSpec((1, LANE), lambda b, p: (0, 0))
    full = pltpu.VMEM((S, LANE), F32)
    part = pltpu.VMEM((l_max, LANE), F32)
    return pl.pallas_call(
        _attn_kernel,
        out_shape=jax.ShapeDtypeStruct((B, S, GROUP_W), BF16),
        grid=(B, n_lb),
        in_specs=[spec(OFF_AQ), spec(OFF_AK), spec(OFF_AV),
                  pl.BlockSpec((len(DILATED_PATTERNS), 2, ATT_BLOCK, 2 * ATT_BLOCK), lambda b, p: (0, p, 0, 0)),
                  vec, vec, pl.BlockSpec((LANE, LANE), lambda b, p: (0, 0))],
        out_specs=pl.BlockSpec((None, S, LANE), lambda b, p: (b, 0, p)),
        scratch_shapes=[full, full, part, part, part, part, part, full, full],
        compiler_params=_cparams("parallel", "parallel"),
        name="attn",
    )(z3, z3, z3, bias, qg, kg, gm)


def _conv_kernel(u_ref, g_ref, cw_ref, pv_ref, gm_ref, o_ref, hbuf, hsh, *, carry_phase):
    ts = u_ref.shape[0]
    t = pl.program_id(1)

    if carry_phase:
        @pl.when(t == 0)
        def _():
            hbuf[0:CONV_HALO, :] = jnp.zeros((CONV_HALO, GROUP_W), F32)

        @pl.when(t > 0)
        def _():
            hbuf[0:CONV_HALO, :] = hbuf[ts:ts + CONV_HALO, :]

        return

    hbuf[CONV_HALO:CONV_HALO + ts, :] = u_ref[...] * _sigmoid(g_ref[...])
    first = CONV_HALO - (CONV_WIDTH - 1)
    n_shift = hsh.shape[1]
    for ph in range(1, SUBLANE):
        hsh[ph - 1, :, :] = hbuf[ph:ph + n_shift, :]
    rows = 64
    gm16 = gm_ref[...].astype(BF16)
    for c in range(ts // rows):
        acc = jnp.zeros((rows, GROUP_W), F32) + pv_ref[0:1, :]
        for j in range(CONV_WIDTH):
            ph = (first + j) % SUBLANE
            s = first + j - ph + c * rows
            tap = hbuf[s:s + rows, :] if ph == 0 else hsh[ph - 1, s:s + rows, :]
            acc = acc + cw_ref[j:j + 1, :] * tap
        mu = _dot_exact_rhs(acc, gm16)
        xc = acc - mu
        var = _dot_exact_rhs(xc * xc, gm16)
        y = xc * lax.rsqrt(var + CONV_LN_EPS) * pv_ref[1:2, :] + pv_ref[2:3, :]
        o_ref[c * rows:(c + 1) * rows, :] = (y * _sigmoid(y)).astype(BF16)


MIX_ROWS = 512


def _seg(off, w):
    return pl.BlockSpec((None, MIX_ROWS, w), lambda b, t: (b, t, off // w))


def _const(shape):
    return pl.BlockSpec(shape, lambda b, t: (0, 0))


def _conv_part(z3, conv_w, pvec, gm):
    ts = MIX_ROWS
    return dict(kernel=_conv_kernel,
                in_specs=[_seg(OFF_BU, GROUP_W), _seg(OFF_BG, GROUP_W), _const((CONV_WIDTH, GROUP_W)),
                          _const((8, GROUP_W)), _const((GROUP_W, GROUP_W))],
                args=(z3, z3, conv_w, pvec, gm),
                scratch=[pltpu.VMEM((ts + CONV_HALO, GROUP_W), F32),
                         pltpu.VMEM((SUBLANE - 1, ts + CONV_HALO - SUBLANE, GROUP_W), F32)])


def _mixers_kernel(*refs, parts):
    n_in = sum(p[1] for p in parts)
    ins, outs, scr = refs[:n_in], refs[n_in:n_in + len(parts)], refs[n_in + len(parts):]
    for carry_phase in (True, False):
        i = s = 0
        for k, (fn, ni, ns) in enumerate(parts):
            fn(*ins[i:i + ni], outs[k], *scr[s:s + ns], carry_phase=carry_phase)
            i += ni
            s += ns


def _mixers(z3, parts):
    B, S, _ = z3.shape
    out_spec = pl.BlockSpec((None, MIX_ROWS, GROUP_W), lambda b, t: (b, t, 0))
    meta = tuple((p["kernel"], len(p["in_specs"]), len(p["scratch"])) for p in parts)
    return pl.pallas_call(
        functools.partial(_mixers_kernel, parts=meta),
        out_shape=[jax.ShapeDtypeStruct((B, S, GROUP_W), BF16)] * len(parts),
        grid=(B, S // MIX_ROWS),
        in_specs=[sp for p in parts for sp in p["in_specs"]],
        out_specs=[out_spec] * len(parts),
        scratch_shapes=[sc for p in parts for sc in p["scratch"]],
        compiler_params=_cparams("parallel", "arbitrary"),
        name="mixers_bcd",
    )(*[a for p in parts for a in p["args"]])


_RW_MU_R, _RW_MU_K, _RW_MU_V, _RW_MU_L, _RW_W0, _RW_A0, _RW_KK, _RW_KA, _RW_RK, _RW_LNG, _RW_LNB = range(11)
SHIFT_PAD = 8
RWKV_GROUP = 8
GLA_GROUP = 8


def _rwkv_kernel(r_ref, k_ref, v_ref, l_ref, pv_ref, w2_ref, a2_ref, g2_ref, bd_ref, st_ref, in_ref, lt_ref,
                 o_ref, rbuf, kbuf, vbuf, lbuf, lw_s, kk_s, be_s, km_s, rr_s, vv_s, gg_s, bo_s, sm_ref, *,
                 carry_phase):
    ts = r_ref.shape[0]
    t = pl.program_id(1)
    bufs = ((rbuf, r_ref), (kbuf, k_ref), (vbuf, v_ref), (lbuf, l_ref))

    if carry_phase:
        @pl.when(t == 0)
        def _():
            for buf, _ in bufs:
                buf[0:SHIFT_PAD, :] = jnp.zeros((SHIFT_PAD, buf.shape[1]), F32)
            sm_ref[...] = jnp.zeros(sm_ref.shape, F32)

        @pl.when(t > 0)
        def _():
            for buf, _ in bufs:
                buf[0:SHIFT_PAD, :] = buf[ts:ts + SHIFT_PAD, :]

        return

    for buf, ref in bufs:
        buf[SHIFT_PAD:SHIFT_PAD + ts, :] = ref[...]

    def pv(i, w=GROUP_W):
        return pv_ref[i:i + 1, 0:w]

    def shifted(buf, mu):
        cur = buf[SHIFT_PAD:SHIFT_PAD + ts, :]
        prev = buf[SHIFT_PAD - 1:SHIFT_PAD - 1 + ts, :]
        return cur + (prev - cur) * mu

    bd = bd_ref[...]
    bd16 = bd.astype(BF16)
    r = shifted(rbuf, pv(_RW_MU_R))
    k = shifted(kbuf, pv(_RW_MU_K))
    v = shifted(vbuf, pv(_RW_MU_V))
    lo = shifted(lbuf, pv(_RW_MU_L, LANE))
    w_log = -_softplus(-(pv(_RW_W0) + _dot(jnp.tanh(lo), w2_ref[...], HI))) - 0.5
    a = _sigmoid(pv(_RW_A0) + _dot(lo, a2_ref[...], HI))
    kk = k * pv(_RW_KK)
    kk = kk / jnp.maximum(jnp.sqrt(_dot_exact_rhs(kk * kk, bd16)), 1e-12)
    km = k * (1.0 + (a - 1.0) * pv(_RW_KA))
    lw_s[...] = -jnp.exp(w_log)
    kk_s[...] = kk
    be_s[...] = kk * a
    km_s[...] = km
    rr_s[...] = r
    vv_s[...] = v
    gg_s[...] = _dot(_sigmoid(lo), g2_ref[...], HI)
    bo_s[...] = _dot_exact_rhs(r * km * pv(_RW_RK), bd16) * v

    n2 = LANE // HEAD_DIM * CHUNK
    bd2 = bd[0:n2, 0:LANE]
    strict = st_ref[0:n2, 0:n2]
    incl = in_ref[0:n2, 0:n2]
    eye = incl - strict
    lt16 = lt_ref[...].astype(BF16)
    gmean16 = (bd2 * (1.0 / HEAD_DIM)).astype(BF16)
    n_pairs = GROUP_W // LANE

    def tile2(x):
        return jnp.concatenate([x] * (LANE // HEAD_DIM), axis=0)

    def fold(x):
        return x[0:CHUNK] + x[CHUNK:2 * CHUNK]

    def group(gi, carry):
        rows = [pl.ds(pl.multiple_of((gi * RWKV_GROUP + j) * CHUNK, CHUNK), CHUNK) for j in range(RWKV_GROUP)]
        items = [(s, slice(p * LANE, (p + 1) * LANE)) for s in rows for p in range(n_pairs)]
        lw = [lw_s[s, ln] for s, ln in items]
        cum = _each(lambda x: _dot_exact_lhs(lt16, x), lw)
        last = [c[CHUNK - 1:CHUNK, :] for c in cum]
        e_neg = [jnp.exp(-c) for c in cum]
        e_last = _each(lambda l, c: jnp.exp(l - c), last, cum)
        bec = [be_s[s, ln] for s, ln in items]
        kmc = [km_s[s, ln] for s, ln in items]
        a_bar = _each(lambda it, c, w: tile2(-kk_s[it[0], it[1]] * jnp.exp(c - w)) * bd2, items, cum, lw)
        r_bar = _each(lambda it, c: tile2(rr_s[it[0], it[1]] * jnp.exp(c)) * bd2, items, cum)
        v_blk = [tile2(vv_s[s, ln]) * bd2 for s, ln in items]
        big = _each(lambda ab, rb, be, km_, en: _mm_nt(
            jnp.concatenate([ab, rb], axis=0),
            jnp.concatenate([tile2(be * en), tile2(km_ * en)], axis=0)), a_bar, r_bar, bec, kmc, e_neg)
        nmat = [b[0:n2, 0:n2] * strict for b in big]
        a_ak = [b[0:n2, n2:2 * n2] * strict for b in big]
        a_rb = [b[n2:2 * n2, 0:n2] * incl for b in big]
        a_rk = [b[n2:2 * n2, n2:2 * n2] * incl for b in big]
        tinv = [eye + n for n in nmat]
        npow = nmat
        for _ in range(5):
            npow = _each(lambda n: _mm(n, n), npow)
            tinv = _each(lambda t_, n: t_ + _mm(t_, n), tinv, npow)
        w_blk = _each(_mm, tinv, a_bar)
        u0 = _each(lambda t_, ak, vb: _mm(t_, _mm(ak, vb)), tinv, a_ak, v_blk)
        o0 = _each(lambda rb, u, rk, vb: fold(_mm(rb, u) + _mm(rk, vb)), a_rb, u0, a_rk, v_blk)
        q_eff = _each(lambda rbar, rb, w: fold(rbar + _mm(rb, w)), r_bar, a_rb, w_blk)
        gh = _each(lambda be, el, w, u: _mm_tn(tile2(be * el) * bd2, jnp.concatenate([w, u], axis=1)),
                   bec, e_last, w_blk, u0)
        g_mat = _each(lambda l, x: eye * jnp.exp(l) + x[:, 0:LANE], last, gh)
        h_mat = _each(lambda x, km_, el, vb: x[:, LANE:2 * LANE] + _mm_tn(tile2(km_ * el) * bd2, vb),
                      gh, kmc, e_last, v_blk)
        outs = [None] * len(items)
        sms = [sm_ref[p] for p in range(n_pairs)]
        for j in range(RWKV_GROUP):
            for p in range(n_pairs):
                i = j * n_pairs + p
                outs[i] = _mm(q_eff[i], sms[p]) + o0[i]
                sms[p] = _mm(g_mat[i], sms[p]) + h_mat[i]
        for p in range(n_pairs):
            sm_ref[p] = sms[p]
        for (s, ln), out in zip(items, outs):
            mu = _dot_exact_rhs(out, gmean16)
            xc = out - mu
            var = _dot_exact_rhs(xc * xc, gmean16)
            y = xc * lax.rsqrt(var + RWKV_LN_EPS) * pv_ref[_RW_LNG:_RW_LNG + 1, ln] + pv_ref[_RW_LNB:_RW_LNB + 1, ln]
            o_ref[s, ln] = ((y + bo_s[s, ln]) * gg_s[s, ln]).astype(BF16)
        return carry

    lax.fori_loop(0, ts // (CHUNK * RWKV_GROUP), group, 0)


def _rwkv_part(z3, pvec, w2p, a2p, g2p, bd, strict, incl, ltri):
    ts = MIX_ROWS
    sq = (4 * CHUNK, 4 * CHUNK)
    wide = pltpu.VMEM((ts, GROUP_W), F32)
    return dict(kernel=_rwkv_kernel,
                in_specs=[_seg(OFF_CR, GROUP_W), _seg(OFF_CK, GROUP_W), _seg(OFF_CV, GROUP_W), _seg(OFF_CL, LANE),
                          _const((16, GROUP_W)), _const((LANE, GROUP_W)), _const((LANE, GROUP_W)),
                          _const((LANE, GROUP_W)), _const(sq), _const(sq), _const(sq), _const((CHUNK, CHUNK))],
                args=(z3, z3, z3, z3, pvec, w2p, a2p, g2p, bd, strict, incl, ltri),
                scratch=[pltpu.VMEM((ts + SHIFT_PAD, GROUP_W), F32)] * 3
                        + [pltpu.VMEM((ts + SHIFT_PAD, LANE), F32)]
                        + [wide] * 8 + [pltpu.VMEM((GROUP_W // LANE, LANE // HEAD_DIM * CHUNK, LANE), F32)])


def _gla_kernel(q_ref, k_ref, v_ref, og_ref, gd_ref, g2_ref, pv_ref, bd_ref, qm_ref, am_ref, lt_ref, o_ref,
                gk_s, st_ref, *, carry_phase):
    ts = q_ref.shape[0]

    if carry_phase:
        @pl.when(pl.program_id(1) == 0)
        def _():
            st_ref[...] = jnp.zeros(st_ref.shape, F32)

        return

    x = _dot(gd_ref[...], g2_ref[...], HI) + pv_ref[0:1, 0:LANE]
    gk_s[...] = -_softplus(-x) * (1.0 / GLA_TAU)
    bd = bd_ref[...]
    qmask = qm_ref[...]
    n_heads = GROUP_W // HEAD_DIM

    lt16 = lt_ref[...].astype(BF16)
    gmean16 = (bd * (1.0 / HEAD_DIM)).astype(BF16)

    def fold(x):
        return x[0:CHUNK] + x[CHUNK:2 * CHUNK] + x[2 * CHUNK:3 * CHUNK] + x[3 * CHUNK:4 * CHUNK]

    def group(gi, carry):
        sls = [pl.ds(pl.multiple_of((gi * GLA_GROUP + j) * CHUNK, CHUNK), CHUNK) for j in range(GLA_GROUP)]
        b = [_dot_exact_lhs(lt16, gk_s[s, :]) for s in sls]
        last = [x[CHUNK - 1:CHUNK, :] for x in b]
        kc = [k_ref[s, :] for s in sls]
        vc = [v_ref[s, :] for s in sls]
        q_d = _each(lambda s, x: q_ref[s, :] * (GLA_KEY_DIM ** -0.5) * jnp.exp(x), sls, b)
        k_d = _each(lambda k_, x: k_ * jnp.exp(-x), kc, b)
        att = _each(lambda q_, k_: _mm_nt(jnp.concatenate([q_] * n_heads, axis=0) * qmask, k_) * am_ref[...],
                    q_d, k_d)
        o_in = _each(lambda a_, v_: fold(_mm(a_, v_) * bd), att, vc)
        upd = _each(lambda v_, k_, l, x: _mm_tn(v_, k_ * jnp.exp(l - x)) * qmask, vc, kc, last, b)
        st = st_ref[...]
        outs = []
        for j in range(GLA_GROUP):
            outs.append(o_in[j] + _mm_nt(q_d[j], st))
            st = st * jnp.exp(last[j]) + upd[j]
        st_ref[...] = st
        for s, o in zip(sls, outs):
            ms = _dot_exact_rhs(o * o, gmean16)
            og = og_ref[s, :]
            y = o * lax.rsqrt(ms + NORM_EPS) * pv_ref[1:2, :] * (og * _sigmoid(og))
            o_ref[s, :] = y.astype(BF16)
        return carry

    lax.fori_loop(0, ts // (CHUNK * GLA_GROUP), group, 0)


def _gla_part(z3, g2p, pvec, bd, qmask, amask, ltri):
    return dict(kernel=_gla_kernel,
                in_specs=[_seg(OFF_DQ, LANE), _seg(OFF_DK, LANE), _seg(OFF_DV, GROUP_W), _seg(OFF_DO, GROUP_W),
                          _seg(OFF_DG, LANE), _const((LANE, LANE)), _const((8, GROUP_W)),
                          _const((GROUP_W, GROUP_W)), _const((GROUP_W, LANE)), _const((GROUP_W, CHUNK)),
                          _const((CHUNK, CHUNK))],
                args=(z3, z3, z3, z3, z3, g2p, pvec, bd, qmask, amask, ltri),
                scratch=[pltpu.VMEM((MIX_ROWS, LANE), F32), pltpu.VMEM((GROUP_W, LANE), F32)])


_RT_E0, _RT_E1, _RT_G0, _RT_G1, _RT_R0, _RT_R1 = range(6)


def _route(xn, wr_ref, tri_ref, run_ref, meta_ref, cnt_ref):
    @pl.when(pl.program_id(0) == 0)
    def _():
        run_ref[...] = jnp.zeros(run_ref.shape, F32)

    tm = xn.shape[0]
    x_hi = xn.astype(BF16)
    x_lo = (xn - x_hi.astype(F32)).astype(BF16)
    y = _dot(jnp.concatenate([x_hi, x_lo], axis=0), wr_ref[...])
    logits = (y[0:tm, 0:LANE] + y[0:tm, LANE:2 * LANE]) + (y[tm:2 * tm, 0:LANE] + y[tm:2 * tm, LANE:2 * LANE])
    lane = lax.broadcasted_iota(I32, logits.shape, 1).astype(F32)
    logits = jnp.where(lane < N_EXPERTS, logits, -jnp.inf)
    m1 = jnp.max(logits, axis=-1, keepdims=True)
    i1 = jnp.min(jnp.where(logits == m1, lane, float(LANE)), axis=-1, keepdims=True)
    rest = jnp.where(lane == i1, -jnp.inf, logits)
    m2 = jnp.max(rest, axis=-1, keepdims=True)
    i2 = jnp.min(jnp.where(rest == m2, lane, float(LANE)), axis=-1, keepdims=True)
    e2 = jnp.exp(m2 - m1)
    den = 1.0 + e2
    hit1 = lane == i1
    hit2 = lane == i2
    member = jnp.where(jnp.logical_or(hit1, hit2), 1.0, 0.0)
    run = run_ref[0:1, :]
    rank = _dot(tri_ref[...], member.astype(BF16)) + run
    r1 = jnp.sum(jnp.where(hit1, rank, 0.0), axis=-1, keepdims=True)
    r2 = jnp.sum(jnp.where(hit2, rank, 0.0), axis=-1, keepdims=True)
    run = run + jnp.sum(member, axis=0, keepdims=True)
    run_ref[0:1, :] = run
    cnt_ref[...] = jnp.broadcast_to(run, cnt_ref.shape)
    rec = jnp.zeros(logits.shape, F32)
    for idx, val in ((_RT_E0, i1), (_RT_E1, i2), (_RT_G0, 1.0 / den), (_RT_G1, e2 / den), (_RT_R0, r1),
                     (_RT_R1, r2)):
        rec = jnp.where(lane == float(idx), val, rec)
    meta_ref[...] = rec


def _outproj_kernel(ya, yb, yc, yd, w_ref, h_ref, g_ref, *rest, route):
    if route:
        wr_ref, tri_ref, ho_ref, hn_ref, meta_ref, cnt_ref, run_ref = rest
    else:
        ho_ref, hn_ref = rest
    acc = h_ref[...]
    for i, y in enumerate((ya, yb, yc, yd)):
        acc = acc + _dot(y[...], w_ref[i * GROUP_W:(i + 1) * GROUP_W, :])
    ho_ref[...] = acc
    ms = jnp.mean(acc * acc, axis=-1, keepdims=True)
    xn = acc * lax.rsqrt(ms + NORM_EPS) * g_ref[...]
    hn_ref[...] = xn.astype(BF16)
    if route:
        _route(xn, wr_ref, tri_ref, run_ref, meta_ref, cnt_ref)


def _outproj(ys, w, h2, g, router=None):
    T = h2.shape[0]
    tm = ROUTE_TILE
    yspec = pl.BlockSpec((tm, GROUP_W), lambda i: (i, 0))
    row = pl.BlockSpec((tm, D_MODEL), lambda i: (i, 0))
    in_specs = [yspec] * 4 + [pl.BlockSpec((D_MODEL, D_MODEL), lambda i: (0, 0)), row,
                              pl.BlockSpec((1, D_MODEL), lambda i: (0, 0))]
    out_shape = [jax.ShapeDtypeStruct((T, D_MODEL), F32), jax.ShapeDtypeStruct((T, D_MODEL), BF16)]
    out_specs = [row, row]
    args = (*ys, w, h2, g)
    scratch = []
    if router is not None:
        in_specs += [pl.BlockSpec((D_MODEL, 2 * LANE), lambda i: (0, 0)), pl.BlockSpec((tm, tm), lambda i: (0, 0))]
        out_shape += [jax.ShapeDtypeStruct((T, LANE), F32), jax.ShapeDtypeStruct((SUBLANE, LANE), F32)]
        out_specs += [pl.BlockSpec((tm, LANE), lambda i: (i, 0)), pl.BlockSpec((SUBLANE, LANE), lambda i: (0, 0))]
        args += router
        scratch = [pltpu.VMEM((SUBLANE, LANE), F32)]
    return pl.pallas_call(
        functools.partial(_outproj_kernel, route=router is not None),
        out_shape=out_shape,
        grid=(T // tm,),
        in_specs=in_specs,
        out_specs=out_specs,
        scratch_shapes=scratch,
        compiler_params=_cparams("arbitrary"),
        name="outproj_route" if router is not None else "outproj",
    )(*args)


FFN_SPLIT = 2


def _swiglu_rows(x, wg_ref, wu_ref, wd_ref):
    piece = D_FF // FFN_SPLIT
    out = None
    for s in range(FFN_SPLIT):
        cols = slice(s * piece, (s + 1) * piece)
        g = _dot(x, wg_ref[:, cols])
        u = _dot(x, wu_ref[:, cols])
        y = _dot((g * _sigmoid(g) * u).astype(BF16), wd_ref[cols, :])
        out = y if out is None else out + y
    return out


def _ffn_kernel(x_ref, h_ref, wg_ref, wu_ref, wd_ref, o_ref):
    o_ref[...] = h_ref[...] + _swiglu_rows(x_ref[...], wg_ref, wu_ref, wd_ref)


def _ffn(hn, h2, wg, wu, wd):
    T = hn.shape[0]
    tm = FFN_ROWS
    row = pl.BlockSpec((tm, D_MODEL), lambda i: (i, 0))
    return pl.pallas_call(
        _ffn_kernel,
        out_shape=jax.ShapeDtypeStruct((T, D_MODEL), F32),
        grid=(T // tm,),
        in_specs=[row, row,
                  _resident((D_MODEL, D_FF), lambda i: (0, 0)),
                  _resident((D_MODEL, D_FF), lambda i: (0, 0)),
                  _resident((D_FF, D_MODEL), lambda i: (0, 0))],
        out_specs=row,
        compiler_params=_cparams("parallel"),
        name="ffn",
    )(hn, h2, wg, wu, wd)


def _row_copy(src_ref, src_row, dst_ref, dst_row, sem):
    return pltpu.make_async_copy(src_ref.at[pl.ds(src_row, 1)], dst_ref.at[pl.ds(dst_row, 1)], sem)


def _dispatch_kernel(ps_ref, pl_ref, pos_ref, hn_ref, xs_ref, buf, zrow, sem, zsem):
    tm = hn_ref.shape[0]

    @pl.when(pl.program_id(0) == 0)
    def _():
        zrow[...] = jnp.zeros(zrow.shape, F32)
        for e in range(N_EXPERTS):
            def fill(r, c, e=e):
                _row_copy(zrow, 0, xs_ref, ps_ref[e] + r, zsem).start()
                return c

            lax.fori_loop(0, pl_ref[e], fill, 0)
        for e in range(N_EXPERTS):
            def drain(r, c, e=e):
                _row_copy(zrow, 0, xs_ref, ps_ref[e] + r, zsem).wait()
                return c

            lax.fori_loop(0, pl_ref[e], drain, 0)

    buf[...] = hn_ref[...].astype(F32)

    def issue(j, c):
        for s in range(TOP_K):
            _row_copy(buf, j, xs_ref, pos_ref[0, s * tm + j], sem).start(priority=s)
        return c

    lax.fori_loop(0, tm, issue, 0, unroll=8)
    for _ in range(TOP_K):
        pltpu.make_async_copy(buf, xs_ref.at[pl.ds(0, tm)], sem).wait()


def _dispatch(pad_start, pad_len, pos, hn, n_rows):
    T = hn.shape[0]
    tm = ROUTE_TILE
    return pl.pallas_call(
        _dispatch_kernel,
        out_shape=jax.ShapeDtypeStruct((n_rows, D_MODEL), F32),
        grid_spec=pltpu.PrefetchScalarGridSpec(
            num_scalar_prefetch=2,
            grid=(T // tm,),
            in_specs=[pl.BlockSpec((None, 1, TOP_K * tm), lambda i, ps, pn: (i, 0, 0), memory_space=pltpu.SMEM),
                      pl.BlockSpec((tm, D_MODEL), lambda i, ps, pn: (i, 0))],
            out_specs=pl.BlockSpec(memory_space=pl.ANY),
            scratch_shapes=[pltpu.VMEM((tm, D_MODEL), F32), pltpu.VMEM((SUBLANE, D_MODEL), F32),
                            pltpu.SemaphoreType.DMA, pltpu.SemaphoreType.DMA]),
        compiler_params=_cparams("arbitrary"),
        name="moe_dispatch",
    )(pad_start, pad_len, pos, hn)


def _moe_ffn_kernel(te_ref, nu_ref, x_ref, wg_ref, wu_ref, wd_ref, o_ref):
    del te_ref
    i = pl.program_id(0)

    @pl.when(i < nu_ref[0])
    def _():
        o_ref[...] = _swiglu_rows(x_ref[...].astype(BF16), wg_ref, wu_ref, wd_ref)

    @pl.when(i >= nu_ref[0])
    def _():
        o_ref[...] = jnp.zeros(o_ref.shape, F32)


def _moe_ffn(tile_expert, n_used, xs, wg, wu, wd):
    n_rows = xs.shape[0]
    tm = MOE_TILE

    def row_map(i, te, nu):
        return (jnp.maximum(jnp.minimum(i, nu[0] - 1), 0), 0)

    def expert_map(i, te, nu):
        return (te[i], 0, 0)

    return pl.pallas_call(
        _moe_ffn_kernel,
        out_shape=jax.ShapeDtypeStruct((n_rows, D_MODEL), F32),
        grid_spec=pltpu.PrefetchScalarGridSpec(
            num_scalar_prefetch=2,
            grid=(n_rows // tm,),
            in_specs=[pl.BlockSpec((tm, D_MODEL), row_map),
                      _resident((None, D_MODEL, D_FF), expert_map),
                      _resident((None, D_MODEL, D_FF), expert_map),
                      _resident((None, D_FF, D_MODEL), expert_map)],
            out_specs=pl.BlockSpec((tm, D_MODEL), lambda i, te, nu: (i, 0))),
        compiler_params=_cparams("arbitrary"),
        name="moe_ffn",
    )(tile_expert, n_used, xs, wg, wu, wd)


def _combine_kernel(pos_ref, meta_ref, h_ref, yb_ref, o_ref, buf0, buf1, sems):
    tm = h_ref.shape[0]
    bufs = (buf0, buf1)

    def issue(j, c):
        for s in range(TOP_K):
            _row_copy(yb_ref, pos_ref[0, s * tm + j], bufs[s], j, sems.at[s]).start(priority=s)
        return c

    lax.fori_loop(0, tm, issue, 0, unroll=8)
    meta = meta_ref[...]
    lane = lax.broadcasted_iota(I32, meta.shape, 1)
    g0 = jnp.sum(jnp.where(lane == _RT_G0, meta, 0.0), axis=-1, keepdims=True)
    g1 = jnp.sum(jnp.where(lane == _RT_G1, meta, 0.0), axis=-1, keepdims=True)
    for s in range(TOP_K):
        pltpu.make_async_copy(yb_ref.at[pl.ds(0, tm)], bufs[s], sems.at[s]).wait()
    o_ref[...] = h_ref[...] + (g0 * buf0[...] + g1 * buf1[...])


def _combine(pos, meta, h2, yb):
    T = h2.shape[0]
    tm = ROUTE_TILE
    row = pl.BlockSpec((tm, D_MODEL), lambda i: (i, 0))
    return pl.pallas_call(
        _combine_kernel,
        out_shape=jax.ShapeDtypeStruct((T, D_MODEL), F32),
        grid=(T // tm,),
        in_specs=[pl.BlockSpec((None, 1, TOP_K * tm), lambda i: (i, 0, 0), memory_space=pltpu.SMEM),
                  pl.BlockSpec((tm, LANE), lambda i: (i, 0)), row,
                  pl.BlockSpec(memory_space=pl.ANY)],
        out_specs=row,
        scratch_shapes=[pltpu.VMEM((tm, D_MODEL), F32), pltpu.VMEM((tm, D_MODEL), F32),
                        pltpu.SemaphoreType.DMA((TOP_K,))],
        compiler_params=_cparams("arbitrary"),
        name="moe_combine",
    )(pos, meta, h2, yb)


def _moe(hn, h2, meta, cnt, wg, wu, wd):
    T, D = h2.shape
    counts = cnt[0, :N_EXPERTS].astype(I32)
    padded = (counts + MOE_TILE - 1) // MOE_TILE * MOE_TILE
    ends = jnp.cumsum(padded)
    starts = ends - padded
    n_rows = TOP_K * T + N_EXPERTS * MOE_TILE
    n_tiles = n_rows // MOE_TILE
    experts = jnp.arange(N_EXPERTS, dtype=I32)

    def slot(e_lane, r_lane):
        e = meta[:, e_lane].astype(I32)
        start = jnp.sum(jnp.where(e[:, None] == experts[None, :], starts[None, :], 0), axis=1)
        return start + meta[:, r_lane].astype(I32)

    nt = T // ROUTE_TILE
    pos = jnp.concatenate([slot(_RT_E0, _RT_R0).reshape(nt, ROUTE_TILE),
                           slot(_RT_E1, _RT_R1).reshape(nt, ROUTE_TILE)], axis=1)[:, None, :]
    tile_start = jnp.arange(n_tiles, dtype=I32) * MOE_TILE
    tile_expert = jnp.minimum(jnp.sum(ends[None, :] <= tile_start[:, None], axis=1), N_EXPERTS - 1).astype(I32)
    n_used = (ends[-1:] // MOE_TILE).astype(I32)
    pad_start = starts + counts
    pad_end = jnp.where(experts == N_EXPERTS - 1, n_rows, ends)
    xs = _dispatch(pad_start.astype(I32), (pad_end - pad_start).astype(I32), pos, hn, n_rows)
    yb = _moe_ffn(tile_expert, n_used, xs, wg.astype(BF16), wu.astype(BF16), wd.astype(BF16))
    return _combine(pos, meta, h2, yb)


def _block_diag_ones(n, blk):
    i = np.arange(n)
    return (i[:, None] // blk == i[None, :] // blk).astype(np.float32)


def _consts():
    n4 = 4 * CHUNK
    i = np.arange(n4)
    bd = _block_diag_ones(n4, CHUNK)
    tr, tc = i[:, None] % CHUNK, i[None, :] % CHUNK
    strict = bd * (tr > tc)
    incl = bd * (tr >= tc)
    ltri = np.tril(np.ones((CHUNK, CHUNK), np.float32))
    pair_mean = _block_diag_ones(LANE, HEAD_DIM) / HEAD_DIM
    qmask = (i[:, None] // CHUNK == np.arange(LANE)[None, :] // GLA_KEY_DIM).astype(np.float32)
    amask = (i[:, None] % CHUNK >= np.arange(CHUNK)[None, :]).astype(np.float32)
    out = {k: jnp.asarray(v, F32) for k, v in dict(
        bd=bd, strict=strict, incl=incl, ltri=ltri, pair_mean=pair_mean, qmask=qmask, amask=amask).items()}
    out["route_tri"] = jnp.asarray(np.tril(np.ones((ROUTE_TILE, ROUTE_TILE), np.float32), -1), BF16)
    return out


def _pad_rows(w, row0, total):
    return jnp.zeros((total, w.shape[1]), F32).at[row0:row0 + w.shape[0]].set(w.astype(F32))


def _pack_rows(rows, width, n_rows):
    out = jnp.zeros((n_rows, width), F32)
    for i, r in enumerate(rows):
        r = r.reshape(-1).astype(F32)
        out = out.at[i, :r.shape[0]].set(r)
    return out


def kernel(x, rel_bias, mix_norm_g, w_in, w_out, ffn_norm_g, attn_q_norm_g, attn_k_norm_g, conv_w, conv_b, conv_ln_g, conv_ln_b, rwkv_mu, rwkv_w0, rwkv_w2, rwkv_a0, rwkv_a2, rwkv_g2, rwkv_k_k, rwkv_k_a, rwkv_r_k, rwkv_ln_g, rwkv_ln_b, gla_g2, gla_gb, gla_norm_g, ffn_wg, ffn_wu, ffn_wd, moe_router, moe_wg, moe_wu, moe_wd):
    B, S, D = x.shape
    T = B * S
    cs = _consts()
    bias = _attn_bias_tables(rel_bias)
    h = x.reshape(T, D)
    for layer in range(DEPTH):
        w = w_in[layer]
        dq, dk, dv, dg, do = (w[:, 2176:2304], w[:, 2304:2432], w[:, 2432:2688], w[:, 2688:2704], w[:, 2704:2960])
        wp = jnp.concatenate([w[:, :2176], dq, dv, do, dk, dg, jnp.zeros((D, IN_WP - IN_W), w.dtype)],
                             axis=1).astype(BF16)
        z = _inproj(h, mix_norm_g[layer][None, :], wp)
        z3 = z.reshape(B, S, IN_WP)

        qg = jnp.tile(attn_q_norm_g[layer].astype(F32) * (HEAD_DIM ** -0.5), 2)[None, :]
        kg = jnp.tile(attn_k_norm_g[layer].astype(F32), 2)[None, :]
        ya = _attn_mixer(z3, bias, qg, kg, cs["pair_mean"]).reshape(T, GROUP_W)

        conv_pv = _pack_rows([conv_b[layer], conv_ln_g[layer], conv_ln_b[layer]], GROUP_W, 8)
        group_mean = cs["bd"] * (1.0 / HEAD_DIM)
        mu = rwkv_mu[layer]
        rw_pv = _pack_rows([mu[0:256], mu[256:512], mu[512:768], mu[768:896], rwkv_w0[layer], rwkv_a0[layer],
                            rwkv_k_k[layer], rwkv_k_a[layer], rwkv_r_k[layer], rwkv_ln_g[layer],
                            rwkv_ln_b[layer]], GROUP_W, 16)
        gla_pv = _pack_rows([gla_gb[layer], jnp.tile(gla_norm_g[layer], GROUP_W // HEAD_DIM)], GROUP_W, 8)
        yc, yd, yb = _mixers(z3, [
            _rwkv_part(z3, rw_pv, _pad_rows(rwkv_w2[layer], 0, LANE), _pad_rows(rwkv_a2[layer], 32, LANE),
                       _pad_rows(rwkv_g2[layer], 64, LANE), cs["bd"], cs["strict"], cs["incl"], cs["ltri"]),
            _gla_part(z3, _pad_rows(gla_g2[layer], 0, LANE), gla_pv, cs["bd"], cs["qmask"], cs["amask"],
                      cs["ltri"]),
            _conv_part(z3, conv_w[layer].astype(F32), conv_pv, group_mean)])

        norm_g = ffn_norm_g[layer][None, :]
        ys = (ya, yb.reshape(T, GROUP_W), yc.reshape(T, GROUP_W), yd.reshape(T, GROUP_W))
        i = layer // 2
        if layer % 2 == 0:
            h, hn = _outproj(ys, w_out[layer].astype(BF16), h, norm_g)
            h = _ffn(hn, h, ffn_wg[i].astype(BF16), ffn_wu[i].astype(BF16), ffn_wd[i].astype(BF16))
        else:
            wr = jnp.zeros((D, LANE), F32).at[:, :N_EXPERTS].set(moe_router[i].astype(F32))
            wr_hi = wr.astype(BF16)
            wr = jnp.concatenate([wr_hi, (wr - wr_hi.astype(F32)).astype(BF16)], axis=1)
            h, hn, meta, cnt = _outproj(ys, w_out[layer].astype(BF16), h, norm_g, router=(wr, cs["route_tri"]))
            h = _moe(hn, h, meta, cnt, moe_wg[i], moe_wu[i], moe_wd[i])
    return h.reshape(B, S, D)
```

```python
import functools
import math

import numpy as np
import jax
import jax.numpy as jnp
from jax import lax
from jax.experimental import pallas as pl
from jax.experimental.pallas import tpu as pltpu

F32 = jnp.float32
BF16 = jnp.bfloat16
I32 = jnp.int32
HI = lax.Precision.HIGHEST

D_MODEL = 1024
DEPTH = 2
GROUP_W = 256
NORM_EPS = 1e-6
HEAD_DIM = 64
DILATED_PATTERNS = ((128, 1), (512, 4), (2048, 16))
ATT_BLOCK = 128
N_BUCKETS = 32
REL_MAX_DIST = 2048
CONV_WIDTH = 31
CONV_HALO = 32
CONV_LN_EPS = 1e-5
RWKV_LN_EPS = 64e-5
GLA_KEY_DIM = 32
GLA_TAU = 16.0
CHUNK = 64
D_FF = 2816
N_EXPERTS = 8
TOP_K = 2
IN_W = 2960
IN_WP = 3072
LANE = 128
SUBLANE = 8
VMEM_LIMIT = 48 * 1024 * 1024
MOE_TILE = 512
FFN_ROWS = 512
ROUTE_TILE = 512
MOVE_TILE = 1024

OFF_AQ, OFF_AK, OFF_AV = 0, 256, 512
OFF_BU, OFF_BG = 768, 1024
OFF_CR, OFF_CK, OFF_CV, OFF_CL = 1280, 1536, 1792, 2048
OFF_DQ, OFF_DV, OFF_DO, OFF_DK, OFF_DG = 2176, 2304, 2560, 2816, 2944


def _cparams(*sem):
    return pltpu.CompilerParams(dimension_semantics=sem, vmem_limit_bytes=VMEM_LIMIT)


def _dot(a, b, prec=None):
    return jnp.dot(a, b, preferred_element_type=F32, precision=prec)


def _mm(a, b):
    return jnp.dot(a.astype(BF16), b.astype(BF16), preferred_element_type=F32)


def _mm_nt(a, b):
    return lax.dot_general(a.astype(BF16), b.astype(BF16), (((1,), (1,)), ((), ())),
                           preferred_element_type=F32)


def _mm_tn(a, b):
    return lax.dot_general(a.astype(BF16), b.astype(BF16), (((0,), (0,)), ((), ())),
                           preferred_element_type=F32)


def _split3(x):
    x1 = x.astype(BF16)
    r1 = x - x1.astype(F32)
    x2 = r1.astype(BF16)
    x3 = (r1 - x2.astype(F32)).astype(BF16)
    return x1, x2, x3


def _dot_exact_rhs(x, m):
    n = x.shape[0]
    y = _dot(jnp.concatenate(_split3(x), axis=0), m)
    return y[0:n] + y[n:2 * n] + y[2 * n:3 * n]


def _dot_exact_lhs(m, x):
    n = x.shape[1]
    y = _dot(m, jnp.concatenate(_split3(x), axis=1))
    return y[:, 0:n] + y[:, n:2 * n] + y[:, 2 * n:3 * n]


def _each(fn, *lists):
    return [fn(*args) for args in zip(*lists)]


def _sigmoid(x):
    return 1.0 / (1.0 + jnp.exp(-x))


def _softplus(x):
    return jnp.maximum(x, 0.0) + jnp.log(1.0 + jnp.exp(-jnp.abs(x)))


def _resident(shape, index_map):
    return pl.BlockSpec(shape, index_map, pipeline_mode=pl.Buffered(1))


def _inproj_kernel(x_ref, g_ref, w_ref, z_ref):
    x = x_ref[...]
    ms = jnp.mean(x * x, axis=-1, keepdims=True)
    xn = (x * lax.rsqrt(ms + NORM_EPS) * g_ref[...]).astype(BF16)
    z_ref[...] = _dot(xn, w_ref[...])


def _inproj(h2, g, w):
    T = h2.shape[0]
    tm = FFN_ROWS
    return pl.pallas_call(
        _inproj_kernel,
        out_shape=jax.ShapeDtypeStruct((T, IN_WP), F32),
        grid=(T // tm,),
        in_specs=[pl.BlockSpec((tm, D_MODEL), lambda i: (i, 0)),
                  pl.BlockSpec((1, D_MODEL), lambda i: (0, 0)),
                  _resident((D_MODEL, IN_WP), lambda i: (0, 0))],
        out_specs=pl.BlockSpec((tm, IN_WP), lambda i: (i, 0)),
        compiler_params=_cparams("parallel"),
        name="inproj",
    )(h2, g, w)


def _t5_bucket(dist):
    max_exact = N_BUCKETS // 2
    n = np.maximum(dist, 0)
    large = max_exact + (np.log(np.maximum(n, 1) / max_exact) / math.log(REL_MAX_DIST / max_exact)
                         * (N_BUCKETS - max_exact)).astype(np.int32)
    large = np.minimum(large, N_BUCKETS - 1)
    return np.where(n < max_exact, n, large).astype(np.int32)


def _bucket_table():
    W = ATT_BLOCK
    delta = np.arange(W)[:, None] + W - np.arange(2 * W)[None, :]
    band = (delta >= 0) & (delta <= W)
    tabs = [np.where(band, _t5_bucket(np.clip(delta, 0, W) * d), -1) for _, d in DILATED_PATTERNS]
    return jnp.asarray(np.stack(tabs), I32)


def _bias_kernel(rb_ref, bk_ref, o_ref):
    n_heads = o_ref.shape[1]
    for di in range(len(DILATED_PATTERNS)):
        bk = bk_ref[di]
        accs = [jnp.full(bk.shape, -jnp.inf, F32) for _ in range(n_heads)]
        for b in range(N_BUCKETS):
            hit = bk == b
            accs = [jnp.where(hit, rb_ref[b, hh], a) for hh, a in enumerate(accs)]
        for hh in range(n_heads):
            o_ref[di, hh] = accs[hh]


def _attn_bias_tables(rel_bias):
    n_heads = rel_bias.shape[1]
    shape = (len(DILATED_PATTERNS), n_heads, ATT_BLOCK, 2 * ATT_BLOCK)
    return pl.pallas_call(
        _bias_kernel,
        out_shape=jax.ShapeDtypeStruct(shape, F32),
        in_specs=[pl.BlockSpec(memory_space=pltpu.SMEM), pl.BlockSpec(memory_space=pltpu.VMEM)],
        out_specs=pl.BlockSpec(memory_space=pltpu.VMEM),
        name="attn_bias",
    )(rel_bias.astype(F32), _bucket_table())


ATT_MERGE_ROWS = 256
ATT_UNROLL = 8


def _attn_kernel(q_ref, k_ref, v_ref, bias_ref, qg_ref, kg_ref, gm_ref, o_ref,
                 qn, kn, qd, kd, vd, od, ld, acc, lse):
    S = q_ref.shape[0]
    rows = 512

    def norm_body(i, c):
        sl = pl.ds(pl.multiple_of(i * rows, rows), rows)
        q = q_ref[sl, :]
        k = k_ref[sl, :]
        gm16 = gm_ref[...].astype(BF16)
        qn[sl, :] = q * lax.rsqrt(_dot_exact_rhs(q * q, gm16) + NORM_EPS) * qg_ref[...]
        kn[sl, :] = k * lax.rsqrt(_dot_exact_rhs(k * k, gm16) + NORM_EPS) * kg_ref[...]
        return c

    lax.fori_loop(0, S // rows, norm_body, 0)

    lane = lax.broadcasted_iota(I32, (ATT_BLOCK, LANE), 1)
    head0 = lane < HEAD_DIM
    col = lax.broadcasted_iota(I32, (ATT_BLOCK, 2 * ATT_BLOCK), 1)

    def run_blocks(di, nb, qs, ks, vs, o_dst, l_dst):
        nu = min(nb, ATT_UNROLL)

        def blk(n):
            return pl.ds(pl.multiple_of(n * ATT_BLOCK, ATT_BLOCK), ATT_BLOCK)

        def body(i, c):
            base = i * nu
            rows = [blk(jnp.maximum(base - 1, 0))] + [blk(base + u) for u in range(nu)]
            kb = [ks[r, :].astype(BF16) for r in rows]
            vb = [vs[r, :].astype(BF16) for r in rows]
            kt = [jnp.concatenate([kb[u], kb[u + 1]], axis=0) for u in range(nu)]
            vt = [jnp.concatenate([vb[u], vb[u + 1]], axis=0) for u in range(nu)]
            qt = [qs[r, :] for r in rows[1:]]
            heads = [(u, hh) for u in range(nu) for hh in range(2)]
            s = [_mm_nt(jnp.where(head0 if hh == 0 else jnp.logical_not(head0), qt[u], 0.0), kt[u])
                 + bias_ref[di, hh] for u, hh in heads]
            no_prev = jnp.logical_and(i == 0, col < ATT_BLOCK)
            s = [jnp.where(no_prev, -jnp.inf, x) if u == 0 else x for (u, hh), x in zip(heads, s)]
            m = [jnp.max(x, axis=-1, keepdims=True) for x in s]
            p = _each(lambda x, mx: jnp.exp(x - mx), s, m)
            den = [jnp.sum(x, axis=-1, keepdims=True) for x in p]
            o = [_mm(x, vt[u]) / dn for (u, hh), x, dn in zip(heads, p, den)]
            ls = _each(lambda mx, dn: mx + jnp.log(dn), m, den)
            for u in range(nu):
                o_dst[rows[u + 1], :] = jnp.where(head0, o[2 * u], o[2 * u + 1])
                l_dst[rows[u + 1], :] = jnp.where(head0, ls[2 * u], ls[2 * u + 1])
            return c

        lax.fori_loop(0, nb // nu, body, 0)

    for di, (_, d) in enumerate(DILATED_PATTERNS):
        L = S // d
        nb = L // ATT_BLOCK
        if d == 1:
            run_blocks(di, nb, qn, kn, v_ref, acc, lse)
            continue
        for r in range(d):
            res = pl.ds(r, L, stride=d)
            qd[0:L, :] = qn[res, :]
            kd[0:L, :] = kn[res, :]
            vd[0:L, :] = v_ref[res, :]
            run_blocks(di, nb, qd, kd, vd, od, ld)
            mr = min(L, ATT_MERGE_ROWS)
            for c in range(L // mr):
                piece = pl.ds(r + c * mr * d, mr, stride=d)
                l0, a0 = lse[piece, :], acc[piece, :]
                l1, a1 = ld[c * mr:(c + 1) * mr, :], od[c * mr:(c + 1) * mr, :]
                m = jnp.maximum(l0, l1)
                e0, e1 = jnp.exp(l0 - m), jnp.exp(l1 - m)
                tot = e0 + e1
                acc[piece, :] = (a0 * e0 + a1 * e1) / tot
                lse[piece, :] = m + jnp.log(tot)

    def out_body(i, c):
        sl = pl.ds(pl.multiple_of(i * rows, rows), rows)
        o_ref[sl, :] = acc[sl, :].astype(BF16)
        return c

    lax.fori_loop(0, S // rows, out_body, 0)


def _attn_mixer(z3, bias, qg, kg, gm):
    B, S, _ = z3.shape
    n_lb = GROUP_W // LANE
    l_max = S // DILATED_PATTERNS[1][1]

    def spec(off):
        return pl.BlockSpec((None, S, LANE), lambda b, p: (b, 0, off // LANE + p))

    vec = pl.BlockSpec((1, LANE), lambda b, p: (0, 0))
    full = pltpu.VMEM((S, LANE), F32)
    part = pltpu.VMEM((l_max, LANE), F32)
    return pl.pallas_call(
        _attn_kernel,
        out_shape=jax.ShapeDtypeStruct((B, S, GROUP_W), BF16),
        grid=(B, n_lb),
        in_specs=[spec(OFF_AQ), spec(OFF_AK), spec(OFF_AV),
                  pl.BlockSpec((len(DILATED_PATTERNS), 2, ATT_BLOCK, 2 * ATT_BLOCK), lambda b, p: (0, p, 0, 0)),
                  vec, vec, pl.BlockSpec((LANE, LANE), lambda b, p: (0, 0))],
        out_specs=pl.BlockSpec((None, S, LANE), lambda b, p: (b, 0, p)),
        scratch_shapes=[full, full, part, part, part, part, part, full, full],
        compiler_params=_cparams("parallel", "parallel"),
        name="attn",
    )(z3, z3, z3, bias, qg, kg, gm)


def _conv_kernel(u_ref, g_ref, cw_ref, pv_ref, gm_ref, o_ref, hbuf, hsh, *, carry_phase):
    ts = u_ref.shape[0]
    t = pl.program_id(1)

    if carry_phase:
        @pl.when(t == 0)
        def _():
            hbuf[0:CONV_HALO, :] = jnp.zeros((CONV_HALO, GROUP_W), F32)

        @pl.when(t > 0)
        def _():
            hbuf[0:CONV_HALO, :] = hbuf[ts:ts + CONV_HALO, :]

        return

    hbuf[CONV_HALO:CONV_HALO + ts, :] = u_ref[...] * _sigmoid(g_ref[...])
    first = CONV_HALO - (CONV_WIDTH - 1)
    n_shift = hsh.shape[1]
    for ph in range(1, SUBLANE):
        hsh[ph - 1, :, :] = hbuf[ph:ph + n_shift, :]
    rows = 64
    gm16 = gm_ref[...].astype(BF16)
    for c in range(ts // rows):
        acc = jnp.zeros((rows, GROUP_W), F32) + pv_ref[0:1, :]
        for j in range(CONV_WIDTH):
            ph = (first + j) % SUBLANE
            s = first + j - ph + c * rows
            tap = hbuf[s:s + rows, :] if ph == 0 else hsh[ph - 1, s:s + rows, :]
            acc = acc + cw_ref[j:j + 1, :] * tap
        mu = _dot_exact_rhs(acc, gm16)
        xc = acc - mu
        var = _dot_exact_rhs(xc * xc, gm16)
        y = xc * lax.rsqrt(var + CONV_LN_EPS) * pv_ref[1:2, :] + pv_ref[2:3, :]
        o_ref[c * rows:(c + 1) * rows, :] = (y * _sigmoid(y)).astype(BF16)


MIX_ROWS = 512


def _seg(off, w):
    return pl.BlockSpec((None, MIX_ROWS, w), lambda b, t: (b, t, off // w))


def _const(shape):
    return pl.BlockSpec(shape, lambda b, t: (0, 0))


def _conv_part(z3, conv_w, pvec, gm):
    ts = MIX_ROWS
    return dict(kernel=_conv_kernel,
                in_specs=[_seg(OFF_BU, GROUP_W), _seg(OFF_BG, GROUP_W), _const((CONV_WIDTH, GROUP_W)),
                          _const((8, GROUP_W)), _const((GROUP_W, GROUP_W))],
                args=(z3, z3, conv_w, pvec, gm),
                scratch=[pltpu.VMEM((ts + CONV_HALO, GROUP_W), F32),
                         pltpu.VMEM((SUBLANE - 1, ts + CONV_HALO - SUBLANE, GROUP_W), F32)])


def _mixers_kernel(*refs, parts):
    n_in = sum(p[1] for p in parts)
    ins, outs, scr = refs[:n_in], refs[n_in:n_in + len(parts)], refs[n_in + len(parts):]
    for carry_phase in (True, False):
        i = s = 0
        for k, (fn, ni, ns) in enumerate(parts):
            fn(*ins[i:i + ni], outs[k], *scr[s:s + ns], carry_phase=carry_phase)
            i += ni
            s += ns


def _mixers(z3, parts):
    B, S, _ = z3.shape
    out_spec = pl.BlockSpec((None, MIX_ROWS, GROUP_W), lambda b, t: (b, t, 0))
    meta = tuple((p["kernel"], len(p["in_specs"]), len(p["scratch"])) for p in parts)
    return pl.pallas_call(
        functools.partial(_mixers_kernel, parts=meta),
        out_shape=[jax.ShapeDtypeStruct((B, S, GROUP_W), BF16)] * len(parts),
        grid=(B, S // MIX_ROWS),
        in_specs=[sp for p in parts for sp in p["in_specs"]],
        out_specs=[out_spec] * len(parts),
        scratch_shapes=[sc for p in parts for sc in p["scratch"]],
        compiler_params=_cparams("parallel", "arbitrary"),
        name="mixers_bcd",
    )(*[a for p in parts for a in p["args"]])


_RW_MU_R, _RW_MU_K, _RW_MU_V, _RW_MU_L, _RW_W0, _RW_A0, _RW_KK, _RW_KA, _RW_RK, _RW_LNG, _RW_LNB = range(11)
SHIFT_PAD = 8
RWKV_GROUP = 8
GLA_GROUP = 8


def _rwkv_kernel(r_ref, k_ref, v_ref, l_ref, pv_ref, w2_ref, a2_ref, g2_ref, bd_ref, st_ref, in_ref, lt_ref,
                 o_ref, rbuf, kbuf, vbuf, lbuf, lw_s, kk_s, be_s, km_s, rr_s, vv_s, gg_s, bo_s, sm_ref, *,
                 carry_phase):
    ts = r_ref.shape[0]
    t = pl.program_id(1)
    bufs = ((rbuf, r_ref), (kbuf, k_ref), (vbuf, v_ref), (lbuf, l_ref))

    if carry_phase:
        @pl.when(t == 0)
        def _():
            for buf, _ in bufs:
                buf[0:SHIFT_PAD, :] = jnp.zeros((SHIFT_PAD, buf.shape[1]), F32)
            sm_ref[...] = jnp.zeros(sm_ref.shape, F32)

        @pl.when(t > 0)
        def _():
            for buf, _ in bufs:
                buf[0:SHIFT_PAD, :] = buf[ts:ts + SHIFT_PAD, :]

        return

    for buf, ref in bufs:
        buf[SHIFT_PAD:SHIFT_PAD + ts, :] = ref[...]

    def pv(i, w=GROUP_W):
        return pv_ref[i:i + 1, 0:w]

    def shifted(buf, mu):
        cur = buf[SHIFT_PAD:SHIFT_PAD + ts, :]
        prev = buf[SHIFT_PAD - 1:SHIFT_PAD - 1 + ts, :]
        return cur + (prev - cur) * mu

    bd = bd_ref[...]
    bd16 = bd.astype(BF16)
    r = shifted(rbuf, pv(_RW_MU_R))
    k = shifted(kbuf, pv(_RW_MU_K))
    v = shifted(vbuf, pv(_RW_MU_V))
    lo = shifted(lbuf, pv(_RW_MU_L, LANE))
    w_log = -_softplus(-(pv(_RW_W0) + _dot(jnp.tanh(lo), w2_ref[...], HI))) - 0.5
    a = _sigmoid(pv(_RW_A0) + _dot(lo, a2_ref[...], HI))
    kk = k * pv(_RW_KK)
    kk = kk / jnp.maximum(jnp.sqrt(_dot_exact_rhs(kk * kk, bd16)), 1e-12)
    km = k * (1.0 + (a - 1.0) * pv(_RW_KA))
    lw_s[...] = -jnp.exp(w_log)
    kk_s[...] = kk
    be_s[...] = kk * a
    km_s[...] = km
    rr_s[...] = r
    vv_s[...] = v
    gg_s[...] = _dot(_sigmoid(lo), g2_ref[...], HI)
    bo_s[...] = _dot_exact_rhs(r * km * pv(_RW_RK), bd16) * v

    n2 = LANE // HEAD_DIM * CHUNK
    bd2 = bd[0:n2, 0:LANE]
    strict = st_ref[0:n2, 0:n2]
    incl = in_ref[0:n2, 0:n2]
    eye = incl - strict
    lt16 = lt_ref[...].astype(BF16)
    gmean16 = (bd2 * (1.0 / HEAD_DIM)).astype(BF16)
    n_pairs = GROUP_W // LANE

    def tile2(x):
        return jnp.concatenate([x] * (LANE // HEAD_DIM), axis=0)

    def fold(x):
        return x[0:CHUNK] + x[CHUNK:2 * CHUNK]

    def group(gi, carry):
        rows = [pl.ds(pl.multiple_of((gi * RWKV_GROUP + j) * CHUNK, CHUNK), CHUNK) for j in range(RWKV_GROUP)]
        items = [(s, slice(p * LANE, (p + 1) * LANE)) for s in rows for p in range(n_pairs)]
        lw = [lw_s[s, ln] for s, ln in items]
        cum = _each(lambda x: _dot_exact_lhs(lt16, x), lw)
        last = [c[CHUNK - 1:CHUNK, :] for c in cum]
        e_neg = [jnp.exp(-c) for c in cum]
        e_last = _each(lambda l, c: jnp.exp(l - c), last, cum)
        bec = [be_s[s, ln] for s, ln in items]
        kmc = [km_s[s, ln] for s, ln in items]
        a_bar = _each(lambda it, c, w: tile2(-kk_s[it[0], it[1]] * jnp.exp(c - w)) * bd2, items, cum, lw)
        r_bar = _each(lambda it, c: tile2(rr_s[it[0], it[1]] * jnp.exp(c)) * bd2, items, cum)
        v_blk = [tile2(vv_s[s, ln]) * bd2 for s, ln in items]
        big = _each(lambda ab, rb, be, km_, en: _mm_nt(
            jnp.concatenate([ab, rb], axis=0),
            jnp.concatenate([tile2(be * en), tile2(km_ * en)], axis=0)), a_bar, r_bar, bec, kmc, e_neg)
        nmat = [b[0:n2, 0:n2] * strict for b in big]
        a_ak = [b[0:n2, n2:2 * n2] * strict for b in big]
        a_rb = [b[n2:2 * n2, 0:n2] * incl for b in big]
        a_rk = [b[n2:2 * n2, n2:2 * n2] * incl for b in big]
        tinv = [eye + n for n in nmat]
        npow = nmat
        for _ in range(5):
            npow = _each(lambda n: _mm(n, n), npow)
            tinv = _each(lambda t_, n: t_ + _mm(t_, n), tinv, npow)
        w_blk = _each(_mm, tinv, a_bar)
        u0 = _each(lambda t_, ak, vb: _mm(t_, _mm(ak, vb)), tinv, a_ak, v_blk)
        o0 = _each(lambda rb, u, rk, vb: fold(_mm(rb, u) + _mm(rk, vb)), a_rb, u0, a_rk, v_blk)
        q_eff = _each(lambda rbar, rb, w: fold(rbar + _mm(rb, w)), r_bar, a_rb, w_blk)
        gh = _each(lambda be, el, w, u: _mm_tn(tile2(be * el) * bd2, jnp.concatenate([w, u], axis=1)),
                   bec, e_last, w_blk, u0)
        g_mat = _each(lambda l, x: eye * jnp.exp(l) + x[:, 0:LANE], last, gh)
        h_mat = _each(lambda x, km_, el, vb: x[:, LANE:2 * LANE] + _mm_tn(tile2(km_ * el) * bd2, vb),
                      gh, kmc, e_last, v_blk)
        outs = [None] * len(items)
        sms = [sm_ref[p] for p in range(n_pairs)]
        for j in range(RWKV_GROUP):
            for p in range(n_pairs):
                i = j * n_pairs + p
                outs[i] = _mm(q_eff[i], sms[p]) + o0[i]
                sms[p] = _mm(g_mat[i], sms[p]) + h_mat[i]
        for p in range(n_pairs):
            sm_ref[p] = sms[p]
        for (s, ln), out in zip(items, outs):
            mu = _dot_exact_rhs(out, gmean16)
            xc = out - mu
            var = _dot_exact_rhs(xc * xc, gmean16)
            y = xc * lax.rsqrt(var + RWKV_LN_EPS) * pv_ref[_RW_LNG:_RW_LNG + 1, ln] + pv_ref[_RW_LNB:_RW_LNB + 1, ln]
            o_ref[s, ln] = ((y + bo_s[s, ln]) * gg_s[s, ln]).astype(BF16)
        return carry

    lax.fori_loop(0, ts // (CHUNK * RWKV_GROUP), group, 0)


def _rwkv_part(z3, pvec, w2p, a2p, g2p, bd, strict, incl, ltri):
    ts = MIX_ROWS
    sq = (4 * CHUNK, 4 * CHUNK)
    wide = pltpu.VMEM((ts, GROUP_W), F32)
    return dict(kernel=_rwkv_kernel,
                in_specs=[_seg(OFF_CR, GROUP_W), _seg(OFF_CK, GROUP_W), _seg(OFF_CV, GROUP_W), _seg(OFF_CL, LANE),
                          _const((16, GROUP_W)), _const((LANE, GROUP_W)), _const((LANE, GROUP_W)),
                          _const((LANE, GROUP_W)), _const(sq), _const(sq), _const(sq), _const((CHUNK, CHUNK))],
                args=(z3, z3, z3, z3, pvec, w2p, a2p, g2p, bd, strict, incl, ltri),
                scratch=[pltpu.VMEM((ts + SHIFT_PAD, GROUP_W), F32)] * 3
                        + [pltpu.VMEM((ts + SHIFT_PAD, LANE), F32)]
                        + [wide] * 8 + [pltpu.VMEM((GROUP_W // LANE, LANE // HEAD_DIM * CHUNK, LANE), F32)])


def _gla_kernel(q_ref, k_ref, v_ref, og_ref, gd_ref, g2_ref, pv_ref, bd_ref, qm_ref, am_ref, lt_ref, o_ref,
                gk_s, st_ref, *, carry_phase):
    ts = q_ref.shape[0]

    if carry_phase:
        @pl.when(pl.program_id(1) == 0)
        def _():
            st_ref[...] = jnp.zeros(st_ref.shape, F32)

        return

    x = _dot(gd_ref[...], g2_ref[...], HI) + pv_ref[0:1, 0:LANE]
    gk_s[...] = -_softplus(-x) * (1.0 / GLA_TAU)
    bd = bd_ref[...]
    qmask = qm_ref[...]
    n_heads = GROUP_W // HEAD_DIM

    lt16 = lt_ref[...].astype(BF16)
    gmean16 = (bd * (1.0 / HEAD_DIM)).astype(BF16)

    def fold(x):
        return x[0:CHUNK] + x[CHUNK:2 * CHUNK] + x[2 * CHUNK:3 * CHUNK] + x[3 * CHUNK:4 * CHUNK]

    def group(gi, carry):
        sls = [pl.ds(pl.multiple_of((gi * GLA_GROUP + j) * CHUNK, CHUNK), CHUNK) for j in range(GLA_GROUP)]
        b = [_dot_exact_lhs(lt16, gk_s[s, :]) for s in sls]
        last = [x[CHUNK - 1:CHUNK, :] for x in b]
        kc = [k_ref[s, :] for s in sls]
        vc = [v_ref[s, :] for s in sls]
        q_d = _each(lambda s, x: q_ref[s, :] * (GLA_KEY_DIM ** -0.5) * jnp.exp(x), sls, b)
        k_d = _each(lambda k_, x: k_ * jnp.exp(-x), kc, b)
        att = _each(lambda q_, k_: _mm_nt(jnp.concatenate([q_] * n_heads, axis=0) * qmask, k_) * am_ref[...],
                    q_d, k_d)
        o_in = _each(lambda a_, v_: fold(_mm(a_, v_) * bd), att, vc)
        upd = _each(lambda v_, k_, l, x: _mm_tn(v_, k_ * jnp.exp(l - x)) * qmask, vc, kc, last, b)
        st = st_ref[...]
        outs = []
        for j in range(GLA_GROUP):
            outs.append(o_in[j] + _mm_nt(q_d[j], st))
            st = st * jnp.exp(last[j]) + upd[j]
        st_ref[...] = st
        for s, o in zip(sls, outs):
            ms = _dot_exact_rhs(o * o, gmean16)
            og = og_ref[s, :]
            y = o * lax.rsqrt(ms + NORM_EPS) * pv_ref[1:2, :] * (og * _sigmoid(og))
            o_ref[s, :] = y.astype(BF16)
        return carry

    lax.fori_loop(0, ts // (CHUNK * GLA_GROUP), group, 0)


def _gla_part(z3, g2p, pvec, bd, qmask, amask, ltri):
    return dict(kernel=_gla_kernel,
                in_specs=[_seg(OFF_DQ, LANE), _seg(OFF_DK, LANE), _seg(OFF_DV, GROUP_W), _seg(OFF_DO, GROUP_W),
                          _seg(OFF_DG, LANE), _const((LANE, LANE)), _const((8, GROUP_W)),
                          _const((GROUP_W, GROUP_W)), _const((GROUP_W, LANE)), _const((GROUP_W, CHUNK)),
                          _const((CHUNK, CHUNK))],
                args=(z3, z3, z3, z3, z3, g2p, pvec, bd, qmask, amask, ltri),
                scratch=[pltpu.VMEM((MIX_ROWS, LANE), F32), pltpu.VMEM((GROUP_W, LANE), F32)])


_RT_E0, _RT_E1, _RT_G0, _RT_G1, _RT_R0, _RT_R1 = range(6)


def _route(xn, wr_ref, tri_ref, run_ref, meta_ref, cnt_ref):
    @pl.when(pl.program_id(0) == 0)
    def _():
        run_ref[...] = jnp.zeros(run_ref.shape, F32)

    tm = xn.shape[0]
    x_hi = xn.astype(BF16)
    x_lo = (xn - x_hi.astype(F32)).astype(BF16)
    y = _dot(jnp.concatenate([x_hi, x_lo], axis=0), wr_ref[...])
    logits = (y[0:tm, 0:LANE] + y[0:tm, LANE:2 * LANE]) + (y[tm:2 * tm, 0:LANE] + y[tm:2 * tm, LANE:2 * LANE])
    lane = lax.broadcasted_iota(I32, logits.shape, 1).astype(F32)
    logits = jnp.where(lane < N_EXPERTS, logits, -jnp.inf)
    m1 = jnp.max(logits, axis=-1, keepdims=True)
    i1 = jnp.min(jnp.where(logits == m1, lane, float(LANE)), axis=-1, keepdims=True)
    rest = jnp.where(lane == i1, -jnp.inf, logits)
    m2 = jnp.max(rest, axis=-1, keepdims=True)
    i2 = jnp.min(jnp.where(rest == m2, lane, float(LANE)), axis=-1, keepdims=True)
    e2 = jnp.exp(m2 - m1)
    den = 1.0 + e2
    hit1 = lane == i1
    hit2 = lane == i2
    member = jnp.where(jnp.logical_or(hit1, hit2), 1.0, 0.0)
    run = run_ref[0:1, :]
    rank = _dot(tri_ref[...], member.astype(BF16)) + run
    r1 = jnp.sum(jnp.where(hit1, rank, 0.0), axis=-1, keepdims=True)
    r2 = jnp.sum(jnp.where(hit2, rank, 0.0), axis=-1, keepdims=True)
    run = run + jnp.sum(member, axis=0, keepdims=True)
    run_ref[0:1, :] = run
    cnt_ref[...] = jnp.broadcast_to(run, cnt_ref.shape)
    rec = jnp.zeros(logits.shape, F32)
    for idx, val in ((_RT_E0, i1), (_RT_E1, i2), (_RT_G0, 1.0 / den), (_RT_G1, e2 / den), (_RT_R0, r1),
                     (_RT_R1, r2)):
        rec = jnp.where(lane == float(idx), val, rec)
    meta_ref[...] = rec


def _outproj_kernel(ya, yb, yc, yd, w_ref, h_ref, g_ref, *rest, route):
    if route:
        wr_ref, tri_ref, ho_ref, hn_ref, meta_ref, cnt_ref, run_ref = rest
    else:
        ho_ref, hn_ref = rest
    acc = h_ref[...]
    for i, y in enumerate((ya, yb, yc, yd)):
        acc = acc + _dot(y[...], w_ref[i * GROUP_W:(i + 1) * GROUP_W, :])
    ho_ref[...] = acc
    ms = jnp.mean(acc * acc, axis=-1, keepdims=True)
    xn = acc * lax.rsqrt(ms + NORM_EPS) * g_ref[...]
    hn_ref[...] = xn.astype(BF16)
    if route:
        _route(xn, wr_ref, tri_ref, run_ref, meta_ref, cnt_ref)


def _outproj(ys, w, h2, g, router=None):
    T = h2.shape[0]
    tm = ROUTE_TILE
    yspec = pl.BlockSpec((tm, GROUP_W), lambda i: (i, 0))
    row = pl.BlockSpec((tm, D_MODEL), lambda i: (i, 0))
    in_specs = [yspec] * 4 + [pl.BlockSpec((D_MODEL, D_MODEL), lambda i: (0, 0)), row,
                              pl.BlockSpec((1, D_MODEL), lambda i: (0, 0))]
    out_shape = [jax.ShapeDtypeStruct((T, D_MODEL), F32), jax.ShapeDtypeStruct((T, D_MODEL), BF16)]
    out_specs = [row, row]
    args = (*ys, w, h2, g)
    scratch = []
    if router is not None:
        in_specs += [pl.BlockSpec((D_MODEL, 2 * LANE), lambda i: (0, 0)), pl.BlockSpec((tm, tm), lambda i: (0, 0))]
        out_shape += [jax.ShapeDtypeStruct((T, LANE), F32), jax.ShapeDtypeStruct((SUBLANE, LANE), F32)]
        out_specs += [pl.BlockSpec((tm, LANE), lambda i: (i, 0)), pl.BlockSpec((SUBLANE, LANE), lambda i: (0, 0))]
        args += router
        scratch = [pltpu.VMEM((SUBLANE, LANE), F32)]
    return pl.pallas_call(
        functools.partial(_outproj_kernel, route=router is not None),
        out_shape=out_shape,
        grid=(T // tm,),
        in_specs=in_specs,
        out_specs=out_specs,
        scratch_shapes=scratch,
        compiler_params=_cparams("arbitrary"),
        name="outproj_route" if router is not None else "outproj",
    )(*args)


FFN_SPLIT = 2


def _swiglu_rows(x, wg_ref, wu_ref, wd_ref):
    piece = D_FF // FFN_SPLIT
    out = None
    for s in range(FFN_SPLIT):
        cols = slice(s * piece, (s + 1) * piece)
        g = _dot(x, wg_ref[:, cols])
        u = _dot(x, wu_ref[:, cols])
        y = _dot((g * _sigmoid(g) * u).astype(BF16), wd_ref[cols, :])
        out = y if out is None else out + y
    return out


def _ffn_kernel(x_ref, h_ref, wg_ref, wu_ref, wd_ref, o_ref):
    o_ref[...] = h_ref[...] + _swiglu_rows(x_ref[...], wg_ref, wu_ref, wd_ref)


def _ffn(hn, h2, wg, wu, wd):
    T = hn.shape[0]
    tm = FFN_ROWS
    row = pl.BlockSpec((tm, D_MODEL), lambda i: (i, 0))
    return pl.pallas_call(
        _ffn_kernel,
        out_shape=jax.ShapeDtypeStruct((T, D_MODEL), F32),
        grid=(T // tm,),
        in_specs=[row, row,
                  _resident((D_MODEL, D_FF), lambda i: (0, 0)),
                  _resident((D_MODEL, D_FF), lambda i: (0, 0)),
                  _resident((D_FF, D_MODEL), lambda i: (0, 0))],
        out_specs=row,
        compiler_params=_cparams("parallel"),
        name="ffn",
    )(hn, h2, wg, wu, wd)


def _row_copy(src_ref, src_row, dst_ref, dst_row, sem):
    return pltpu.make_async_copy(src_ref.at[pl.ds(src_row, 1)], dst_ref.at[pl.ds(dst_row, 1)], sem)


def _dispatch_kernel(ps_ref, pl_ref, pos_ref, hn_ref, xs_ref, buf, zrow, sem, zsem):
    tm = hn_ref.shape[0]

    @pl.when(pl.program_id(0) == 0)
    def _():
        zrow[...] = jnp.zeros(zrow.shape, F32)
        for e in range(N_EXPERTS):
            def fill(r, c, e=e):
                _row_copy(zrow, 0, xs_ref, ps_ref[e] + r, zsem).start()
                return c

            lax.fori_loop(0, pl_ref[e], fill, 0)
        for e in range(N_EXPERTS):
            def drain(r, c, e=e):
                _row_copy(zrow, 0, xs_ref, ps_ref[e] + r, zsem).wait()
                return c

            lax.fori_loop(0, pl_ref[e], drain, 0)

    buf[...] = hn_ref[...].astype(F32)

    def issue(j, c):
        for s in range(TOP_K):
            _row_copy(buf, j, xs_ref, pos_ref[0, s * tm + j], sem).start(priority=s)
        return c

    lax.fori_loop(0, tm, issue, 0, unroll=8)
    for _ in range(TOP_K):
        pltpu.make_async_copy(buf, xs_ref.at[pl.ds(0, tm)], sem).wait()


def _dispatch(pad_start, pad_len, pos, hn, n_rows):
    T = hn.shape[0]
    tm = MOVE_TILE
    return pl.pallas_call(
        _dispatch_kernel,
        out_shape=jax.ShapeDtypeStruct((n_rows, D_MODEL), F32),
        grid_spec=pltpu.PrefetchScalarGridSpec(
            num_scalar_prefetch=2,
            grid=(T // tm,),
            in_specs=[pl.BlockSpec((None, 1, TOP_K * tm), lambda i, ps, pn: (i, 0, 0), memory_space=pltpu.SMEM),
                      pl.BlockSpec((tm, D_MODEL), lambda i, ps, pn: (i, 0))],
            out_specs=pl.BlockSpec(memory_space=pl.ANY),
            scratch_shapes=[pltpu.VMEM((tm, D_MODEL), F32), pltpu.VMEM((SUBLANE, D_MODEL), F32),
                            pltpu.SemaphoreType.DMA, pltpu.SemaphoreType.DMA]),
        compiler_params=_cparams("arbitrary"),
        name="moe_dispatch",
    )(pad_start, pad_len, pos, hn)


def _moe_ffn_kernel(te_ref, nu_ref, x_ref, wg_ref, wu_ref, wd_ref, o_ref):
    del te_ref
    i = pl.program_id(0)

    @pl.when(i < nu_ref[0])
    def _():
        o_ref[...] = _swiglu_rows(x_ref[...].astype(BF16), wg_ref, wu_ref, wd_ref)

    @pl.when(i >= nu_ref[0])
    def _():
        o_ref[...] = jnp.zeros(o_ref.shape, F32)


def _moe_ffn(tile_expert, n_used, xs, wg, wu, wd):
    n_rows = xs.shape[0]
    tm = MOE_TILE

    def row_map(i, te, nu):
        return (jnp.maximum(jnp.minimum(i, nu[0] - 1), 0), 0)

    def expert_map(i, te, nu):
        return (te[i], 0, 0)

    return pl.pallas_call(
        _moe_ffn_kernel,
        out_shape=jax.ShapeDtypeStruct((n_rows, D_MODEL), F32),
        grid_spec=pltpu.PrefetchScalarGridSpec(
            num_scalar_prefetch=2,
            grid=(n_rows // tm,),
            in_specs=[pl.BlockSpec((tm, D_MODEL), row_map),
                      _resident((None, D_MODEL, D_FF), expert_map),
                      _resident((None, D_MODEL, D_FF), expert_map),
                      _resident((None, D_FF, D_MODEL), expert_map)],
            out_specs=pl.BlockSpec((tm, D_MODEL), lambda i, te, nu: (i, 0))),
        compiler_params=_cparams("arbitrary"),
        name="moe_ffn",
    )(tile_expert, n_used, xs, wg, wu, wd)


def _combine_kernel(pos_ref, meta_ref, h_ref, yb_ref, o_ref, buf0, buf1, sems):
    tm = h_ref.shape[0]
    bufs = (buf0, buf1)

    def issue(j, c):
        for s in range(TOP_K):
            _row_copy(yb_ref, pos_ref[0, s * tm + j], bufs[s], j, sems.at[s]).start(priority=s)
        return c

    lax.fori_loop(0, tm, issue, 0, unroll=8)
    meta = meta_ref[...]
    lane = lax.broadcasted_iota(I32, meta.shape, 1)
    g0 = jnp.sum(jnp.where(lane == _RT_G0, meta, 0.0), axis=-1, keepdims=True)
    g1 = jnp.sum(jnp.where(lane == _RT_G1, meta, 0.0), axis=-1, keepdims=True)
    for s in range(TOP_K):
        pltpu.make_async_copy(yb_ref.at[pl.ds(0, tm)], bufs[s], sems.at[s]).wait()
    o_ref[...] = h_ref[...] + (g0 * buf0[...] + g1 * buf1[...])


def _combine(pos, meta, h2, yb):
    T = h2.shape[0]
    tm = MOVE_TILE
    row = pl.BlockSpec((tm, D_MODEL), lambda i: (i, 0))
    return pl.pallas_call(
        _combine_kernel,
        out_shape=jax.ShapeDtypeStruct((T, D_MODEL), F32),
        grid=(T // tm,),
        in_specs=[pl.BlockSpec((None, 1, TOP_K * tm), lambda i: (i, 0, 0), memory_space=pltpu.SMEM),
                  pl.BlockSpec((tm, LANE), lambda i: (i, 0)), row,
                  pl.BlockSpec(memory_space=pl.ANY)],
        out_specs=row,
        scratch_shapes=[pltpu.VMEM((tm, D_MODEL), F32), pltpu.VMEM((tm, D_MODEL), F32),
                        pltpu.SemaphoreType.DMA((TOP_K,))],
        compiler_params=_cparams("arbitrary"),
        name="moe_combine",
    )(pos, meta, h2, yb)


def _moe(hn, h2, meta, cnt, wg, wu, wd):
    T, D = h2.shape
    counts = cnt[0, :N_EXPERTS].astype(I32)
    padded = (counts + MOE_TILE - 1) // MOE_TILE * MOE_TILE
    ends = jnp.cumsum(padded)
    starts = ends - padded
    n_rows = TOP_K * T + N_EXPERTS * MOE_TILE
    n_tiles = n_rows // MOE_TILE
    experts = jnp.arange(N_EXPERTS, dtype=I32)

    def slot(e_lane, r_lane):
        e = meta[:, e_lane].astype(I32)
        start = jnp.sum(jnp.where(e[:, None] == experts[None, :], starts[None, :], 0), axis=1)
        return start + meta[:, r_lane].astype(I32)

    nt = T // MOVE_TILE
    pos = jnp.concatenate([slot(_RT_E0, _RT_R0).reshape(nt, MOVE_TILE),
                           slot(_RT_E1, _RT_R1).reshape(nt, MOVE_TILE)], axis=1)[:, None, :]
    tile_start = jnp.arange(n_tiles, dtype=I32) * MOE_TILE
    tile_expert = jnp.minimum(jnp.sum(ends[None, :] <= tile_start[:, None], axis=1), N_EXPERTS - 1).astype(I32)
    n_used = (ends[-1:] // MOE_TILE).astype(I32)
    pad_start = starts + counts
    pad_end = jnp.where(experts == N_EXPERTS - 1, n_rows, ends)
    xs = _dispatch(pad_start.astype(I32), (pad_end - pad_start).astype(I32), pos, hn, n_rows)
    yb = _moe_ffn(tile_expert, n_used, xs, wg.astype(BF16), wu.astype(BF16), wd.astype(BF16))
    return _combine(pos, meta, h2, yb)


def _block_diag_ones(n, blk):
    i = np.arange(n)
    return (i[:, None] // blk == i[None, :] // blk).astype(np.float32)


def _consts():
    n4 = 4 * CHUNK
    i = np.arange(n4)
    bd = _block_diag_ones(n4, CHUNK)
    tr, tc = i[:, None] % CHUNK, i[None, :] % CHUNK
    strict = bd * (tr > tc)
    incl = bd * (tr >= tc)
    ltri = np.tril(np.ones((CHUNK, CHUNK), np.float32))
    pair_mean = _block_diag_ones(LANE, HEAD_DIM) / HEAD_DIM
    qmask = (i[:, None] // CHUNK == np.arange(LANE)[None, :] // GLA_KEY_DIM).astype(np.float32)
    amask = (i[:, None] % CHUNK >= np.arange(CHUNK)[None, :]).astype(np.float32)
    out = {k: jnp.asarray(v, F32) for k, v in dict(
        bd=bd, strict=strict, incl=incl, ltri=ltri, pair_mean=pair_mean, qmask=qmask, amask=amask).items()}
    out["route_tri"] = jnp.asarray(np.tril(np.ones((ROUTE_TILE, ROUTE_TILE), np.float32), -1), BF16)
    return out


def _pad_rows(w, row0, total):
    return jnp.zeros((total, w.shape[1]), F32).at[row0:row0 + w.shape[0]].set(w.astype(F32))


def _pack_rows(rows, width, n_rows):
    out = jnp.zeros((n_rows, width), F32)
    for i, r in enumerate(rows):
        r = r.reshape(-1).astype(F32)
        out = out.at[i, :r.shape[0]].set(r)
    return out


def kernel(x, rel_bias, mix_norm_g, w_in, w_out, ffn_norm_g, attn_q_norm_g, attn_k_norm_g, conv_w, conv_b, conv_ln_g, conv_ln_b, rwkv_mu, rwkv_w0, rwkv_w2, rwkv_a0, rwkv_a2, rwkv_g2, rwkv_k_k, rwkv_k_a, rwkv_r_k, rwkv_ln_g, rwkv_ln_b, gla_g2, gla_gb, gla_norm_g, ffn_wg, ffn_wu, ffn_wd, moe_router, moe_wg, moe_wu, moe_wd):
    B, S, D = x.shape
    T = B * S
    cs = _consts()
    bias = _attn_bias_tables(rel_bias)
    h = x.reshape(T, D)
    for layer in range(DEPTH):
        w = w_in[layer]
        dq, dk, dv, dg, do = (w[:, 2176:2304], w[:, 2304:2432], w[:, 2432:2688], w[:, 2688:2704], w[:, 2704:2960])
        wp = jnp.concatenate([w[:, :2176], dq, dv, do, dk, dg, jnp.zeros((D, IN_WP - IN_W), w.dtype)],
                             axis=1).astype(BF16)
        z = _inproj(h, mix_norm_g[layer][None, :], wp)
        z3 = z.reshape(B, S, IN_WP)

        qg = jnp.tile(attn_q_norm_g[layer].astype(F32) * (HEAD_DIM ** -0.5), 2)[None, :]
        kg = jnp.tile(attn_k_norm_g[layer].astype(F32), 2)[None, :]
        ya = _attn_mixer(z3, bias, qg, kg, cs["pair_mean"]).reshape(T, GROUP_W)

        conv_pv = _pack_rows([conv_b[layer], conv_ln_g[layer], conv_ln_b[layer]], GROUP_W, 8)
        group_mean = cs["bd"] * (1.0 / HEAD_DIM)
        mu = rwkv_mu[layer]
        rw_pv = _pack_rows([mu[0:256], mu[256:512], mu[512:768], mu[768:896], rwkv_w0[layer], rwkv_a0[layer],
                            rwkv_k_k[layer], rwkv_k_a[layer], rwkv_r_k[layer], rwkv_ln_g[layer],
                            rwkv_ln_b[layer]], GROUP_W, 16)
        gla_pv = _pack_rows([gla_gb[layer], jnp.tile(gla_norm_g[layer], GROUP_W // HEAD_DIM)], GROUP_W, 8)
        yc, yd, yb = _mixers(z3, [
            _rwkv_part(z3, rw_pv, _pad_rows(rwkv_w2[layer], 0, LANE), _pad_rows(rwkv_a2[layer], 32, LANE),
                       _pad_rows(rwkv_g2[layer], 64, LANE), cs["bd"], cs["strict"], cs["incl"], cs["ltri"]),
            _gla_part(z3, _pad_rows(gla_g2[layer], 0, LANE), gla_pv, cs["bd"], cs["qmask"], cs["amask"],
                      cs["ltri"]),
            _conv_part(z3, conv_w[layer].astype(F32), conv_pv, group_mean)])

        norm_g = ffn_norm_g[layer][None, :]
        ys = (ya, yb.reshape(T, GROUP_W), yc.reshape(T, GROUP_W), yd.reshape(T, GROUP_W))
        i = layer // 2
        if layer % 2 == 0:
            h, hn = _outproj(ys, w_out[layer].astype(BF16), h, norm_g)
            h = _ffn(hn, h, ffn_wg[i].astype(BF16), ffn_wu[i].astype(BF16), ffn_wd[i].astype(BF16))
        else:
            wr = jnp.zeros((D, LANE), F32).at[:, :N_EXPERTS].set(moe_router[i].astype(F32))
            wr_hi = wr.astype(BF16)
            wr = jnp.concatenate([wr_hi, (wr - wr_hi.astype(F32)).astype(BF16)], axis=1)
            h, hn, meta, cnt = _outproj(ys, w_out[layer].astype(BF16), h, norm_g, router=(wr, cs["route_tri"]))
            h = _moe(hn, h, meta, cnt, moe_wg[i], moe_wu[i], moe_wd[i])
    return h.reshape(B, S, D)
```

```python
import functools
import math

import numpy as np
import jax
import jax.numpy as jnp
from jax import lax
from jax.experimental import pallas as pl
from jax.experimental.pallas import tpu as pltpu

F32 = jnp.float32
BF16 = jnp.bfloat16
I32 = jnp.int32
HI = lax.Precision.HIGHEST

D_MODEL = 1024
DEPTH = 2
GROUP_W = 256
NORM_EPS = 1e-6
HEAD_DIM = 64
DILATED_PATTERNS = ((128, 1), (512, 4), (2048, 16))
ATT_BLOCK = 128
N_BUCKETS = 32
REL_MAX_DIST = 2048
CONV_WIDTH = 31
CONV_HALO = 32
CONV_LN_EPS = 1e-5
RWKV_LN_EPS = 64e-5
GLA_KEY_DIM = 32
GLA_TAU = 16.0
CHUNK = 64
D_FF = 2816
N_EXPERTS = 8
TOP_K = 2
IN_W = 2960
IN_WP = 3072
LANE = 128
SUBLANE = 8
VMEM_LIMIT = 48 * 1024 * 1024
MOE_TILE = 512
FFN_ROWS = 512
ROUTE_TILE = 512
MOVE_TILE = 1024

OFF_AQ, OFF_AK, OFF_AV = 0, 256, 512
OFF_BU, OFF_BG = 768, 1024
OFF_CR, OFF_CK, OFF_CV, OFF_CL = 1280, 1536, 1792, 2048
OFF_DQ, OFF_DV, OFF_DO, OFF_DK, OFF_DG = 2176, 2304, 2560, 2816, 2944


def _cparams(*sem):
    return pltpu.CompilerParams(dimension_semantics=sem, vmem_limit_bytes=VMEM_LIMIT)


def _dot(a, b, prec=None):
    return jnp.dot(a, b, preferred_element_type=F32, precision=prec)


def _mm(a, b):
    return jnp.dot(a.astype(BF16), b.astype(BF16), preferred_element_type=F32)


def _mm_nt(a, b):
    return lax.dot_general(a.astype(BF16), b.astype(BF16), (((1,), (1,)), ((), ())),
                           preferred_element_type=F32)


def _mm_tn(a, b):
    return lax.dot_general(a.astype(BF16), b.astype(BF16), (((0,), (0,)), ((), ())),
                           preferred_element_type=F32)


def _split3(x):
    x1 = x.astype(BF16)
    r1 = x - x1.astype(F32)
    x2 = r1.astype(BF16)
    x3 = (r1 - x2.astype(F32)).astype(BF16)
    return x1, x2, x3


def _dot_exact_rhs(x, m):
    n = x.shape[0]
    y = _dot(jnp.concatenate(_split3(x), axis=0), m)
    return y[0:n] + y[n:2 * n] + y[2 * n:3 * n]


def _dot_exact_lhs(m, x):
    n = x.shape[1]
    y = _dot(m, jnp.concatenate(_split3(x), axis=1))
    return y[:, 0:n] + y[:, n:2 * n] + y[:, 2 * n:3 * n]


def _each(fn, *lists):
    return [fn(*args) for args in zip(*lists)]


def _sigmoid(x):
    return 1.0 / (1.0 + jnp.exp(-x))


def _softplus(x):
    return jnp.maximum(x, 0.0) + jnp.log(1.0 + jnp.exp(-jnp.abs(x)))


def _resident(shape, index_map):
    return pl.BlockSpec(shape, index_map, pipeline_mode=pl.Buffered(1))


def _inproj_kernel(x_ref, g_ref, w_ref, z_ref):
    x = x_ref[...]
    ms = jnp.mean(x * x, axis=-1, keepdims=True)
    xn = (x * lax.rsqrt(ms + NORM_EPS) * g_ref[...]).astype(BF16)
    z_ref[...] = _dot(xn, w_ref[...])


def _inproj(h2, g, w):
    T = h2.shape[0]
    tm = FFN_ROWS
    return pl.pallas_call(
        _inproj_kernel,
        out_shape=jax.ShapeDtypeStruct((T, IN_WP), F32),
        grid=(T // tm,),
        in_specs=[pl.BlockSpec((tm, D_MODEL), lambda i: (i, 0)),
                  pl.BlockSpec((1, D_MODEL), lambda i: (0, 0)),
                  _resident((D_MODEL, IN_WP), lambda i: (0, 0))],
        out_specs=pl.BlockSpec((tm, IN_WP), lambda i: (i, 0)),
        compiler_params=_cparams("parallel"),
        name="inproj",
    )(h2, g, w)


def _t5_bucket(dist):
    max_exact = N_BUCKETS // 2
    n = np.maximum(dist, 0)
    large = max_exact + (np.log(np.maximum(n, 1) / max_exact) / math.log(REL_MAX_DIST / max_exact)
                         * (N_BUCKETS - max_exact)).astype(np.int32)
    large = np.minimum(large, N_BUCKETS - 1)
    return np.where(n < max_exact, n, large).astype(np.int32)


def _bucket_table():
    W = ATT_BLOCK
    delta = np.arange(W)[:, None] + W - np.arange(2 * W)[None, :]
    band = (delta >= 0) & (delta <= W)
    tabs = [np.where(band, _t5_bucket(np.clip(delta, 0, W) * d), -1) for _, d in DILATED_PATTERNS]
    return jnp.asarray(np.stack(tabs), I32)


def _bias_kernel(rb_ref, bk_ref, o_ref):
    n_heads = o_ref.shape[1]
    for di in range(len(DILATED_PATTERNS)):
        bk = bk_ref[di]
        accs = [jnp.full(bk.shape, -jnp.inf, F32) for _ in range(n_heads)]
        for b in range(N_BUCKETS):
            hit = bk == b
            accs = [jnp.where(hit, rb_ref[b, hh], a) for hh, a in enumerate(accs)]
        for hh in range(n_heads):
            o_ref[di, hh] = accs[hh]


def _attn_bias_tables(rel_bias):
    n_heads = rel_bias.shape[1]
    shape = (len(DILATED_PATTERNS), n_heads, ATT_BLOCK, 2 * ATT_BLOCK)
    return pl.pallas_call(
        _bias_kernel,
        out_shape=jax.ShapeDtypeStruct(shape, F32),
        in_specs=[pl.BlockSpec(memory_space=pltpu.SMEM), pl.BlockSpec(memory_space=pltpu.VMEM)],
        out_specs=pl.BlockSpec(memory_space=pltpu.VMEM),
        name="attn_bias",
    )(rel_bias.astype(F32), _bucket_table())


ATT_MERGE_ROWS = 256
ATT_UNROLL = 8


def _attn_kernel(q_ref, k_ref, v_ref, bias_ref, qg_ref, kg_ref, gm_ref, o_ref,
                 qn, kn, qd, kd, vd, od, ld, acc, lse):
    S = q_ref.shape[0]
    rows = 512

    def norm_body(i, c):
        sl = pl.ds(pl.multiple_of(i * rows, rows), rows)
        q = q_ref[sl, :]
        k = k_ref[sl, :]
        gm16 = gm_ref[...].astype(BF16)
        qn[sl, :] = q * lax.rsqrt(_dot_exact_rhs(q * q, gm16) + NORM_EPS) * qg_ref[...]
        kn[sl, :] = k * lax.rsqrt(_dot_exact_rhs(k * k, gm16) + NORM_EPS) * kg_ref[...]
        return c

    lax.fori_loop(0, S // rows, norm_body, 0)

    lane = lax.broadcasted_iota(I32, (ATT_BLOCK, LANE), 1)
    head0 = lane < HEAD_DIM
    col = lax.broadcasted_iota(I32, (ATT_BLOCK, 2 * ATT_BLOCK), 1)

    def run_blocks(di, nb, qs, ks, vs, o_dst, l_dst):
        nu = min(nb, ATT_UNROLL)

        def blk(n):
            return pl.ds(pl.multiple_of(n * ATT_BLOCK, ATT_BLOCK), ATT_BLOCK)

        def body(i, c):
            base = i * nu
            rows = [blk(jnp.maximum(base - 1, 0))] + [blk(base + u) for u in range(nu)]
            kb = [ks[r, :].astype(BF16) for r in rows]
            vb = [vs[r, :].astype(BF16) for r in rows]
            kt = [jnp.concatenate([kb[u], kb[u + 1]], axis=0) for u in range(nu)]
            vt = [jnp.concatenate([vb[u], vb[u + 1]], axis=0) for u in range(nu)]
            qt = [qs[r, :] for r in rows[1:]]
            heads = [(u, hh) for u in range(nu) for hh in range(2)]
            s = [_mm_nt(jnp.where(head0 if hh == 0 else jnp.logical_not(head0), qt[u], 0.0), kt[u])
                 + bias_ref[di, hh] for u, hh in heads]
            no_prev = jnp.logical_and(i == 0, col < ATT_BLOCK)
            s = [jnp.where(no_prev, -jnp.inf, x) if u == 0 else x for (u, hh), x in zip(heads, s)]
            m = [jnp.max(x, axis=-1, keepdims=True) for x in s]
            p = _each(lambda x, mx: jnp.exp(x - mx), s, m)
            den = [jnp.sum(x, axis=-1, keepdims=True) for x in p]
            o = [_mm(x, vt[u]) / dn for (u, hh), x, dn in zip(heads, p, den)]
            ls = _each(lambda mx, dn: mx + jnp.log(dn), m, den)
            for u in range(nu):
                o_dst[rows[u + 1], :] = jnp.where(head0, o[2 * u], o[2 * u + 1])
                l_dst[rows[u + 1], :] = jnp.where(head0, ls[2 * u], ls[2 * u + 1])
            return c

        lax.fori_loop(0, nb // nu, body, 0)

    for di, (_, d) in enumerate(DILATED_PATTERNS):
        L = S // d
        nb = L // ATT_BLOCK
        if d == 1:
            run_blocks(di, nb, qn, kn, v_ref, acc, lse)
            continue
        for r in range(d):
            res = pl.ds(r, L, stride=d)
            qd[0:L, :] = qn[res, :]
            kd[0:L, :] = kn[res, :]
            vd[0:L, :] = v_ref[res, :]
            run_blocks(di, nb, qd, kd, vd, od, ld)
            mr = min(L, ATT_MERGE_ROWS)
            for c in range(L // mr):
                piece = pl.ds(r + c * mr * d, mr, stride=d)
                l0, a0 = lse[piece, :], acc[piece, :]
                l1, a1 = ld[c * mr:(c + 1) * mr, :], od[c * mr:(c + 1) * mr, :]
                m = jnp.maximum(l0, l1)
                e0, e1 = jnp.exp(l0 - m), jnp.exp(l1 - m)
                tot = e0 + e1
                acc[piece, :] = (a0 * e0 + a1 * e1) / tot
                lse[piece, :] = m + jnp.log(tot)

    def out_body(i, c):
        sl = pl.ds(pl.multiple_of(i * rows, rows), rows)
        o_ref[sl, :] = acc[sl, :].astype(BF16)
        return c

    lax.fori_loop(0, S // rows, out_body, 0)


def _attn_mixer(z3, bias, qg, kg, gm):
    B, S, _ = z3.shape
    n_lb = GROUP_W // LANE
    l_max = S // DILATED_PATTERNS[1][1]

    def spec(off):
        return pl.BlockSpec((None, S, LANE), lambda b, p: (b, 0, off // LANE + p))

    vec = pl.BlockSpec((1, LANE), lambda b, p: (0, 0))
    full = pltpu.VMEM((S, LANE), F32)
    part = pltpu.VMEM((l_max, LANE), F32)
    return pl.pallas_call(
        _attn_kernel,
        out_shape=jax.ShapeDtypeStruct((B, S, GROUP_W), BF16),
        grid=(B, n_lb),
        in_specs=[spec(OFF_AQ), spec(OFF_AK), spec(OFF_AV),
                  pl.BlockSpec((len(DILATED_PATTERNS), 2, ATT_BLOCK, 2 * ATT_BLOCK), lambda b, p: (0, p, 0, 0)),
                  vec, vec, pl.BlockSpec((LANE, LANE), lambda b, p: (0, 0))],
        out_specs=pl.BlockSpec((None, S, LANE), lambda b, p: (b, 0, p)),
        scratch_shapes=[full, full, part, part, part, part, part, full, full],
        compiler_params=_cparams("parallel", "parallel"),
        name="attn",
    )(z3, z3, z3, bias, qg, kg, gm)


def _conv_kernel(u_ref, g_ref, cw_ref, pv_ref, gm_ref, o_ref, hbuf, hsh, *, carry_phase):
    ts = u_ref.shape[0]
    t = pl.program_id(1)

    if carry_phase:
        @pl.when(t == 0)
        def _():
            hbuf[0:CONV_HALO, :] = jnp.zeros((CONV_HALO, GROUP_W), F32)

        @pl.when(t > 0)
        def _():
            hbuf[0:CONV_HALO, :] = hbuf[ts:ts + CONV_HALO, :]

        return

    hbuf[CONV_HALO:CONV_HALO + ts, :] = u_ref[...] * _sigmoid(g_ref[...])
    first = CONV_HALO - (CONV_WIDTH - 1)
    n_shift = hsh.shape[1]
    for ph in range(1, SUBLANE):
        hsh[ph - 1, :, :] = hbuf[ph:ph + n_shift, :]
    rows = 64
    gm16 = gm_ref[...].astype(BF16)
    for c in range(ts // rows):
        acc = jnp.zeros((rows, GROUP_W), F32) + pv_ref[0:1, :]
        for j in range(CONV_WIDTH):
            ph = (first + j) % SUBLANE
            s = first + j - ph + c * rows
            tap = hbuf[s:s + rows, :] if ph == 0 else hsh[ph - 1, s:s + rows, :]
            acc = acc + cw_ref[j:j + 1, :] * tap
        mu = _dot_exact_rhs(acc, gm16)
        xc = acc - mu
        var = _dot_exact_rhs(xc * xc, gm16)
        y = xc * lax.rsqrt(var + CONV_LN_EPS) * pv_ref[1:2, :] + pv_ref[2:3, :]
        o_ref[c * rows:(c + 1) * rows, :] = (y * _sigmoid(y)).astype(BF16)


MIX_ROWS = 512


def _seg(off, w):
    return pl.BlockSpec((None, MIX_ROWS, w), lambda b, t: (b, t, off // w))


def _const(shape):
    return pl.BlockSpec(shape, lambda b, t: (0, 0))


def _conv_part(z3, conv_w, pvec, gm):
    ts = MIX_ROWS
    return dict(kernel=_conv_kernel,
                in_specs=[_seg(OFF_BU, GROUP_W), _seg(OFF_BG, GROUP_W), _const((CONV_WIDTH, GROUP_W)),
                          _const((8, GROUP_W)), _const((GROUP_W, GROUP_W))],
                args=(z3, z3, conv_w, pvec, gm),
                scratch=[pltpu.VMEM((ts + CONV_HALO, GROUP_W), F32),
                         pltpu.VMEM((SUBLANE - 1, ts + CONV_HALO - SUBLANE, GROUP_W), F32)])


def _mixers_kernel(*refs, parts):
    n_in = sum(p[1] for p in parts)
    ins, outs, scr = refs[:n_in], refs[n_in:n_in + len(parts)], refs[n_in + len(parts):]
    for carry_phase in (True, False):
        i = s = 0
        for k, (fn, ni, ns) in enumerate(parts):
            fn(*ins[i:i + ni], outs[k], *scr[s:s + ns], carry_phase=carry_phase)
            i += ni
            s += ns


def _mixers(z3, parts):
    B, S, _ = z3.shape
    out_spec = pl.BlockSpec((None, MIX_ROWS, GROUP_W), lambda b, t: (b, t, 0))
    meta = tuple((p["kernel"], len(p["in_specs"]), len(p["scratch"])) for p in parts)
    return pl.pallas_call(
        functools.partial(_mixers_kernel, parts=meta),
        out_shape=[jax.ShapeDtypeStruct((B, S, GROUP_W), BF16)] * len(parts),
        grid=(B, S // MIX_ROWS),
        in_specs=[sp for p in parts for sp in p["in_specs"]],
        out_specs=[out_spec] * len(parts),
        scratch_shapes=[sc for p in parts for sc in p["scratch"]],
        compiler_params=_cparams("parallel", "arbitrary"),
        name="mixers_bcd",
    )(*[a for p in parts for a in p["args"]])


_RW_MU_R, _RW_MU_K, _RW_MU_V, _RW_MU_L, _RW_W0, _RW_A0, _RW_KK, _RW_KA, _RW_RK, _RW_LNG, _RW_LNB = range(11)
SHIFT_PAD = 8
RWKV_GROUP = 8
GLA_GROUP = 8


def _rwkv_kernel(r_ref, k_ref, v_ref, l_ref, pv_ref, w2_ref, a2_ref, g2_ref, bd_ref, st_ref, in_ref, lt_ref,
                 o_ref, rbuf, kbuf, vbuf, lbuf, lw_s, kk_s, be_s, km_s, rr_s, vv_s, gg_s, bo_s, sm_ref, *,
                 carry_phase):
    ts = r_ref.shape[0]
    t = pl.program_id(1)
    bufs = ((rbuf, r_ref), (kbuf, k_ref), (vbuf, v_ref), (lbuf, l_ref))

    if carry_phase:
        @pl.when(t == 0)
        def _():
            for buf, _ in bufs:
                buf[0:SHIFT_PAD, :] = jnp.zeros((SHIFT_PAD, buf.shape[1]), F32)
            sm_ref[...] = jnp.zeros(sm_ref.shape, F32)

        @pl.when(t > 0)
        def _():
            for buf, _ in bufs:
                buf[0:SHIFT_PAD, :] = buf[ts:ts + SHIFT_PAD, :]

        return

    for buf, ref in bufs:
        buf[SHIFT_PAD:SHIFT_PAD + ts, :] = ref[...]

    def pv(i, w=GROUP_W):
        return pv_ref[i:i + 1, 0:w]

    def shifted(buf, mu):
        cur = buf[SHIFT_PAD:SHIFT_PAD + ts, :]
        prev = buf[SHIFT_PAD - 1:SHIFT_PAD - 1 + ts, :]
        return cur + (prev - cur) * mu

    bd = bd_ref[...]
    bd16 = bd.astype(BF16)
    r = shifted(rbuf, pv(_RW_MU_R))
    k = shifted(kbuf, pv(_RW_MU_K))
    v = shifted(vbuf, pv(_RW_MU_V))
    lo = shifted(lbuf, pv(_RW_MU_L, LANE))
    w_log = -_softplus(-(pv(_RW_W0) + _dot(jnp.tanh(lo), w2_ref[...], HI))) - 0.5
    a = _sigmoid(pv(_RW_A0) + _dot(lo, a2_ref[...], HI))
    kk = k * pv(_RW_KK)
    kk = kk / jnp.maximum(jnp.sqrt(_dot_exact_rhs(kk * kk, bd16)), 1e-12)
    km = k * (1.0 + (a - 1.0) * pv(_RW_KA))
    lw_s[...] = -jnp.exp(w_log)
    kk_s[...] = kk
    be_s[...] = kk * a
    km_s[...] = km
    rr_s[...] = r
    vv_s[...] = v
    gg_s[...] = _dot(_sigmoid(lo), g2_ref[...], HI)
    bo_s[...] = _dot_exact_rhs(r * km * pv(_RW_RK), bd16) * v

    n2 = LANE // HEAD_DIM * CHUNK
    bd2 = bd[0:n2, 0:LANE]
    strict = st_ref[0:n2, 0:n2]
    incl = in_ref[0:n2, 0:n2]
    eye = incl - strict
    lt16 = lt_ref[...].astype(BF16)
    gmean16 = (bd2 * (1.0 / HEAD_DIM)).astype(BF16)
    n_pairs = GROUP_W // LANE

    def tile2(x):
        return jnp.concatenate([x] * (LANE // HEAD_DIM), axis=0)

    def fold(x):
        return x[0:CHUNK] + x[CHUNK:2 * CHUNK]

    def group(gi, carry):
        rows = [pl.ds(pl.multiple_of((gi * RWKV_GROUP + j) * CHUNK, CHUNK), CHUNK) for j in range(RWKV_GROUP)]
        items = [(s, slice(p * LANE, (p + 1) * LANE)) for s in rows for p in range(n_pairs)]
        lw = [lw_s[s, ln] for s, ln in items]
        cum = _each(lambda x: _dot_exact_lhs(lt16, x), lw)
        last = [c[CHUNK - 1:CHUNK, :] for c in cum]
        e_neg = [jnp.exp(-c) for c in cum]
        e_last = _each(lambda l, c: jnp.exp(l - c), last, cum)
        bec = [be_s[s, ln] for s, ln in items]
        kmc = [km_s[s, ln] for s, ln in items]
        a_bar = _each(lambda it, c, w: tile2(-kk_s[it[0], it[1]] * jnp.exp(c - w)) * bd2, items, cum, lw)
        r_bar = _each(lambda it, c: tile2(rr_s[it[0], it[1]] * jnp.exp(c)) * bd2, items, cum)
        v_blk = [tile2(vv_s[s, ln]) * bd2 for s, ln in items]
        big = _each(lambda ab, rb, be, km_, en: _mm_nt(
            jnp.concatenate([ab, rb], axis=0),
            jnp.concatenate([tile2(be * en), tile2(km_ * en)], axis=0)), a_bar, r_bar, bec, kmc, e_neg)
        nmat = [b[0:n2, 0:n2] * strict for b in big]
        a_ak = [b[0:n2, n2:2 * n2] * strict for b in big]
        a_rb = [b[n2:2 * n2, 0:n2] * incl for b in big]
        a_rk = [b[n2:2 * n2, n2:2 * n2] * incl for b in big]
        tinv = [eye + n for n in nmat]
        npow = nmat
        for _ in range(5):
            npow = _each(lambda n: _mm(n, n), npow)
            tinv = _each(lambda t_, n: t_ + _mm(t_, n), tinv, npow)
        w_blk = _each(_mm, tinv, a_bar)
        u0 = _each(lambda t_, ak, vb: _mm(t_, _mm(ak, vb)), tinv, a_ak, v_blk)
        o0 = _each(lambda rb, u, rk, vb: fold(_mm(rb, u) + _mm(rk, vb)), a_rb, u0, a_rk, v_blk)
        q_eff = _each(lambda rbar, rb, w: fold(rbar + _mm(rb, w)), r_bar, a_rb, w_blk)
        gh = _each(lambda be, el, w, u: _mm_tn(tile2(be * el) * bd2, jnp.concatenate([w, u], axis=1)),
                   bec, e_last, w_blk, u0)
        g_mat = _each(lambda l, x: eye * jnp.exp(l) + x[:, 0:LANE], last, gh)
        h_mat = _each(lambda x, km_, el, vb: x[:, LANE:2 * LANE] + _mm_tn(tile2(km_ * el) * bd2, vb),
                      gh, kmc, e_last, v_blk)
        outs = [None] * len(items)
        sms = [sm_ref[p] for p in range(n_pairs)]
        for j in range(RWKV_GROUP):
            for p in range(n_pairs):
                i = j * n_pairs + p
                outs[i] = _mm(q_eff[i], sms[p]) + o0[i]
                sms[p] = _mm(g_mat[i], sms[p]) + h_mat[i]
        for p in range(n_pairs):
            sm_ref[p] = sms[p]
        for (s, ln), out in zip(items, outs):
            mu = _dot_exact_rhs(out, gmean16)
            xc = out - mu
            var = _dot_exact_rhs(xc * xc, gmean16)
            y = xc * lax.rsqrt(var + RWKV_LN_EPS) * pv_ref[_RW_LNG:_RW_LNG + 1, ln] + pv_ref[_RW_LNB:_RW_LNB + 1, ln]
            o_ref[s, ln] = ((y + bo_s[s, ln]) * gg_s[s, ln]).astype(BF16)
        return carry

    lax.fori_loop(0, ts // (CHUNK * RWKV_GROUP), group, 0)


def _rwkv_part(z3, pvec, w2p, a2p, g2p, bd, strict, incl, ltri):
    ts = MIX_ROWS
    sq = (4 * CHUNK, 4 * CHUNK)
    wide = pltpu.VMEM((ts, GROUP_W), F32)
    return dict(kernel=_rwkv_kernel,
                in_specs=[_seg(OFF_CR, GROUP_W), _seg(OFF_CK, GROUP_W), _seg(OFF_CV, GROUP_W), _seg(OFF_CL, LANE),
                          _const((16, GROUP_W)), _const((LANE, GROUP_W)), _const((LANE, GROUP_W)),
                          _const((LANE, GROUP_W)), _const(sq), _const(sq), _const(sq), _const((CHUNK, CHUNK))],
                args=(z3, z3, z3, z3, pvec, w2p, a2p, g2p, bd, strict, incl, ltri),
                scratch=[pltpu.VMEM((ts + SHIFT_PAD, GROUP_W), F32)] * 3
                        + [pltpu.VMEM((ts + SHIFT_PAD, LANE), F32)]
                        + [wide] * 8 + [pltpu.VMEM((GROUP_W // LANE, LANE // HEAD_DIM * CHUNK, LANE), F32)])


def _gla_kernel(q_ref, k_ref, v_ref, og_ref, gd_ref, g2_ref, pv_ref, bd_ref, qm_ref, am_ref, lt_ref, o_ref,
                gk_s, st_ref, *, carry_phase):
    ts = q_ref.shape[0]

    if carry_phase:
        @pl.when(pl.program_id(1) == 0)
        def _():
            st_ref[...] = jnp.zeros(st_ref.shape, F32)

        return

    x = _dot(gd_ref[...], g2_ref[...], HI) + pv_ref[0:1, 0:LANE]
    gk_s[...] = -_softplus(-x) * (1.0 / GLA_TAU)
    bd = bd_ref[...]
    qmask = qm_ref[...]
    n_heads = GROUP_W // HEAD_DIM

    lt16 = lt_ref[...].astype(BF16)
    gmean16 = (bd * (1.0 / HEAD_DIM)).astype(BF16)

    def fold(x):
        return x[0:CHUNK] + x[CHUNK:2 * CHUNK] + x[2 * CHUNK:3 * CHUNK] + x[3 * CHUNK:4 * CHUNK]

    def group(gi, carry):
        sls = [pl.ds(pl.multiple_of((gi * GLA_GROUP + j) * CHUNK, CHUNK), CHUNK) for j in range(GLA_GROUP)]
        b = [_dot_exact_lhs(lt16, gk_s[s, :]) for s in sls]
        last = [x[CHUNK - 1:CHUNK, :] for x in b]
        kc = [k_ref[s, :] for s in sls]
        vc = [v_ref[s, :] for s in sls]
        q_d = _each(lambda s, x: q_ref[s, :] * (GLA_KEY_DIM ** -0.5) * jnp.exp(x), sls, b)
        k_d = _each(lambda k_, x: k_ * jnp.exp(-x), kc, b)
        att = _each(lambda q_, k_: _mm_nt(jnp.concatenate([q_] * n_heads, axis=0) * qmask, k_) * am_ref[...],
                    q_d, k_d)
        o_in = _each(lambda a_, v_: fold(_mm(a_, v_) * bd), att, vc)
        upd = _each(lambda v_, k_, l, x: _mm_tn(v_, k_ * jnp.exp(l - x)) * qmask, vc, kc, last, b)
        st = st_ref[...]
        outs = []
        for j in range(GLA_GROUP):
            outs.append(o_in[j] + _mm_nt(q_d[j], st))
            st = st * jnp.exp(last[j]) + upd[j]
        st_ref[...] = st
        for s, o in zip(sls, outs):
            ms = _dot_exact_rhs(o * o, gmean16)
            og = og_ref[s, :]
            y = o * lax.rsqrt(ms + NORM_EPS) * pv_ref[1:2, :] * (og * _sigmoid(og))
            o_ref[s, :] = y.astype(BF16)
        return carry

    lax.fori_loop(0, ts // (CHUNK * GLA_GROUP), group, 0)


def _gla_part(z3, g2p, pvec, bd, qmask, amask, ltri):
    return dict(kernel=_gla_kernel,
                in_specs=[_seg(OFF_DQ, LANE), _seg(OFF_DK, LANE), _seg(OFF_DV, GROUP_W), _seg(OFF_DO, GROUP_W),
                          _seg(OFF_DG, LANE), _const((LANE, LANE)), _const((8, GROUP_W)),
                          _const((GROUP_W, GROUP_W)), _const((GROUP_W, LANE)), _const((GROUP_W, CHUNK)),
                          _const((CHUNK, CHUNK))],
                args=(z3, z3, z3, z3, z3, g2p, pvec, bd, qmask, amask, ltri),
                scratch=[pltpu.VMEM((MIX_ROWS, LANE), F32), pltpu.VMEM((GROUP_W, LANE), F32)])


_RT_E0, _RT_E1, _RT_G0, _RT_G1, _RT_R0, _RT_R1 = range(6)


def _route(xn, wr_ref, tri_ref, run_ref, meta_ref, cnt_ref):
    @pl.when(pl.program_id(0) == 0)
    def _():
        run_ref[...] = jnp.zeros(run_ref.shape, F32)

    tm = xn.shape[0]
    x_hi = xn.astype(BF16)
    x_lo = (xn - x_hi.astype(F32)).astype(BF16)
    y = _dot(jnp.concatenate([x_hi, x_lo], axis=0), wr_ref[...])
    logits = (y[0:tm, 0:LANE] + y[0:tm, LANE:2 * LANE]) + (y[tm:2 * tm, 0:LANE] + y[tm:2 * tm, LANE:2 * LANE])
    lane = lax.broadcasted_iota(I32, logits.shape, 1).astype(F32)
    logits = jnp.where(lane < N_EXPERTS, logits, -jnp.inf)
    m1 = jnp.max(logits, axis=-1, keepdims=True)
    i1 = jnp.min(jnp.where(logits == m1, lane, float(LANE)), axis=-1, keepdims=True)
    rest = jnp.where(lane == i1, -jnp.inf, logits)
    m2 = jnp.max(rest, axis=-1, keepdims=True)
    i2 = jnp.min(jnp.where(rest == m2, lane, float(LANE)), axis=-1, keepdims=True)
    e2 = jnp.exp(m2 - m1)
    den = 1.0 + e2
    hit1 = lane == i1
    hit2 = lane == i2
    member = jnp.where(jnp.logical_or(hit1, hit2), 1.0, 0.0)
    run = run_ref[0:1, :]
    rank = _dot(tri_ref[...], member.astype(BF16)) + run
    r1 = jnp.sum(jnp.where(hit1, rank, 0.0), axis=-1, keepdims=True)
    r2 = jnp.sum(jnp.where(hit2, rank, 0.0), axis=-1, keepdims=True)
    run = run + jnp.sum(member, axis=0, keepdims=True)
    run_ref[0:1, :] = run
    cnt_ref[...] = jnp.broadcast_to(run, cnt_ref.shape)
    rec = jnp.zeros(logits.shape, F32)
    for idx, val in ((_RT_E0, i1), (_RT_E1, i2), (_RT_G0, 1.0 / den), (_RT_G1, e2 / den), (_RT_R0, r1),
                     (_RT_R1, r2)):
        rec = jnp.where(lane == float(idx), val, rec)
    meta_ref[...] = rec


def _outproj_kernel(ya, yb, yc, yd, w_ref, h_ref, g_ref, *rest, route):
    if route:
        wr_ref, tri_ref, ho_ref, hn_ref, meta_ref, cnt_ref, run_ref = rest
    else:
        ho_ref, hn_ref = rest
    acc = h_ref[...]
    for i, y in enumerate((ya, yb, yc, yd)):
        acc = acc + _dot(y[...], w_ref[i * GROUP_W:(i + 1) * GROUP_W, :])
    ho_ref[...] = acc
    ms = jnp.mean(acc * acc, axis=-1, keepdims=True)
    xn = acc * lax.rsqrt(ms + NORM_EPS) * g_ref[...]
    hn_ref[...] = xn.astype(BF16)
    if route:
        _route(xn, wr_ref, tri_ref, run_ref, meta_ref, cnt_ref)


def _outproj(ys, w, h2, g, router=None):
    T = h2.shape[0]
    tm = ROUTE_TILE
    yspec = pl.BlockSpec((tm, GROUP_W), lambda i: (i, 0))
    row = pl.BlockSpec((tm, D_MODEL), lambda i: (i, 0))
    in_specs = [yspec] * 4 + [pl.BlockSpec((D_MODEL, D_MODEL), lambda i: (0, 0)), row,
                              pl.BlockSpec((1, D_MODEL), lambda i: (0, 0))]
    out_shape = [jax.ShapeDtypeStruct((T, D_MODEL), F32), jax.ShapeDtypeStruct((T, D_MODEL), BF16)]
    out_specs = [row, row]
    args = (*ys, w, h2, g)
    scratch = []
    if router is not None:
        in_specs += [pl.BlockSpec((D_MODEL, 2 * LANE), lambda i: (0, 0)), pl.BlockSpec((tm, tm), lambda i: (0, 0))]
        out_shape += [jax.ShapeDtypeStruct((T, LANE), F32), jax.ShapeDtypeStruct((SUBLANE, LANE), F32)]
        out_specs += [pl.BlockSpec((tm, LANE), lambda i: (i, 0)), pl.BlockSpec((SUBLANE, LANE), lambda i: (0, 0))]
        args += router
        scratch = [pltpu.VMEM((SUBLANE, LANE), F32)]
    return pl.pallas_call(
        functools.partial(_outproj_kernel, route=router is not None),
        out_shape=out_shape,
        grid=(T // tm,),
        in_specs=in_specs,
        out_specs=out_specs,
        scratch_shapes=scratch,
        compiler_params=_cparams("arbitrary"),
        name="outproj_route" if router is not None else "outproj",
    )(*args)


FFN_SPLIT = 2


def _swiglu_rows(x, wg_ref, wu_ref, wd_ref):
    piece = D_FF // FFN_SPLIT
    out = None
    for s in range(FFN_SPLIT):
        cols = slice(s * piece, (s + 1) * piece)
        g = _dot(x, wg_ref[:, cols])
        u = _dot(x, wu_ref[:, cols])
        y = _dot((g * _sigmoid(g) * u).astype(BF16), wd_ref[cols, :])
        out = y if out is None else out + y
    return out


def _ffn_kernel(x_ref, h_ref, wg_ref, wu_ref, wd_ref, o_ref):
    o_ref[...] = h_ref[...] + _swiglu_rows(x_ref[...], wg_ref, wu_ref, wd_ref)


def _ffn(hn, h2, wg, wu, wd):
    T = hn.shape[0]
    tm = FFN_ROWS
    row = pl.BlockSpec((tm, D_MODEL), lambda i: (i, 0))
    return pl.pallas_call(
        _ffn_kernel,
        out_shape=jax.ShapeDtypeStruct((T, D_MODEL), F32),
        grid=(T // tm,),
        in_specs=[row, row,
                  _resident((D_MODEL, D_FF), lambda i: (0, 0)),
                  _resident((D_MODEL, D_FF), lambda i: (0, 0)),
                  _resident((D_FF, D_MODEL), lambda i: (0, 0))],
        out_specs=row,
        compiler_params=_cparams("parallel"),
        name="ffn",
    )(hn, h2, wg, wu, wd)


def _row_copy(src_ref, src_row, dst_ref, dst_row, sem):
    return pltpu.make_async_copy(src_ref.at[pl.ds(src_row, 1)], dst_ref.at[pl.ds(dst_row, 1)], sem)


def _pad_fill_copies(ps_ref, pl_ref, zbuf, xs_ref, zsem):
    out = []
    for e in range(N_EXPERTS):
        start = ps_ref[e]
        end = start + pl_ref[e]
        aligned = jnp.minimum((start + SUBLANE - 1) // SUBLANE * SUBLANE, end)
        for u in range(SUBLANE - 1):
            out.append((start + u < aligned, _row_copy(zbuf, 0, xs_ref, start + u, zsem)))
        rest = end - aligned
        size = MOE_TILE // 2
        while size >= SUBLANE:
            dst = pl.multiple_of(aligned + rest // (2 * size) * (2 * size), SUBLANE)
            out.append(((rest // size) % 2 == 1,
                        pltpu.make_async_copy(zbuf.at[pl.ds(0, size)], xs_ref.at[pl.ds(dst, size)], zsem)))
            size //= 2
    for k in range(N_EXPERTS):
        dst = pl.multiple_of(ps_ref[N_EXPERTS] + k * MOE_TILE, MOE_TILE)
        out.append((k < pl_ref[N_EXPERTS],
                    pltpu.make_async_copy(zbuf, xs_ref.at[pl.ds(dst, MOE_TILE)], zsem)))
    return out


def _dispatch_kernel(ps_ref, pl_ref, pos_ref, hn_ref, xs_ref, buf, zbuf, sem, zsem):
    tm = hn_ref.shape[0]

    @pl.when(pl.program_id(0) == 0)
    def _():
        zbuf[...] = jnp.zeros(zbuf.shape, F32)
        copies = _pad_fill_copies(ps_ref, pl_ref, zbuf, xs_ref, zsem)
        for pred, copy in copies:
            pl.when(pred)(copy.start)
        for pred, copy in copies:
            pl.when(pred)(copy.wait)

    buf[...] = hn_ref[...].astype(F32)

    def issue(j, c):
        for s in range(TOP_K):
            _row_copy(buf, j, xs_ref, pos_ref[0, s * tm + j], sem).start(priority=s)
        return c

    lax.fori_loop(0, tm, issue, 0, unroll=8)
    for _ in range(TOP_K):
        pltpu.make_async_copy(buf, xs_ref.at[pl.ds(0, tm)], sem).wait()


def _dispatch(pad_start, pad_len, pos, hn, n_rows):
    T = hn.shape[0]
    tm = MOVE_TILE
    return pl.pallas_call(
        _dispatch_kernel,
        out_shape=jax.ShapeDtypeStruct((n_rows, D_MODEL), F32),
        grid_spec=pltpu.PrefetchScalarGridSpec(
            num_scalar_prefetch=2,
            grid=(T // tm,),
            in_specs=[pl.BlockSpec((None, 1, TOP_K * tm), lambda i, ps, pn: (i, 0, 0), memory_space=pltpu.SMEM),
                      pl.BlockSpec((tm, D_MODEL), lambda i, ps, pn: (i, 0))],
            out_specs=pl.BlockSpec(memory_space=pl.ANY),
            scratch_shapes=[pltpu.VMEM((tm, D_MODEL), F32), pltpu.VMEM((MOE_TILE, D_MODEL), F32),
                            pltpu.SemaphoreType.DMA, pltpu.SemaphoreType.DMA]),
        compiler_params=_cparams("arbitrary"),
        name="moe_dispatch",
    )(pad_start, pad_len, pos, hn)


def _moe_ffn_kernel(te_ref, nu_ref, x_ref, wg_ref, wu_ref, wd_ref, o_ref):
    del te_ref
    i = pl.program_id(0)

    @pl.when(i < nu_ref[0])
    def _():
        o_ref[...] = _swiglu_rows(x_ref[...].astype(BF16), wg_ref, wu_ref, wd_ref)

    @pl.when(i >= nu_ref[0])
    def _():
        o_ref[...] = jnp.zeros(o_ref.shape, F32)


def _moe_ffn(tile_expert, n_used, xs, wg, wu, wd):
    n_rows = xs.shape[0]
    tm = MOE_TILE

    def row_map(i, te, nu):
        return (jnp.maximum(jnp.minimum(i, nu[0] - 1), 0), 0)

    def expert_map(i, te, nu):
        return (te[i], 0, 0)

    return pl.pallas_call(
        _moe_ffn_kernel,
        out_shape=jax.ShapeDtypeStruct((n_rows, D_MODEL), F32),
        grid_spec=pltpu.PrefetchScalarGridSpec(
            num_scalar_prefetch=2,
            grid=(n_rows // tm,),
            in_specs=[pl.BlockSpec((tm, D_MODEL), row_map),
                      _resident((None, D_MODEL, D_FF), expert_map),
                      _resident((None, D_MODEL, D_FF), expert_map),
                      _resident((None, D_FF, D_MODEL), expert_map)],
            out_specs=pl.BlockSpec((tm, D_MODEL), lambda i, te, nu: (i, 0))),
        compiler_params=_cparams("arbitrary"),
        name="moe_ffn",
    )(tile_expert, n_used, xs, wg, wu, wd)


def _combine_kernel(pos_ref, meta_ref, h_ref, yb_ref, o_ref, buf0, buf1, sems):
    tm = h_ref.shape[0]
    bufs = (buf0, buf1)

    def issue(j, c):
        for s in range(TOP_K):
            _row_copy(yb_ref, pos_ref[0, s * tm + j], bufs[s], j, sems.at[s]).start(priority=s)
        return c

    lax.fori_loop(0, tm, issue, 0, unroll=8)
    meta = meta_ref[...]
    lane = lax.broadcasted_iota(I32, meta.shape, 1)
    g0 = jnp.sum(jnp.where(lane == _RT_G0, meta, 0.0), axis=-1, keepdims=True)
    g1 = jnp.sum(jnp.where(lane == _RT_G1, meta, 0.0), axis=-1, keepdims=True)
    for s in range(TOP_K):
        pltpu.make_async_copy(yb_ref.at[pl.ds(0, tm)], bufs[s], sems.at[s]).wait()
    o_ref[...] = h_ref[...] + (g0 * buf0[...] + g1 * buf1[...])


def _combine(pos, meta, h2, yb):
    T = h2.shape[0]
    tm = MOVE_TILE
    row = pl.BlockSpec((tm, D_MODEL), lambda i: (i, 0))
    return pl.pallas_call(
        _combine_kernel,
        out_shape=jax.ShapeDtypeStruct((T, D_MODEL), F32),
        grid=(T // tm,),
        in_specs=[pl.BlockSpec((None, 1, TOP_K * tm), lambda i: (i, 0, 0), memory_space=pltpu.SMEM),
                  pl.BlockSpec((tm, LANE), lambda i: (i, 0)), row,
                  pl.BlockSpec(memory_space=pl.ANY)],
        out_specs=row,
        scratch_shapes=[pltpu.VMEM((tm, D_MODEL), F32), pltpu.VMEM((tm, D_MODEL), F32),
                        pltpu.SemaphoreType.DMA((TOP_K,))],
        compiler_params=_cparams("arbitrary"),
        name="moe_combine",
    )(pos, meta, h2, yb)


def _moe(hn, h2, meta, cnt, wg, wu, wd):
    T, D = h2.shape
    counts = cnt[0, :N_EXPERTS].astype(I32)
    padded = (counts + MOE_TILE - 1) // MOE_TILE * MOE_TILE
    ends = jnp.cumsum(padded)
    starts = ends - padded
    n_rows = TOP_K * T + N_EXPERTS * MOE_TILE
    n_tiles = n_rows // MOE_TILE
    experts = jnp.arange(N_EXPERTS, dtype=I32)

    def slot(e_lane, r_lane):
        e = meta[:, e_lane].astype(I32)
        start = jnp.sum(jnp.where(e[:, None] == experts[None, :], starts[None, :], 0), axis=1)
        return start + meta[:, r_lane].astype(I32)

    nt = T // MOVE_TILE
    pos = jnp.concatenate([slot(_RT_E0, _RT_R0).reshape(nt, MOVE_TILE),
                           slot(_RT_E1, _RT_R1).reshape(nt, MOVE_TILE)], axis=1)[:, None, :]
    tile_start = jnp.arange(n_tiles, dtype=I32) * MOE_TILE
    tile_expert = jnp.minimum(jnp.sum(ends[None, :] <= tile_start[:, None], axis=1), N_EXPERTS - 1).astype(I32)
    n_used = (ends[-1:] // MOE_TILE).astype(I32)
    pad_start = jnp.concatenate([starts + counts, ends[-1:]]).astype(I32)
    pad_len = jnp.concatenate([padded - counts, (n_rows - ends[-1:]) // MOE_TILE]).astype(I32)
    xs = _dispatch(pad_start, pad_len, pos, hn, n_rows)
    yb = _moe_ffn(tile_expert, n_used, xs, wg.astype(BF16), wu.astype(BF16), wd.astype(BF16))
    return _combine(pos, meta, h2, yb)


def _block_diag_ones(n, blk):
    i = np.arange(n)
    return (i[:, None] // blk == i[None, :] // blk).astype(np.float32)


def _consts():
    n4 = 4 * CHUNK
    i = np.arange(n4)
    bd = _block_diag_ones(n4, CHUNK)
    tr, tc = i[:, None] % CHUNK, i[None, :] % CHUNK
    strict = bd * (tr > tc)
    incl = bd * (tr >= tc)
    ltri = np.tril(np.ones((CHUNK, CHUNK), np.float32))
    pair_mean = _block_diag_ones(LANE, HEAD_DIM) / HEAD_DIM
    qmask = (i[:, None] // CHUNK == np.arange(LANE)[None, :] // GLA_KEY_DIM).astype(np.float32)
    amask = (i[:, None] % CHUNK >= np.arange(CHUNK)[None, :]).astype(np.float32)
    out = {k: jnp.asarray(v, F32) for k, v in dict(
        bd=bd, strict=strict, incl=incl, ltri=ltri, pair_mean=pair_mean, qmask=qmask, amask=amask).items()}
    out["route_tri"] = jnp.asarray(np.tril(np.ones((ROUTE_TILE, ROUTE_TILE), np.float32), -1), BF16)
    return out


def _pad_rows(w, row0, total):
    return jnp.pad(w.astype(F32), ((row0, total - row0 - w.shape[0]), (0, 0)))


def _pack_rows(rows, width, n_rows):
    rows = [r.reshape(-1).astype(F32) for r in rows]
    rows = [jnp.pad(r, (0, width - r.shape[0])) for r in rows]
    return jnp.pad(jnp.stack(rows), ((0, n_rows - len(rows)), (0, 0)))


def kernel(x, rel_bias, mix_norm_g, w_in, w_out, ffn_norm_g, attn_q_norm_g, attn_k_norm_g, conv_w, conv_b, conv_ln_g, conv_ln_b, rwkv_mu, rwkv_w0, rwkv_w2, rwkv_a0, rwkv_a2, rwkv_g2, rwkv_k_k, rwkv_k_a, rwkv_r_k, rwkv_ln_g, rwkv_ln_b, gla_g2, gla_gb, gla_norm_g, ffn_wg, ffn_wu, ffn_wd, moe_router, moe_wg, moe_wu, moe_wd):
    B, S, D = x.shape
    T = B * S
    cs = _consts()
    bias = _attn_bias_tables(rel_bias)
    h = x.reshape(T, D)
    for layer in range(DEPTH):
        w = w_in[layer]
        dq, dk, dv, dg, do = (w[:, 2176:2304], w[:, 2304:2432], w[:, 2432:2688], w[:, 2688:2704], w[:, 2704:2960])
        wp = jnp.concatenate([w[:, :2176], dq, dv, do, dk, dg, jnp.zeros((D, IN_WP - IN_W), w.dtype)],
                             axis=1).astype(BF16)
        z = _inproj(h, mix_norm_g[layer][None, :], wp)
        z3 = z.reshape(B, S, IN_WP)

        qg = jnp.tile(attn_q_norm_g[layer].astype(F32) * (HEAD_DIM ** -0.5), 2)[None, :]
        kg = jnp.tile(attn_k_norm_g[layer].astype(F32), 2)[None, :]
        ya = _attn_mixer(z3, bias, qg, kg, cs["pair_mean"]).reshape(T, GROUP_W)

        conv_pv = _pack_rows([conv_b[layer], conv_ln_g[layer], conv_ln_b[layer]], GROUP_W, 8)
        group_mean = cs["bd"] * (1.0 / HEAD_DIM)
        mu = rwkv_mu[layer]
        rw_pv = _pack_rows([mu[0:256], mu[256:512], mu[512:768], mu[768:896], rwkv_w0[layer], rwkv_a0[layer],
                            rwkv_k_k[layer], rwkv_k_a[layer], rwkv_r_k[layer], rwkv_ln_g[layer],
                            rwkv_ln_b[layer]], GROUP_W, 16)
        gla_pv = _pack_rows([gla_gb[layer], jnp.tile(gla_norm_g[layer], GROUP_W // HEAD_DIM)], GROUP_W, 8)
        yc, yd, yb = _mixers(z3, [
            _rwkv_part(z3, rw_pv, _pad_rows(rwkv_w2[layer], 0, LANE), _pad_rows(rwkv_a2[layer], 32, LANE),
                       _pad_rows(rwkv_g2[layer], 64, LANE), cs["bd"], cs["strict"], cs["incl"], cs["ltri"]),
            _gla_part(z3, _pad_rows(gla_g2[layer], 0, LANE), gla_pv, cs["bd"], cs["qmask"], cs["amask"],
                      cs["ltri"]),
            _conv_part(z3, conv_w[layer].astype(F32), conv_pv, group_mean)])

        norm_g = ffn_norm_g[layer][None, :]
        ys = (ya, yb.reshape(T, GROUP_W), yc.reshape(T, GROUP_W), yd.reshape(T, GROUP_W))
        i = layer // 2
        if layer % 2 == 0:
            h, hn = _outproj(ys, w_out[layer].astype(BF16), h, norm_g)
            h = _ffn(hn, h, ffn_wg[i].astype(BF16), ffn_wu[i].astype(BF16), ffn_wd[i].astype(BF16))
        else:
            wr = jnp.zeros((D, LANE), F32).at[:, :N_EXPERTS].set(moe_router[i].astype(F32))
            wr_hi = wr.astype(BF16)
            wr = jnp.concatenate([wr_hi, (wr - wr_hi.astype(F32)).astype(BF16)], axis=1)
            h, hn, meta, cnt = _outproj(ys, w_out[layer].astype(BF16), h, norm_g, router=(wr, cs["route_tri"]))
            h = _moe(hn, h, meta, cnt, moe_wg[i], moe_wu[i], moe_wd[i])
    return h.reshape(B, S, D)
```

```python
import functools
import math

import numpy as np
import jax
import jax.numpy as jnp
from jax import lax
from jax.experimental import pallas as pl
from jax.experimental.pallas import tpu as pltpu

F32 = jnp.float32
BF16 = jnp.bfloat16
I32 = jnp.int32
HI = lax.Precision.HIGHEST

D_MODEL = 1024
DEPTH = 2
GROUP_W = 256
NORM_EPS = 1e-6
HEAD_DIM = 64
DILATED_PATTERNS = ((128, 1), (512, 4), (2048, 16))
ATT_BLOCK = 128
N_BUCKETS = 32
REL_MAX_DIST = 2048
CONV_WIDTH = 31
CONV_HALO = 32
CONV_LN_EPS = 1e-5
RWKV_LN_EPS = 64e-5
GLA_KEY_DIM = 32
GLA_TAU = 16.0
CHUNK = 64
D_FF = 2816
N_EXPERTS = 8
TOP_K = 2
IN_W = 2960
IN_WP = 3072
LANE = 128
SUBLANE = 8
VMEM_LIMIT = 48 * 1024 * 1024
MOE_TILE = 512
FFN_ROWS = 512
ROUTE_TILE = 1024
MOVE_TILE = 1024

OFF_AQ, OFF_AK, OFF_AV = 0, 256, 512
OFF_BU, OFF_BG = 768, 1024
OFF_CR, OFF_CK, OFF_CV, OFF_CL = 1280, 1536, 1792, 2048
OFF_DQ, OFF_DV, OFF_DO, OFF_DK, OFF_DG = 2176, 2304, 2560, 2816, 2944


def _cparams(*sem):
    return pltpu.CompilerParams(dimension_semantics=sem, vmem_limit_bytes=VMEM_LIMIT)


def _dot(a, b, prec=None):
    return jnp.dot(a, b, preferred_element_type=F32, precision=prec)


def _mm(a, b):
    return jnp.dot(a.astype(BF16), b.astype(BF16), preferred_element_type=F32)


def _mm_nt(a, b):
    return lax.dot_general(a.astype(BF16), b.astype(BF16), (((1,), (1,)), ((), ())),
                           preferred_element_type=F32)


def _mm_tn(a, b):
    return lax.dot_general(a.astype(BF16), b.astype(BF16), (((0,), (0,)), ((), ())),
                           preferred_element_type=F32)


def _split3(x):
    x1 = x.astype(BF16)
    r1 = x - x1.astype(F32)
    x2 = r1.astype(BF16)
    x3 = (r1 - x2.astype(F32)).astype(BF16)
    return x1, x2, x3


def _dot_exact_rhs(x, m):
    n = x.shape[0]
    y = _dot(jnp.concatenate(_split3(x), axis=0), m)
    return y[0:n] + y[n:2 * n] + y[2 * n:3 * n]


def _dot_exact_lhs(m, x):
    n = x.shape[1]
    y = _dot(m, jnp.concatenate(_split3(x), axis=1))
    return y[:, 0:n] + y[:, n:2 * n] + y[:, 2 * n:3 * n]


def _each(fn, *lists):
    return [fn(*args) for args in zip(*lists)]


def _sigmoid(x):
    return 1.0 / (1.0 + jnp.exp(-x))


def _softplus(x):
    return jnp.maximum(x, 0.0) + jnp.log(1.0 + jnp.exp(-jnp.abs(x)))


def _resident(shape, index_map):
    return pl.BlockSpec(shape, index_map, pipeline_mode=pl.Buffered(1))


def _inproj_kernel(x_ref, g_ref, w_ref, z_ref):
    x = x_ref[...]
    ms = jnp.mean(x * x, axis=-1, keepdims=True)
    xn = (x * lax.rsqrt(ms + NORM_EPS) * g_ref[...]).astype(BF16)
    z_ref[...] = _dot(xn, w_ref[...])


def _inproj(h2, g, w):
    T = h2.shape[0]
    tm = FFN_ROWS
    return pl.pallas_call(
        _inproj_kernel,
        out_shape=jax.ShapeDtypeStruct((T, IN_WP), F32),
        grid=(T // tm,),
        in_specs=[pl.BlockSpec((tm, D_MODEL), lambda i: (i, 0)),
                  pl.BlockSpec((1, D_MODEL), lambda i: (0, 0)),
                  _resident((D_MODEL, IN_WP), lambda i: (0, 0))],
        out_specs=pl.BlockSpec((tm, IN_WP), lambda i: (i, 0)),
        compiler_params=_cparams("parallel"),
        name="inproj",
    )(h2, g, w)


def _t5_bucket(dist):
    max_exact = N_BUCKETS // 2
    n = np.maximum(dist, 0)
    large = max_exact + (np.log(np.maximum(n, 1) / max_exact) / math.log(REL_MAX_DIST / max_exact)
                         * (N_BUCKETS - max_exact)).astype(np.int32)
    large = np.minimum(large, N_BUCKETS - 1)
    return np.where(n < max_exact, n, large).astype(np.int32)


def _bucket_table():
    W = ATT_BLOCK
    delta = np.arange(W)[:, None] + W - np.arange(2 * W)[None, :]
    band = (delta >= 0) & (delta <= W)
    tabs = [np.where(band, _t5_bucket(np.clip(delta, 0, W) * d), -1) for _, d in DILATED_PATTERNS]
    return jnp.asarray(np.stack(tabs), I32)


def _bias_kernel(rb_ref, bk_ref, o_ref):
    n_heads = o_ref.shape[1]
    for di in range(len(DILATED_PATTERNS)):
        bk = bk_ref[di]
        accs = [jnp.full(bk.shape, -jnp.inf, F32) for _ in range(n_heads)]
        for b in range(N_BUCKETS):
            hit = bk == b
            accs = [jnp.where(hit, rb_ref[b, hh], a) for hh, a in enumerate(accs)]
        for hh in range(n_heads):
            o_ref[di, hh] = accs[hh]


def _attn_bias_tables(rel_bias):
    n_heads = rel_bias.shape[1]
    shape = (len(DILATED_PATTERNS), n_heads, ATT_BLOCK, 2 * ATT_BLOCK)
    return pl.pallas_call(
        _bias_kernel,
        out_shape=jax.ShapeDtypeStruct(shape, F32),
        in_specs=[pl.BlockSpec(memory_space=pltpu.SMEM), pl.BlockSpec(memory_space=pltpu.VMEM)],
        out_specs=pl.BlockSpec(memory_space=pltpu.VMEM),
        name="attn_bias",
    )(rel_bias.astype(F32), _bucket_table())


ATT_MERGE_ROWS = 256
ATT_UNROLL = 8


def _attn_kernel(q_ref, k_ref, v_ref, bias_ref, qg_ref, kg_ref, gm_ref, o_ref,
                 qn, kn, qd, kd, vd, od, ld, acc, lse):
    S = q_ref.shape[0]
    rows = 512

    def norm_body(i, c):
        sl = pl.ds(pl.multiple_of(i * rows, rows), rows)
        q = q_ref[sl, :]
        k = k_ref[sl, :]
        gm16 = gm_ref[...].astype(BF16)
        qn[sl, :] = q * lax.rsqrt(_dot_exact_rhs(q * q, gm16) + NORM_EPS) * qg_ref[...]
        kn[sl, :] = k * lax.rsqrt(_dot_exact_rhs(k * k, gm16) + NORM_EPS) * kg_ref[...]
        return c

    lax.fori_loop(0, S // rows, norm_body, 0)

    lane = lax.broadcasted_iota(I32, (ATT_BLOCK, LANE), 1)
    head0 = lane < HEAD_DIM
    col = lax.broadcasted_iota(I32, (ATT_BLOCK, 2 * ATT_BLOCK), 1)

    def run_blocks(di, nb, qs, ks, vs, o_dst, l_dst):
        nu = min(nb, ATT_UNROLL)

        def blk(n):
            return pl.ds(pl.multiple_of(n * ATT_BLOCK, ATT_BLOCK), ATT_BLOCK)

        def body(i, c):
            base = i * nu
            rows = [blk(jnp.maximum(base - 1, 0))] + [blk(base + u) for u in range(nu)]
            kb = [ks[r, :].astype(BF16) for r in rows]
            vb = [vs[r, :].astype(BF16) for r in rows]
            kt = [jnp.concatenate([kb[u], kb[u + 1]], axis=0) for u in range(nu)]
            vt = [jnp.concatenate([vb[u], vb[u + 1]], axis=0) for u in range(nu)]
            qt = [qs[r, :] for r in rows[1:]]
            heads = [(u, hh) for u in range(nu) for hh in range(2)]
            s = [_mm_nt(jnp.where(head0 if hh == 0 else jnp.logical_not(head0), qt[u], 0.0), kt[u])
                 + bias_ref[di, hh] for u, hh in heads]
            no_prev = jnp.logical_and(i == 0, col < ATT_BLOCK)
            s = [jnp.where(no_prev, -jnp.inf, x) if u == 0 else x for (u, hh), x in zip(heads, s)]
            m = [jnp.max(x, axis=-1, keepdims=True) for x in s]
            p = _each(lambda x, mx: jnp.exp(x - mx), s, m)
            den = [jnp.sum(x, axis=-1, keepdims=True) for x in p]
            o = [_mm(x, vt[u]) / dn for (u, hh), x, dn in zip(heads, p, den)]
            ls = _each(lambda mx, dn: mx + jnp.log(dn), m, den)
            for u in range(nu):
                o_dst[rows[u + 1], :] = jnp.where(head0, o[2 * u], o[2 * u + 1])
                l_dst[rows[u + 1], :] = jnp.where(head0, ls[2 * u], ls[2 * u + 1])
            return c

        lax.fori_loop(0, nb // nu, body, 0)

    for di, (_, d) in enumerate(DILATED_PATTERNS):
        L = S // d
        nb = L // ATT_BLOCK
        if d == 1:
            run_blocks(di, nb, qn, kn, v_ref, acc, lse)
            continue
        for r in range(d):
            res = pl.ds(r, L, stride=d)
            qd[0:L, :] = qn[res, :]
            kd[0:L, :] = kn[res, :]
            vd[0:L, :] = v_ref[res, :]
            run_blocks(di, nb, qd, kd, vd, od, ld)
            mr = min(L, ATT_MERGE_ROWS)
            for c in range(L // mr):
                piece = pl.ds(r + c * mr * d, mr, stride=d)
                l0, a0 = lse[piece, :], acc[piece, :]
                l1, a1 = ld[c * mr:(c + 1) * mr, :], od[c * mr:(c + 1) * mr, :]
                m = jnp.maximum(l0, l1)
                e0, e1 = jnp.exp(l0 - m), jnp.exp(l1 - m)
                tot = e0 + e1
                acc[piece, :] = (a0 * e0 + a1 * e1) / tot
                lse[piece, :] = m + jnp.log(tot)

    def out_body(i, c):
        sl = pl.ds(pl.multiple_of(i * rows, rows), rows)
        o_ref[sl, :] = acc[sl, :].astype(BF16)
        return c

    lax.fori_loop(0, S // rows, out_body, 0)


def _attn_mixer(z3, bias, qg, kg, gm):
    B, S, _ = z3.shape
    n_lb = GROUP_W // LANE
    l_max = S // DILATED_PATTERNS[1][1]

    def spec(off):
        return pl.BlockSpec((None, S, LANE), lambda b, p: (b, 0, off // LANE + p))

    vec = pl.BlockSpec((1, LANE), lambda b, p: (0, 0))
    full = pltpu.VMEM((S, LANE), F32)
    part = pltpu.VMEM((l_max, LANE), F32)
    return pl.pallas_call(
        _attn_kernel,
        out_shape=jax.ShapeDtypeStruct((B, S, GROUP_W), BF16),
        grid=(B, n_lb),
        in_specs=[spec(OFF_AQ), spec(OFF_AK), spec(OFF_AV),
                  pl.BlockSpec((len(DILATED_PATTERNS), 2, ATT_BLOCK, 2 * ATT_BLOCK), lambda b, p: (0, p, 0, 0)),
                  vec, vec, pl.BlockSpec((LANE, LANE), lambda b, p: (0, 0))],
        out_specs=pl.BlockSpec((None, S, LANE), lambda b, p: (b, 0, p)),
        scratch_shapes=[full, full, part, part, part, part, part, full, full],
        compiler_params=_cparams("parallel", "parallel"),
        name="attn",
    )(z3, z3, z3, bias, qg, kg, gm)


def _conv_kernel(u_ref, g_ref, cw_ref, pv_ref, gm_ref, o_ref, hbuf, hsh, *, carry_phase):
    ts = u_ref.shape[0]
    t = pl.program_id(1)

    if carry_phase:
        @pl.when(t == 0)
        def _():
            hbuf[0:CONV_HALO, :] = jnp.zeros((CONV_HALO, GROUP_W), F32)

        @pl.when(t > 0)
        def _():
            hbuf[0:CONV_HALO, :] = hbuf[ts:ts + CONV_HALO, :]

        return

    hbuf[CONV_HALO:CONV_HALO + ts, :] = u_ref[...] * _sigmoid(g_ref[...])
    first = CONV_HALO - (CONV_WIDTH - 1)
    n_shift = hsh.shape[1]
    for ph in range(1, SUBLANE):
        hsh[ph - 1, :, :] = hbuf[ph:ph + n_shift, :]
    rows = 64
    gm16 = gm_ref[...].astype(BF16)
    for c in range(ts // rows):
        acc = jnp.zeros((rows, GROUP_W), F32) + pv_ref[0:1, :]
        for j in range(CONV_WIDTH):
            ph = (first + j) % SUBLANE
            s = first + j - ph + c * rows
            tap = hbuf[s:s + rows, :] if ph == 0 else hsh[ph - 1, s:s + rows, :]
            acc = acc + cw_ref[j:j + 1, :] * tap
        mu = _dot_exact_rhs(acc, gm16)
        xc = acc - mu
        var = _dot_exact_rhs(xc * xc, gm16)
        y = xc * lax.rsqrt(var + CONV_LN_EPS) * pv_ref[1:2, :] + pv_ref[2:3, :]
        o_ref[c * rows:(c + 1) * rows, :] = (y * _sigmoid(y)).astype(BF16)


MIX_ROWS = 512


def _seg(off, w):
    return pl.BlockSpec((None, MIX_ROWS, w), lambda b, t: (b, t, off // w))


def _const(shape):
    return pl.BlockSpec(shape, lambda b, t: (0, 0))


def _conv_part(z3, conv_w, pvec, gm):
    ts = MIX_ROWS
    return dict(kernel=_conv_kernel,
                in_specs=[_seg(OFF_BU, GROUP_W), _seg(OFF_BG, GROUP_W), _const((CONV_WIDTH, GROUP_W)),
                          _const((8, GROUP_W)), _const((GROUP_W, GROUP_W))],
                args=(z3, z3, conv_w, pvec, gm),
                scratch=[pltpu.VMEM((ts + CONV_HALO, GROUP_W), F32),
                         pltpu.VMEM((SUBLANE - 1, ts + CONV_HALO - SUBLANE, GROUP_W), F32)])


def _mixers_kernel(*refs, parts):
    n_in = sum(p[1] for p in parts)
    ins, outs, scr = refs[:n_in], refs[n_in:n_in + len(parts)], refs[n_in + len(parts):]
    for carry_phase in (True, False):
        i = s = 0
        for k, (fn, ni, ns) in enumerate(parts):
            fn(*ins[i:i + ni], outs[k], *scr[s:s + ns], carry_phase=carry_phase)
            i += ni
            s += ns


def _mixers(z3, parts):
    B, S, _ = z3.shape
    out_spec = pl.BlockSpec((None, MIX_ROWS, GROUP_W), lambda b, t: (b, t, 0))
    meta = tuple((p["kernel"], len(p["in_specs"]), len(p["scratch"])) for p in parts)
    return pl.pallas_call(
        functools.partial(_mixers_kernel, parts=meta),
        out_shape=[jax.ShapeDtypeStruct((B, S, GROUP_W), BF16)] * len(parts),
        grid=(B, S // MIX_ROWS),
        in_specs=[sp for p in parts for sp in p["in_specs"]],
        out_specs=[out_spec] * len(parts),
        scratch_shapes=[sc for p in parts for sc in p["scratch"]],
        compiler_params=_cparams("parallel", "arbitrary"),
        name="mixers_bcd",
    )(*[a for p in parts for a in p["args"]])


_RW_MU_R, _RW_MU_K, _RW_MU_V, _RW_MU_L, _RW_W0, _RW_A0, _RW_KK, _RW_KA, _RW_RK, _RW_LNG, _RW_LNB = range(11)
SHIFT_PAD = 8
RWKV_GROUP = 8
GLA_GROUP = 8


def _rwkv_kernel(r_ref, k_ref, v_ref, l_ref, pv_ref, w2_ref, a2_ref, g2_ref, bd_ref, st_ref, in_ref, lt_ref,
                 o_ref, rbuf, kbuf, vbuf, lbuf, lw_s, kk_s, be_s, km_s, rr_s, vv_s, gg_s, bo_s, sm_ref, *,
                 carry_phase):
    ts = r_ref.shape[0]
    t = pl.program_id(1)
    bufs = ((rbuf, r_ref), (kbuf, k_ref), (vbuf, v_ref), (lbuf, l_ref))

    if carry_phase:
        @pl.when(t == 0)
        def _():
            for buf, _ in bufs:
                buf[0:SHIFT_PAD, :] = jnp.zeros((SHIFT_PAD, buf.shape[1]), F32)
            sm_ref[...] = jnp.zeros(sm_ref.shape, F32)

        @pl.when(t > 0)
        def _():
            for buf, _ in bufs:
                buf[0:SHIFT_PAD, :] = buf[ts:ts + SHIFT_PAD, :]

        return

    for buf, ref in bufs:
        buf[SHIFT_PAD:SHIFT_PAD + ts, :] = ref[...]

    def pv(i, w=GROUP_W):
        return pv_ref[i:i + 1, 0:w]

    def shifted(buf, mu):
        cur = buf[SHIFT_PAD:SHIFT_PAD + ts, :]
        prev = buf[SHIFT_PAD - 1:SHIFT_PAD - 1 + ts, :]
        return cur + (prev - cur) * mu

    bd = bd_ref[...]
    bd16 = bd.astype(BF16)
    r = shifted(rbuf, pv(_RW_MU_R))
    k = shifted(kbuf, pv(_RW_MU_K))
    v = shifted(vbuf, pv(_RW_MU_V))
    lo = shifted(lbuf, pv(_RW_MU_L, LANE))
    w_log = -_softplus(-(pv(_RW_W0) + _dot(jnp.tanh(lo), w2_ref[...], HI))) - 0.5
    a = _sigmoid(pv(_RW_A0) + _dot(lo, a2_ref[...], HI))
    kk = k * pv(_RW_KK)
    kk = kk / jnp.maximum(jnp.sqrt(_dot_exact_rhs(kk * kk, bd16)), 1e-12)
    km = k * (1.0 + (a - 1.0) * pv(_RW_KA))
    lw_s[...] = -jnp.exp(w_log)
    kk_s[...] = kk
    be_s[...] = kk * a
    km_s[...] = km
    rr_s[...] = r
    vv_s[...] = v
    gg_s[...] = _dot(_sigmoid(lo), g2_ref[...], HI)
    bo_s[...] = _dot_exact_rhs(r * km * pv(_RW_RK), bd16) * v

    n2 = LANE // HEAD_DIM * CHUNK
    bd2 = bd[0:n2, 0:LANE]
    strict = st_ref[0:n2, 0:n2]
    incl = in_ref[0:n2, 0:n2]
    eye = incl - strict
    lt16 = lt_ref[...].astype(BF16)
    gmean16 = (bd2 * (1.0 / HEAD_DIM)).astype(BF16)
    n_pairs = GROUP_W // LANE

    def tile2(x):
        return jnp.concatenate([x] * (LANE // HEAD_DIM), axis=0)

    def fold(x):
        return x[0:CHUNK] + x[CHUNK:2 * CHUNK]

    def group(gi, carry):
        rows = [pl.ds(pl.multiple_of((gi * RWKV_GROUP + j) * CHUNK, CHUNK), CHUNK) for j in range(RWKV_GROUP)]
        items = [(s, slice(p * LANE, (p + 1) * LANE)) for s in rows for p in range(n_pairs)]
        lw = [lw_s[s, ln] for s, ln in items]
        cum = _each(lambda x: _dot_exact_lhs(lt16, x), lw)
        last = [c[CHUNK - 1:CHUNK, :] for c in cum]
        e_neg = [jnp.exp(-c) for c in cum]
        e_last = _each(lambda l, c: jnp.exp(l - c), last, cum)
        bec = [be_s[s, ln] for s, ln in items]
        kmc = [km_s[s, ln] for s, ln in items]
        a_bar = _each(lambda it, c, w: tile2(-kk_s[it[0], it[1]] * jnp.exp(c - w)) * bd2, items, cum, lw)
        r_bar = _each(lambda it, c: tile2(rr_s[it[0], it[1]] * jnp.exp(c)) * bd2, items, cum)
        v_blk = [tile2(vv_s[s, ln]) * bd2 for s, ln in items]
        big = _each(lambda ab, rb, be, km_, en: _mm_nt(
            jnp.concatenate([ab, rb], axis=0),
            jnp.concatenate([tile2(be * en), tile2(km_ * en)], axis=0)), a_bar, r_bar, bec, kmc, e_neg)
        nmat = [b[0:n2, 0:n2] * strict for b in big]
        a_ak = [b[0:n2, n2:2 * n2] * strict for b in big]
        a_rb = [b[n2:2 * n2, 0:n2] * incl for b in big]
        a_rk = [b[n2:2 * n2, n2:2 * n2] * incl for b in big]
        tinv = [eye + n for n in nmat]
        npow = nmat
        for _ in range(5):
            npow = _each(lambda n: _mm(n, n), npow)
            tinv = _each(lambda t_, n: t_ + _mm(t_, n), tinv, npow)
        w_blk = _each(_mm, tinv, a_bar)
        u0 = _each(lambda t_, ak, vb: _mm(t_, _mm(ak, vb)), tinv, a_ak, v_blk)
        o0 = _each(lambda rb, u, rk, vb: fold(_mm(rb, u) + _mm(rk, vb)), a_rb, u0, a_rk, v_blk)
        q_eff = _each(lambda rbar, rb, w: fold(rbar + _mm(rb, w)), r_bar, a_rb, w_blk)
        gh = _each(lambda be, el, w, u: _mm_tn(tile2(be * el) * bd2, jnp.concatenate([w, u], axis=1)),
                   bec, e_last, w_blk, u0)
        g_mat = _each(lambda l, x: eye * jnp.exp(l) + x[:, 0:LANE], last, gh)
        h_mat = _each(lambda x, km_, el, vb: x[:, LANE:2 * LANE] + _mm_tn(tile2(km_ * el) * bd2, vb),
                      gh, kmc, e_last, v_blk)
        outs = [None] * len(items)
        sms = [sm_ref[p] for p in range(n_pairs)]
        for j in range(RWKV_GROUP):
            for p in range(n_pairs):
                i = j * n_pairs + p
                outs[i] = _mm(q_eff[i], sms[p]) + o0[i]
                sms[p] = _mm(g_mat[i], sms[p]) + h_mat[i]
        for p in range(n_pairs):
            sm_ref[p] = sms[p]
        for (s, ln), out in zip(items, outs):
            mu = _dot_exact_rhs(out, gmean16)
            xc = out - mu
            var = _dot_exact_rhs(xc * xc, gmean16)
            y = xc * lax.rsqrt(var + RWKV_LN_EPS) * pv_ref[_RW_LNG:_RW_LNG + 1, ln] + pv_ref[_RW_LNB:_RW_LNB + 1, ln]
            o_ref[s, ln] = ((y + bo_s[s, ln]) * gg_s[s, ln]).astype(BF16)
        return carry

    lax.fori_loop(0, ts // (CHUNK * RWKV_GROUP), group, 0)


def _rwkv_part(z3, pvec, w2p, a2p, g2p, bd, strict, incl, ltri):
    ts = MIX_ROWS
    sq = (4 * CHUNK, 4 * CHUNK)
    wide = pltpu.VMEM((ts, GROUP_W), F32)
    return dict(kernel=_rwkv_kernel,
                in_specs=[_seg(OFF_CR, GROUP_W), _seg(OFF_CK, GROUP_W), _seg(OFF_CV, GROUP_W), _seg(OFF_CL, LANE),
                          _const((16, GROUP_W)), _const((LANE, GROUP_W)), _const((LANE, GROUP_W)),
                          _const((LANE, GROUP_W)), _const(sq), _const(sq), _const(sq), _const((CHUNK, CHUNK))],
                args=(z3, z3, z3, z3, pvec, w2p, a2p, g2p, bd, strict, incl, ltri),
                scratch=[pltpu.VMEM((ts + SHIFT_PAD, GROUP_W), F32)] * 3
                        + [pltpu.VMEM((ts + SHIFT_PAD, LANE), F32)]
                        + [wide] * 8 + [pltpu.VMEM((GROUP_W // LANE, LANE // HEAD_DIM * CHUNK, LANE), F32)])


def _gla_kernel(q_ref, k_ref, v_ref, og_ref, gd_ref, g2_ref, pv_ref, bd_ref, qm_ref, am_ref, lt_ref, o_ref,
                gk_s, st_ref, *, carry_phase):
    ts = q_ref.shape[0]

    if carry_phase:
        @pl.when(pl.program_id(1) == 0)
        def _():
            st_ref[...] = jnp.zeros(st_ref.shape, F32)

        return

    x = _dot(gd_ref[...], g2_ref[...], HI) + pv_ref[0:1, 0:LANE]
    gk_s[...] = -_softplus(-x) * (1.0 / GLA_TAU)
    bd = bd_ref[...]
    qmask = qm_ref[...]
    n_heads = GROUP_W // HEAD_DIM

    lt16 = lt_ref[...].astype(BF16)
    gmean16 = (bd * (1.0 / HEAD_DIM)).astype(BF16)

    def fold(x):
        return x[0:CHUNK] + x[CHUNK:2 * CHUNK] + x[2 * CHUNK:3 * CHUNK] + x[3 * CHUNK:4 * CHUNK]

    def group(gi, carry):
        sls = [pl.ds(pl.multiple_of((gi * GLA_GROUP + j) * CHUNK, CHUNK), CHUNK) for j in range(GLA_GROUP)]
        b = [_dot_exact_lhs(lt16, gk_s[s, :]) for s in sls]
        last = [x[CHUNK - 1:CHUNK, :] for x in b]
        kc = [k_ref[s, :] for s in sls]
        vc = [v_ref[s, :] for s in sls]
        q_d = _each(lambda s, x: q_ref[s, :] * (GLA_KEY_DIM ** -0.5) * jnp.exp(x), sls, b)
        k_d = _each(lambda k_, x: k_ * jnp.exp(-x), kc, b)
        att = _each(lambda q_, k_: _mm_nt(jnp.concatenate([q_] * n_heads, axis=0) * qmask, k_) * am_ref[...],
                    q_d, k_d)
        o_in = _each(lambda a_, v_: fold(_mm(a_, v_) * bd), att, vc)
        upd = _each(lambda v_, k_, l, x: _mm_tn(v_, k_ * jnp.exp(l - x)) * qmask, vc, kc, last, b)
        st = st_ref[...]
        outs = []
        for j in range(GLA_GROUP):
            outs.append(o_in[j] + _mm_nt(q_d[j], st))
            st = st * jnp.exp(last[j]) + upd[j]
        st_ref[...] = st
        for s, o in zip(sls, outs):
            ms = _dot_exact_rhs(o * o, gmean16)
            og = og_ref[s, :]
            y = o * lax.rsqrt(ms + NORM_EPS) * pv_ref[1:2, :] * (og * _sigmoid(og))
            o_ref[s, :] = y.astype(BF16)
        return carry

    lax.fori_loop(0, ts // (CHUNK * GLA_GROUP), group, 0)


def _gla_part(z3, g2p, pvec, bd, qmask, amask, ltri):
    return dict(kernel=_gla_kernel,
                in_specs=[_seg(OFF_DQ, LANE), _seg(OFF_DK, LANE), _seg(OFF_DV, GROUP_W), _seg(OFF_DO, GROUP_W),
                          _seg(OFF_DG, LANE), _const((LANE, LANE)), _const((8, GROUP_W)),
                          _const((GROUP_W, GROUP_W)), _const((GROUP_W, LANE)), _const((GROUP_W, CHUNK)),
                          _const((CHUNK, CHUNK))],
                args=(z3, z3, z3, z3, z3, g2p, pvec, bd, qmask, amask, ltri),
                scratch=[pltpu.VMEM((MIX_ROWS, LANE), F32), pltpu.VMEM((GROUP_W, LANE), F32)])


_RT_E0, _RT_E1, _RT_G0, _RT_G1, _RT_R0, _RT_R1 = range(6)


def _route(xn, wr_ref, tri_ref, run_ref, meta_ref, cnt_ref):
    @pl.when(pl.program_id(0) == 0)
    def _():
        run_ref[...] = jnp.zeros(run_ref.shape, F32)

    tm = xn.shape[0]
    x_hi = xn.astype(BF16)
    x_lo = (xn - x_hi.astype(F32)).astype(BF16)
    y = _mm_nt(wr_ref[...], jnp.concatenate([x_hi, x_lo], axis=0))
    ne = N_EXPERTS
    logits = (y[0:ne, 0:tm] + y[ne:2 * ne, 0:tm]) + (y[0:ne, tm:2 * tm] + y[ne:2 * ne, tm:2 * tm])
    sub = lax.broadcasted_iota(I32, logits.shape, 0).astype(F32)
    m1 = jnp.max(logits, axis=0, keepdims=True)
    i1 = jnp.min(jnp.where(logits == m1, sub, float(ne)), axis=0, keepdims=True)
    rest = jnp.where(sub == i1, -jnp.inf, logits)
    m2 = jnp.max(rest, axis=0, keepdims=True)
    i2 = jnp.min(jnp.where(rest == m2, sub, float(ne)), axis=0, keepdims=True)
    e2 = jnp.exp(m2 - m1)
    den = 1.0 + e2
    hit1 = sub == i1
    hit2 = sub == i2
    member = jnp.where(jnp.logical_or(hit1, hit2), 1.0, 0.0)
    run = run_ref[:, 0:1]
    rank = _dot(member.astype(BF16), tri_ref[...]) + run
    r1 = jnp.sum(jnp.where(hit1, rank, 0.0), axis=0, keepdims=True)
    r2 = jnp.sum(jnp.where(hit2, rank, 0.0), axis=0, keepdims=True)
    run = run + jnp.sum(member, axis=1, keepdims=True)
    run_ref[...] = jnp.broadcast_to(run, run_ref.shape)
    cnt_ref[...] = jnp.broadcast_to(run, cnt_ref.shape)
    rec = jnp.zeros(logits.shape, F32)
    for idx, val in ((_RT_E0, i1), (_RT_E1, i2), (_RT_G0, 1.0 / den), (_RT_G1, e2 / den), (_RT_R0, r1),
                     (_RT_R1, r2)):
        rec = jnp.where(sub == float(idx), val, rec)
    meta_ref[...] = rec


def _outproj_kernel(ya, yb, yc, yd, w_ref, h_ref, g_ref, *rest, route):
    if route:
        wr_ref, tri_ref, ho_ref, hn_ref, meta_ref, cnt_ref, run_ref = rest
    else:
        ho_ref, hn_ref = rest
    acc = h_ref[...]
    for i, y in enumerate((ya, yb, yc, yd)):
        acc = acc + _dot(y[...], w_ref[i * GROUP_W:(i + 1) * GROUP_W, :])
    ho_ref[...] = acc
    ms = jnp.mean(acc * acc, axis=-1, keepdims=True)
    xn = acc * lax.rsqrt(ms + NORM_EPS) * g_ref[...]
    hn_ref[...] = xn.astype(BF16)
    if route:
        _route(xn, wr_ref, tri_ref, run_ref, meta_ref, cnt_ref)


def _outproj(ys, w, h2, g, router=None):
    T = h2.shape[0]
    tm = ROUTE_TILE
    yspec = pl.BlockSpec((tm, GROUP_W), lambda i: (i, 0))
    row = pl.BlockSpec((tm, D_MODEL), lambda i: (i, 0))
    in_specs = [yspec] * 4 + [pl.BlockSpec((D_MODEL, D_MODEL), lambda i: (0, 0)), row,
                              pl.BlockSpec((1, D_MODEL), lambda i: (0, 0))]
    out_shape = [jax.ShapeDtypeStruct((T, D_MODEL), F32), jax.ShapeDtypeStruct((T, D_MODEL), BF16)]
    out_specs = [row, row]
    args = (*ys, w, h2, g)
    scratch = []
    if router is not None:
        in_specs += [pl.BlockSpec((2 * N_EXPERTS, D_MODEL), lambda i: (0, 0)),
                     pl.BlockSpec((tm, tm), lambda i: (0, 0))]
        out_shape += [jax.ShapeDtypeStruct((N_EXPERTS, T), F32), jax.ShapeDtypeStruct((N_EXPERTS, LANE), F32)]
        out_specs += [pl.BlockSpec((N_EXPERTS, tm), lambda i: (0, i)),
                      pl.BlockSpec((N_EXPERTS, LANE), lambda i: (0, 0))]
        args += router
        scratch = [pltpu.VMEM((N_EXPERTS, LANE), F32)]
    return pl.pallas_call(
        functools.partial(_outproj_kernel, route=router is not None),
        out_shape=out_shape,
        grid=(T // tm,),
        in_specs=in_specs,
        out_specs=out_specs,
        scratch_shapes=scratch,
        compiler_params=_cparams("arbitrary"),
        name="outproj_route" if router is not None else "outproj",
    )(*args)


FFN_SPLIT = 2


def _swiglu_rows(x, wg_ref, wu_ref, wd_ref):
    piece = D_FF // FFN_SPLIT
    out = None
    for s in range(FFN_SPLIT):
        cols = slice(s * piece, (s + 1) * piece)
        g = _dot(x, wg_ref[:, cols])
        u = _dot(x, wu_ref[:, cols])
        y = _dot((g * _sigmoid(g) * u).astype(BF16), wd_ref[cols, :])
        out = y if out is None else out + y
    return out


def _ffn_kernel(x_ref, h_ref, wg_ref, wu_ref, wd_ref, o_ref):
    o_ref[...] = h_ref[...] + _swiglu_rows(x_ref[...], wg_ref, wu_ref, wd_ref)


def _ffn(hn, h2, wg, wu, wd):
    T = hn.shape[0]
    tm = FFN_ROWS
    row = pl.BlockSpec((tm, D_MODEL), lambda i: (i, 0))
    return pl.pallas_call(
        _ffn_kernel,
        out_shape=jax.ShapeDtypeStruct((T, D_MODEL), F32),
        grid=(T // tm,),
        in_specs=[row, row,
                  _resident((D_MODEL, D_FF), lambda i: (0, 0)),
                  _resident((D_MODEL, D_FF), lambda i: (0, 0)),
                  _resident((D_FF, D_MODEL), lambda i: (0, 0))],
        out_specs=row,
        compiler_params=_cparams("parallel"),
        name="ffn",
    )(hn, h2, wg, wu, wd)


def _row_copy(src_ref, src_row, dst_ref, dst_row, sem):
    return pltpu.make_async_copy(src_ref.at[pl.ds(src_row, 1)], dst_ref.at[pl.ds(dst_row, 1)], sem)


def _pad_fill_copies(ps_ref, pl_ref, zbuf, xs_ref, zsem):
    out = []
    for e in range(N_EXPERTS):
        start = ps_ref[e]
        end = start + pl_ref[e]
        aligned = jnp.minimum((start + SUBLANE - 1) // SUBLANE * SUBLANE, end)
        for u in range(SUBLANE - 1):
            out.append((start + u < aligned, _row_copy(zbuf, 0, xs_ref, start + u, zsem)))
        rest = end - aligned
        size = MOE_TILE // 2
        while size >= SUBLANE:
            dst = pl.multiple_of(aligned + rest // (2 * size) * (2 * size), SUBLANE)
            out.append(((rest // size) % 2 == 1,
                        pltpu.make_async_copy(zbuf.at[pl.ds(0, size)], xs_ref.at[pl.ds(dst, size)], zsem)))
            size //= 2
    for k in range(N_EXPERTS):
        dst = pl.multiple_of(ps_ref[N_EXPERTS] + k * MOE_TILE, MOE_TILE)
        out.append((k < pl_ref[N_EXPERTS],
                    pltpu.make_async_copy(zbuf, xs_ref.at[pl.ds(dst, MOE_TILE)], zsem)))
    return out


def _dispatch_kernel(ps_ref, pl_ref, pos_ref, hn_ref, xs_ref, buf, zbuf, sem, zsem):
    tm = hn_ref.shape[0]

    @pl.when(pl.program_id(0) == 0)
    def _():
        zbuf[...] = jnp.zeros(zbuf.shape, F32)
        copies = _pad_fill_copies(ps_ref, pl_ref, zbuf, xs_ref, zsem)
        for pred, copy in copies:
            pl.when(pred)(copy.start)
        for pred, copy in copies:
            pl.when(pred)(copy.wait)

    buf[...] = hn_ref[...].astype(F32)

    def issue(j, c):
        for s in range(TOP_K):
            _row_copy(buf, j, xs_ref, pos_ref[0, s * tm + j], sem).start(priority=s)
        return c

    lax.fori_loop(0, tm, issue, 0, unroll=8)
    for _ in range(TOP_K):
        pltpu.make_async_copy(buf, xs_ref.at[pl.ds(0, tm)], sem).wait()


def _dispatch(pad_start, pad_len, pos, hn, n_rows):
    T = hn.shape[0]
    tm = MOVE_TILE
    return pl.pallas_call(
        _dispatch_kernel,
        out_shape=jax.ShapeDtypeStruct((n_rows, D_MODEL), F32),
        grid_spec=pltpu.PrefetchScalarGridSpec(
            num_scalar_prefetch=2,
            grid=(T // tm,),
            in_specs=[pl.BlockSpec((None, 1, TOP_K * tm), lambda i, ps, pn: (i, 0, 0), memory_space=pltpu.SMEM),
                      pl.BlockSpec((tm, D_MODEL), lambda i, ps, pn: (i, 0))],
            out_specs=pl.BlockSpec(memory_space=pl.ANY),
            scratch_shapes=[pltpu.VMEM((tm, D_MODEL), F32), pltpu.VMEM((MOE_TILE, D_MODEL), F32),
                            pltpu.SemaphoreType.DMA, pltpu.SemaphoreType.DMA]),
        compiler_params=_cparams("arbitrary"),
        name="moe_dispatch",
    )(pad_start, pad_len, pos, hn)


def _moe_ffn_kernel(te_ref, nu_ref, x_ref, wg_ref, wu_ref, wd_ref, o_ref):
    del te_ref
    i = pl.program_id(0)

    @pl.when(i < nu_ref[0])
    def _():
        o_ref[...] = _swiglu_rows(x_ref[...].astype(BF16), wg_ref, wu_ref, wd_ref)

    @pl.when(i >= nu_ref[0])
    def _():
        o_ref[...] = jnp.zeros(o_ref.shape, F32)


def _moe_ffn(tile_expert, n_used, xs, wg, wu, wd):
    n_rows = xs.shape[0]
    tm = MOE_TILE

    def row_map(i, te, nu):
        return (jnp.maximum(jnp.minimum(i, nu[0] - 1), 0), 0)

    def expert_map(i, te, nu):
        return (te[i], 0, 0)

    return pl.pallas_call(
        _moe_ffn_kernel,
        out_shape=jax.ShapeDtypeStruct((n_rows, D_MODEL), F32),
        grid_spec=pltpu.PrefetchScalarGridSpec(
            num_scalar_prefetch=2,
            grid=(n_rows // tm,),
            in_specs=[pl.BlockSpec((tm, D_MODEL), row_map),
                      _resident((None, D_MODEL, D_FF), expert_map),
                      _resident((None, D_MODEL, D_FF), expert_map),
                      _resident((None, D_FF, D_MODEL), expert_map)],
            out_specs=pl.BlockSpec((tm, D_MODEL), lambda i, te, nu: (i, 0))),
        compiler_params=_cparams("arbitrary"),
        name="moe_ffn",
    )(tile_expert, n_used, xs, wg, wu, wd)


def _combine_kernel(pos_ref, meta_ref, h_ref, yb_ref, o_ref, buf0, buf1, sems):
    tm = h_ref.shape[0]
    bufs = (buf0, buf1)

    def issue(j, c):
        for s in range(TOP_K):
            _row_copy(yb_ref, pos_ref[0, s * tm + j], bufs[s], j, sems.at[s]).start(priority=s)
        return c

    lax.fori_loop(0, tm, issue, 0, unroll=8)
    fields = jnp.transpose(meta_ref[...])
    g0 = fields[:, _RT_G0:_RT_G0 + 1]
    g1 = fields[:, _RT_G1:_RT_G1 + 1]
    for s in range(TOP_K):
        pltpu.make_async_copy(yb_ref.at[pl.ds(0, tm)], bufs[s], sems.at[s]).wait()
    o_ref[...] = h_ref[...] + (g0 * buf0[...] + g1 * buf1[...])


def _combine(pos, meta, h2, yb):
    T = h2.shape[0]
    tm = MOVE_TILE
    row = pl.BlockSpec((tm, D_MODEL), lambda i: (i, 0))
    return pl.pallas_call(
        _combine_kernel,
        out_shape=jax.ShapeDtypeStruct((T, D_MODEL), F32),
        grid=(T // tm,),
        in_specs=[pl.BlockSpec((None, 1, TOP_K * tm), lambda i: (i, 0, 0), memory_space=pltpu.SMEM),
                  pl.BlockSpec((N_EXPERTS, tm), lambda i: (0, i)), row,
                  pl.BlockSpec(memory_space=pl.ANY)],
        out_specs=row,
        scratch_shapes=[pltpu.VMEM((tm, D_MODEL), F32), pltpu.VMEM((tm, D_MODEL), F32),
                        pltpu.SemaphoreType.DMA((TOP_K,))],
        compiler_params=_cparams("arbitrary"),
        name="moe_combine",
    )(pos, meta, h2, yb)


def _moe(hn, h2, meta, cnt, wg, wu, wd):
    T, D = h2.shape
    counts = cnt[:, 0].astype(I32)
    padded = (counts + MOE_TILE - 1) // MOE_TILE * MOE_TILE
    ends = jnp.cumsum(padded)
    starts = ends - padded
    n_rows = TOP_K * T + N_EXPERTS * MOE_TILE
    n_tiles = n_rows // MOE_TILE
    experts = jnp.arange(N_EXPERTS, dtype=I32)

    def slot(e_row, r_row):
        e = meta[e_row].astype(I32)
        start = jnp.sum(jnp.where(e[None, :] == experts[:, None], starts[:, None], 0), axis=0)
        return start + meta[r_row].astype(I32)

    nt = T // MOVE_TILE
    pos = jnp.concatenate([slot(_RT_E0, _RT_R0).reshape(nt, MOVE_TILE),
                           slot(_RT_E1, _RT_R1).reshape(nt, MOVE_TILE)], axis=1)[:, None, :]
    tile_start = jnp.arange(n_tiles, dtype=I32) * MOE_TILE
    tile_expert = jnp.minimum(jnp.sum(ends[None, :] <= tile_start[:, None], axis=1), N_EXPERTS - 1).astype(I32)
    n_used = (ends[-1:] // MOE_TILE).astype(I32)
    pad_start = jnp.concatenate([starts + counts, ends[-1:]]).astype(I32)
    pad_len = jnp.concatenate([padded - counts, (n_rows - ends[-1:]) // MOE_TILE]).astype(I32)
    xs = _dispatch(pad_start, pad_len, pos, hn, n_rows)
    yb = _moe_ffn(tile_expert, n_used, xs, wg.astype(BF16), wu.astype(BF16), wd.astype(BF16))
    return _combine(pos, meta, h2, yb)


def _block_diag_ones(n, blk):
    i = np.arange(n)
    return (i[:, None] // blk == i[None, :] // blk).astype(np.float32)


def _consts():
    n4 = 4 * CHUNK
    i = np.arange(n4)
    bd = _block_diag_ones(n4, CHUNK)
    tr, tc = i[:, None] % CHUNK, i[None, :] % CHUNK
    strict = bd * (tr > tc)
    incl = bd * (tr >= tc)
    ltri = np.tril(np.ones((CHUNK, CHUNK), np.float32))
    pair_mean = _block_diag_ones(LANE, HEAD_DIM) / HEAD_DIM
    qmask = (i[:, None] // CHUNK == np.arange(LANE)[None, :] // GLA_KEY_DIM).astype(np.float32)
    amask = (i[:, None] % CHUNK >= np.arange(CHUNK)[None, :]).astype(np.float32)
    out = {k: jnp.asarray(v, F32) for k, v in dict(
        bd=bd, strict=strict, incl=incl, ltri=ltri, pair_mean=pair_mean, qmask=qmask, amask=amask).items()}
    out["route_tri"] = jnp.asarray(np.triu(np.ones((ROUTE_TILE, ROUTE_TILE), np.float32), 1), BF16)
    return out


def _pad_rows(w, row0, total):
    return jnp.pad(w.astype(F32), ((row0, total - row0 - w.shape[0]), (0, 0)))


def _pack_rows(rows, width, n_rows):
    rows = [r.reshape(-1).astype(F32) for r in rows]
    rows = [jnp.pad(r, (0, width - r.shape[0])) for r in rows]
    return jnp.pad(jnp.stack(rows), ((0, n_rows - len(rows)), (0, 0)))


def kernel(x, rel_bias, mix_norm_g, w_in, w_out, ffn_norm_g, attn_q_norm_g, attn_k_norm_g, conv_w, conv_b, conv_ln_g, conv_ln_b, rwkv_mu, rwkv_w0, rwkv_w2, rwkv_a0, rwkv_a2, rwkv_g2, rwkv_k_k, rwkv_k_a, rwkv_r_k, rwkv_ln_g, rwkv_ln_b, gla_g2, gla_gb, gla_norm_g, ffn_wg, ffn_wu, ffn_wd, moe_router, moe_wg, moe_wu, moe_wd):
    B, S, D = x.shape
    T = B * S
    cs = _consts()
    bias = _attn_bias_tables(rel_bias)
    h = x.reshape(T, D)
    for layer in range(DEPTH):
        w = w_in[layer]
        dq, dk, dv, dg, do = (w[:, 2176:2304], w[:, 2304:2432], w[:, 2432:2688], w[:, 2688:2704], w[:, 2704:2960])
        wp = jnp.concatenate([w[:, :2176], dq, dv, do, dk, dg, jnp.zeros((D, IN_WP - IN_W), w.dtype)],
                             axis=1).astype(BF16)
        z = _inproj(h, mix_norm_g[layer][None, :], wp)
        z3 = z.reshape(B, S, IN_WP)

        qg = jnp.tile(attn_q_norm_g[layer].astype(F32) * (HEAD_DIM ** -0.5), 2)[None, :]
        kg = jnp.tile(attn_k_norm_g[layer].astype(F32), 2)[None, :]
        ya = _attn_mixer(z3, bias, qg, kg, cs["pair_mean"]).reshape(T, GROUP_W)

        conv_pv = _pack_rows([conv_b[layer], conv_ln_g[layer], conv_ln_b[layer]], GROUP_W, 8)
        group_mean = cs["bd"] * (1.0 / HEAD_DIM)
        mu = rwkv_mu[layer]
        rw_pv = _pack_rows([mu[0:256], mu[256:512], mu[512:768], mu[768:896], rwkv_w0[layer], rwkv_a0[layer],
                            rwkv_k_k[layer], rwkv_k_a[layer], rwkv_r_k[layer], rwkv_ln_g[layer],
                            rwkv_ln_b[layer]], GROUP_W, 16)
        gla_pv = _pack_rows([gla_gb[layer], jnp.tile(gla_norm_g[layer], GROUP_W // HEAD_DIM)], GROUP_W, 8)
        yc, yd, yb = _mixers(z3, [
            _rwkv_part(z3, rw_pv, _pad_rows(rwkv_w2[layer], 0, LANE), _pad_rows(rwkv_a2[layer], 32, LANE),
                       _pad_rows(rwkv_g2[layer], 64, LANE), cs["bd"], cs["strict"], cs["incl"], cs["ltri"]),
            _gla_part(z3, _pad_rows(gla_g2[layer], 0, LANE), gla_pv, cs["bd"], cs["qmask"], cs["amask"],
                      cs["ltri"]),
            _conv_part(z3, conv_w[layer].astype(F32), conv_pv, group_mean)])

        norm_g = ffn_norm_g[layer][None, :]
        ys = (ya, yb.reshape(T, GROUP_W), yc.reshape(T, GROUP_W), yd.reshape(T, GROUP_W))
        i = layer // 2
        if layer % 2 == 0:
            h, hn = _outproj(ys, w_out[layer].astype(BF16), h, norm_g)
            h = _ffn(hn, h, ffn_wg[i].astype(BF16), ffn_wu[i].astype(BF16), ffn_wd[i].astype(BF16))
        else:
            wr = jnp.transpose(moe_router[i].astype(F32))
            wr_hi = wr.astype(BF16)
            wr = jnp.concatenate([wr_hi, (wr - wr_hi.astype(F32)).astype(BF16)], axis=0)
            h, hn, meta, cnt = _outproj(ys, w_out[layer].astype(BF16), h, norm_g, router=(wr, cs["route_tri"]))
            h = _moe(hn, h, meta, cnt, moe_wg[i], moe_wu[i], moe_wd[i])
    return h.reshape(B, S, D)
```

```python
import functools
import math

import numpy as np
import jax
import jax.numpy as jnp
from jax import lax
from jax.experimental import pallas as pl
from jax.experimental.pallas import tpu as pltpu

F32 = jnp.float32
BF16 = jnp.bfloat16
I32 = jnp.int32
HI = lax.Precision.HIGHEST

D_MODEL = 1024
DEPTH = 2
GROUP_W = 256
NORM_EPS = 1e-6
HEAD_DIM = 64
DILATED_PATTERNS = ((128, 1), (512, 4), (2048, 16))
ATT_BLOCK = 128
N_BUCKETS = 32
REL_MAX_DIST = 2048
CONV_WIDTH = 31
CONV_HALO = 32
CONV_LN_EPS = 1e-5
RWKV_LN_EPS = 64e-5
GLA_KEY_DIM = 32
GLA_TAU = 16.0
CHUNK = 64
D_FF = 2816
N_EXPERTS = 8
TOP_K = 2
IN_W = 2960
IN_WP = 3072
LANE = 128
SUBLANE = 8
VMEM_LIMIT = 48 * 1024 * 1024
MOE_TILE = 512
FFN_ROWS = 512
ROUTE_TILE = 1024
MOVE_TILE = 1024

OFF_AQ, OFF_AK, OFF_AV = 0, 256, 512
OFF_BU, OFF_BG = 768, 1024
OFF_CR, OFF_CK, OFF_CV, OFF_CL = 1280, 1536, 1792, 2048
OFF_DQ, OFF_DV, OFF_DO, OFF_DK, OFF_DG = 2176, 2304, 2560, 2816, 2944


def _cparams(*sem):
    return pltpu.CompilerParams(dimension_semantics=sem, vmem_limit_bytes=VMEM_LIMIT)


def _dot(a, b, prec=None):
    return jnp.dot(a, b, preferred_element_type=F32, precision=prec)


def _mm(a, b):
    return jnp.dot(a.astype(BF16), b.astype(BF16), preferred_element_type=F32)


def _mm_nt(a, b):
    return lax.dot_general(a.astype(BF16), b.astype(BF16), (((1,), (1,)), ((), ())),
                           preferred_element_type=F32)


def _mm_tn(a, b):
    return lax.dot_general(a.astype(BF16), b.astype(BF16), (((0,), (0,)), ((), ())),
                           preferred_element_type=F32)


def _split3(x):
    x1 = x.astype(BF16)
    r1 = x - x1.astype(F32)
    x2 = r1.astype(BF16)
    x3 = (r1 - x2.astype(F32)).astype(BF16)
    return x1, x2, x3


def _dot_exact_rhs(x, m):
    n = x.shape[0]
    y = _dot(jnp.concatenate(_split3(x), axis=0), m)
    return y[0:n] + y[n:2 * n] + y[2 * n:3 * n]


def _dot_exact_lhs(m, x):
    n = x.shape[1]
    y = _dot(m, jnp.concatenate(_split3(x), axis=1))
    return y[:, 0:n] + y[:, n:2 * n] + y[:, 2 * n:3 * n]


def _each(fn, *lists):
    return [fn(*args) for args in zip(*lists)]


def _sigmoid(x):
    return 1.0 / (1.0 + jnp.exp(-x))


def _softplus(x):
    return jnp.maximum(x, 0.0) + jnp.log(1.0 + jnp.exp(-jnp.abs(x)))


def _resident(shape, index_map):
    return pl.BlockSpec(shape, index_map, pipeline_mode=pl.Buffered(1))


def _inproj_kernel(x_ref, g_ref, w_ref, z_ref):
    x = x_ref[...]
    ms = jnp.mean(x * x, axis=-1, keepdims=True)
    xn = (x * lax.rsqrt(ms + NORM_EPS) * g_ref[...]).astype(BF16)
    z_ref[...] = _dot(xn, w_ref[...])


def _inproj(h2, g, w):
    T = h2.shape[0]
    tm = FFN_ROWS
    return pl.pallas_call(
        _inproj_kernel,
        out_shape=jax.ShapeDtypeStruct((T, IN_WP), F32),
        grid=(T // tm,),
        in_specs=[pl.BlockSpec((tm, D_MODEL), lambda i: (i, 0)),
                  pl.BlockSpec((1, D_MODEL), lambda i: (0, 0)),
                  _resident((D_MODEL, IN_WP), lambda i: (0, 0))],
        out_specs=pl.BlockSpec((tm, IN_WP), lambda i: (i, 0)),
        compiler_params=_cparams("parallel"),
        name="inproj",
    )(h2, g, w)


def _t5_bucket(dist):
    max_exact = N_BUCKETS // 2
    n = np.maximum(dist, 0)
    large = max_exact + (np.log(np.maximum(n, 1) / max_exact) / math.log(REL_MAX_DIST / max_exact)
                         * (N_BUCKETS - max_exact)).astype(np.int32)
    large = np.minimum(large, N_BUCKETS - 1)
    return np.where(n < max_exact, n, large).astype(np.int32)


def _bucket_table():
    W = ATT_BLOCK
    delta = np.arange(W)[:, None] + W - np.arange(2 * W)[None, :]
    band = (delta >= 0) & (delta <= W)
    tabs = [np.where(band, _t5_bucket(np.clip(delta, 0, W) * d), -1) for _, d in DILATED_PATTERNS]
    return jnp.asarray(np.stack(tabs), I32)


def _bias_kernel(rb_ref, bk_ref, o_ref):
    n_heads = o_ref.shape[1]
    for di in range(len(DILATED_PATTERNS)):
        bk = bk_ref[di]
        accs = [jnp.full(bk.shape, -jnp.inf, F32) for _ in range(n_heads)]
        for b in range(N_BUCKETS):
            hit = bk == b
            accs = [jnp.where(hit, rb_ref[b, hh], a) for hh, a in enumerate(accs)]
        for hh in range(n_heads):
            o_ref[di, hh] = accs[hh]


def _attn_bias_tables(rel_bias):
    n_heads = rel_bias.shape[1]
    shape = (len(DILATED_PATTERNS), n_heads, ATT_BLOCK, 2 * ATT_BLOCK)
    return pl.pallas_call(
        _bias_kernel,
        out_shape=jax.ShapeDtypeStruct(shape, F32),
        in_specs=[pl.BlockSpec(memory_space=pltpu.SMEM), pl.BlockSpec(memory_space=pltpu.VMEM)],
        out_specs=pl.BlockSpec(memory_space=pltpu.VMEM),
        name="attn_bias",
    )(rel_bias.astype(F32), _bucket_table())


ATT_MERGE_ROWS = 256
ATT_UNROLL = 8


def _attn_kernel(q_ref, k_ref, v_ref, bias_ref, qg_ref, kg_ref, gm_ref, o_ref,
                 qn, kn, qd, kd, vd, od, ld, acc, lse):
    S = q_ref.shape[0]
    rows = 512

    def norm_body(i, c):
        sl = pl.ds(pl.multiple_of(i * rows, rows), rows)
        q = q_ref[sl, :]
        k = k_ref[sl, :]
        gm16 = gm_ref[...].astype(BF16)
        qn[sl, :] = q * lax.rsqrt(_dot_exact_rhs(q * q, gm16) + NORM_EPS) * qg_ref[...]
        kn[sl, :] = k * lax.rsqrt(_dot_exact_rhs(k * k, gm16) + NORM_EPS) * kg_ref[...]
        return c

    lax.fori_loop(0, S // rows, norm_body, 0)

    lane = lax.broadcasted_iota(I32, (ATT_BLOCK, LANE), 1)
    head0 = lane < HEAD_DIM
    col = lax.broadcasted_iota(I32, (ATT_BLOCK, 2 * ATT_BLOCK), 1)

    def run_blocks(di, nb, qs, ks, vs, o_dst, l_dst):
        nu = min(nb, ATT_UNROLL)

        def blk(n):
            return pl.ds(pl.multiple_of(n * ATT_BLOCK, ATT_BLOCK), ATT_BLOCK)

        def body(i, c):
            base = i * nu
            rows = [blk(jnp.maximum(base - 1, 0))] + [blk(base + u) for u in range(nu)]
            kb = [ks[r, :].astype(BF16) for r in rows]
            vb = [vs[r, :].astype(BF16) for r in rows]
            kt = [jnp.concatenate([kb[u], kb[u + 1]], axis=0) for u in range(nu)]
            vt = [jnp.concatenate([vb[u], vb[u + 1]], axis=0) for u in range(nu)]
            qt = [qs[r, :] for r in rows[1:]]
            heads = [(u, hh) for u in range(nu) for hh in range(2)]
            s = [_mm_nt(jnp.where(head0 if hh == 0 else jnp.logical_not(head0), qt[u], 0.0), kt[u])
                 + bias_ref[di, hh] for u, hh in heads]
            no_prev = jnp.logical_and(i == 0, col < ATT_BLOCK)
            s = [jnp.where(no_prev, -jnp.inf, x) if u == 0 else x for (u, hh), x in zip(heads, s)]
            m = [jnp.max(x, axis=-1, keepdims=True) for x in s]
            p = _each(lambda x, mx: jnp.exp(x - mx), s, m)
            den = [jnp.sum(x, axis=-1, keepdims=True) for x in p]
            o = [_mm(x, vt[u]) / dn for (u, hh), x, dn in zip(heads, p, den)]
            ls = _each(lambda mx, dn: mx + jnp.log(dn), m, den)
            for u in range(nu):
                o_dst[rows[u + 1], :] = jnp.where(head0, o[2 * u], o[2 * u + 1])
                l_dst[rows[u + 1], :] = jnp.where(head0, ls[2 * u], ls[2 * u + 1])
            return c

        lax.fori_loop(0, nb // nu, body, 0)

    for di, (_, d) in enumerate(DILATED_PATTERNS):
        L = S // d
        nb = L // ATT_BLOCK
        if d == 1:
            run_blocks(di, nb, qn, kn, v_ref, acc, lse)
            continue
        for r in range(d):
            res = pl.ds(r, L, stride=d)
            qd[0:L, :] = qn[res, :]
            kd[0:L, :] = kn[res, :]
            vd[0:L, :] = v_ref[res, :]
            run_blocks(di, nb, qd, kd, vd, od, ld)
            mr = min(L, ATT_MERGE_ROWS)
            for c in range(L // mr):
                piece = pl.ds(r + c * mr * d, mr, stride=d)
                l0, a0 = lse[piece, :], acc[piece, :]
                l1, a1 = ld[c * mr:(c + 1) * mr, :], od[c * mr:(c + 1) * mr, :]
                m = jnp.maximum(l0, l1)
                e0, e1 = jnp.exp(l0 - m), jnp.exp(l1 - m)
                tot = e0 + e1
                acc[piece, :] = (a0 * e0 + a1 * e1) / tot
                lse[piece, :] = m + jnp.log(tot)

    def out_body(i, c):
        sl = pl.ds(pl.multiple_of(i * rows, rows), rows)
        o_ref[sl, :] = acc[sl, :].astype(BF16)
        return c

    lax.fori_loop(0, S // rows, out_body, 0)


def _attn_mixer(z3, bias, qg, kg, gm):
    B, S, _ = z3.shape
    n_lb = GROUP_W // LANE
    l_max = S // DILATED_PATTERNS[1][1]

    def spec(off):
        return pl.BlockSpec((None, S, LANE), lambda b, p: (b, 0, off // LANE + p))

    vec = pl.BlockSpec((1, LANE), lambda b, p: (0, 0))
    full = pltpu.VMEM((S, LANE), F32)
    part = pltpu.VMEM((l_max, LANE), F32)
    return pl.pallas_call(
        _attn_kernel,
        out_shape=jax.ShapeDtypeStruct((B, S, GROUP_W), BF16),
        grid=(B, n_lb),
        in_specs=[spec(OFF_AQ), spec(OFF_AK), spec(OFF_AV),
                  pl.BlockSpec((len(DILATED_PATTERNS), 2, ATT_BLOCK, 2 * ATT_BLOCK), lambda b, p: (0, p, 0, 0)),
                  vec, vec, pl.BlockSpec((LANE, LANE), lambda b, p: (0, 0))],
        out_specs=pl.BlockSpec((None, S, LANE), lambda b, p: (b, 0, p)),
        scratch_shapes=[full, full, part, part, part, part, part, full, full],
        compiler_params=_cparams("parallel", "parallel"),
        name="attn",
    )(z3, z3, z3, bias, qg, kg, gm)


def _conv_kernel(u_ref, g_ref, cw_ref, pv_ref, gm_ref, o_ref, hbuf, hsh, *, carry_phase):
    ts = u_ref.shape[0]
    t = pl.program_id(1)

    if carry_phase:
        @pl.when(t == 0)
        def _():
            hbuf[0:CONV_HALO, :] = jnp.zeros((CONV_HALO, GROUP_W), F32)

        @pl.when(t > 0)
        def _():
            hbuf[0:CONV_HALO, :] = hbuf[ts:ts + CONV_HALO, :]

        return

    hbuf[CONV_HALO:CONV_HALO + ts, :] = u_ref[...] * _sigmoid(g_ref[...])
    first = CONV_HALO - (CONV_WIDTH - 1)
    n_shift = hsh.shape[1]
    for ph in range(1, SUBLANE):
        hsh[ph - 1, :, :] = hbuf[ph:ph + n_shift, :]
    rows = 64
    gm16 = gm_ref[...].astype(BF16)
    for c in range(ts // rows):
        acc = jnp.zeros((rows, GROUP_W), F32) + pv_ref[0:1, :]
        for j in range(CONV_WIDTH):
            ph = (first + j) % SUBLANE
            s = first + j - ph + c * rows
            tap = hbuf[s:s + rows, :] if ph == 0 else hsh[ph - 1, s:s + rows, :]
            acc = acc + cw_ref[j:j + 1, :] * tap
        mu = _dot_exact_rhs(acc, gm16)
        xc = acc - mu
        var = _dot_exact_rhs(xc * xc, gm16)
        y = xc * lax.rsqrt(var + CONV_LN_EPS) * pv_ref[1:2, :] + pv_ref[2:3, :]
        o_ref[c * rows:(c + 1) * rows, :] = (y * _sigmoid(y)).astype(BF16)


MIX_ROWS = 512


def _seg(off, w):
    return pl.BlockSpec((None, MIX_ROWS, w), lambda b, t: (b, t, off // w))


def _const(shape):
    return pl.BlockSpec(shape, lambda b, t: (0, 0))


def _conv_part(z3, conv_w, pvec, gm):
    ts = MIX_ROWS
    return dict(kernel=_conv_kernel,
                in_specs=[_seg(OFF_BU, GROUP_W), _seg(OFF_BG, GROUP_W), _const((CONV_WIDTH, GROUP_W)),
                          _const((8, GROUP_W)), _const((GROUP_W, GROUP_W))],
                args=(z3, z3, conv_w, pvec, gm),
                scratch=[pltpu.VMEM((ts + CONV_HALO, GROUP_W), F32),
                         pltpu.VMEM((SUBLANE - 1, ts + CONV_HALO - SUBLANE, GROUP_W), F32)])


def _mixers_kernel(*refs, parts, n_casts):
    n_in = sum(p[1] for p in parts)
    n_out = len(parts)
    ins, casts_in = refs[:n_in], refs[n_in:n_in + n_casts]
    outs = refs[n_in + n_casts:n_in + n_casts + n_out]
    casts_out = refs[n_in + n_casts + n_out:n_in + 2 * n_casts + n_out]
    scr = refs[n_in + 2 * n_casts + n_out:]
    for carry_phase in (True, False):
        i = s = 0
        for k, (fn, ni, ns) in enumerate(parts):
            fn(*ins[i:i + ni], outs[k], *scr[s:s + ns], carry_phase=carry_phase)
            i += ni
            s += ns
    for src, dst in zip(casts_in, casts_out):
        dst[...] = src[...].astype(BF16)


def _mixers(z3, parts, casts=()):
    B, S, _ = z3.shape
    nt = S // MIX_ROWS
    out_spec = pl.BlockSpec((None, MIX_ROWS, GROUP_W), lambda b, t: (b, t, 0))
    meta = tuple((p["kernel"], len(p["in_specs"]), len(p["scratch"])) for p in parts)
    for c in casts:
        assert c.shape[0] == B * nt, c.shape
    cast_specs = [pl.BlockSpec((None,) + c.shape[1:], lambda b, t: (b * nt + t, 0, 0)) for c in casts]
    return pl.pallas_call(
        functools.partial(_mixers_kernel, parts=meta, n_casts=len(casts)),
        out_shape=[jax.ShapeDtypeStruct((B, S, GROUP_W), BF16)] * len(parts)
                  + [jax.ShapeDtypeStruct(c.shape, BF16) for c in casts],
        grid=(B, nt),
        in_specs=[sp for p in parts for sp in p["in_specs"]] + cast_specs,
        out_specs=[out_spec] * len(parts) + cast_specs,
        scratch_shapes=[sc for p in parts for sc in p["scratch"]],
        compiler_params=_cparams("parallel", "arbitrary"),
        name="mixers_bcd",
    )(*[a for p in parts for a in p["args"]], *casts)


_RW_MU_R, _RW_MU_K, _RW_MU_V, _RW_MU_L, _RW_W0, _RW_A0, _RW_KK, _RW_KA, _RW_RK, _RW_LNG, _RW_LNB = range(11)
SHIFT_PAD = 8
RWKV_GROUP = 8
GLA_GROUP = 8


def _rwkv_kernel(r_ref, k_ref, v_ref, l_ref, pv_ref, w2_ref, a2_ref, g2_ref, bd_ref, st_ref, in_ref, lt_ref,
                 o_ref, rbuf, kbuf, vbuf, lbuf, lw_s, kk_s, be_s, km_s, rr_s, vv_s, gg_s, bo_s, sm_ref, *,
                 carry_phase):
    ts = r_ref.shape[0]
    t = pl.program_id(1)
    bufs = ((rbuf, r_ref), (kbuf, k_ref), (vbuf, v_ref), (lbuf, l_ref))

    if carry_phase:
        @pl.when(t == 0)
        def _():
            for buf, _ in bufs:
                buf[0:SHIFT_PAD, :] = jnp.zeros((SHIFT_PAD, buf.shape[1]), F32)
            sm_ref[...] = jnp.zeros(sm_ref.shape, F32)

        @pl.when(t > 0)
        def _():
            for buf, _ in bufs:
                buf[0:SHIFT_PAD, :] = buf[ts:ts + SHIFT_PAD, :]

        return

    for buf, ref in bufs:
        buf[SHIFT_PAD:SHIFT_PAD + ts, :] = ref[...]

    def pv(i, w=GROUP_W):
        return pv_ref[i:i + 1, 0:w]

    def shifted(buf, mu):
        cur = buf[SHIFT_PAD:SHIFT_PAD + ts, :]
        prev = buf[SHIFT_PAD - 1:SHIFT_PAD - 1 + ts, :]
        return cur + (prev - cur) * mu

    bd = bd_ref[...]
    bd16 = bd.astype(BF16)
    r = shifted(rbuf, pv(_RW_MU_R))
    k = shifted(kbuf, pv(_RW_MU_K))
    v = shifted(vbuf, pv(_RW_MU_V))
    lo = shifted(lbuf, pv(_RW_MU_L, LANE))
    w_log = -_softplus(-(pv(_RW_W0) + _dot(jnp.tanh(lo), w2_ref[...], HI))) - 0.5
    a = _sigmoid(pv(_RW_A0) + _dot(lo, a2_ref[...], HI))
    kk = k * pv(_RW_KK)
    kk = kk / jnp.maximum(jnp.sqrt(_dot_exact_rhs(kk * kk, bd16)), 1e-12)
    km = k * (1.0 + (a - 1.0) * pv(_RW_KA))
    lw_s[...] = -jnp.exp(w_log)
    kk_s[...] = kk
    be_s[...] = kk * a
    km_s[...] = km
    rr_s[...] = r
    vv_s[...] = v
    gg_s[...] = _dot(_sigmoid(lo), g2_ref[...], HI)
    bo_s[...] = _dot_exact_rhs(r * km * pv(_RW_RK), bd16) * v

    n2 = LANE // HEAD_DIM * CHUNK
    bd2 = bd[0:n2, 0:LANE]
    strict = st_ref[0:n2, 0:n2]
    incl = in_ref[0:n2, 0:n2]
    eye = incl - strict
    lt16 = lt_ref[...].astype(BF16)
    gmean16 = (bd2 * (1.0 / HEAD_DIM)).astype(BF16)
    n_pairs = GROUP_W // LANE

    def tile2(x):
        return jnp.concatenate([x] * (LANE // HEAD_DIM), axis=0)

    def fold(x):
        return x[0:CHUNK] + x[CHUNK:2 * CHUNK]

    def group(gi, carry):
        rows = [pl.ds(pl.multiple_of((gi * RWKV_GROUP + j) * CHUNK, CHUNK), CHUNK) for j in range(RWKV_GROUP)]
        items = [(s, slice(p * LANE, (p + 1) * LANE)) for s in rows for p in range(n_pairs)]
        lw = [lw_s[s, ln] for s, ln in items]
        cum = _each(lambda x: _dot_exact_lhs(lt16, x), lw)
        last = [c[CHUNK - 1:CHUNK, :] for c in cum]
        e_neg = [jnp.exp(-c) for c in cum]
        e_last = _each(lambda l, c: jnp.exp(l - c), last, cum)
        bec = [be_s[s, ln] for s, ln in items]
        kmc = [km_s[s, ln] for s, ln in items]
        a_bar = _each(lambda it, c, w: tile2(-kk_s[it[0], it[1]] * jnp.exp(c - w)) * bd2, items, cum, lw)
        r_bar = _each(lambda it, c: tile2(rr_s[it[0], it[1]] * jnp.exp(c)) * bd2, items, cum)
        v_blk = [tile2(vv_s[s, ln]) * bd2 for s, ln in items]
        big = _each(lambda ab, rb, be, km_, en: _mm_nt(
            jnp.concatenate([ab, rb], axis=0),
            jnp.concatenate([tile2(be * en), tile2(km_ * en)], axis=0)), a_bar, r_bar, bec, kmc, e_neg)
        nmat = [b[0:n2, 0:n2] * strict for b in big]
        a_ak = [b[0:n2, n2:2 * n2] * strict for b in big]
        a_rb = [b[n2:2 * n2, 0:n2] * incl for b in big]
        a_rk = [b[n2:2 * n2, n2:2 * n2] * incl for b in big]
        tinv = [eye + n for n in nmat]
        npow = nmat
        for _ in range(5):
            npow = _each(lambda n: _mm(n, n), npow)
            tinv = _each(lambda t_, n: t_ + _mm(t_, n), tinv, npow)
        w_blk = _each(_mm, tinv, a_bar)
        u0 = _each(lambda t_, ak, vb: _mm(t_, _mm(ak, vb)), tinv, a_ak, v_blk)
        o0 = _each(lambda rb, u, rk, vb: fold(_mm(rb, u) + _mm(rk, vb)), a_rb, u0, a_rk, v_blk)
        q_eff = _each(lambda rbar, rb, w: fold(rbar + _mm(rb, w)), r_bar, a_rb, w_blk)
        gh = _each(lambda be, el, w, u: _mm_tn(tile2(be * el) * bd2, jnp.concatenate([w, u], axis=1)),
                   bec, e_last, w_blk, u0)
        g_mat = _each(lambda l, x: eye * jnp.exp(l) + x[:, 0:LANE], last, gh)
        h_mat = _each(lambda x, km_, el, vb: x[:, LANE:2 * LANE] + _mm_tn(tile2(km_ * el) * bd2, vb),
                      gh, kmc, e_last, v_blk)
        outs = [None] * len(items)
        sms = [sm_ref[p] for p in range(n_pairs)]
        for j in range(RWKV_GROUP):
            for p in range(n_pairs):
                i = j * n_pairs + p
                outs[i] = _mm(q_eff[i], sms[p]) + o0[i]
                sms[p] = _mm(g_mat[i], sms[p]) + h_mat[i]
        for p in range(n_pairs):
            sm_ref[p] = sms[p]
        for (s, ln), out in zip(items, outs):
            mu = _dot_exact_rhs(out, gmean16)
            xc = out - mu
            var = _dot_exact_rhs(xc * xc, gmean16)
            y = xc * lax.rsqrt(var + RWKV_LN_EPS) * pv_ref[_RW_LNG:_RW_LNG + 1, ln] + pv_ref[_RW_LNB:_RW_LNB + 1, ln]
            o_ref[s, ln] = ((y + bo_s[s, ln]) * gg_s[s, ln]).astype(BF16)
        return carry

    lax.fori_loop(0, ts // (CHUNK * RWKV_GROUP), group, 0)


def _rwkv_part(z3, pvec, w2p, a2p, g2p, bd, strict, incl, ltri):
    ts = MIX_ROWS
    sq = (4 * CHUNK, 4 * CHUNK)
    wide = pltpu.VMEM((ts, GROUP_W), F32)
    return dict(kernel=_rwkv_kernel,
                in_specs=[_seg(OFF_CR, GROUP_W), _seg(OFF_CK, GROUP_W), _seg(OFF_CV, GROUP_W), _seg(OFF_CL, LANE),
                          _const((16, GROUP_W)), _const((LANE, GROUP_W)), _const((LANE, GROUP_W)),
                          _const((LANE, GROUP_W)), _const(sq), _const(sq), _const(sq), _const((CHUNK, CHUNK))],
                args=(z3, z3, z3, z3, pvec, w2p, a2p, g2p, bd, strict, incl, ltri),
                scratch=[pltpu.VMEM((ts + SHIFT_PAD, GROUP_W), F32)] * 3
                        + [pltpu.VMEM((ts + SHIFT_PAD, LANE), F32)]
                        + [wide] * 8 + [pltpu.VMEM((GROUP_W // LANE, LANE // HEAD_DIM * CHUNK, LANE), F32)])


def _gla_kernel(q_ref, k_ref, v_ref, og_ref, gd_ref, g2_ref, pv_ref, bd_ref, qm_ref, am_ref, lt_ref, o_ref,
                gk_s, st_ref, *, carry_phase):
    ts = q_ref.shape[0]

    if carry_phase:
        @pl.when(pl.program_id(1) == 0)
        def _():
            st_ref[...] = jnp.zeros(st_ref.shape, F32)

        return

    x = _dot(gd_ref[...], g2_ref[...], HI) + pv_ref[0:1, 0:LANE]
    gk_s[...] = -_softplus(-x) * (1.0 / GLA_TAU)
    bd = bd_ref[...]
    qmask = qm_ref[...]
    n_heads = GROUP_W // HEAD_DIM

    lt16 = lt_ref[...].astype(BF16)
    gmean16 = (bd * (1.0 / HEAD_DIM)).astype(BF16)

    def fold(x):
        return x[0:CHUNK] + x[CHUNK:2 * CHUNK] + x[2 * CHUNK:3 * CHUNK] + x[3 * CHUNK:4 * CHUNK]

    def group(gi, carry):
        sls = [pl.ds(pl.multiple_of((gi * GLA_GROUP + j) * CHUNK, CHUNK), CHUNK) for j in range(GLA_GROUP)]
        b = [_dot_exact_lhs(lt16, gk_s[s, :]) for s in sls]
        last = [x[CHUNK - 1:CHUNK, :] for x in b]
        kc = [k_ref[s, :] for s in sls]
        vc = [v_ref[s, :] for s in sls]
        q_d = _each(lambda s, x: q_ref[s, :] * (GLA_KEY_DIM ** -0.5) * jnp.exp(x), sls, b)
        k_d = _each(lambda k_, x: k_ * jnp.exp(-x), kc, b)
        att = _each(lambda q_, k_: _mm_nt(jnp.concatenate([q_] * n_heads, axis=0) * qmask, k_) * am_ref[...],
                    q_d, k_d)
        o_in = _each(lambda a_, v_: fold(_mm(a_, v_) * bd), att, vc)
        upd = _each(lambda v_, k_, l, x: _mm_tn(v_, k_ * jnp.exp(l - x)) * qmask, vc, kc, last, b)
        st = st_ref[...]
        outs = []
        for j in range(GLA_GROUP):
            outs.append(o_in[j] + _mm_nt(q_d[j], st))
            st = st * jnp.exp(last[j]) + upd[j]
        st_ref[...] = st
        for s, o in zip(sls, outs):
            ms = _dot_exact_rhs(o * o, gmean16)
            og = og_ref[s, :]
            y = o * lax.rsqrt(ms + NORM_EPS) * pv_ref[1:2, :] * (og * _sigmoid(og))
            o_ref[s, :] = y.astype(BF16)
        return carry

    lax.fori_loop(0, ts // (CHUNK * GLA_GROUP), group, 0)


def _gla_part(z3, g2p, pvec, bd, qmask, amask, ltri):
    return dict(kernel=_gla_kernel,
                in_specs=[_seg(OFF_DQ, LANE), _seg(OFF_DK, LANE), _seg(OFF_DV, GROUP_W), _seg(OFF_DO, GROUP_W),
                          _seg(OFF_DG, LANE), _const((LANE, LANE)), _const((8, GROUP_W)),
                          _const((GROUP_W, GROUP_W)), _const((GROUP_W, LANE)), _const((GROUP_W, CHUNK)),
                          _const((CHUNK, CHUNK))],
                args=(z3, z3, z3, z3, z3, g2p, pvec, bd, qmask, amask, ltri),
                scratch=[pltpu.VMEM((MIX_ROWS, LANE), F32), pltpu.VMEM((GROUP_W, LANE), F32)])


_RT_E0, _RT_E1, _RT_G0, _RT_G1, _RT_R0, _RT_R1 = range(6)


def _route(xn, wr_ref, tri_ref, run_ref, meta_ref, cnt_ref):
    @pl.when(pl.program_id(0) == 0)
    def _():
        run_ref[...] = jnp.zeros(run_ref.shape, F32)

    tm = xn.shape[0]
    x_hi = xn.astype(BF16)
    x_lo = (xn - x_hi.astype(F32)).astype(BF16)
    y = _mm_nt(wr_ref[...], jnp.concatenate([x_hi, x_lo], axis=0))
    ne = N_EXPERTS
    logits = (y[0:ne, 0:tm] + y[ne:2 * ne, 0:tm]) + (y[0:ne, tm:2 * tm] + y[ne:2 * ne, tm:2 * tm])
    sub = lax.broadcasted_iota(I32, logits.shape, 0).astype(F32)
    m1 = jnp.max(logits, axis=0, keepdims=True)
    i1 = jnp.min(jnp.where(logits == m1, sub, float(ne)), axis=0, keepdims=True)
    rest = jnp.where(sub == i1, -jnp.inf, logits)
    m2 = jnp.max(rest, axis=0, keepdims=True)
    i2 = jnp.min(jnp.where(rest == m2, sub, float(ne)), axis=0, keepdims=True)
    e2 = jnp.exp(m2 - m1)
    den = 1.0 + e2
    hit1 = sub == i1
    hit2 = sub == i2
    member = jnp.where(jnp.logical_or(hit1, hit2), 1.0, 0.0)
    run = run_ref[:, 0:1]
    rank = _dot(member.astype(BF16), tri_ref[...]) + run
    r1 = jnp.sum(jnp.where(hit1, rank, 0.0), axis=0, keepdims=True)
    r2 = jnp.sum(jnp.where(hit2, rank, 0.0), axis=0, keepdims=True)
    run = run + jnp.sum(member, axis=1, keepdims=True)
    run_ref[...] = jnp.broadcast_to(run, run_ref.shape)
    cnt_ref[...] = jnp.broadcast_to(run, cnt_ref.shape)
    rec = jnp.zeros(logits.shape, F32)
    for idx, val in ((_RT_E0, i1), (_RT_E1, i2), (_RT_G0, 1.0 / den), (_RT_G1, e2 / den), (_RT_R0, r1),
                     (_RT_R1, r2)):
        rec = jnp.where(sub == float(idx), val, rec)
    meta_ref[...] = rec


def _outproj_kernel(ya, yb, yc, yd, w_ref, h_ref, g_ref, *rest, route):
    if route:
        wr_ref, tri_ref, ho_ref, hn_ref, meta_ref, cnt_ref, run_ref = rest
    else:
        ho_ref, hn_ref = rest
    acc = h_ref[...]
    for i, y in enumerate((ya, yb, yc, yd)):
        acc = acc + _dot(y[...], w_ref[i * GROUP_W:(i + 1) * GROUP_W, :])
    ho_ref[...] = acc
    ms = jnp.mean(acc * acc, axis=-1, keepdims=True)
    xn = acc * lax.rsqrt(ms + NORM_EPS) * g_ref[...]
    hn_ref[...] = xn.astype(BF16)
    if route:
        _route(xn, wr_ref, tri_ref, run_ref, meta_ref, cnt_ref)


def _outproj(ys, w, h2, g, router=None):
    T = h2.shape[0]
    tm = ROUTE_TILE
    yspec = pl.BlockSpec((tm, GROUP_W), lambda i: (i, 0))
    row = pl.BlockSpec((tm, D_MODEL), lambda i: (i, 0))
    in_specs = [yspec] * 4 + [pl.BlockSpec((D_MODEL, D_MODEL), lambda i: (0, 0)), row,
                              pl.BlockSpec((1, D_MODEL), lambda i: (0, 0))]
    out_shape = [jax.ShapeDtypeStruct((T, D_MODEL), F32), jax.ShapeDtypeStruct((T, D_MODEL), BF16)]
    out_specs = [row, row]
    args = (*ys, w, h2, g)
    scratch = []
    if router is not None:
        in_specs += [pl.BlockSpec((2 * N_EXPERTS, D_MODEL), lambda i: (0, 0)),
                     pl.BlockSpec((tm, tm), lambda i: (0, 0))]
        out_shape += [jax.ShapeDtypeStruct((N_EXPERTS, T), F32), jax.ShapeDtypeStruct((N_EXPERTS, LANE), F32)]
        out_specs += [pl.BlockSpec((N_EXPERTS, tm), lambda i: (0, i)),
                      pl.BlockSpec((N_EXPERTS, LANE), lambda i: (0, 0))]
        args += router
        scratch = [pltpu.VMEM((N_EXPERTS, LANE), F32)]
    return pl.pallas_call(
        functools.partial(_outproj_kernel, route=router is not None),
        out_shape=out_shape,
        grid=(T // tm,),
        in_specs=in_specs,
        out_specs=out_specs,
        scratch_shapes=scratch,
        compiler_params=_cparams("arbitrary"),
        name="outproj_route" if router is not None else "outproj",
    )(*args)


FFN_SPLIT = 2


def _swiglu_rows(x, wg_ref, wu_ref, wd_ref):
    piece = D_FF // FFN_SPLIT
    out = None
    for s in range(FFN_SPLIT):
        cols = slice(s * piece, (s + 1) * piece)
        g = _dot(x, wg_ref[:, cols])
        u = _dot(x, wu_ref[:, cols])
        y = _dot((g * _sigmoid(g) * u).astype(BF16), wd_ref[cols, :])
        out = y if out is None else out + y
    return out


def _ffn_kernel(x_ref, h_ref, wg_ref, wu_ref, wd_ref, o_ref):
    o_ref[...] = h_ref[...] + _swiglu_rows(x_ref[...], wg_ref, wu_ref, wd_ref)


def _ffn(hn, h2, wg, wu, wd):
    T = hn.shape[0]
    tm = FFN_ROWS
    row = pl.BlockSpec((tm, D_MODEL), lambda i: (i, 0))
    return pl.pallas_call(
        _ffn_kernel,
        out_shape=jax.ShapeDtypeStruct((T, D_MODEL), F32),
        grid=(T // tm,),
        in_specs=[row, row,
                  _resident((D_MODEL, D_FF), lambda i: (0, 0)),
                  _resident((D_MODEL, D_FF), lambda i: (0, 0)),
                  _resident((D_FF, D_MODEL), lambda i: (0, 0))],
        out_specs=row,
        compiler_params=_cparams("parallel"),
        name="ffn",
    )(hn, h2, wg, wu, wd)


def _row_copy(src_ref, src_row, dst_ref, dst_row, sem):
    return pltpu.make_async_copy(src_ref.at[pl.ds(src_row, 1)], dst_ref.at[pl.ds(dst_row, 1)], sem)


def _pad_fill_copies(ps_ref, pl_ref, zbuf, xs_ref, zsem):
    out = []
    for e in range(N_EXPERTS):
        start = ps_ref[e]
        end = start + pl_ref[e]
        aligned = jnp.minimum((start + SUBLANE - 1) // SUBLANE * SUBLANE, end)
        for u in range(SUBLANE - 1):
            out.append((start + u < aligned, _row_copy(zbuf, 0, xs_ref, start + u, zsem)))
        rest = end - aligned
        size = MOE_TILE // 2
        while size >= SUBLANE:
            dst = pl.multiple_of(aligned + rest // (2 * size) * (2 * size), SUBLANE)
            out.append(((rest // size) % 2 == 1,
                        pltpu.make_async_copy(zbuf.at[pl.ds(0, size)], xs_ref.at[pl.ds(dst, size)], zsem)))
            size //= 2
    for k in range(N_EXPERTS):
        dst = pl.multiple_of(ps_ref[N_EXPERTS] + k * MOE_TILE, MOE_TILE)
        out.append((k < pl_ref[N_EXPERTS],
                    pltpu.make_async_copy(zbuf, xs_ref.at[pl.ds(dst, MOE_TILE)], zsem)))
    return out


def _dispatch_kernel(ps_ref, pl_ref, pos_ref, hn_ref, xs_ref, buf, zbuf, sem, zsem):
    tm = hn_ref.shape[0]

    @pl.when(pl.program_id(0) == 0)
    def _():
        zbuf[...] = jnp.zeros(zbuf.shape, F32)
        copies = _pad_fill_copies(ps_ref, pl_ref, zbuf, xs_ref, zsem)
        for pred, copy in copies:
            pl.when(pred)(copy.start)
        for pred, copy in copies:
            pl.when(pred)(copy.wait)

    buf[...] = hn_ref[...].astype(F32)

    def issue(j, c):
        for s in range(TOP_K):
            _row_copy(buf, j, xs_ref, pos_ref[0, s * tm + j], sem).start(priority=s)
        return c

    lax.fori_loop(0, tm, issue, 0, unroll=8)
    for _ in range(TOP_K):
        pltpu.make_async_copy(buf, xs_ref.at[pl.ds(0, tm)], sem).wait()


def _dispatch(pad_start, pad_len, pos, hn, n_rows):
    T = hn.shape[0]
    tm = MOVE_TILE
    return pl.pallas_call(
        _dispatch_kernel,
        out_shape=jax.ShapeDtypeStruct((n_rows, D_MODEL), F32),
        grid_spec=pltpu.PrefetchScalarGridSpec(
            num_scalar_prefetch=2,
            grid=(T // tm,),
            in_specs=[pl.BlockSpec((None, 1, TOP_K * tm), lambda i, ps, pn: (i, 0, 0), memory_space=pltpu.SMEM),
                      pl.BlockSpec((tm, D_MODEL), lambda i, ps, pn: (i, 0))],
            out_specs=pl.BlockSpec(memory_space=pl.ANY),
            scratch_shapes=[pltpu.VMEM((tm, D_MODEL), F32), pltpu.VMEM((MOE_TILE, D_MODEL), F32),
                            pltpu.SemaphoreType.DMA, pltpu.SemaphoreType.DMA]),
        compiler_params=_cparams("arbitrary"),
        name="moe_dispatch",
    )(pad_start, pad_len, pos, hn)


def _moe_ffn_kernel(te_ref, nu_ref, x_ref, wg_ref, wu_ref, wd_ref, o_ref):
    del te_ref
    i = pl.program_id(0)

    @pl.when(i < nu_ref[0])
    def _():
        o_ref[...] = _swiglu_rows(x_ref[...].astype(BF16), wg_ref, wu_ref, wd_ref)

    @pl.when(i >= nu_ref[0])
    def _():
        o_ref[...] = jnp.zeros(o_ref.shape, F32)


def _moe_ffn(tile_expert, n_used, xs, wg, wu, wd):
    n_rows = xs.shape[0]
    tm = MOE_TILE

    def row_map(i, te, nu):
        return (jnp.maximum(jnp.minimum(i, nu[0] - 1), 0), 0)

    def expert_map(i, te, nu):
        return (te[i], 0, 0)

    return pl.pallas_call(
        _moe_ffn_kernel,
        out_shape=jax.ShapeDtypeStruct((n_rows, D_MODEL), F32),
        grid_spec=pltpu.PrefetchScalarGridSpec(
            num_scalar_prefetch=2,
            grid=(n_rows // tm,),
            in_specs=[pl.BlockSpec((tm, D_MODEL), row_map),
                      _resident((None, D_MODEL, D_FF), expert_map),
                      _resident((None, D_MODEL, D_FF), expert_map),
                      _resident((None, D_FF, D_MODEL), expert_map)],
            out_specs=pl.BlockSpec((tm, D_MODEL), lambda i, te, nu: (i, 0))),
        compiler_params=_cparams("arbitrary"),
        name="moe_ffn",
    )(tile_expert, n_used, xs, wg, wu, wd)


def _combine_kernel(pos_ref, meta_ref, h_ref, yb_ref, o_ref, buf0, buf1, sems):
    tm = h_ref.shape[0]
    bufs = (buf0, buf1)

    def issue(j, c):
        for s in range(TOP_K):
            _row_copy(yb_ref, pos_ref[0, s * tm + j], bufs[s], j, sems.at[s]).start(priority=s)
        return c

    lax.fori_loop(0, tm, issue, 0, unroll=8)
    fields = jnp.transpose(meta_ref[...])
    g0 = fields[:, _RT_G0:_RT_G0 + 1]
    g1 = fields[:, _RT_G1:_RT_G1 + 1]
    for s in range(TOP_K):
        pltpu.make_async_copy(yb_ref.at[pl.ds(0, tm)], bufs[s], sems.at[s]).wait()
    o_ref[...] = h_ref[...] + (g0 * buf0[...] + g1 * buf1[...])


def _combine(pos, meta, h2, yb):
    T = h2.shape[0]
    tm = MOVE_TILE
    row = pl.BlockSpec((tm, D_MODEL), lambda i: (i, 0))
    return pl.pallas_call(
        _combine_kernel,
        out_shape=jax.ShapeDtypeStruct((T, D_MODEL), F32),
        grid=(T // tm,),
        in_specs=[pl.BlockSpec((None, 1, TOP_K * tm), lambda i: (i, 0, 0), memory_space=pltpu.SMEM),
                  pl.BlockSpec((N_EXPERTS, tm), lambda i: (0, i)), row,
                  pl.BlockSpec(memory_space=pl.ANY)],
        out_specs=row,
        scratch_shapes=[pltpu.VMEM((tm, D_MODEL), F32), pltpu.VMEM((tm, D_MODEL), F32),
                        pltpu.SemaphoreType.DMA((TOP_K,))],
        compiler_params=_cparams("arbitrary"),
        name="moe_combine",
    )(pos, meta, h2, yb)


def _moe(hn, h2, meta, cnt, wg, wu, wd):
    T, D = h2.shape
    counts = cnt[:, 0].astype(I32)
    padded = (counts + MOE_TILE - 1) // MOE_TILE * MOE_TILE
    ends = jnp.cumsum(padded)
    starts = ends - padded
    n_rows = TOP_K * T + N_EXPERTS * MOE_TILE
    n_tiles = n_rows // MOE_TILE
    experts = jnp.arange(N_EXPERTS, dtype=I32)

    def slot(e_row, r_row):
        e = meta[e_row].astype(I32)
        start = jnp.sum(jnp.where(e[None, :] == experts[:, None], starts[:, None], 0), axis=0)
        return start + meta[r_row].astype(I32)

    nt = T // MOVE_TILE
    pos = jnp.concatenate([slot(_RT_E0, _RT_R0).reshape(nt, MOVE_TILE),
                           slot(_RT_E1, _RT_R1).reshape(nt, MOVE_TILE)], axis=1)[:, None, :]
    tile_start = jnp.arange(n_tiles, dtype=I32) * MOE_TILE
    tile_expert = jnp.minimum(jnp.sum(ends[None, :] <= tile_start[:, None], axis=1), N_EXPERTS - 1).astype(I32)
    n_used = (ends[-1:] // MOE_TILE).astype(I32)
    pad_start = jnp.concatenate([starts + counts, ends[-1:]]).astype(I32)
    pad_len = jnp.concatenate([padded - counts, (n_rows - ends[-1:]) // MOE_TILE]).astype(I32)
    xs = _dispatch(pad_start, pad_len, pos, hn, n_rows)
    yb = _moe_ffn(tile_expert, n_used, xs, wg, wu, wd)
    return _combine(pos, meta, h2, yb)


def _block_diag_ones(n, blk):
    i = np.arange(n)
    return (i[:, None] // blk == i[None, :] // blk).astype(np.float32)


def _consts():
    n4 = 4 * CHUNK
    i = np.arange(n4)
    bd = _block_diag_ones(n4, CHUNK)
    tr, tc = i[:, None] % CHUNK, i[None, :] % CHUNK
    strict = bd * (tr > tc)
    incl = bd * (tr >= tc)
    ltri = np.tril(np.ones((CHUNK, CHUNK), np.float32))
    pair_mean = _block_diag_ones(LANE, HEAD_DIM) / HEAD_DIM
    qmask = (i[:, None] // CHUNK == np.arange(LANE)[None, :] // GLA_KEY_DIM).astype(np.float32)
    amask = (i[:, None] % CHUNK >= np.arange(CHUNK)[None, :]).astype(np.float32)
    out = {k: jnp.asarray(v, F32) for k, v in dict(
        bd=bd, strict=strict, incl=incl, ltri=ltri, pair_mean=pair_mean, qmask=qmask, amask=amask).items()}
    out["route_tri"] = jnp.asarray(np.triu(np.ones((ROUTE_TILE, ROUTE_TILE), np.float32), 1), BF16)
    return out


def _pad_rows(w, row0, total):
    return jnp.pad(w.astype(F32), ((row0, total - row0 - w.shape[0]), (0, 0)))


def _pack_rows(rows, width, n_rows):
    rows = [r.reshape(-1).astype(F32) for r in rows]
    rows = [jnp.pad(r, (0, width - r.shape[0])) for r in rows]
    return jnp.pad(jnp.stack(rows), ((0, n_rows - len(rows)), (0, 0)))


def kernel(x, rel_bias, mix_norm_g, w_in, w_out, ffn_norm_g, attn_q_norm_g, attn_k_norm_g, conv_w, conv_b, conv_ln_g, conv_ln_b, rwkv_mu, rwkv_w0, rwkv_w2, rwkv_a0, rwkv_a2, rwkv_g2, rwkv_k_k, rwkv_k_a, rwkv_r_k, rwkv_ln_g, rwkv_ln_b, gla_g2, gla_gb, gla_norm_g, ffn_wg, ffn_wu, ffn_wd, moe_router, moe_wg, moe_wu, moe_wd):
    B, S, D = x.shape
    T = B * S
    cs = _consts()
    bias = _attn_bias_tables(rel_bias)
    h = x.reshape(T, D)
    for layer in range(DEPTH):
        w = w_in[layer]
        dq, dk, dv, dg, do = (w[:, 2176:2304], w[:, 2304:2432], w[:, 2432:2688], w[:, 2688:2704], w[:, 2704:2960])
        wp = jnp.concatenate([w[:, :2176], dq, dv, do, dk, dg, jnp.zeros((D, IN_WP - IN_W), w.dtype)],
                             axis=1).astype(BF16)
        z = _inproj(h, mix_norm_g[layer][None, :], wp)
        z3 = z.reshape(B, S, IN_WP)

        qg = jnp.tile(attn_q_norm_g[layer].astype(F32) * (HEAD_DIM ** -0.5), 2)[None, :]
        kg = jnp.tile(attn_k_norm_g[layer].astype(F32), 2)[None, :]
        ya = _attn_mixer(z3, bias, qg, kg, cs["pair_mean"]).reshape(T, GROUP_W)

        conv_pv = _pack_rows([conv_b[layer], conv_ln_g[layer], conv_ln_b[layer]], GROUP_W, 8)
        group_mean = cs["bd"] * (1.0 / HEAD_DIM)
        mu = rwkv_mu[layer]
        rw_pv = _pack_rows([mu[0:256], mu[256:512], mu[512:768], mu[768:896], rwkv_w0[layer], rwkv_a0[layer],
                            rwkv_k_k[layer], rwkv_k_a[layer], rwkv_r_k[layer], rwkv_ln_g[layer],
                            rwkv_ln_b[layer]], GROUP_W, 16)
        gla_pv = _pack_rows([gla_gb[layer], jnp.tile(gla_norm_g[layer], GROUP_W // HEAD_DIM)], GROUP_W, 8)
        casts, n_steps = [], T // MIX_ROWS
        if layer % 2 == 0 and layer + 1 < DEPTH:
            nxt = (layer + 1) // 2
            casts = [w.astype(F32).reshape(n_steps, -1, w.shape[-1]) for w in (moe_wg[nxt], moe_wu[nxt], moe_wd[nxt])]
        yc, yd, yb, *cast_out = _mixers(z3, [
            _rwkv_part(z3, rw_pv, _pad_rows(rwkv_w2[layer], 0, LANE), _pad_rows(rwkv_a2[layer], 32, LANE),
                       _pad_rows(rwkv_g2[layer], 64, LANE), cs["bd"], cs["strict"], cs["incl"], cs["ltri"]),
            _gla_part(z3, _pad_rows(gla_g2[layer], 0, LANE), gla_pv, cs["bd"], cs["qmask"], cs["amask"],
                      cs["ltri"]),
            _conv_part(z3, conv_w[layer].astype(F32), conv_pv, group_mean)], casts)
        if cast_out:
            experts_bf16 = [c.reshape(w.shape) for c, w in zip(cast_out, (moe_wg[nxt], moe_wu[nxt], moe_wd[nxt]))]

        norm_g = ffn_norm_g[layer][None, :]
        ys = (ya, yb.reshape(T, GROUP_W), yc.reshape(T, GROUP_W), yd.reshape(T, GROUP_W))
        i = layer // 2
        if layer % 2 == 0:
            h, hn = _outproj(ys, w_out[layer].astype(BF16), h, norm_g)
            h = _ffn(hn, h, ffn_wg[i].astype(BF16), ffn_wu[i].astype(BF16), ffn_wd[i].astype(BF16))
        else:
            wr = jnp.transpose(moe_router[i].astype(F32))
            wr_hi = wr.astype(BF16)
            wr = jnp.concatenate([wr_hi, (wr - wr_hi.astype(F32)).astype(BF16)], axis=0)
            h, hn, meta, cnt = _outproj(ys, w_out[layer].astype(BF16), h, norm_g, router=(wr, cs["route_tri"]))
            h = _moe(hn, h, meta, cnt, *experts_bf16)
    return h.reshape(B, S, D)
```

```python
import functools
import math

import numpy as np
import jax
import jax.numpy as jnp
from jax import lax
from jax.experimental import pallas as pl
from jax.experimental.pallas import tpu as pltpu

F32 = jnp.float32
BF16 = jnp.bfloat16
I32 = jnp.int32
HI = lax.Precision.HIGHEST

D_MODEL = 1024
DEPTH = 2
GROUP_W = 256
NORM_EPS = 1e-6
HEAD_DIM = 64
DILATED_PATTERNS = ((128, 1), (512, 4), (2048, 16))
ATT_BLOCK = 128
N_BUCKETS = 32
REL_MAX_DIST = 2048
CONV_WIDTH = 31
CONV_HALO = 32
CONV_LN_EPS = 1e-5
RWKV_LN_EPS = 64e-5
GLA_KEY_DIM = 32
GLA_TAU = 16.0
CHUNK = 64
D_FF = 2816
N_EXPERTS = 8
TOP_K = 2
IN_W = 2960
IN_WP = 3072
LANE = 128
SUBLANE = 8
VMEM_LIMIT = 48 * 1024 * 1024
MOE_TILE = 512
FFN_ROWS = 512
ROUTE_TILE = 1024
MOVE_TILE = 1024

OFF_AQ, OFF_AK, OFF_AV = 0, 256, 512
OFF_BU, OFF_BG = 768, 1024
OFF_CR, OFF_CK, OFF_CV, OFF_CL = 1280, 1536, 1792, 2048
OFF_DQ, OFF_DV, OFF_DO, OFF_DK, OFF_DG = 2176, 2304, 2560, 2816, 2944


def _cparams(*sem):
    return pltpu.CompilerParams(dimension_semantics=sem, vmem_limit_bytes=VMEM_LIMIT)


def _dot(a, b, prec=None):
    return jnp.dot(a, b, preferred_element_type=F32, precision=prec)


def _mm(a, b):
    return jnp.dot(a.astype(BF16), b.astype(BF16), preferred_element_type=F32)


def _mm_nt(a, b):
    return lax.dot_general(a.astype(BF16), b.astype(BF16), (((1,), (1,)), ((), ())),
                           preferred_element_type=F32)


def _mm_tn(a, b):
    return lax.dot_general(a.astype(BF16), b.astype(BF16), (((0,), (0,)), ((), ())),
                           preferred_element_type=F32)


def _split3(x):
    x1 = x.astype(BF16)
    r1 = x - x1.astype(F32)
    x2 = r1.astype(BF16)
    x3 = (r1 - x2.astype(F32)).astype(BF16)
    return x1, x2, x3


def _dot_exact_rhs(x, m):
    n = x.shape[0]
    y = _dot(jnp.concatenate(_split3(x), axis=0), m)
    return y[0:n] + y[n:2 * n] + y[2 * n:3 * n]


def _dot_exact_lhs(m, x):
    n = x.shape[1]
    y = _dot(m, jnp.concatenate(_split3(x), axis=1))
    return y[:, 0:n] + y[:, n:2 * n] + y[:, 2 * n:3 * n]


def _each(fn, *lists):
    return [fn(*args) for args in zip(*lists)]


def _sigmoid(x):
    return 1.0 / (1.0 + jnp.exp(-x))


def _softplus(x):
    return jnp.maximum(x, 0.0) + jnp.log(1.0 + jnp.exp(-jnp.abs(x)))


def _resident(shape, index_map):
    return pl.BlockSpec(shape, index_map, pipeline_mode=pl.Buffered(1))


def _inproj_kernel(x_ref, g_ref, w_ref, z_ref):
    x = x_ref[...]
    ms = jnp.mean(x * x, axis=-1, keepdims=True)
    xn = (x * lax.rsqrt(ms + NORM_EPS) * g_ref[...]).astype(BF16)
    z_ref[...] = _dot(xn, w_ref[...])


def _inproj(h2, g, w):
    T = h2.shape[0]
    tm = FFN_ROWS
    return pl.pallas_call(
        _inproj_kernel,
        out_shape=jax.ShapeDtypeStruct((T, IN_WP), F32),
        grid=(T // tm,),
        in_specs=[pl.BlockSpec((tm, D_MODEL), lambda i: (i, 0)),
                  pl.BlockSpec((1, D_MODEL), lambda i: (0, 0)),
                  _resident((D_MODEL, IN_WP), lambda i: (0, 0))],
        out_specs=pl.BlockSpec((tm, IN_WP), lambda i: (i, 0)),
        compiler_params=_cparams("parallel"),
        name="inproj",
    )(h2, g, w)


def _t5_bucket(dist):
    max_exact = N_BUCKETS // 2
    n = np.maximum(dist, 0)
    large = max_exact + (np.log(np.maximum(n, 1) / max_exact) / math.log(REL_MAX_DIST / max_exact)
                         * (N_BUCKETS - max_exact)).astype(np.int32)
    large = np.minimum(large, N_BUCKETS - 1)
    return np.where(n < max_exact, n, large).astype(np.int32)


def _bucket_table():
    W = ATT_BLOCK
    delta = np.arange(W)[:, None] + W - np.arange(2 * W)[None, :]
    band = (delta >= 0) & (delta <= W)
    tabs = [np.where(band, _t5_bucket(np.clip(delta, 0, W) * d), -1) for _, d in DILATED_PATTERNS]
    return jnp.asarray(np.stack(tabs), I32)


def _bias_kernel(rb_ref, bk_ref, o_ref):
    n_heads = o_ref.shape[1]
    for di in range(len(DILATED_PATTERNS)):
        bk = bk_ref[di]
        accs = [jnp.full(bk.shape, -jnp.inf, F32) for _ in range(n_heads)]
        for b in range(N_BUCKETS):
            hit = bk == b
            accs = [jnp.where(hit, rb_ref[b, hh], a) for hh, a in enumerate(accs)]
        for hh in range(n_heads):
            o_ref[di, hh] = accs[hh]


def _attn_bias_tables(rel_bias):
    n_heads = rel_bias.shape[1]
    shape = (len(DILATED_PATTERNS), n_heads, ATT_BLOCK, 2 * ATT_BLOCK)
    return pl.pallas_call(
        _bias_kernel,
        out_shape=jax.ShapeDtypeStruct(shape, F32),
        in_specs=[pl.BlockSpec(memory_space=pltpu.SMEM), pl.BlockSpec(memory_space=pltpu.VMEM)],
        out_specs=pl.BlockSpec(memory_space=pltpu.VMEM),
        name="attn_bias",
    )(rel_bias.astype(F32), _bucket_table())


ATT_MERGE_ROWS = 256
ATT_UNROLL = 8


def _attn_kernel(q_ref, k_ref, v_ref, bias_ref, qg_ref, kg_ref, gm_ref, o_ref,
                 qn, kn, qd, kd, vd, od, ld, acc, lse):
    S = q_ref.shape[0]
    rows = 512

    def norm_body(i, c):
        sl = pl.ds(pl.multiple_of(i * rows, rows), rows)
        q = q_ref[sl, :]
        k = k_ref[sl, :]
        gm16 = gm_ref[...].astype(BF16)
        qn[sl, :] = q * lax.rsqrt(_dot_exact_rhs(q * q, gm16) + NORM_EPS) * qg_ref[...]
        kn[sl, :] = k * lax.rsqrt(_dot_exact_rhs(k * k, gm16) + NORM_EPS) * kg_ref[...]
        return c

    lax.fori_loop(0, S // rows, norm_body, 0)

    lane = lax.broadcasted_iota(I32, (ATT_BLOCK, LANE), 1)
    head0 = lane < HEAD_DIM
    col = lax.broadcasted_iota(I32, (ATT_BLOCK, 2 * ATT_BLOCK), 1)

    def run_blocks(di, nb, qs, ks, vs, o_dst, l_dst):
        nu = min(nb, ATT_UNROLL)

        def blk(n):
            return pl.ds(pl.multiple_of(n * ATT_BLOCK, ATT_BLOCK), ATT_BLOCK)

        def body(i, c):
            base = i * nu
            rows = [blk(jnp.maximum(base - 1, 0))] + [blk(base + u) for u in range(nu)]
            kb = [ks[r, :].astype(BF16) for r in rows]
            vb = [vs[r, :].astype(BF16) for r in rows]
            kt = [jnp.concatenate([kb[u], kb[u + 1]], axis=0) for u in range(nu)]
            vt = [jnp.concatenate([vb[u], vb[u + 1]], axis=0) for u in range(nu)]
            qt = [qs[r, :] for r in rows[1:]]
            heads = [(u, hh) for u in range(nu) for hh in range(2)]
            s = [_mm_nt(jnp.where(head0 if hh == 0 else jnp.logical_not(head0), qt[u], 0.0), kt[u])
                 + bias_ref[di, hh] for u, hh in heads]
            no_prev = jnp.logical_and(i == 0, col < ATT_BLOCK)
            s = [jnp.where(no_prev, -jnp.inf, x) if u == 0 else x for (u, hh), x in zip(heads, s)]
            m = [jnp.max(x, axis=-1, keepdims=True) for x in s]
            p = _each(lambda x, mx: jnp.exp(x - mx), s, m)
            den = [jnp.sum(x, axis=-1, keepdims=True) for x in p]
            o = [_mm(x, vt[u]) / dn for (u, hh), x, dn in zip(heads, p, den)]
            ls = _each(lambda mx, dn: mx + jnp.log(dn), m, den)
            for u in range(nu):
                o_dst[rows[u + 1], :] = jnp.where(head0, o[2 * u], o[2 * u + 1])
                l_dst[rows[u + 1], :] = jnp.where(head0, ls[2 * u], ls[2 * u + 1])
            return c

        lax.fori_loop(0, nb // nu, body, 0)

    for di, (_, d) in enumerate(DILATED_PATTERNS):
        L = S // d
        nb = L // ATT_BLOCK
        if d == 1:
            run_blocks(di, nb, qn, kn, v_ref, acc, lse)
            continue
        for r in range(d):
            res = pl.ds(r, L, stride=d)
            qd[0:L, :] = qn[res, :]
            kd[0:L, :] = kn[res, :]
            vd[0:L, :] = v_ref[res, :]
            run_blocks(di, nb, qd, kd, vd, od, ld)
            mr = min(L, ATT_MERGE_ROWS)
            for c in range(L // mr):
                piece = pl.ds(r + c * mr * d, mr, stride=d)
                l0, a0 = lse[piece, :], acc[piece, :]
                l1, a1 = ld[c * mr:(c + 1) * mr, :], od[c * mr:(c + 1) * mr, :]
                m = jnp.maximum(l0, l1)
                e0, e1 = jnp.exp(l0 - m), jnp.exp(l1 - m)
                tot = e0 + e1
                acc[piece, :] = (a0 * e0 + a1 * e1) / tot
                lse[piece, :] = m + jnp.log(tot)

    def out_body(i, c):
        sl = pl.ds(pl.multiple_of(i * rows, rows), rows)
        o_ref[sl, :] = acc[sl, :].astype(BF16)
        return c

    lax.fori_loop(0, S // rows, out_body, 0)


def _attn_mixer(z3, bias, qg, kg, gm):
    B, S, _ = z3.shape
    n_lb = GROUP_W // LANE
    l_max = S // DILATED_PATTERNS[1][1]

    def spec(off):
        return pl.BlockSpec((None, S, LANE), lambda b, p: (b, 0, off // LANE + p))

    vec = pl.BlockSpec((1, LANE), lambda b, p: (0, 0))
    full = pltpu.VMEM((S, LANE), F32)
    part = pltpu.VMEM((l_max, LANE), F32)
    return pl.pallas_call(
        _attn_kernel,
        out_shape=jax.ShapeDtypeStruct((B, S, GROUP_W), BF16),
        grid=(B, n_lb),
        in_specs=[spec(OFF_AQ), spec(OFF_AK), spec(OFF_AV),
                  pl.BlockSpec((len(DILATED_PATTERNS), 2, ATT_BLOCK, 2 * ATT_BLOCK), lambda b, p: (0, p, 0, 0)),
                  vec, vec, pl.BlockSpec((LANE, LANE), lambda b, p: (0, 0))],
        out_specs=pl.BlockSpec((None, S, LANE), lambda b, p: (b, 0, p)),
        scratch_shapes=[full, full, part, part, part, part, part, full, full],
        compiler_params=_cparams("parallel", "parallel"),
        name="attn",
    )(z3, z3, z3, bias, qg, kg, gm)


def _conv_kernel(u_ref, g_ref, cw_ref, pv_ref, gm_ref, o_ref, hbuf, hsh, *, carry_phase):
    ts = u_ref.shape[0]
    t = pl.program_id(1)

    if carry_phase:
        @pl.when(t == 0)
        def _():
            hbuf[0:CONV_HALO, :] = jnp.zeros((CONV_HALO, GROUP_W), F32)

        @pl.when(t > 0)
        def _():
            hbuf[0:CONV_HALO, :] = hbuf[ts:ts + CONV_HALO, :]

        return

    hbuf[CONV_HALO:CONV_HALO + ts, :] = u_ref[...] * _sigmoid(g_ref[...])
    first = CONV_HALO - (CONV_WIDTH - 1)
    n_shift = hsh.shape[1]
    for ph in range(1, SUBLANE):
        hsh[ph - 1, :, :] = hbuf[ph:ph + n_shift, :]
    rows = 64
    gm16 = gm_ref[...].astype(BF16)
    for c in range(ts // rows):
        acc = jnp.zeros((rows, GROUP_W), F32) + pv_ref[0:1, :]
        for j in range(CONV_WIDTH):
            ph = (first + j) % SUBLANE
            s = first + j - ph + c * rows
            tap = hbuf[s:s + rows, :] if ph == 0 else hsh[ph - 1, s:s + rows, :]
            acc = acc + cw_ref[j:j + 1, :] * tap
        mu = _dot_exact_rhs(acc, gm16)
        xc = acc - mu
        var = _dot_exact_rhs(xc * xc, gm16)
        y = xc * lax.rsqrt(var + CONV_LN_EPS) * pv_ref[1:2, :] + pv_ref[2:3, :]
        o_ref[c * rows:(c + 1) * rows, :] = (y * _sigmoid(y)).astype(BF16)


MIX_ROWS = 512


def _seg(off, w):
    return pl.BlockSpec((None, MIX_ROWS, w), lambda b, t: (b, t, off // w))


def _const(shape):
    return pl.BlockSpec(shape, lambda b, t: (0, 0))


def _conv_part(z3, conv_w, pvec, gm):
    ts = MIX_ROWS
    return dict(kernel=_conv_kernel,
                in_specs=[_seg(OFF_BU, GROUP_W), _seg(OFF_BG, GROUP_W), _const((CONV_WIDTH, GROUP_W)),
                          _const((8, GROUP_W)), _const((GROUP_W, GROUP_W))],
                args=(z3, z3, conv_w, pvec, gm),
                scratch=[pltpu.VMEM((ts + CONV_HALO, GROUP_W), F32),
                         pltpu.VMEM((SUBLANE - 1, ts + CONV_HALO - SUBLANE, GROUP_W), F32)])


def _mixers_kernel(*refs, parts, n_casts):
    n_in = sum(p[1] for p in parts)
    n_out = len(parts)
    ins, casts_in = refs[:n_in], refs[n_in:n_in + n_casts]
    outs = refs[n_in + n_casts:n_in + n_casts + n_out]
    casts_out = refs[n_in + n_casts + n_out:n_in + 2 * n_casts + n_out]
    scr = refs[n_in + 2 * n_casts + n_out:]
    for carry_phase in (True, False):
        i = s = 0
        for k, (fn, ni, ns) in enumerate(parts):
            fn(*ins[i:i + ni], outs[k], *scr[s:s + ns], carry_phase=carry_phase)
            i += ni
            s += ns
    for src, dst in zip(casts_in, casts_out):
        dst[...] = src[...].astype(BF16)


def _mixers(z3, parts, casts=()):
    B, S, _ = z3.shape
    nt = S // MIX_ROWS
    out_spec = pl.BlockSpec((None, MIX_ROWS, GROUP_W), lambda b, t: (b, t, 0))
    meta = tuple((p["kernel"], len(p["in_specs"]), len(p["scratch"])) for p in parts)
    for c in casts:
        assert c.shape[0] == B * nt, c.shape
    cast_specs = [pl.BlockSpec((None,) + c.shape[1:], lambda b, t: (b * nt + t, 0, 0)) for c in casts]
    return pl.pallas_call(
        functools.partial(_mixers_kernel, parts=meta, n_casts=len(casts)),
        out_shape=[jax.ShapeDtypeStruct((B, S, GROUP_W), BF16)] * len(parts)
                  + [jax.ShapeDtypeStruct(c.shape, BF16) for c in casts],
        grid=(B, nt),
        in_specs=[sp for p in parts for sp in p["in_specs"]] + cast_specs,
        out_specs=[out_spec] * len(parts) + cast_specs,
        scratch_shapes=[sc for p in parts for sc in p["scratch"]],
        compiler_params=_cparams("parallel", "arbitrary"),
        name="mixers_bcd",
    )(*[a for p in parts for a in p["args"]], *casts)


_RW_MU_R, _RW_MU_K, _RW_MU_V, _RW_MU_L, _RW_W0, _RW_A0, _RW_KK, _RW_KA, _RW_RK, _RW_LNG, _RW_LNB = range(11)
SHIFT_PAD = 8
RWKV_GROUP = 8
GLA_GROUP = 8


def _rwkv_kernel(r_ref, k_ref, v_ref, l_ref, pv_ref, w2_ref, a2_ref, g2_ref, bd_ref, st_ref, in_ref, lt_ref,
                 o_ref, rbuf, kbuf, vbuf, lbuf, lw_s, kk_s, be_s, km_s, rr_s, vv_s, gg_s, bo_s, sm_ref, *,
                 carry_phase):
    ts = r_ref.shape[0]
    t = pl.program_id(1)
    bufs = ((rbuf, r_ref), (kbuf, k_ref), (vbuf, v_ref), (lbuf, l_ref))

    if carry_phase:
        @pl.when(t == 0)
        def _():
            for buf, _ in bufs:
                buf[0:SHIFT_PAD, :] = jnp.zeros((SHIFT_PAD, buf.shape[1]), F32)
            sm_ref[...] = jnp.zeros(sm_ref.shape, F32)

        @pl.when(t > 0)
        def _():
            for buf, _ in bufs:
                buf[0:SHIFT_PAD, :] = buf[ts:ts + SHIFT_PAD, :]

        return

    for buf, ref in bufs:
        buf[SHIFT_PAD:SHIFT_PAD + ts, :] = ref[...]

    def pv(i, w=GROUP_W):
        return pv_ref[i:i + 1, 0:w]

    def shifted(buf, mu):
        cur = buf[SHIFT_PAD:SHIFT_PAD + ts, :]
        prev = buf[SHIFT_PAD - 1:SHIFT_PAD - 1 + ts, :]
        return cur + (prev - cur) * mu

    bd = bd_ref[...]
    bd16 = bd.astype(BF16)
    r = shifted(rbuf, pv(_RW_MU_R))
    k = shifted(kbuf, pv(_RW_MU_K))
    v = shifted(vbuf, pv(_RW_MU_V))
    lo = shifted(lbuf, pv(_RW_MU_L, LANE))
    w_log = -_softplus(-(pv(_RW_W0) + _dot(jnp.tanh(lo), w2_ref[...], HI))) - 0.5
    a = _sigmoid(pv(_RW_A0) + _dot(lo, a2_ref[...], HI))
    kk = k * pv(_RW_KK)
    kk = kk / jnp.maximum(jnp.sqrt(_dot_exact_rhs(kk * kk, bd16)), 1e-12)
    km = k * (1.0 + (a - 1.0) * pv(_RW_KA))
    lw_s[...] = -jnp.exp(w_log)
    kk_s[...] = kk
    be_s[...] = kk * a
    km_s[...] = km
    rr_s[...] = r
    vv_s[...] = v
    gg_s[...] = _dot(_sigmoid(lo), g2_ref[...], HI)
    bo_s[...] = _dot_exact_rhs(r * km * pv(_RW_RK), bd16) * v

    n2 = LANE // HEAD_DIM * CHUNK
    bd2 = bd[0:n2, 0:LANE]
    strict = st_ref[0:n2, 0:n2]
    incl = in_ref[0:n2, 0:n2]
    eye = incl - strict
    lt16 = lt_ref[...].astype(BF16)
    gmean16 = (bd2 * (1.0 / HEAD_DIM)).astype(BF16)
    n_pairs = GROUP_W // LANE

    def tile2(x):
        return jnp.concatenate([x] * (LANE // HEAD_DIM), axis=0)

    def fold(x):
        return x[0:CHUNK] + x[CHUNK:2 * CHUNK]

    def group(gi, carry):
        rows = [pl.ds(pl.multiple_of((gi * RWKV_GROUP + j) * CHUNK, CHUNK), CHUNK) for j in range(RWKV_GROUP)]
        items = [(s, slice(p * LANE, (p + 1) * LANE)) for s in rows for p in range(n_pairs)]
        lw = [lw_s[s, ln] for s, ln in items]
        cum = _each(lambda x: _dot_exact_lhs(lt16, x), lw)
        last = [c[CHUNK - 1:CHUNK, :] for c in cum]
        e_neg = [jnp.exp(-c) for c in cum]
        e_last = _each(lambda l, c: jnp.exp(l - c), last, cum)
        bec = [be_s[s, ln] for s, ln in items]
        kmc = [km_s[s, ln] for s, ln in items]
        a_bar = _each(lambda it, c, w: tile2(-kk_s[it[0], it[1]] * jnp.exp(c - w)) * bd2, items, cum, lw)
        r_bar = _each(lambda it, c: tile2(rr_s[it[0], it[1]] * jnp.exp(c)) * bd2, items, cum)
        v_blk = [tile2(vv_s[s, ln]) * bd2 for s, ln in items]
        big = _each(lambda ab, rb, be, km_, en: _mm_nt(
            jnp.concatenate([ab, rb], axis=0),
            jnp.concatenate([tile2(be * en), tile2(km_ * en)], axis=0)), a_bar, r_bar, bec, kmc, e_neg)
        nmat = [b[0:n2, 0:n2] * strict for b in big]
        a_ak = [b[0:n2, n2:2 * n2] * strict for b in big]
        a_rb = [b[n2:2 * n2, 0:n2] * incl for b in big]
        a_rk = [b[n2:2 * n2, n2:2 * n2] * incl for b in big]
        tinv = [eye + n for n in nmat]
        npow = nmat
        for _ in range(5):
            npow = _each(lambda n: _mm(n, n), npow)
            tinv = _each(lambda t_, n: t_ + _mm(t_, n), tinv, npow)
        w_blk = _each(_mm, tinv, a_bar)
        u0 = _each(lambda t_, ak, vb: _mm(t_, _mm(ak, vb)), tinv, a_ak, v_blk)
        o0 = _each(lambda rb, u, rk, vb: fold(_mm(rb, u) + _mm(rk, vb)), a_rb, u0, a_rk, v_blk)
        q_eff = _each(lambda rbar, rb, w: fold(rbar + _mm(rb, w)), r_bar, a_rb, w_blk)
        gh = _each(lambda be, el, w, u: _mm_tn(tile2(be * el) * bd2, jnp.concatenate([w, u], axis=1)),
                   bec, e_last, w_blk, u0)
        g_mat = _each(lambda l, x: eye * jnp.exp(l) + x[:, 0:LANE], last, gh)
        h_mat = _each(lambda x, km_, el, vb: x[:, LANE:2 * LANE] + _mm_tn(tile2(km_ * el) * bd2, vb),
                      gh, kmc, e_last, v_blk)
        outs = [None] * len(items)
        sms = [sm_ref[p] for p in range(n_pairs)]
        for j in range(RWKV_GROUP):
            for p in range(n_pairs):
                i = j * n_pairs + p
                outs[i] = _mm(q_eff[i], sms[p]) + o0[i]
                sms[p] = _mm(g_mat[i], sms[p]) + h_mat[i]
        for p in range(n_pairs):
            sm_ref[p] = sms[p]
        for (s, ln), out in zip(items, outs):
            mu = _dot_exact_rhs(out, gmean16)
            xc = out - mu
            var = _dot_exact_rhs(xc * xc, gmean16)
            y = xc * lax.rsqrt(var + RWKV_LN_EPS) * pv_ref[_RW_LNG:_RW_LNG + 1, ln] + pv_ref[_RW_LNB:_RW_LNB + 1, ln]
            o_ref[s, ln] = ((y + bo_s[s, ln]) * gg_s[s, ln]).astype(BF16)
        return carry

    lax.fori_loop(0, ts // (CHUNK * RWKV_GROUP), group, 0)


def _rwkv_part(z3, pvec, w2p, a2p, g2p, bd, strict, incl, ltri):
    ts = MIX_ROWS
    sq = (4 * CHUNK, 4 * CHUNK)
    wide = pltpu.VMEM((ts, GROUP_W), F32)
    return dict(kernel=_rwkv_kernel,
                in_specs=[_seg(OFF_CR, GROUP_W), _seg(OFF_CK, GROUP_W), _seg(OFF_CV, GROUP_W), _seg(OFF_CL, LANE),
                          _const((16, GROUP_W)), _const((LANE, GROUP_W)), _const((LANE, GROUP_W)),
                          _const((LANE, GROUP_W)), _const(sq), _const(sq), _const(sq), _const((CHUNK, CHUNK))],
                args=(z3, z3, z3, z3, pvec, w2p, a2p, g2p, bd, strict, incl, ltri),
                scratch=[pltpu.VMEM((ts + SHIFT_PAD, GROUP_W), F32)] * 3
                        + [pltpu.VMEM((ts + SHIFT_PAD, LANE), F32)]
                        + [wide] * 8 + [pltpu.VMEM((GROUP_W // LANE, LANE // HEAD_DIM * CHUNK, LANE), F32)])


def _gla_kernel(q_ref, k_ref, v_ref, og_ref, gd_ref, g2_ref, pv_ref, bd_ref, qm_ref, am_ref, lt_ref, o_ref,
                gk_s, st_ref, *, carry_phase):
    ts = q_ref.shape[0]

    if carry_phase:
        @pl.when(pl.program_id(1) == 0)
        def _():
            st_ref[...] = jnp.zeros(st_ref.shape, F32)

        return

    x = _dot(gd_ref[...], g2_ref[...], HI) + pv_ref[0:1, 0:LANE]
    gk_s[...] = -_softplus(-x) * (1.0 / GLA_TAU)
    bd = bd_ref[...]
    qmask = qm_ref[...]
    n_heads = GROUP_W // HEAD_DIM

    lt16 = lt_ref[...].astype(BF16)
    gmean16 = (bd * (1.0 / HEAD_DIM)).astype(BF16)

    def fold(x):
        return x[0:CHUNK] + x[CHUNK:2 * CHUNK] + x[2 * CHUNK:3 * CHUNK] + x[3 * CHUNK:4 * CHUNK]

    def group(gi, carry):
        sls = [pl.ds(pl.multiple_of((gi * GLA_GROUP + j) * CHUNK, CHUNK), CHUNK) for j in range(GLA_GROUP)]
        b = [_dot_exact_lhs(lt16, gk_s[s, :]) for s in sls]
        last = [x[CHUNK - 1:CHUNK, :] for x in b]
        kc = [k_ref[s, :] for s in sls]
        vc = [v_ref[s, :] for s in sls]
        q_d = _each(lambda s, x: q_ref[s, :] * (GLA_KEY_DIM ** -0.5) * jnp.exp(x), sls, b)
        k_d = _each(lambda k_, x: k_ * jnp.exp(-x), kc, b)
        att = _each(lambda q_, k_: _mm_nt(jnp.concatenate([q_] * n_heads, axis=0) * qmask, k_) * am_ref[...],
                    q_d, k_d)
        o_in = _each(lambda a_, v_: fold(_mm(a_, v_) * bd), att, vc)
        upd = _each(lambda v_, k_, l, x: _mm_tn(v_, k_ * jnp.exp(l - x)) * qmask, vc, kc, last, b)
        st = st_ref[...]
        outs = []
        for j in range(GLA_GROUP):
            outs.append(o_in[j] + _mm_nt(q_d[j], st))
            st = st * jnp.exp(last[j]) + upd[j]
        st_ref[...] = st
        for s, o in zip(sls, outs):
            ms = _dot_exact_rhs(o * o, gmean16)
            og = og_ref[s, :]
            y = o * lax.rsqrt(ms + NORM_EPS) * pv_ref[1:2, :] * (og * _sigmoid(og))
            o_ref[s, :] = y.astype(BF16)
        return carry

    lax.fori_loop(0, ts // (CHUNK * GLA_GROUP), group, 0)


def _gla_part(z3, g2p, pvec, bd, qmask, amask, ltri):
    return dict(kernel=_gla_kernel,
                in_specs=[_seg(OFF_DQ, LANE), _seg(OFF_DK, LANE), _seg(OFF_DV, GROUP_W), _seg(OFF_DO, GROUP_W),
                          _seg(OFF_DG, LANE), _const((LANE, LANE)), _const((8, GROUP_W)),
                          _const((GROUP_W, GROUP_W)), _const((GROUP_W, LANE)), _const((GROUP_W, CHUNK)),
                          _const((CHUNK, CHUNK))],
                args=(z3, z3, z3, z3, z3, g2p, pvec, bd, qmask, amask, ltri),
                scratch=[pltpu.VMEM((MIX_ROWS, LANE), F32), pltpu.VMEM((GROUP_W, LANE), F32)])


_RT_E0, _RT_E1, _RT_G0, _RT_G1, _RT_R0, _RT_R1 = range(6)


def _route(xn, wr_ref, tri_ref, run_ref, meta_ref, cnt_ref):
    @pl.when(pl.program_id(0) == 0)
    def _():
        run_ref[...] = jnp.zeros(run_ref.shape, F32)

    tm = xn.shape[0]
    x_hi = xn.astype(BF16)
    x_lo = (xn - x_hi.astype(F32)).astype(BF16)
    y = _mm_nt(wr_ref[...], jnp.concatenate([x_hi, x_lo], axis=0))
    ne = N_EXPERTS
    logits = (y[0:ne, 0:tm] + y[ne:2 * ne, 0:tm]) + (y[0:ne, tm:2 * tm] + y[ne:2 * ne, tm:2 * tm])
    sub = lax.broadcasted_iota(I32, logits.shape, 0).astype(F32)
    m1 = jnp.max(logits, axis=0, keepdims=True)
    i1 = jnp.min(jnp.where(logits == m1, sub, float(ne)), axis=0, keepdims=True)
    rest = jnp.where(sub == i1, -jnp.inf, logits)
    m2 = jnp.max(rest, axis=0, keepdims=True)
    i2 = jnp.min(jnp.where(rest == m2, sub, float(ne)), axis=0, keepdims=True)
    e2 = jnp.exp(m2 - m1)
    den = 1.0 + e2
    hit1 = sub == i1
    hit2 = sub == i2
    member = jnp.where(jnp.logical_or(hit1, hit2), 1.0, 0.0)
    run = run_ref[:, 0:1]
    rank = _dot(member.astype(BF16), tri_ref[...]) + run
    r1 = jnp.sum(jnp.where(hit1, rank, 0.0), axis=0, keepdims=True)
    r2 = jnp.sum(jnp.where(hit2, rank, 0.0), axis=0, keepdims=True)
    run = run + jnp.sum(member, axis=1, keepdims=True)
    run_ref[...] = jnp.broadcast_to(run, run_ref.shape)
    cnt_ref[...] = jnp.broadcast_to(run, cnt_ref.shape)
    rec = jnp.zeros(logits.shape, F32)
    for idx, val in ((_RT_E0, i1), (_RT_E1, i2), (_RT_G0, 1.0 / den), (_RT_G1, e2 / den), (_RT_R0, r1),
                     (_RT_R1, r2)):
        rec = jnp.where(sub == float(idx), val, rec)
    meta_ref[...] = rec


def _outproj_kernel(ya, yb, yc, yd, w_ref, h_ref, g_ref, *rest, route):
    if route:
        wr_ref, tri_ref, ho_ref, hn_ref, meta_ref, cnt_ref, run_ref = rest
    else:
        ho_ref, hn_ref = rest
    acc = h_ref[...]
    for i, y in enumerate((ya, yb, yc, yd)):
        acc = acc + _dot(y[...], w_ref[i * GROUP_W:(i + 1) * GROUP_W, :])
    ho_ref[...] = acc
    ms = jnp.mean(acc * acc, axis=-1, keepdims=True)
    xn = acc * lax.rsqrt(ms + NORM_EPS) * g_ref[...]
    hn_ref[...] = xn.astype(BF16)
    if route:
        _route(xn, wr_ref, tri_ref, run_ref, meta_ref, cnt_ref)


def _outproj(ys, w, h2, g, router=None):
    T = h2.shape[0]
    tm = ROUTE_TILE
    yspec = pl.BlockSpec((tm, GROUP_W), lambda i: (i, 0))
    row = pl.BlockSpec((tm, D_MODEL), lambda i: (i, 0))
    in_specs = [yspec] * 4 + [pl.BlockSpec((D_MODEL, D_MODEL), lambda i: (0, 0)), row,
                              pl.BlockSpec((1, D_MODEL), lambda i: (0, 0))]
    out_shape = [jax.ShapeDtypeStruct((T, D_MODEL), F32), jax.ShapeDtypeStruct((T, D_MODEL), BF16)]
    out_specs = [row, row]
    args = (*ys, w, h2, g)
    scratch = []
    if router is not None:
        in_specs += [pl.BlockSpec((2 * N_EXPERTS, D_MODEL), lambda i: (0, 0)),
                     pl.BlockSpec((tm, tm), lambda i: (0, 0))]
        out_shape += [jax.ShapeDtypeStruct((N_EXPERTS, T), F32), jax.ShapeDtypeStruct((N_EXPERTS, LANE), F32)]
        out_specs += [pl.BlockSpec((N_EXPERTS, tm), lambda i: (0, i)),
                      pl.BlockSpec((N_EXPERTS, LANE), lambda i: (0, 0))]
        args += router
        scratch = [pltpu.VMEM((N_EXPERTS, LANE), F32)]
    return pl.pallas_call(
        functools.partial(_outproj_kernel, route=router is not None),
        out_shape=out_shape,
        grid=(T // tm,),
        in_specs=in_specs,
        out_specs=out_specs,
        scratch_shapes=scratch,
        compiler_params=_cparams("arbitrary"),
        name="outproj_route" if router is not None else "outproj",
    )(*args)


FFN_SPLIT = 2


def _swiglu_rows(x, wg_ref, wu_ref, wd_ref):
    piece = D_FF // FFN_SPLIT
    out = None
    for s in range(FFN_SPLIT):
        cols = slice(s * piece, (s + 1) * piece)
        g = _dot(x, wg_ref[:, cols])
        u = _dot(x, wu_ref[:, cols])
        y = _dot((g * _sigmoid(g) * u).astype(BF16), wd_ref[cols, :])
        out = y if out is None else out + y
    return out


def _ffn_kernel(x_ref, h_ref, wg_ref, wu_ref, wd_ref, o_ref):
    o_ref[...] = h_ref[...] + _swiglu_rows(x_ref[...], wg_ref, wu_ref, wd_ref)


def _ffn(hn, h2, wg, wu, wd):
    T = hn.shape[0]
    tm = FFN_ROWS
    row = pl.BlockSpec((tm, D_MODEL), lambda i: (i, 0))
    return pl.pallas_call(
        _ffn_kernel,
        out_shape=jax.ShapeDtypeStruct((T, D_MODEL), F32),
        grid=(T // tm,),
        in_specs=[row, row,
                  _resident((D_MODEL, D_FF), lambda i: (0, 0)),
                  _resident((D_MODEL, D_FF), lambda i: (0, 0)),
                  _resident((D_FF, D_MODEL), lambda i: (0, 0))],
        out_specs=row,
        compiler_params=_cparams("parallel"),
        name="ffn",
    )(hn, h2, wg, wu, wd)


def _row_copy(src_ref, src_row, dst_ref, dst_row, sem):
    return pltpu.make_async_copy(src_ref.at[pl.ds(src_row, 1)], dst_ref.at[pl.ds(dst_row, 1)], sem)


def _pad_fill_copies(ps_ref, pl_ref, zbuf, xs_ref, zsem):
    out = []
    for e in range(N_EXPERTS):
        start = ps_ref[e]
        end = start + pl_ref[e]
        aligned = jnp.minimum((start + SUBLANE - 1) // SUBLANE * SUBLANE, end)
        for u in range(SUBLANE - 1):
            out.append((start + u < aligned, _row_copy(zbuf, 0, xs_ref, start + u, zsem)))
        rest = end - aligned
        size = MOE_TILE // 2
        while size >= SUBLANE:
            dst = pl.multiple_of(aligned + rest // (2 * size) * (2 * size), SUBLANE)
            out.append(((rest // size) % 2 == 1,
                        pltpu.make_async_copy(zbuf.at[pl.ds(0, size)], xs_ref.at[pl.ds(dst, size)], zsem)))
            size //= 2
    for k in range(N_EXPERTS):
        dst = pl.multiple_of(ps_ref[N_EXPERTS] + k * MOE_TILE, MOE_TILE)
        out.append((k < pl_ref[N_EXPERTS],
                    pltpu.make_async_copy(zbuf, xs_ref.at[pl.ds(dst, MOE_TILE)], zsem)))
    return out


def _dispatch_kernel(ps_ref, pl_ref, pos_ref, hn_ref, xs_ref, buf, zbuf, sem, zsem):
    tm = hn_ref.shape[0]

    @pl.when(pl.program_id(0) == 0)
    def _():
        zbuf[...] = jnp.zeros(zbuf.shape, F32)
        copies = _pad_fill_copies(ps_ref, pl_ref, zbuf, xs_ref, zsem)
        for pred, copy in copies:
            pl.when(pred)(copy.start)
        for pred, copy in copies:
            pl.when(pred)(copy.wait)

    buf[...] = hn_ref[...].astype(F32).reshape(buf.shape)

    def issue(g, c):
        for u in range(SUBLANE):
            for s in range(TOP_K):
                dst = pos_ref[0, s * tm + g * SUBLANE + u]
                pltpu.make_async_copy(buf.at[g, pl.ds(u, 1)], xs_ref.at[pl.ds(dst, 1)],
                                      sem).start(priority=s)
        return c

    lax.fori_loop(0, tm // SUBLANE, issue, 0)
    for _ in range(TOP_K):
        pltpu.make_async_copy(buf, buf, sem).wait()


def _dispatch(pad_start, pad_len, pos, hn, n_rows):
    T = hn.shape[0]
    tm = MOVE_TILE
    return pl.pallas_call(
        _dispatch_kernel,
        out_shape=jax.ShapeDtypeStruct((n_rows, D_MODEL), F32),
        grid_spec=pltpu.PrefetchScalarGridSpec(
            num_scalar_prefetch=2,
            grid=(T // tm,),
            in_specs=[pl.BlockSpec((None, 1, TOP_K * tm), lambda i, ps, pn: (i, 0, 0), memory_space=pltpu.SMEM),
                      pl.BlockSpec((tm, D_MODEL), lambda i, ps, pn: (i, 0))],
            out_specs=pl.BlockSpec(memory_space=pl.ANY),
            scratch_shapes=[pltpu.VMEM((tm // SUBLANE, SUBLANE, D_MODEL), F32),
                            pltpu.VMEM((MOE_TILE, D_MODEL), F32),
                            pltpu.SemaphoreType.DMA, pltpu.SemaphoreType.DMA]),
        compiler_params=_cparams("arbitrary"),
        name="moe_dispatch",
    )(pad_start, pad_len, pos, hn)


def _moe_ffn_kernel(te_ref, nu_ref, x_ref, wg_ref, wu_ref, wd_ref, o_ref):
    del te_ref
    i = pl.program_id(0)

    @pl.when(i < nu_ref[0])
    def _():
        o_ref[...] = _swiglu_rows(x_ref[...].astype(BF16), wg_ref, wu_ref, wd_ref)

    @pl.when(i >= nu_ref[0])
    def _():
        o_ref[...] = jnp.zeros(o_ref.shape, F32)


def _moe_ffn(tile_expert, n_used, xs, wg, wu, wd):
    n_rows = xs.shape[0]
    tm = MOE_TILE

    def row_map(i, te, nu):
        return (jnp.maximum(jnp.minimum(i, nu[0] - 1), 0), 0)

    def expert_map(i, te, nu):
        return (te[i], 0, 0)

    return pl.pallas_call(
        _moe_ffn_kernel,
        out_shape=jax.ShapeDtypeStruct((n_rows, D_MODEL), F32),
        grid_spec=pltpu.PrefetchScalarGridSpec(
            num_scalar_prefetch=2,
            grid=(n_rows // tm,),
            in_specs=[pl.BlockSpec((tm, D_MODEL), row_map),
                      _resident((None, D_MODEL, D_FF), expert_map),
                      _resident((None, D_MODEL, D_FF), expert_map),
                      _resident((None, D_FF, D_MODEL), expert_map)],
            out_specs=pl.BlockSpec((tm, D_MODEL), lambda i, te, nu: (i, 0))),
        compiler_params=_cparams("arbitrary"),
        name="moe_ffn",
    )(tile_expert, n_used, xs, wg, wu, wd)


def _combine_kernel(pos_ref, meta_ref, h_ref, yb_ref, o_ref, buf0, buf1, sems):
    tm = h_ref.shape[0]
    bufs = (buf0, buf1)

    def issue(j, c):
        for s in range(TOP_K):
            _row_copy(yb_ref, pos_ref[0, s * tm + j], bufs[s], j, sems.at[s]).start(priority=s)
        return c

    lax.fori_loop(0, tm, issue, 0, unroll=8)
    fields = jnp.transpose(meta_ref[...])
    g0 = fields[:, _RT_G0:_RT_G0 + 1]
    g1 = fields[:, _RT_G1:_RT_G1 + 1]
    for s in range(TOP_K):
        pltpu.make_async_copy(yb_ref.at[pl.ds(0, tm)], bufs[s], sems.at[s]).wait()
    o_ref[...] = h_ref[...] + (g0 * buf0[...] + g1 * buf1[...])


def _combine(pos, meta, h2, yb):
    T = h2.shape[0]
    tm = MOVE_TILE
    row = pl.BlockSpec((tm, D_MODEL), lambda i: (i, 0))
    return pl.pallas_call(
        _combine_kernel,
        out_shape=jax.ShapeDtypeStruct((T, D_MODEL), F32),
        grid=(T // tm,),
        in_specs=[pl.BlockSpec((None, 1, TOP_K * tm), lambda i: (i, 0, 0), memory_space=pltpu.SMEM),
                  pl.BlockSpec((N_EXPERTS, tm), lambda i: (0, i)), row,
                  pl.BlockSpec(memory_space=pl.ANY)],
        out_specs=row,
        scratch_shapes=[pltpu.VMEM((tm, D_MODEL), F32), pltpu.VMEM((tm, D_MODEL), F32),
                        pltpu.SemaphoreType.DMA((TOP_K,))],
        compiler_params=_cparams("arbitrary"),
        name="moe_combine",
    )(pos, meta, h2, yb)


def _moe(hn, h2, meta, cnt, wg, wu, wd):
    T, D = h2.shape
    counts = cnt[:, 0].astype(I32)
    padded = (counts + MOE_TILE - 1) // MOE_TILE * MOE_TILE
    ends = jnp.cumsum(padded)
    starts = ends - padded
    n_rows = TOP_K * T + N_EXPERTS * MOE_TILE
    n_tiles = n_rows // MOE_TILE
    experts = jnp.arange(N_EXPERTS, dtype=I32)

    def slot(e_row, r_row):
        e = meta[e_row].astype(I32)
        start = jnp.sum(jnp.where(e[None, :] == experts[:, None], starts[:, None], 0), axis=0)
        return start + meta[r_row].astype(I32)

    nt = T // MOVE_TILE
    pos = jnp.concatenate([slot(_RT_E0, _RT_R0).reshape(nt, MOVE_TILE),
                           slot(_RT_E1, _RT_R1).reshape(nt, MOVE_TILE)], axis=1)[:, None, :]
    tile_start = jnp.arange(n_tiles, dtype=I32) * MOE_TILE
    tile_expert = jnp.minimum(jnp.sum(ends[None, :] <= tile_start[:, None], axis=1), N_EXPERTS - 1).astype(I32)
    n_used = (ends[-1:] // MOE_TILE).astype(I32)
    pad_start = jnp.concatenate([starts + counts, ends[-1:]]).astype(I32)
    pad_len = jnp.concatenate([padded - counts, (n_rows - ends[-1:]) // MOE_TILE]).astype(I32)
    xs = _dispatch(pad_start, pad_len, pos, hn, n_rows)
    yb = _moe_ffn(tile_expert, n_used, xs, wg, wu, wd)
    return _combine(pos, meta, h2, yb)


def _block_diag_ones(n, blk):
    i = np.arange(n)
    return (i[:, None] // blk == i[None, :] // blk).astype(np.float32)


def _consts():
    n4 = 4 * CHUNK
    i = np.arange(n4)
    bd = _block_diag_ones(n4, CHUNK)
    tr, tc = i[:, None] % CHUNK, i[None, :] % CHUNK
    strict = bd * (tr > tc)
    incl = bd * (tr >= tc)
    ltri = np.tril(np.ones((CHUNK, CHUNK), np.float32))
    pair_mean = _block_diag_ones(LANE, HEAD_DIM) / HEAD_DIM
    qmask = (i[:, None] // CHUNK == np.arange(LANE)[None, :] // GLA_KEY_DIM).astype(np.float32)
    amask = (i[:, None] % CHUNK >= np.arange(CHUNK)[None, :]).astype(np.float32)
    out = {k: jnp.asarray(v, F32) for k, v in dict(
        bd=bd, strict=strict, incl=incl, ltri=ltri, pair_mean=pair_mean, qmask=qmask, amask=amask).items()}
    out["route_tri"] = jnp.asarray(np.triu(np.ones((ROUTE_TILE, ROUTE_TILE), np.float32), 1), BF16)
    return out


def _pad_rows(w, row0, total):
    return jnp.pad(w.astype(F32), ((row0, total - row0 - w.shape[0]), (0, 0)))


def _pack_rows(rows, width, n_rows):
    rows = [r.reshape(-1).astype(F32) for r in rows]
    rows = [jnp.pad(r, (0, width - r.shape[0])) for r in rows]
    return jnp.pad(jnp.stack(rows), ((0, n_rows - len(rows)), (0, 0)))


def kernel(x, rel_bias, mix_norm_g, w_in, w_out, ffn_norm_g, attn_q_norm_g, attn_k_norm_g, conv_w, conv_b, conv_ln_g, conv_ln_b, rwkv_mu, rwkv_w0, rwkv_w2, rwkv_a0, rwkv_a2, rwkv_g2, rwkv_k_k, rwkv_k_a, rwkv_r_k, rwkv_ln_g, rwkv_ln_b, gla_g2, gla_gb, gla_norm_g, ffn_wg, ffn_wu, ffn_wd, moe_router, moe_wg, moe_wu, moe_wd):
    B, S, D = x.shape
    T = B * S
    cs = _consts()
    bias = _attn_bias_tables(rel_bias)
    h = x.reshape(T, D)
    for layer in range(DEPTH):
        w = w_in[layer]
        dq, dk, dv, dg, do = (w[:, 2176:2304], w[:, 2304:2432], w[:, 2432:2688], w[:, 2688:2704], w[:, 2704:2960])
        wp = jnp.concatenate([w[:, :2176], dq, dv, do, dk, dg, jnp.zeros((D, IN_WP - IN_W), w.dtype)],
                             axis=1).astype(BF16)
        z = _inproj(h, mix_norm_g[layer][None, :], wp)
        z3 = z.reshape(B, S, IN_WP)

        qg = jnp.tile(attn_q_norm_g[layer].astype(F32) * (HEAD_DIM ** -0.5), 2)[None, :]
        kg = jnp.tile(attn_k_norm_g[layer].astype(F32), 2)[None, :]
        ya = _attn_mixer(z3, bias, qg, kg, cs["pair_mean"]).reshape(T, GROUP_W)

        conv_pv = _pack_rows([conv_b[layer], conv_ln_g[layer], conv_ln_b[layer]], GROUP_W, 8)
        group_mean = cs["bd"] * (1.0 / HEAD_DIM)
        mu = rwkv_mu[layer]
        rw_pv = _pack_rows([mu[0:256], mu[256:512], mu[512:768], mu[768:896], rwkv_w0[layer], rwkv_a0[layer],
                            rwkv_k_k[layer], rwkv_k_a[layer], rwkv_r_k[layer], rwkv_ln_g[layer],
                            rwkv_ln_b[layer]], GROUP_W, 16)
        gla_pv = _pack_rows([gla_gb[layer], jnp.tile(gla_norm_g[layer], GROUP_W // HEAD_DIM)], GROUP_W, 8)
        casts, n_steps = [], T // MIX_ROWS
        if layer % 2 == 0 and layer + 1 < DEPTH:
            nxt = (layer + 1) // 2
            casts = [w.astype(F32).reshape(n_steps, -1, w.shape[-1]) for w in (moe_wg[nxt], moe_wu[nxt], moe_wd[nxt])]
        yc, yd, yb, *cast_out = _mixers(z3, [
            _rwkv_part(z3, rw_pv, _pad_rows(rwkv_w2[layer], 0, LANE), _pad_rows(rwkv_a2[layer], 32, LANE),
                       _pad_rows(rwkv_g2[layer], 64, LANE), cs["bd"], cs["strict"], cs["incl"], cs["ltri"]),
            _gla_part(z3, _pad_rows(gla_g2[layer], 0, LANE), gla_pv, cs["bd"], cs["qmask"], cs["amask"],
                      cs["ltri"]),
            _conv_part(z3, conv_w[layer].astype(F32), conv_pv, group_mean)], casts)
        if cast_out:
            experts_bf16 = [c.reshape(w.shape) for c, w in zip(cast_out, (moe_wg[nxt], moe_wu[nxt], moe_wd[nxt]))]

        norm_g = ffn_norm_g[layer][None, :]
        ys = (ya, yb.reshape(T, GROUP_W), yc.reshape(T, GROUP_W), yd.reshape(T, GROUP_W))
        i = layer // 2
        if layer % 2 == 0:
            h, hn = _outproj(ys, w_out[layer].astype(BF16), h, norm_g)
            h = _ffn(hn, h, ffn_wg[i].astype(BF16), ffn_wu[i].astype(BF16), ffn_wd[i].astype(BF16))
        else:
            wr = jnp.transpose(moe_router[i].astype(F32))
            wr_hi = wr.astype(BF16)
            wr = jnp.concatenate([wr_hi, (wr - wr_hi.astype(F32)).astype(BF16)], axis=0)
            h, hn, meta, cnt = _outproj(ys, w_out[layer].astype(BF16), h, norm_g, router=(wr, cs["route_tri"]))
            h = _moe(hn, h, meta, cnt, *experts_bf16)
    return h.reshape(B, S, D)
```

```python
import functools
import math

import numpy as np
import jax
import jax.numpy as jnp
from jax import lax
from jax.experimental import pallas as pl
from jax.experimental.pallas import tpu as pltpu

F32 = jnp.float32
BF16 = jnp.bfloat16
I32 = jnp.int32
HI = lax.Precision.HIGHEST

D_MODEL = 1024
DEPTH = 2
GROUP_W = 256
NORM_EPS = 1e-6
HEAD_DIM = 64
DILATED_PATTERNS = ((128, 1), (512, 4), (2048, 16))
ATT_BLOCK = 128
N_BUCKETS = 32
REL_MAX_DIST = 2048
CONV_WIDTH = 31
CONV_HALO = 32
CONV_LN_EPS = 1e-5
RWKV_LN_EPS = 64e-5
GLA_KEY_DIM = 32
GLA_TAU = 16.0
CHUNK = 64
D_FF = 2816
N_EXPERTS = 8
TOP_K = 2
IN_W = 2960
IN_WP = 3072
LANE = 128
SUBLANE = 8
VMEM_LIMIT = 48 * 1024 * 1024
MOE_TILE = 512
FFN_ROWS = 512
ROUTE_TILE = 1024
MOVE_TILE = 1024

OFF_AQ, OFF_AK, OFF_AV = 0, 256, 512
OFF_BU, OFF_BG = 768, 1024
OFF_CR, OFF_CK, OFF_CV, OFF_CL = 1280, 1536, 1792, 2048
OFF_DQ, OFF_DV, OFF_DO, OFF_DK, OFF_DG = 2176, 2304, 2560, 2816, 2944


def _cparams(*sem):
    return pltpu.CompilerParams(dimension_semantics=sem, vmem_limit_bytes=VMEM_LIMIT)


def _dot(a, b, prec=None):
    return jnp.dot(a, b, preferred_element_type=F32, precision=prec)


def _mm(a, b):
    return jnp.dot(a.astype(BF16), b.astype(BF16), preferred_element_type=F32)


def _mm_nt(a, b):
    return lax.dot_general(a.astype(BF16), b.astype(BF16), (((1,), (1,)), ((), ())),
                           preferred_element_type=F32)


def _mm_tn(a, b):
    return lax.dot_general(a.astype(BF16), b.astype(BF16), (((0,), (0,)), ((), ())),
                           preferred_element_type=F32)


def _split3(x):
    x1 = x.astype(BF16)
    r1 = x - x1.astype(F32)
    x2 = r1.astype(BF16)
    x3 = (r1 - x2.astype(F32)).astype(BF16)
    return x1, x2, x3


def _dot_exact_rhs(x, m):
    n = x.shape[0]
    y = _dot(jnp.concatenate(_split3(x), axis=0), m)
    return y[0:n] + y[n:2 * n] + y[2 * n:3 * n]


def _dot_exact_lhs(m, x):
    n = x.shape[1]
    y = _dot(m, jnp.concatenate(_split3(x), axis=1))
    return y[:, 0:n] + y[:, n:2 * n] + y[:, 2 * n:3 * n]


def _each(fn, *lists):
    return [fn(*args) for args in zip(*lists)]


def _sigmoid(x):
    return 1.0 / (1.0 + jnp.exp(-x))


def _softplus(x):
    return jnp.maximum(x, 0.0) + jnp.log(1.0 + jnp.exp(-jnp.abs(x)))


def _resident(shape, index_map):
    return pl.BlockSpec(shape, index_map, pipeline_mode=pl.Buffered(1))


def _inproj_kernel(x_ref, g_ref, w_ref, z_ref):
    x = x_ref[...]
    ms = jnp.mean(x * x, axis=-1, keepdims=True)
    xn = (x * lax.rsqrt(ms + NORM_EPS) * g_ref[...]).astype(BF16)
    z_ref[...] = _dot(xn, w_ref[...])


def _inproj(h2, g, w):
    T = h2.shape[0]
    tm = FFN_ROWS
    return pl.pallas_call(
        _inproj_kernel,
        out_shape=jax.ShapeDtypeStruct((T, IN_WP), F32),
        grid=(T // tm,),
        in_specs=[pl.BlockSpec((tm, D_MODEL), lambda i: (i, 0)),
                  pl.BlockSpec((1, D_MODEL), lambda i: (0, 0)),
                  _resident((D_MODEL, IN_WP), lambda i: (0, 0))],
        out_specs=pl.BlockSpec((tm, IN_WP), lambda i: (i, 0)),
        compiler_params=_cparams("parallel"),
        name="inproj",
    )(h2, g, w)


def _t5_bucket(dist):
    max_exact = N_BUCKETS // 2
    n = np.maximum(dist, 0)
    large = max_exact + (np.log(np.maximum(n, 1) / max_exact) / math.log(REL_MAX_DIST / max_exact)
                         * (N_BUCKETS - max_exact)).astype(np.int32)
    large = np.minimum(large, N_BUCKETS - 1)
    return np.where(n < max_exact, n, large).astype(np.int32)


def _bucket_table():
    W = ATT_BLOCK
    delta = np.arange(W)[:, None] + W - np.arange(2 * W)[None, :]
    band = (delta >= 0) & (delta <= W)
    tabs = [np.where(band, _t5_bucket(np.clip(delta, 0, W) * d), -1) for _, d in DILATED_PATTERNS]
    return jnp.asarray(np.stack(tabs), I32)


def _bias_kernel(rb_ref, bk_ref, o_ref):
    n_heads = o_ref.shape[1]
    for di in range(len(DILATED_PATTERNS)):
        bk = bk_ref[di]
        accs = [jnp.full(bk.shape, -jnp.inf, F32) for _ in range(n_heads)]
        for b in range(N_BUCKETS):
            hit = bk == b
            accs = [jnp.where(hit, rb_ref[b, hh], a) for hh, a in enumerate(accs)]
        for hh in range(n_heads):
            o_ref[di, hh] = accs[hh]


def _attn_bias_tables(rel_bias):
    n_heads = rel_bias.shape[1]
    shape = (len(DILATED_PATTERNS), n_heads, ATT_BLOCK, 2 * ATT_BLOCK)
    return pl.pallas_call(
        _bias_kernel,
        out_shape=jax.ShapeDtypeStruct(shape, F32),
        in_specs=[pl.BlockSpec(memory_space=pltpu.SMEM), pl.BlockSpec(memory_space=pltpu.VMEM)],
        out_specs=pl.BlockSpec(memory_space=pltpu.VMEM),
        name="attn_bias",
    )(rel_bias.astype(F32), _bucket_table())


ATT_MERGE_ROWS = 256
ATT_UNROLL = 8


def _attn_kernel(q_ref, k_ref, v_ref, bias_ref, qg_ref, kg_ref, gm_ref, o_ref,
                 qn, kn, qd, kd, vd, od, ld, acc, lse):
    S = q_ref.shape[0]
    rows = 512

    def norm_body(i, c):
        sl = pl.ds(pl.multiple_of(i * rows, rows), rows)
        q = q_ref[sl, :]
        k = k_ref[sl, :]
        gm16 = gm_ref[...].astype(BF16)
        qn[sl, :] = q * lax.rsqrt(_dot_exact_rhs(q * q, gm16) + NORM_EPS) * qg_ref[...]
        kn[sl, :] = k * lax.rsqrt(_dot_exact_rhs(k * k, gm16) + NORM_EPS) * kg_ref[...]
        return c

    lax.fori_loop(0, S // rows, norm_body, 0)

    lane = lax.broadcasted_iota(I32, (ATT_BLOCK, LANE), 1)
    head0 = lane < HEAD_DIM
    col = lax.broadcasted_iota(I32, (ATT_BLOCK, 2 * ATT_BLOCK), 1)

    def run_blocks(di, nb, qs, ks, vs, o_dst, l_dst):
        nu = min(nb, ATT_UNROLL)

        def blk(n):
            return pl.ds(pl.multiple_of(n * ATT_BLOCK, ATT_BLOCK), ATT_BLOCK)

        def body(i, c):
            base = i * nu
            rows = [blk(jnp.maximum(base - 1, 0))] + [blk(base + u) for u in range(nu)]
            kb = [ks[r, :].astype(BF16) for r in rows]
            vb = [vs[r, :].astype(BF16) for r in rows]
            kt = [jnp.concatenate([kb[u], kb[u + 1]], axis=0) for u in range(nu)]
            vt = [jnp.concatenate([vb[u], vb[u + 1]], axis=0) for u in range(nu)]
            qt = [qs[r, :] for r in rows[1:]]
            heads = [(u, hh) for u in range(nu) for hh in range(2)]
            s = [_mm_nt(jnp.where(head0 if hh == 0 else jnp.logical_not(head0), qt[u], 0.0), kt[u])
                 + bias_ref[di, hh] for u, hh in heads]
            no_prev = jnp.logical_and(i == 0, col < ATT_BLOCK)
            s = [jnp.where(no_prev, -jnp.inf, x) if u == 0 else x for (u, hh), x in zip(heads, s)]
            m = [jnp.max(x, axis=-1, keepdims=True) for x in s]
            p = _each(lambda x, mx: jnp.exp(x - mx), s, m)
            den = [jnp.sum(x, axis=-1, keepdims=True) for x in p]
            o = [_mm(x, vt[u]) / dn for (u, hh), x, dn in zip(heads, p, den)]
            ls = _each(lambda mx, dn: mx + jnp.log(dn), m, den)
            for u in range(nu):
                o_dst[rows[u + 1], :] = jnp.where(head0, o[2 * u], o[2 * u + 1])
                l_dst[rows[u + 1], :] = jnp.where(head0, ls[2 * u], ls[2 * u + 1])
            return c

        lax.fori_loop(0, nb // nu, body, 0)

    for di, (_, d) in enumerate(DILATED_PATTERNS):
        L = S // d
        nb = L // ATT_BLOCK
        if d == 1:
            run_blocks(di, nb, qn, kn, v_ref, acc, lse)
            continue
        for r in range(d):
            res = pl.ds(r, L, stride=d)
            qd[0:L, :] = qn[res, :]
            kd[0:L, :] = kn[res, :]
            vd[0:L, :] = v_ref[res, :]
            run_blocks(di, nb, qd, kd, vd, od, ld)
            mr = min(L, ATT_MERGE_ROWS)
            for c in range(L // mr):
                piece = pl.ds(r + c * mr * d, mr, stride=d)
                l0, a0 = lse[piece, :], acc[piece, :]
                l1, a1 = ld[c * mr:(c + 1) * mr, :], od[c * mr:(c + 1) * mr, :]
                m = jnp.maximum(l0, l1)
                e0, e1 = jnp.exp(l0 - m), jnp.exp(l1 - m)
                tot = e0 + e1
                acc[piece, :] = (a0 * e0 + a1 * e1) / tot
                lse[piece, :] = m + jnp.log(tot)

    def out_body(i, c):
        sl = pl.ds(pl.multiple_of(i * rows, rows), rows)
        o_ref[sl, :] = acc[sl, :].astype(BF16)
        return c

    lax.fori_loop(0, S // rows, out_body, 0)


def _attn_mixer(z3, bias, qg, kg, gm):
    B, S, _ = z3.shape
    n_lb = GROUP_W // LANE
    l_max = S // DILATED_PATTERNS[1][1]

    def spec(off):
        return pl.BlockSpec((None, S, LANE), lambda b, p: (b, 0, off // LANE + p))

    vec = pl.BlockSpec((1, LANE), lambda b, p: (0, 0))
    full = pltpu.VMEM((S, LANE), F32)
    part = pltpu.VMEM((l_max, LANE), F32)
    return pl.pallas_call(
        _attn_kernel,
        out_shape=jax.ShapeDtypeStruct((B, S, GROUP_W), BF16),
        grid=(B, n_lb),
        in_specs=[spec(OFF_AQ), spec(OFF_AK), spec(OFF_AV),
                  pl.BlockSpec((len(DILATED_PATTERNS), 2, ATT_BLOCK, 2 * ATT_BLOCK), lambda b, p: (0, p, 0, 0)),
                  vec, vec, pl.BlockSpec((LANE, LANE), lambda b, p: (0, 0))],
        out_specs=pl.BlockSpec((None, S, LANE), lambda b, p: (b, 0, p)),
        scratch_shapes=[full, full, part, part, part, part, part, full, full],
        compiler_params=_cparams("parallel", "parallel"),
        name="attn",
    )(z3, z3, z3, bias, qg, kg, gm)


def _conv_kernel(u_ref, g_ref, cw_ref, pv_ref, gm_ref, o_ref, hbuf, hsh, *, carry_phase):
    ts = u_ref.shape[0]
    t = pl.program_id(1)

    if carry_phase:
        @pl.when(t == 0)
        def _():
            hbuf[0:CONV_HALO, :] = jnp.zeros((CONV_HALO, GROUP_W), F32)

        @pl.when(t > 0)
        def _():
            hbuf[0:CONV_HALO, :] = hbuf[ts:ts + CONV_HALO, :]

        return

    hbuf[CONV_HALO:CONV_HALO + ts, :] = u_ref[...] * _sigmoid(g_ref[...])
    first = CONV_HALO - (CONV_WIDTH - 1)
    n_shift = hsh.shape[1]
    for ph in range(1, SUBLANE):
        hsh[ph - 1, :, :] = hbuf[ph:ph + n_shift, :]
    rows = 64
    gm16 = gm_ref[...].astype(BF16)
    for c in range(ts // rows):
        acc = jnp.zeros((rows, GROUP_W), F32) + pv_ref[0:1, :]
        for j in range(CONV_WIDTH):
            ph = (first + j) % SUBLANE
            s = first + j - ph + c * rows
            tap = hbuf[s:s + rows, :] if ph == 0 else hsh[ph - 1, s:s + rows, :]
            acc = acc + cw_ref[j:j + 1, :] * tap
        mu = _dot_exact_rhs(acc, gm16)
        xc = acc - mu
        var = _dot_exact_rhs(xc * xc, gm16)
        y = xc * lax.rsqrt(var + CONV_LN_EPS) * pv_ref[1:2, :] + pv_ref[2:3, :]
        o_ref[c * rows:(c + 1) * rows, :] = (y * _sigmoid(y)).astype(BF16)


MIX_ROWS = 512


def _seg(off, w):
    return pl.BlockSpec((None, MIX_ROWS, w), lambda b, t: (b, t, off // w))


def _const(shape):
    return pl.BlockSpec(shape, lambda b, t: (0, 0))


def _conv_part(z3, conv_w, pvec, gm):
    ts = MIX_ROWS
    return dict(kernel=_conv_kernel,
                in_specs=[_seg(OFF_BU, GROUP_W), _seg(OFF_BG, GROUP_W), _const((CONV_WIDTH, GROUP_W)),
                          _const((8, GROUP_W)), _const((GROUP_W, GROUP_W))],
                args=(z3, z3, conv_w, pvec, gm),
                scratch=[pltpu.VMEM((ts + CONV_HALO, GROUP_W), F32),
                         pltpu.VMEM((SUBLANE - 1, ts + CONV_HALO - SUBLANE, GROUP_W), F32)])


def _mixers_kernel(*refs, parts, n_casts):
    n_in = sum(p[1] for p in parts)
    n_out = len(parts)
    ins, casts_in = refs[:n_in], refs[n_in:n_in + n_casts]
    outs = refs[n_in + n_casts:n_in + n_casts + n_out]
    casts_out = refs[n_in + n_casts + n_out:n_in + 2 * n_casts + n_out]
    scr = refs[n_in + 2 * n_casts + n_out:]
    for carry_phase in (True, False):
        i = s = 0
        for k, (fn, ni, ns) in enumerate(parts):
            fn(*ins[i:i + ni], outs[k], *scr[s:s + ns], carry_phase=carry_phase)
            i += ni
            s += ns
    for src, dst in zip(casts_in, casts_out):
        dst[...] = src[...].astype(BF16)


def _mixers(z3, parts, casts=()):
    B, S, _ = z3.shape
    nt = S // MIX_ROWS
    out_spec = pl.BlockSpec((None, MIX_ROWS, GROUP_W), lambda b, t: (b, t, 0))
    meta = tuple((p["kernel"], len(p["in_specs"]), len(p["scratch"])) for p in parts)
    for c in casts:
        assert c.shape[0] == B * nt, c.shape
    cast_specs = [pl.BlockSpec((None,) + c.shape[1:], lambda b, t: (b * nt + t, 0, 0)) for c in casts]
    return pl.pallas_call(
        functools.partial(_mixers_kernel, parts=meta, n_casts=len(casts)),
        out_shape=[jax.ShapeDtypeStruct((B, S, GROUP_W), BF16)] * len(parts)
                  + [jax.ShapeDtypeStruct(c.shape, BF16) for c in casts],
        grid=(B, nt),
        in_specs=[sp for p in parts for sp in p["in_specs"]] + cast_specs,
        out_specs=[out_spec] * len(parts) + cast_specs,
        scratch_shapes=[sc for p in parts for sc in p["scratch"]],
        compiler_params=_cparams("parallel", "arbitrary"),
        name="mixers_bcd",
    )(*[a for p in parts for a in p["args"]], *casts)


_RW_MU_R, _RW_MU_K, _RW_MU_V, _RW_MU_L, _RW_W0, _RW_A0, _RW_KK, _RW_KA, _RW_RK, _RW_LNG, _RW_LNB = range(11)
SHIFT_PAD = 8
RWKV_GROUP = 8
GLA_GROUP = 8


def _rwkv_kernel(r_ref, k_ref, v_ref, l_ref, pv_ref, w2_ref, a2_ref, g2_ref, bd_ref, st_ref, in_ref, lt_ref,
                 o_ref, rbuf, kbuf, vbuf, lbuf, lw_s, kk_s, be_s, km_s, rr_s, vv_s, gg_s, bo_s, sm_ref, *,
                 carry_phase):
    ts = r_ref.shape[0]
    t = pl.program_id(1)
    bufs = ((rbuf, r_ref), (kbuf, k_ref), (vbuf, v_ref), (lbuf, l_ref))

    if carry_phase:
        @pl.when(t == 0)
        def _():
            for buf, _ in bufs:
                buf[0:SHIFT_PAD, :] = jnp.zeros((SHIFT_PAD, buf.shape[1]), F32)
            sm_ref[...] = jnp.zeros(sm_ref.shape, F32)

        @pl.when(t > 0)
        def _():
            for buf, _ in bufs:
                buf[0:SHIFT_PAD, :] = buf[ts:ts + SHIFT_PAD, :]

        return

    for buf, ref in bufs:
        buf[SHIFT_PAD:SHIFT_PAD + ts, :] = ref[...]

    def pv(i, w=GROUP_W):
        return pv_ref[i:i + 1, 0:w]

    def shifted(buf, mu):
        cur = buf[SHIFT_PAD:SHIFT_PAD + ts, :]
        prev = buf[SHIFT_PAD - 1:SHIFT_PAD - 1 + ts, :]
        return cur + (prev - cur) * mu

    bd = bd_ref[...]
    bd16 = bd.astype(BF16)
    r = shifted(rbuf, pv(_RW_MU_R))
    k = shifted(kbuf, pv(_RW_MU_K))
    v = shifted(vbuf, pv(_RW_MU_V))
    lo = shifted(lbuf, pv(_RW_MU_L, LANE))
    w_log = -_softplus(-(pv(_RW_W0) + _dot(jnp.tanh(lo), w2_ref[...], HI))) - 0.5
    a = _sigmoid(pv(_RW_A0) + _dot(lo, a2_ref[...], HI))
    kk = k * pv(_RW_KK)
    kk = kk / jnp.maximum(jnp.sqrt(_dot_exact_rhs(kk * kk, bd16)), 1e-12)
    km = k * (1.0 + (a - 1.0) * pv(_RW_KA))
    lw_s[...] = -jnp.exp(w_log)
    kk_s[...] = kk
    be_s[...] = kk * a
    km_s[...] = km
    rr_s[...] = r
    vv_s[...] = v
    gg_s[...] = _dot(_sigmoid(lo), g2_ref[...], HI)
    bo_s[...] = _dot_exact_rhs(r * km * pv(_RW_RK), bd16) * v

    n2 = LANE // HEAD_DIM * CHUNK
    bd2 = bd[0:n2, 0:LANE]
    strict = st_ref[0:n2, 0:n2]
    incl = in_ref[0:n2, 0:n2]
    eye = incl - strict
    lt16 = lt_ref[...].astype(BF16)
    gmean16 = (bd2 * (1.0 / HEAD_DIM)).astype(BF16)
    n_pairs = GROUP_W // LANE

    def tile2(x):
        return jnp.concatenate([x] * (LANE // HEAD_DIM), axis=0)

    def fold(x):
        return x[0:CHUNK] + x[CHUNK:2 * CHUNK]

    def group(gi, carry):
        rows = [pl.ds(pl.multiple_of((gi * RWKV_GROUP + j) * CHUNK, CHUNK), CHUNK) for j in range(RWKV_GROUP)]
        items = [(s, slice(p * LANE, (p + 1) * LANE)) for s in rows for p in range(n_pairs)]
        lw = [lw_s[s, ln] for s, ln in items]
        cum = _each(lambda x: _dot_exact_lhs(lt16, x), lw)
        last = [c[CHUNK - 1:CHUNK, :] for c in cum]
        e_neg = [jnp.exp(-c) for c in cum]
        e_last = _each(lambda l, c: jnp.exp(l - c), last, cum)
        bec = [be_s[s, ln] for s, ln in items]
        kmc = [km_s[s, ln] for s, ln in items]
        a_bar = _each(lambda it, c, w: tile2(-kk_s[it[0], it[1]] * jnp.exp(c - w)) * bd2, items, cum, lw)
        r_bar = _each(lambda it, c: tile2(rr_s[it[0], it[1]] * jnp.exp(c)) * bd2, items, cum)
        v_blk = [tile2(vv_s[s, ln]) * bd2 for s, ln in items]
        big = _each(lambda ab, rb, be, km_, en: _mm_nt(
            jnp.concatenate([ab, rb], axis=0),
            jnp.concatenate([tile2(be * en), tile2(km_ * en)], axis=0)), a_bar, r_bar, bec, kmc, e_neg)
        nmat = [b[0:n2, 0:n2] * strict for b in big]
        a_ak = [b[0:n2, n2:2 * n2] * strict for b in big]
        a_rb = [b[n2:2 * n2, 0:n2] * incl for b in big]
        a_rk = [b[n2:2 * n2, n2:2 * n2] * incl for b in big]
        tinv = [eye + n for n in nmat]
        npow = nmat
        for _ in range(5):
            npow = _each(lambda n: _mm(n, n), npow)
            tinv = _each(lambda t_, n: t_ + _mm(t_, n), tinv, npow)
        w_blk = _each(_mm, tinv, a_bar)
        u0 = _each(lambda t_, ak, vb: _mm(t_, _mm(ak, vb)), tinv, a_ak, v_blk)
        o0 = _each(lambda rb, u, rk, vb: fold(_mm(rb, u) + _mm(rk, vb)), a_rb, u0, a_rk, v_blk)
        q_eff = _each(lambda rbar, rb, w: fold(rbar + _mm(rb, w)), r_bar, a_rb, w_blk)
        gh = _each(lambda be, el, w, u: _mm_tn(tile2(be * el) * bd2, jnp.concatenate([w, u], axis=1)),
                   bec, e_last, w_blk, u0)
        g_mat = _each(lambda l, x: eye * jnp.exp(l) + x[:, 0:LANE], last, gh)
        h_mat = _each(lambda x, km_, el, vb: x[:, LANE:2 * LANE] + _mm_tn(tile2(km_ * el) * bd2, vb),
                      gh, kmc, e_last, v_blk)
        outs = [None] * len(items)
        sms = [sm_ref[p] for p in range(n_pairs)]
        for j in range(RWKV_GROUP):
            for p in range(n_pairs):
                i = j * n_pairs + p
                outs[i] = _mm(q_eff[i], sms[p]) + o0[i]
                sms[p] = _mm(g_mat[i], sms[p]) + h_mat[i]
        for p in range(n_pairs):
            sm_ref[p] = sms[p]
        for (s, ln), out in zip(items, outs):
            mu = _dot_exact_rhs(out, gmean16)
            xc = out - mu
            var = _dot_exact_rhs(xc * xc, gmean16)
            y = xc * lax.rsqrt(var + RWKV_LN_EPS) * pv_ref[_RW_LNG:_RW_LNG + 1, ln] + pv_ref[_RW_LNB:_RW_LNB + 1, ln]
            o_ref[s, ln] = ((y + bo_s[s, ln]) * gg_s[s, ln]).astype(BF16)
        return carry

    lax.fori_loop(0, ts // (CHUNK * RWKV_GROUP), group, 0)


def _rwkv_part(z3, pvec, w2p, a2p, g2p, bd, strict, incl, ltri):
    ts = MIX_ROWS
    sq = (4 * CHUNK, 4 * CHUNK)
    wide = pltpu.VMEM((ts, GROUP_W), F32)
    return dict(kernel=_rwkv_kernel,
                in_specs=[_seg(OFF_CR, GROUP_W), _seg(OFF_CK, GROUP_W), _seg(OFF_CV, GROUP_W), _seg(OFF_CL, LANE),
                          _const((16, GROUP_W)), _const((LANE, GROUP_W)), _const((LANE, GROUP_W)),
                          _const((LANE, GROUP_W)), _const(sq), _const(sq), _const(sq), _const((CHUNK, CHUNK))],
                args=(z3, z3, z3, z3, pvec, w2p, a2p, g2p, bd, strict, incl, ltri),
                scratch=[pltpu.VMEM((ts + SHIFT_PAD, GROUP_W), F32)] * 3
                        + [pltpu.VMEM((ts + SHIFT_PAD, LANE), F32)]
                        + [wide] * 8 + [pltpu.VMEM((GROUP_W // LANE, LANE // HEAD_DIM * CHUNK, LANE), F32)])


def _gla_kernel(q_ref, k_ref, v_ref, og_ref, gd_ref, g2_ref, pv_ref, bd_ref, qm_ref, am_ref, lt_ref, o_ref,
                gk_s, st_ref, *, carry_phase):
    ts = q_ref.shape[0]

    if carry_phase:
        @pl.when(pl.program_id(1) == 0)
        def _():
            st_ref[...] = jnp.zeros(st_ref.shape, F32)

        return

    x = _dot(gd_ref[...], g2_ref[...], HI) + pv_ref[0:1, 0:LANE]
    gk_s[...] = -_softplus(-x) * (1.0 / GLA_TAU)
    bd = bd_ref[...]
    qmask = qm_ref[...]
    n_heads = GROUP_W // HEAD_DIM

    lt16 = lt_ref[...].astype(BF16)
    gmean16 = (bd * (1.0 / HEAD_DIM)).astype(BF16)

    def fold(x):
        return x[0:CHUNK] + x[CHUNK:2 * CHUNK] + x[2 * CHUNK:3 * CHUNK] + x[3 * CHUNK:4 * CHUNK]

    def group(gi, carry):
        sls = [pl.ds(pl.multiple_of((gi * GLA_GROUP + j) * CHUNK, CHUNK), CHUNK) for j in range(GLA_GROUP)]
        b = [_dot_exact_lhs(lt16, gk_s[s, :]) for s in sls]
        last = [x[CHUNK - 1:CHUNK, :] for x in b]
        kc = [k_ref[s, :] for s in sls]
        vc = [v_ref[s, :] for s in sls]
        q_d = _each(lambda s, x: q_ref[s, :] * (GLA_KEY_DIM ** -0.5) * jnp.exp(x), sls, b)
        k_d = _each(lambda k_, x: k_ * jnp.exp(-x), kc, b)
        att = _each(lambda q_, k_: _mm_nt(jnp.concatenate([q_] * n_heads, axis=0) * qmask, k_) * am_ref[...],
                    q_d, k_d)
        o_in = _each(lambda a_, v_: fold(_mm(a_, v_) * bd), att, vc)
        upd = _each(lambda v_, k_, l, x: _mm_tn(v_, k_ * jnp.exp(l - x)) * qmask, vc, kc, last, b)
        st = st_ref[...]
        outs = []
        for j in range(GLA_GROUP):
            outs.append(o_in[j] + _mm_nt(q_d[j], st))
            st = st * jnp.exp(last[j]) + upd[j]
        st_ref[...] = st
        for s, o in zip(sls, outs):
            ms = _dot_exact_rhs(o * o, gmean16)
            og = og_ref[s, :]
            y = o * lax.rsqrt(ms + NORM_EPS) * pv_ref[1:2, :] * (og * _sigmoid(og))
            o_ref[s, :] = y.astype(BF16)
        return carry

    lax.fori_loop(0, ts // (CHUNK * GLA_GROUP), group, 0)


def _gla_part(z3, g2p, pvec, bd, qmask, amask, ltri):
    return dict(kernel=_gla_kernel,
                in_specs=[_seg(OFF_DQ, LANE), _seg(OFF_DK, LANE), _seg(OFF_DV, GROUP_W), _seg(OFF_DO, GROUP_W),
                          _seg(OFF_DG, LANE), _const((LANE, LANE)), _const((8, GROUP_W)),
                          _const((GROUP_W, GROUP_W)), _const((GROUP_W, LANE)), _const((GROUP_W, CHUNK)),
                          _const((CHUNK, CHUNK))],
                args=(z3, z3, z3, z3, z3, g2p, pvec, bd, qmask, amask, ltri),
                scratch=[pltpu.VMEM((MIX_ROWS, LANE), F32), pltpu.VMEM((GROUP_W, LANE), F32)])


_RT_E0, _RT_E1, _RT_G0, _RT_G1, _RT_R0, _RT_R1 = range(6)


def _route(xn, wr_ref, tri_ref, run_ref, meta_ref, cnt_ref):
    @pl.when(pl.program_id(0) == 0)
    def _():
        run_ref[...] = jnp.zeros(run_ref.shape, F32)

    tm = xn.shape[0]
    x_hi = xn.astype(BF16)
    x_lo = (xn - x_hi.astype(F32)).astype(BF16)
    y = _mm_nt(wr_ref[...], jnp.concatenate([x_hi, x_lo], axis=0))
    ne = N_EXPERTS
    logits = (y[0:ne, 0:tm] + y[ne:2 * ne, 0:tm]) + (y[0:ne, tm:2 * tm] + y[ne:2 * ne, tm:2 * tm])
    sub = lax.broadcasted_iota(I32, logits.shape, 0).astype(F32)
    m1 = jnp.max(logits, axis=0, keepdims=True)
    i1 = jnp.min(jnp.where(logits == m1, sub, float(ne)), axis=0, keepdims=True)
    rest = jnp.where(sub == i1, -jnp.inf, logits)
    m2 = jnp.max(rest, axis=0, keepdims=True)
    i2 = jnp.min(jnp.where(rest == m2, sub, float(ne)), axis=0, keepdims=True)
    e2 = jnp.exp(m2 - m1)
    den = 1.0 + e2
    hit1 = sub == i1
    hit2 = sub == i2
    member = jnp.where(jnp.logical_or(hit1, hit2), 1.0, 0.0)
    run = run_ref[:, 0:1]
    rank = _dot(member.astype(BF16), tri_ref[...]) + run
    r1 = jnp.sum(jnp.where(hit1, rank, 0.0), axis=0, keepdims=True)
    r2 = jnp.sum(jnp.where(hit2, rank, 0.0), axis=0, keepdims=True)
    run = run + jnp.sum(member, axis=1, keepdims=True)
    run_ref[...] = jnp.broadcast_to(run, run_ref.shape)
    cnt_ref[...] = jnp.broadcast_to(run, cnt_ref.shape)
    rec = jnp.zeros(logits.shape, F32)
    for idx, val in ((_RT_E0, i1), (_RT_E1, i2), (_RT_G0, 1.0 / den), (_RT_G1, e2 / den), (_RT_R0, r1),
                     (_RT_R1, r2)):
        rec = jnp.where(sub == float(idx), val, rec)
    meta_ref[...] = rec


def _outproj_kernel(ya, yb, yc, yd, w_ref, h_ref, g_ref, *rest, route):
    if route:
        wr_ref, tri_ref, ho_ref, hn_ref, meta_ref, cnt_ref, run_ref = rest
    else:
        ho_ref, hn_ref = rest
    acc = h_ref[...]
    for i, y in enumerate((ya, yb, yc, yd)):
        acc = acc + _dot(y[...], w_ref[i * GROUP_W:(i + 1) * GROUP_W, :])
    ho_ref[...] = acc
    ms = jnp.mean(acc * acc, axis=-1, keepdims=True)
    xn = acc * lax.rsqrt(ms + NORM_EPS) * g_ref[...]
    hn_ref[...] = xn.astype(BF16)
    if route:
        _route(xn, wr_ref, tri_ref, run_ref, meta_ref, cnt_ref)


def _outproj(ys, w, h2, g, router=None):
    T = h2.shape[0]
    tm = ROUTE_TILE
    yspec = pl.BlockSpec((tm, GROUP_W), lambda i: (i, 0))
    row = pl.BlockSpec((tm, D_MODEL), lambda i: (i, 0))
    in_specs = [yspec] * 4 + [pl.BlockSpec((D_MODEL, D_MODEL), lambda i: (0, 0)), row,
                              pl.BlockSpec((1, D_MODEL), lambda i: (0, 0))]
    out_shape = [jax.ShapeDtypeStruct((T, D_MODEL), F32), jax.ShapeDtypeStruct((T, D_MODEL), BF16)]
    out_specs = [row, row]
    args = (*ys, w, h2, g)
    scratch = []
    if router is not None:
        in_specs += [pl.BlockSpec((2 * N_EXPERTS, D_MODEL), lambda i: (0, 0)),
                     pl.BlockSpec((tm, tm), lambda i: (0, 0))]
        out_shape += [jax.ShapeDtypeStruct((N_EXPERTS, T), F32), jax.ShapeDtypeStruct((N_EXPERTS, LANE), F32)]
        out_specs += [pl.BlockSpec((N_EXPERTS, tm), lambda i: (0, i)),
                      pl.BlockSpec((N_EXPERTS, LANE), lambda i: (0, 0))]
        args += router
        scratch = [pltpu.VMEM((N_EXPERTS, LANE), F32)]
    return pl.pallas_call(
        functools.partial(_outproj_kernel, route=router is not None),
        out_shape=out_shape,
        grid=(T // tm,),
        in_specs=in_specs,
        out_specs=out_specs,
        scratch_shapes=scratch,
        compiler_params=_cparams("arbitrary"),
        name="outproj_route" if router is not None else "outproj",
    )(*args)


FFN_SPLIT = 2


def _swiglu_rows(x, wg_ref, wu_ref, wd_ref):
    piece = D_FF // FFN_SPLIT
    out = None
    for s in range(FFN_SPLIT):
        cols = slice(s * piece, (s + 1) * piece)
        g = _dot(x, wg_ref[:, cols])
        u = _dot(x, wu_ref[:, cols])
        y = _dot((g * _sigmoid(g) * u).astype(BF16), wd_ref[cols, :])
        out = y if out is None else out + y
    return out


def _ffn_kernel(x_ref, h_ref, wg_ref, wu_ref, wd_ref, o_ref):
    o_ref[...] = h_ref[...] + _swiglu_rows(x_ref[...], wg_ref, wu_ref, wd_ref)


def _ffn(hn, h2, wg, wu, wd):
    T = hn.shape[0]
    tm = FFN_ROWS
    row = pl.BlockSpec((tm, D_MODEL), lambda i: (i, 0))
    return pl.pallas_call(
        _ffn_kernel,
        out_shape=jax.ShapeDtypeStruct((T, D_MODEL), F32),
        grid=(T // tm,),
        in_specs=[row, row,
                  _resident((D_MODEL, D_FF), lambda i: (0, 0)),
                  _resident((D_MODEL, D_FF), lambda i: (0, 0)),
                  _resident((D_FF, D_MODEL), lambda i: (0, 0))],
        out_specs=row,
        compiler_params=_cparams("parallel"),
        name="ffn",
    )(hn, h2, wg, wu, wd)


def _row_copy(src_ref, src_row, dst_ref, dst_row, sem):
    return pltpu.make_async_copy(src_ref.at[pl.ds(src_row, 1)], dst_ref.at[pl.ds(dst_row, 1)], sem)


def _pad_fill_copies(ps_ref, pl_ref, zbuf, xs_ref, zsem):
    out = []
    for e in range(N_EXPERTS):
        start = ps_ref[e]
        end = start + pl_ref[e]
        aligned = jnp.minimum((start + SUBLANE - 1) // SUBLANE * SUBLANE, end)
        for u in range(SUBLANE - 1):
            out.append((start + u < aligned, _row_copy(zbuf, 0, xs_ref, start + u, zsem)))
        rest = end - aligned
        size = MOE_TILE // 2
        while size >= SUBLANE:
            dst = pl.multiple_of(aligned + rest // (2 * size) * (2 * size), SUBLANE)
            out.append(((rest // size) % 2 == 1,
                        pltpu.make_async_copy(zbuf.at[pl.ds(0, size)], xs_ref.at[pl.ds(dst, size)], zsem)))
            size //= 2
    for k in range(N_EXPERTS):
        dst = pl.multiple_of(ps_ref[N_EXPERTS] + k * MOE_TILE, MOE_TILE)
        out.append((k < pl_ref[N_EXPERTS],
                    pltpu.make_async_copy(zbuf, xs_ref.at[pl.ds(dst, MOE_TILE)], zsem)))
    return out


def _dispatch_kernel(ps_ref, pl_ref, pos_ref, hn_ref, xs_ref, buf, zbuf, sem, zsem):
    tm = hn_ref.shape[0]

    @pl.when(pl.program_id(0) == 0)
    def _():
        zbuf[...] = jnp.zeros(zbuf.shape, F32)
        copies = _pad_fill_copies(ps_ref, pl_ref, zbuf, xs_ref, zsem)
        for pred, copy in copies:
            pl.when(pred)(copy.start)
        for pred, copy in copies:
            pl.when(pred)(copy.wait)

    buf[...] = hn_ref[...].astype(F32).reshape(buf.shape)

    def issue(g, c):
        for u in range(SUBLANE):
            for s in range(TOP_K):
                dst = pos_ref[0, s * tm + g * SUBLANE + u]
                pltpu.make_async_copy(buf.at[g, pl.ds(u, 1)], xs_ref.at[pl.ds(dst, 1)],
                                      sem).start(priority=s)
        return c

    lax.fori_loop(0, tm // SUBLANE, issue, 0)
    for _ in range(TOP_K):
        pltpu.make_async_copy(buf, buf, sem).wait()


def _dispatch(pad_start, pad_len, pos, hn, n_rows):
    T = hn.shape[0]
    tm = MOVE_TILE
    return pl.pallas_call(
        _dispatch_kernel,
        out_shape=jax.ShapeDtypeStruct((n_rows, D_MODEL), F32),
        grid_spec=pltpu.PrefetchScalarGridSpec(
            num_scalar_prefetch=2,
            grid=(T // tm,),
            in_specs=[pl.BlockSpec((None, 1, TOP_K * tm), lambda i, ps, pn: (i, 0, 0), memory_space=pltpu.SMEM),
                      pl.BlockSpec((tm, D_MODEL), lambda i, ps, pn: (i, 0))],
            out_specs=pl.BlockSpec(memory_space=pl.ANY),
            scratch_shapes=[pltpu.VMEM((tm // SUBLANE, SUBLANE, D_MODEL), F32),
                            pltpu.VMEM((MOE_TILE, D_MODEL), F32),
                            pltpu.SemaphoreType.DMA, pltpu.SemaphoreType.DMA]),
        compiler_params=_cparams("arbitrary"),
        name="moe_dispatch",
    )(pad_start, pad_len, pos, hn)


def _moe_ffn_kernel(te_ref, nu_ref, x_ref, wg_ref, wu_ref, wd_ref, o_ref):
    del te_ref
    i = pl.program_id(0)

    @pl.when(i < nu_ref[0])
    def _():
        o_ref[...] = _swiglu_rows(x_ref[...].astype(BF16), wg_ref, wu_ref, wd_ref)

    @pl.when(i >= nu_ref[0])
    def _():
        o_ref[...] = jnp.zeros(o_ref.shape, F32)


def _moe_ffn(tile_expert, n_used, xs, wg, wu, wd):
    n_rows = xs.shape[0]
    tm = MOE_TILE

    def row_map(i, te, nu):
        return (jnp.maximum(jnp.minimum(i, nu[0] - 1), 0), 0)

    def expert_map(i, te, nu):
        return (te[i], 0, 0)

    return pl.pallas_call(
        _moe_ffn_kernel,
        out_shape=jax.ShapeDtypeStruct((n_rows, D_MODEL), F32),
        grid_spec=pltpu.PrefetchScalarGridSpec(
            num_scalar_prefetch=2,
            grid=(n_rows // tm,),
            in_specs=[pl.BlockSpec((tm, D_MODEL), row_map),
                      _resident((None, D_MODEL, D_FF), expert_map),
                      _resident((None, D_MODEL, D_FF), expert_map),
                      _resident((None, D_FF, D_MODEL), expert_map)],
            out_specs=pl.BlockSpec((tm, D_MODEL), lambda i, te, nu: (i, 0))),
        compiler_params=_cparams("arbitrary"),
        name="moe_ffn",
    )(tile_expert, n_used, xs, wg, wu, wd)


def _combine_kernel(pos_ref, nxt_ref, meta_ref, h_ref, yb_ref, o_ref, bufs, sems):
    tm = h_ref.shape[0]
    i = pl.program_id(0)
    cur = i % 2

    def gather(slots_ref, side):
        def issue(j, c):
            for s in range(TOP_K):
                pltpu.make_async_copy(yb_ref.at[pl.ds(slots_ref[0, s * tm + j], 1)],
                                      bufs.at[side, s, pl.ds(j, 1)], sems.at[side, s]).start(priority=s)
            return c

        lax.fori_loop(0, tm, issue, 0, unroll=8)

    @pl.when(i == 0)
    def _():
        gather(pos_ref, 0)

    @pl.when(i + 1 < pl.num_programs(0))
    def _():
        gather(nxt_ref, 1 - cur)

    fields = jnp.transpose(meta_ref[...])
    g0 = fields[:, _RT_G0:_RT_G0 + 1]
    g1 = fields[:, _RT_G1:_RT_G1 + 1]
    for s in range(TOP_K):
        pltpu.make_async_copy(yb_ref.at[pl.ds(0, tm)], bufs.at[cur, s], sems.at[cur, s]).wait()
    o_ref[...] = h_ref[...] + (g0 * bufs[cur, 0] + g1 * bufs[cur, 1])


def _combine(pos, meta, h2, yb):
    T = h2.shape[0]
    tm = MOVE_TILE
    row = pl.BlockSpec((tm, D_MODEL), lambda i: (i, 0))
    n_steps = T // tm
    return pl.pallas_call(
        _combine_kernel,
        out_shape=jax.ShapeDtypeStruct((T, D_MODEL), F32),
        grid=(n_steps,),
        in_specs=[pl.BlockSpec((None, 1, TOP_K * tm), lambda i: (i, 0, 0), memory_space=pltpu.SMEM),
                  pl.BlockSpec((None, 1, TOP_K * tm), lambda i: (jnp.minimum(i + 1, n_steps - 1), 0, 0),
                               memory_space=pltpu.SMEM),
                  pl.BlockSpec((N_EXPERTS, tm), lambda i: (0, i)), row,
                  pl.BlockSpec(memory_space=pl.ANY)],
        out_specs=row,
        scratch_shapes=[pltpu.VMEM((2, TOP_K, tm, D_MODEL), F32), pltpu.SemaphoreType.DMA((2, TOP_K))],
        compiler_params=_cparams("arbitrary"),
        name="moe_combine",
    )(pos, pos, meta, h2, yb)


def _moe(hn, h2, meta, cnt, wg, wu, wd):
    T, D = h2.shape
    counts = cnt[:, 0].astype(I32)
    padded = (counts + MOE_TILE - 1) // MOE_TILE * MOE_TILE
    ends = jnp.cumsum(padded)
    starts = ends - padded
    n_rows = TOP_K * T + N_EXPERTS * MOE_TILE
    n_tiles = n_rows // MOE_TILE
    experts = jnp.arange(N_EXPERTS, dtype=I32)

    def slot(e_row, r_row):
        e = meta[e_row].astype(I32)
        start = jnp.sum(jnp.where(e[None, :] == experts[:, None], starts[:, None], 0), axis=0)
        return start + meta[r_row].astype(I32)

    nt = T // MOVE_TILE
    pos = jnp.concatenate([slot(_RT_E0, _RT_R0).reshape(nt, MOVE_TILE),
                           slot(_RT_E1, _RT_R1).reshape(nt, MOVE_TILE)], axis=1)[:, None, :]
    tile_start = jnp.arange(n_tiles, dtype=I32) * MOE_TILE
    tile_expert = jnp.minimum(jnp.sum(ends[None, :] <= tile_start[:, None], axis=1), N_EXPERTS - 1).astype(I32)
    n_used = (ends[-1:] // MOE_TILE).astype(I32)
    pad_start = jnp.concatenate([starts + counts, ends[-1:]]).astype(I32)
    pad_len = jnp.concatenate([padded - counts, (n_rows - ends[-1:]) // MOE_TILE]).astype(I32)
    xs = _dispatch(pad_start, pad_len, pos, hn, n_rows)
    yb = _moe_ffn(tile_expert, n_used, xs, wg, wu, wd)
    return _combine(pos, meta, h2, yb)


def _block_diag_ones(n, blk):
    i = np.arange(n)
    return (i[:, None] // blk == i[None, :] // blk).astype(np.float32)


def _consts():
    n4 = 4 * CHUNK
    i = np.arange(n4)
    bd = _block_diag_ones(n4, CHUNK)
    tr, tc = i[:, None] % CHUNK, i[None, :] % CHUNK
    strict = bd * (tr > tc)
    incl = bd * (tr >= tc)
    ltri = np.tril(np.ones((CHUNK, CHUNK), np.float32))
    pair_mean = _block_diag_ones(LANE, HEAD_DIM) / HEAD_DIM
    qmask = (i[:, None] // CHUNK == np.arange(LANE)[None, :] // GLA_KEY_DIM).astype(np.float32)
    amask = (i[:, None] % CHUNK >= np.arange(CHUNK)[None, :]).astype(np.float32)
    out = {k: jnp.asarray(v, F32) for k, v in dict(
        bd=bd, strict=strict, incl=incl, ltri=ltri, pair_mean=pair_mean, qmask=qmask, amask=amask).items()}
    out["route_tri"] = jnp.asarray(np.triu(np.ones((ROUTE_TILE, ROUTE_TILE), np.float32), 1), BF16)
    return out


def _pad_rows(w, row0, total):
    return jnp.pad(w.astype(F32), ((row0, total - row0 - w.shape[0]), (0, 0)))


def _pack_rows(rows, width, n_rows):
    rows = [r.reshape(-1).astype(F32) for r in rows]
    rows = [jnp.pad(r, (0, width - r.shape[0])) for r in rows]
    return jnp.pad(jnp.stack(rows), ((0, n_rows - len(rows)), (0, 0)))


def kernel(x, rel_bias, mix_norm_g, w_in, w_out, ffn_norm_g, attn_q_norm_g, attn_k_norm_g, conv_w, conv_b, conv_ln_g, conv_ln_b, rwkv_mu, rwkv_w0, rwkv_w2, rwkv_a0, rwkv_a2, rwkv_g2, rwkv_k_k, rwkv_k_a, rwkv_r_k, rwkv_ln_g, rwkv_ln_b, gla_g2, gla_gb, gla_norm_g, ffn_wg, ffn_wu, ffn_wd, moe_router, moe_wg, moe_wu, moe_wd):
    B, S, D = x.shape
    T = B * S
    cs = _consts()
    bias = _attn_bias_tables(rel_bias)
    h = x.reshape(T, D)
    for layer in range(DEPTH):
        w = w_in[layer]
        dq, dk, dv, dg, do = (w[:, 2176:2304], w[:, 2304:2432], w[:, 2432:2688], w[:, 2688:2704], w[:, 2704:2960])
        wp = jnp.concatenate([w[:, :2176], dq, dv, do, dk, dg, jnp.zeros((D, IN_WP - IN_W), w.dtype)],
                             axis=1).astype(BF16)
        z = _inproj(h, mix_norm_g[layer][None, :], wp)
        z3 = z.reshape(B, S, IN_WP)

        qg = jnp.tile(attn_q_norm_g[layer].astype(F32) * (HEAD_DIM ** -0.5), 2)[None, :]
        kg = jnp.tile(attn_k_norm_g[layer].astype(F32), 2)[None, :]
        ya = _attn_mixer(z3, bias, qg, kg, cs["pair_mean"]).reshape(T, GROUP_W)

        conv_pv = _pack_rows([conv_b[layer], conv_ln_g[layer], conv_ln_b[layer]], GROUP_W, 8)
        group_mean = cs["bd"] * (1.0 / HEAD_DIM)
        mu = rwkv_mu[layer]
        rw_pv = _pack_rows([mu[0:256], mu[256:512], mu[512:768], mu[768:896], rwkv_w0[layer], rwkv_a0[layer],
                            rwkv_k_k[layer], rwkv_k_a[layer], rwkv_r_k[layer], rwkv_ln_g[layer],
                            rwkv_ln_b[layer]], GROUP_W, 16)
        gla_pv = _pack_rows([gla_gb[layer], jnp.tile(gla_norm_g[layer], GROUP_W // HEAD_DIM)], GROUP_W, 8)
        casts, n_steps = [], T // MIX_ROWS
        if layer % 2 == 0 and layer + 1 < DEPTH:
            nxt = (layer + 1) // 2
            casts = [w.astype(F32).reshape(n_steps, -1, w.shape[-1]) for w in (moe_wg[nxt], moe_wu[nxt], moe_wd[nxt])]
        yc, yd, yb, *cast_out = _mixers(z3, [
            _rwkv_part(z3, rw_pv, _pad_rows(rwkv_w2[layer], 0, LANE), _pad_rows(rwkv_a2[layer], 32, LANE),
                       _pad_rows(rwkv_g2[layer], 64, LANE), cs["bd"], cs["strict"], cs["incl"], cs["ltri"]),
            _gla_part(z3, _pad_rows(gla_g2[layer], 0, LANE), gla_pv, cs["bd"], cs["qmask"], cs["amask"],
                      cs["ltri"]),
            _conv_part(z3, conv_w[layer].astype(F32), conv_pv, group_mean)], casts)
        if cast_out:
            experts_bf16 = [c.reshape(w.shape) for c, w in zip(cast_out, (moe_wg[nxt], moe_wu[nxt], moe_wd[nxt]))]

        norm_g = ffn_norm_g[layer][None, :]
        ys = (ya, yb.reshape(T, GROUP_W), yc.reshape(T, GROUP_W), yd.reshape(T, GROUP_W))
        i = layer // 2
        if layer % 2 == 0:
            h, hn = _outproj(ys, w_out[layer].astype(BF16), h, norm_g)
            h = _ffn(hn, h, ffn_wg[i].astype(BF16), ffn_wu[i].astype(BF16), ffn_wd[i].astype(BF16))
        else:
            wr = jnp.transpose(moe_router[i].astype(F32))
            wr_hi = wr.astype(BF16)
            wr = jnp.concatenate([wr_hi, (wr - wr_hi.astype(F32)).astype(BF16)], axis=0)
            h, hn, meta, cnt = _outproj(ys, w_out[layer].astype(BF16), h, norm_g, router=(wr, cs["route_tri"]))
            h = _moe(hn, h, meta, cnt, *experts_bf16)
    return h.reshape(B, S, D)
```
